```python
import math
import jax, jax.numpy as jnp
from jax import lax
import numpy as np

D_MODEL = 1024
BATCH = 8
SEQ = 8192
DEPTH = 2

GRID_W = 64
CTX_LEN = 256
N_BRANCH = 4
BRANCH_W = 512
HY_CH = 512
HY_ORDER = 2
HY_BANDS = 16
HY_EMB = 1 + 2 * HY_BANDS
HY_FFN = 64
HY_MIN_DECAY = 3.07
HY_MAX_DECAY = 15.35
GA_HEADS = 4
GA_KV = 2
GA_HD = 128
ML_HEADS = 4
ML_HD = 128
ML_CHUNK = 64
ML_GATES = 2 * 2 * ML_HEADS
WA_HEADS = 8
WA_KV = 2
WA_HD = 64
WINDOW = 128
Q_BLOCK = 128
ROPE_BASE = 10000.0
D_FF = 4 * D_MODEL
EPS = 1e-6
NEG_INF = -1e30

HY_COLS = 3 * HY_CH
GA_COLS = (GA_HEADS + 2 * GA_KV) * GA_HD
ML_COLS = 4 * ML_HEADS * ML_HD + ML_GATES
WA_COLS = (WA_HEADS + 2 * WA_KV) * WA_HD
GATE_COLS = N_BRANCH * D_MODEL
N_IN = HY_COLS + GA_COLS + ML_COLS + WA_COLS + GATE_COLS

kernel_name = "hybrid_parallel_gated_diffusion_block"


def rmsnorm(x, g):
    xf = x.astype(jnp.float32)
    y = xf * lax.rsqrt(jnp.mean(xf * xf, axis=-1, keepdims=True) + EPS)
    return (y * g.astype(jnp.float32)).astype(x.dtype)


def short_conv3(u, w, b):
    up = jnp.pad(u, ((0, 0), (1, 1), (0, 0)))
    return w[0] * up[:, :-2] + w[1] * up[:, 1:-1] + w[2] * up[:, 2:] + b


def axial_rope(x):
    L, d = x.shape[1], x.shape[-1]
    n_rows = L // GRID_W
    row = jnp.repeat(jnp.arange(n_rows, dtype=jnp.float32), GRID_W)
    col = jnp.tile(jnp.arange(GRID_W, dtype=jnp.float32), n_rows)
    quarter = d // 4
    inv = ROPE_BASE ** (-jnp.arange(quarter, dtype=jnp.float32) / quarter)

    def rotate(xa, pos):
        ang = pos[:, None] * inv[None, :]
        cos = jnp.cos(ang)[None, :, None, :]
        sin = jnp.sin(ang)[None, :, None, :]
        a, b = jnp.split(xa, 2, axis=-1)
        return jnp.concatenate([a * cos - b * sin, a * sin + b * cos], axis=-1)

    x_row, x_col = jnp.split(x.astype(jnp.float32), 2, axis=-1)
    return jnp.concatenate([rotate(x_row, row), rotate(x_col, col)], axis=-1).astype(x.dtype)


def scores(qg, k):
    return jnp.einsum('bqkgd,bskd->bkgqs', qg, k).astype(jnp.float32) * (qg.shape[-1] ** -0.5)


def mix_values(p, v):
    return jnp.einsum('bkgqs,bskd->bqkgd', p.astype(v.dtype), v)


def with_sink(s, sink):
    kv, g = s.shape[1], s.shape[2]
    col = jnp.broadcast_to(sink.astype(jnp.float32).reshape(1, kv, g, 1, 1), s.shape[:-1] + (1,))
    return jnp.concatenate([s, col], axis=-1)


def dense_attention(q, k, v, sink):
    B, L, Hq, d = q.shape
    kv = k.shape[2]
    qg = q.reshape(B, L, kv, Hq // kv, d)
    s = scores(qg, k)
    if sink is None:
        p = jax.nn.softmax(s, axis=-1)
    else:
        p = jax.nn.softmax(with_sink(s, sink), axis=-1)[..., :-1]
    return mix_values(p, v).reshape(B, L, Hq * d)


def global_block_attention(q, k_all, v_all):
    B, L, Hq, d = q.shape
    kv = k_all.shape[2]
    nb = L // Q_BLOCK
    qb = jnp.moveaxis(q.reshape(B, nb, Q_BLOCK, kv, Hq // kv, d), 1, 0)

    def one(qblk):
        p = jax.nn.softmax(scores(qblk, k_all), axis=-1)
        return mix_values(p, v_all)

    o = lax.map(one, qb)
    return jnp.moveaxis(o, 0, 1).reshape(B, L, Hq * d)


def window_block_attention(q, k, v, kc, vc, sink):
    B, L, Hq, d = q.shape
    kv = k.shape[2]
    nb = L // Q_BLOCK
    band = Q_BLOCK + 2 * WINDOW
    pad = ((0, 0), (WINDOW, WINDOW), (0, 0), (0, 0))
    kp = jnp.pad(k, pad)
    vp = jnp.pad(v, pad)
    qb = jnp.moveaxis(q.reshape(B, nb, Q_BLOCK, kv, Hq // kv, d), 1, 0)
    q_off = jnp.arange(Q_BLOCK)
    k_off = jnp.arange(band) - WINDOW

    def one(args):
        i, qblk = args
        start = i * Q_BLOCK
        kb = lax.dynamic_slice_in_dim(kp, start, band, axis=1)
        vb = lax.dynamic_slice_in_dim(vp, start, band, axis=1)
        k_pos = start + k_off
        q_pos = start + q_off
        valid = ((jnp.abs(q_pos[:, None] - k_pos[None, :]) <= WINDOW)
                 & (k_pos >= 0)[None, :] & (k_pos < L)[None, :])
        s_loc = jnp.where(valid, scores(qblk, kb), NEG_INF)
        s = with_sink(jnp.concatenate([s_loc, scores(qblk, kc)], axis=-1), sink)
        p = jax.nn.softmax(s, axis=-1)
        return mix_values(p[..., :band], vb) + mix_values(p[..., band:-1], vc)

    o = lax.map(one, (jnp.arange(nb), qb))
    return jnp.moveaxis(o, 0, 1).reshape(B, L, Hq * d)


def hyena_filter_spectrum(L, w1, b1, freq, w2, b2, w3, decay):
    f32 = jnp.float32
    t = jnp.arange(L, dtype=f32)
    tn = t / (L - 1)
    w = 2.0 * math.pi * t / L
    bands = jnp.linspace(1e-4, HY_BANDS - 1, HY_BANDS, dtype=f32)
    ang = w[:, None] * bands[None, :]
    z = jnp.concatenate([tn[:, None], jnp.cos(ang), -jnp.sin(ang)], axis=-1)
    fr = freq.astype(f32)
    h = jnp.sin(fr[0] * (z @ w1.astype(f32) + b1.astype(f32)))
    h = jnp.sin(fr[1] * (h @ w2.astype(f32) + b2.astype(f32)))
    h = (h @ w3.astype(f32)) * jnp.exp(-tn[:, None] * jnp.abs(decay.astype(f32)))
    h = h.reshape(L, HY_ORDER, 2, HY_CH)
    fwd, bwd = h[:, :, 0], h[:, :, 1]
    k = jnp.concatenate([fwd, jnp.zeros((1, HY_ORDER, HY_CH), f32), jnp.flip(bwd[1:], axis=0)], axis=0)
    k = k * lax.rsqrt(jnp.sum(k * k, axis=0, keepdims=True) + EPS)
    return jnp.fft.rfft(k, axis=0)


def hyena_branch(z, p):
    L = z.shape[1]
    u = short_conv3(z, p['hy_conv_w'], p['hy_conv_b']).astype(jnp.float32)
    v, x1, x2 = jnp.split(u, 3, axis=-1)
    spec = hyena_filter_spectrum(L, p['hy_pe_w1'], p['hy_pe_b1'], p['hy_freq'], p['hy_pe_w2'],
                                 p['hy_pe_b2'], p['hy_pe_w3'], p['hy_decay'])
    skip = p['hy_skip'].astype(jnp.float32)
    y = v
    for n, gate in enumerate((x1, x2)):
        yf = jnp.fft.irfft(jnp.fft.rfft(y, n=2 * L, axis=1) * spec[None, :, n], n=2 * L, axis=1)[:, :L]
        y = gate * (yf + skip[n] * y)
    return y.astype(z.dtype)


def global_attention_branch(zl, zc, p, with_ctx_out):
    def heads(z):
        B, L, _ = z.shape
        q, k, v = jnp.split(z, [GA_HEADS * GA_HD, (GA_HEADS + GA_KV) * GA_HD], axis=-1)
        q = rmsnorm(q.reshape(B, L, GA_HEADS, GA_HD), p['ga_q_g'])
        k = rmsnorm(k.reshape(B, L, GA_KV, GA_HD), p['ga_k_g'])
        return q, k, v.reshape(B, L, GA_KV, GA_HD)

    ql, kl, vl = heads(zl)
    qc, kc, vc = heads(zc)
    ql, kl = axial_rope(ql), axial_rope(kl)
    out_l = global_block_attention(ql, jnp.concatenate([kl, kc], axis=1), jnp.concatenate([vl, vc], axis=1))
    out_c = dense_attention(qc, kc, vc, None) if with_ctx_out else None
    return out_l, out_c


def window_attention_branch(zl, zc, p, with_ctx_out):
    def heads(z):
        B, L, _ = z.shape
        q, k, v = jnp.split(z, [WA_HEADS * WA_HD, (WA_HEADS + WA_KV) * WA_HD], axis=-1)
        return (q.reshape(B, L, WA_HEADS, WA_HD), k.reshape(B, L, WA_KV, WA_HD),
                v.reshape(B, L, WA_KV, WA_HD))

    ql, kl, vl = heads(zl)
    qc, kc, vc = heads(zc)
    ql, kl = axial_rope(ql), axial_rope(kl)
    out_l = window_block_attention(ql, kl, vl, kc, vc, p['wa_sink'])
    out_c = dense_attention(qc, kc, vc, p['wa_sink']) if with_ctx_out else None
    return out_l, out_c


def mlstm_chunkwise(q, k, v, i_pre, f_pre, state):
    B, L, H, d = q.shape
    nc = L // ML_CHUNK

    def to_chunks(a):
        a = a.reshape((B, nc, ML_CHUNK) + a.shape[2:])
        return jnp.moveaxis(jnp.moveaxis(a, 1, 0), 3, 2)

    causal = jnp.tril(jnp.ones((ML_CHUNK, ML_CHUNK), dtype=bool))

    def step(carry, inp):
        C, n, m = carry
        qc, kc, vc, ic, lfc = inp
        b = jnp.cumsum(lfc, axis=-1)
        log_d = jnp.where(causal, b[..., :, None] - b[..., None, :] + ic[..., None, :], -jnp.inf)
        m_inter = b + m[..., None]
        m_t = jnp.maximum(m_inter, jnp.max(log_d, axis=-1))
        w_qk = jnp.einsum('bhtd,bhsd->bhts', qc, kc) * jnp.exp(log_d - m_t[..., None])
        carry_scale = jnp.exp(m_inter - m_t)
        num = (jnp.einsum('bhts,bhse->bhte', w_qk, vc)
               + carry_scale[..., None] * jnp.einsum('bhed,bhtd->bhte', C, qc))
        den = jnp.sum(w_qk, axis=-1) + carry_scale * jnp.einsum('bhd,bhtd->bht', n, qc)
        h = num / jnp.maximum(jnp.abs(den), jnp.exp(-m_t))[..., None]
        b_end = b[..., -1]
        log_w = b_end[..., None] - b + ic
        m_next = jnp.maximum(b_end + m, jnp.max(log_w, axis=-1))
        w = jnp.exp(log_w - m_next[..., None])
        decay = jnp.exp(b_end + m - m_next)
        C = decay[..., None, None] * C + jnp.einsum('bhs,bhse,bhsd->bhed', w, vc, kc)
        n = decay[..., None] * n + jnp.einsum('bhs,bhsd->bhd', w, kc)
        return (C, n, m_next), h

    xs = (to_chunks(q), to_chunks(k), to_chunks(v), to_chunks(i_pre),
          to_chunks(jax.nn.log_sigmoid(f_pre)))
    state, h = lax.scan(step, state, xs)
    h = jnp.moveaxis(jnp.moveaxis(h, 2, 3), 0, 1).reshape(B, L, H, d)
    return h, state


def mlstm_streams(z, p):
    B, L, _ = z.shape
    w = ML_HEADS * ML_HD
    q, k, v, o, g = jnp.split(z, [w, 2 * w, 3 * w, 4 * w], axis=-1)
    qk = jax.nn.silu(short_conv3(jnp.concatenate([q, k], axis=-1), p['ml_conv_w'], p['ml_conv_b']))
    q, k = jnp.split(qk.astype(jnp.float32), 2, axis=-1)
    heads = lambda a: a.reshape(B, L, ML_HEADS, ML_HD)
    g = (g + p['ml_gate_b']).astype(jnp.float32).reshape(B, L, 2, 2, ML_HEADS)
    return heads(q), heads(k) * (ML_HD ** -0.5), heads(v.astype(jnp.float32)), o, g


def mlstm_branch(zl, zc, p, with_ctx_out):
    flip = lambda a: jnp.flip(a, axis=1)
    ql, kl, vl, ol, gl = mlstm_streams(zl, p)
    qc, kc, vc, oc, gc = mlstm_streams(zc, p)
    B = zl.shape[0]
    zero = (jnp.zeros((B, ML_HEADS, ML_HD, ML_HD), jnp.float32),
            jnp.zeros((B, ML_HEADS, ML_HD), jnp.float32),
            jnp.zeros((B, ML_HEADS), jnp.float32))
    h_cf, st_f = mlstm_chunkwise(qc, kc, vc, gc[:, :, 0, 0], gc[:, :, 0, 1], zero)
    h_cb, st_b = mlstm_chunkwise(flip(qc), flip(kc), flip(vc), flip(gc[:, :, 1, 0]), flip(gc[:, :, 1, 1]), zero)
    h_lf, _ = mlstm_chunkwise(ql, kl, vl, gl[:, :, 0, 0], gl[:, :, 0, 1], st_f)
    h_lb, _ = mlstm_chunkwise(flip(ql), flip(kl), flip(vl), flip(gl[:, :, 1, 0]), flip(gl[:, :, 1, 1]), st_b)
    g_norm = p['ml_norm_g'].reshape(ML_HEADS, ML_HD)

    def finish(h, o):
        Bh, Lh = o.shape[:2]
        h = rmsnorm(h, g_norm).reshape(Bh, Lh, ML_HEADS * ML_HD)
        return (jax.nn.sigmoid(o.astype(jnp.float32)) * h).astype(o.dtype)

    out_l = finish(h_lf + flip(h_lb), ol)
    out_c = finish(h_cf + flip(h_cb), oc) if with_ctx_out else None
    return out_l, out_c


def merge_branches(ys, gate_logits, w_up, w_out):
    B, L, _ = gate_logits.shape
    g = jax.nn.sigmoid(gate_logits.astype(jnp.float32)).reshape(B, L, N_BRANCH, D_MODEL)
    acc = g[:, :, 0] * (ys[0] @ w_up[0]).astype(jnp.float32)
    for n in range(1, N_BRANCH):
        acc = acc + g[:, :, n] * (ys[n] @ w_up[n]).astype(jnp.float32)
    return (acc.astype(gate_logits.dtype) @ w_out).astype(gate_logits.dtype)


def token_mixers(zl, zc, p, with_ctx_out):
    idx = np.cumsum([HY_COLS, GA_COLS, ML_COLS, WA_COLS]).tolist()
    hy_l, ga_l, ml_l, wa_l, gate_l = jnp.split(zl, idx, axis=-1)
    hy_c, ga_c, ml_c, wa_c, gate_c = jnp.split(zc, idx, axis=-1)
    ya_l = hyena_branch(hy_l, p)
    yb_l, yb_c = global_attention_branch(ga_l, ga_c, p, with_ctx_out)
    yc_l, yc_c = mlstm_branch(ml_l, ml_c, p, with_ctx_out)
    yd_l, yd_c = window_attention_branch(wa_l, wa_c, p, with_ctx_out)
    out_l = merge_branches([ya_l, yb_l, yc_l, yd_l], gate_l, p['w_up'], p['w_out'])
    if with_ctx_out:
        ya_c = hyena_branch(hy_c, p)
        out_c = merge_branches([ya_c, yb_c, yc_c, yd_c], gate_c, p['w_up'], p['w_out'])
    else:
        out_c = None
    return out_l, out_c


def sq_relu_mlp(h, w1, b1, w2, b2):
    a = jax.nn.relu(h @ w1 + b1)
    return (a * a) @ w2 + b2


def setup_inputs(seed: int = 0) -> dict:
    key = jax.random.key(seed)
    ks = iter(jax.random.split(key, 64))
    f32 = jnp.float32

    def nrm(shape, scale):
        return jax.random.normal(next(ks), shape, f32) * scale

    Dm = D_MODEL
    n_filt = HY_ORDER * 2 * HY_CH
    gate_base = jnp.stack([jnp.zeros((ML_HEADS,), f32), jnp.linspace(3.0, 6.0, ML_HEADS, dtype=f32)])
    return {
        'x': nrm((BATCH, SEQ, Dm), 1.0),
        'c': nrm((BATCH, Dm), 1.0),
        'ctx': nrm((BATCH, CTX_LEN, Dm), 1.0),
        'c_ctx': nrm((Dm,), 1.0),
        'w_mod': nrm((DEPTH, Dm, 6 * Dm), Dm ** -0.5),
        'b_mod': nrm((DEPTH, 6 * Dm), 0.02),
        'ln1_g': 1.0 + nrm((DEPTH, Dm), 0.02),
        'ln2_g': 1.0 + nrm((DEPTH, Dm), 0.02),
        'w_in': nrm((DEPTH, Dm, N_IN), Dm ** -0.5),
        'hy_conv_w': nrm((DEPTH, 3, HY_COLS), 3 ** -0.5),
        'hy_conv_b': nrm((DEPTH, HY_COLS), 0.02),
        'hy_pe_w1': nrm((DEPTH, HY_EMB, HY_FFN), HY_EMB ** -0.5),
        'hy_pe_b1': nrm((DEPTH, HY_FFN), 0.1),
        'hy_freq': 1.0 + nrm((DEPTH, 2, HY_FFN), 0.1),
        'hy_pe_w2': nrm((DEPTH, HY_FFN, HY_FFN), HY_FFN ** -0.5),
        'hy_pe_b2': nrm((DEPTH, HY_FFN), 0.1),
        'hy_pe_w3': nrm((DEPTH, HY_FFN, n_filt), HY_FFN ** -0.5),
        'hy_decay': jnp.linspace(HY_MIN_DECAY, HY_MAX_DECAY, n_filt, dtype=f32)[None, :] * (1.0 + nrm((DEPTH, n_filt), 0.05)),
        'hy_skip': nrm((DEPTH, HY_ORDER, HY_CH), 0.5),
        'ga_q_g': 1.0 + nrm((DEPTH, GA_HD), 0.02),
        'ga_k_g': 1.0 + nrm((DEPTH, GA_HD), 0.02),
        'ml_conv_w': nrm((DEPTH, 3, 2 * ML_HEADS * ML_HD), 3 ** -0.5),
        'ml_conv_b': nrm((DEPTH, 2 * ML_HEADS * ML_HD), 0.02),
        'ml_gate_b': (jnp.broadcast_to(gate_base, (DEPTH, 2, 2, ML_HEADS)) + nrm((DEPTH, 2, 2, ML_HEADS), 0.1)).reshape(DEPTH, ML_GATES),
        'ml_norm_g': 1.0 + nrm((DEPTH, ML_HEADS * ML_HD), 0.02),
        'wa_sink': nrm((DEPTH, WA_HEADS), 0.5),
        'w_up': nrm((DEPTH, N_BRANCH, BRANCH_W, Dm), BRANCH_W ** -0.5),
        'w_out': nrm((DEPTH, Dm, Dm), Dm ** -0.5),
        'mlp_w1': nrm((DEPTH, Dm, D_FF), Dm ** -0.5),
        'mlp_b1': nrm((DEPTH, D_FF), 0.02),
        'mlp_w2': nrm((DEPTH, D_FF, Dm), D_FF ** -0.5),
        'mlp_b2': nrm((DEPTH, Dm), 0.02),
        'final_g': 1.0 + nrm((Dm,), 0.02),
    }


def reference(x, c, ctx, c_ctx, w_mod, b_mod, ln1_g, ln2_g, w_in, hy_conv_w, hy_conv_b,
              hy_pe_w1, hy_pe_b1, hy_freq, hy_pe_w2, hy_pe_b2, hy_pe_w3, hy_decay, hy_skip,
              ga_q_g, ga_k_g, ml_conv_w, ml_conv_b, ml_gate_b, ml_norm_g, wa_sink, w_up, w_out,
              mlp_w1, mlp_b1, mlp_w2, mlp_b2, final_g):
    xc = ctx
    silu_c = jax.nn.silu(c)
    silu_cc = jax.nn.silu(c_ctx)
    for l in range(DEPTH):
        with_ctx_out = l < DEPTH - 1
        p = dict(hy_conv_w=hy_conv_w[l], hy_conv_b=hy_conv_b[l], hy_pe_w1=hy_pe_w1[l],
                 hy_pe_b1=hy_pe_b1[l], hy_freq=hy_freq[l], hy_pe_w2=hy_pe_w2[l], hy_pe_b2=hy_pe_b2[l],
                 hy_pe_w3=hy_pe_w3[l], hy_decay=hy_decay[l], hy_skip=hy_skip[l],
                 ga_q_g=ga_q_g[l], ga_k_g=ga_k_g[l], ml_conv_w=ml_conv_w[l], ml_conv_b=ml_conv_b[l],
                 ml_gate_b=ml_gate_b[l], ml_norm_g=ml_norm_g[l], wa_sink=wa_sink[l],
                 w_up=w_up[l], w_out=w_out[l])
        mod_l = (silu_c @ w_mod[l] + b_mod[l])[:, None, :]
        mod_c = silu_cc @ w_mod[l] + b_mod[l]
        sh1, sc1, g1, sh2, sc2, g2 = jnp.split(mod_l, 6, axis=-1)
        csh1, csc1, cg1, csh2, csc2, cg2 = jnp.split(mod_c, 6, axis=-1)
        hl = rmsnorm(x, ln1_g[l]) * (1.0 + sc1) + sh1
        hc = rmsnorm(xc, ln1_g[l]) * (1.0 + csc1) + csh1
        yl, yc = token_mixers(hl @ w_in[l], hc @ w_in[l], p, with_ctx_out)
        x = x + g1 * yl
        h2 = rmsnorm(x, ln2_g[l]) * (1.0 + sc2) + sh2
        x = x + g2 * sq_relu_mlp(h2, mlp_w1[l], mlp_b1[l], mlp_w2[l], mlp_b2[l])
        if with_ctx_out:
            xc = xc + cg1 * yc
            hc2 = rmsnorm(xc, ln2_g[l]) * (1.0 + csc2) + csh2
            xc = xc + cg2 * sq_relu_mlp(hc2, mlp_w1[l], mlp_b1[l], mlp_w2[l], mlp_b2[l])
    return rmsnorm(x, final_g)
```

```python
import jax
import jax.numpy as jnp
from jax import lax
from jax.experimental import pallas as pl

import reference as _r


def _final_norm_kernel(x_ref, g_ref, o_ref):
    xf = x_ref[...]
    y = xf * lax.rsqrt(jnp.mean(xf * xf, axis=-1, keepdims=True) + _r.EPS)
    o_ref[...] = y * g_ref[...]


def _final_norm(x, g):
    B, L, D = x.shape
    x2 = x.reshape(B * L, D)
    tm = 1024
    out = pl.pallas_call(
        _final_norm_kernel,
        grid=(B * L // tm,),
        in_specs=[pl.BlockSpec((tm, D), lambda i: (i, 0)), pl.BlockSpec((1, D), lambda i: (0, 0))],
        out_specs=pl.BlockSpec((tm, D), lambda i: (i, 0)),
        out_shape=jax.ShapeDtypeStruct((B * L, D), x.dtype),
    )(x2, g.reshape(1, D))
    return out.reshape(B, L, D)


def kernel(x, c, ctx, c_ctx, w_mod, b_mod, ln1_g, ln2_g, w_in, hy_conv_w, hy_conv_b,
           hy_pe_w1, hy_pe_b1, hy_freq, hy_pe_w2, hy_pe_b2, hy_pe_w3, hy_decay, hy_skip,
           ga_q_g, ga_k_g, ml_conv_w, ml_conv_b, ml_gate_b, ml_norm_g, wa_sink, w_up, w_out,
           mlp_w1, mlp_b1, mlp_w2, mlp_b2, final_g):
    ones = jnp.ones_like(final_g)
    xn = _r.reference(x, c, ctx, c_ctx, w_mod, b_mod, ln1_g, ln2_g, w_in, hy_conv_w, hy_conv_b,
                      hy_pe_w1, hy_pe_b1, hy_freq, hy_pe_w2, hy_pe_b2, hy_pe_w3, hy_decay, hy_skip,
                      ga_q_g, ga_k_g, ml_conv_w, ml_conv_b, ml_gate_b, ml_norm_g, wa_sink, w_up, w_out,
                      mlp_w1, mlp_b1, mlp_w2, mlp_b2, ones)
    return _final_norm(xn, final_g)
```

```python
import functools
import math

import jax
import jax.numpy as jnp
import numpy as np
from jax import lax
from jax.experimental import pallas as pl
from jax.experimental.pallas import tpu as pltpu

F32 = jnp.float32
BF16 = jnp.bfloat16
HIGHEST = lax.Precision.HIGHEST

D_MODEL = 1024
DEPTH = 2
GRID_W = 64
HY_CH = 512
HY_ORDER = 2
HY_BANDS = 16
HY_EMB = 1 + 2 * HY_BANDS
HY_FFN = 64
GA_HEADS, GA_KV, GA_HD = 4, 2, 128
ML_HEADS, ML_HD = 4, 128
WA_HEADS, WA_KV, WA_HD = 8, 2, 64
WINDOW = 128
ROPE_BASE = 10000.0
D_FF = 4 * D_MODEL
EPS = 1e-6
NEG = -1e30

LANES = 128
V7X_VMEM_LIMIT = 48 * 1024 * 1024

Z_HY = 0
Z_WAQ = 1536
Z_GA = 2048
Z_MLQK = 3072
Z_MLV = 4096
Z_MLO = 4608
Z_GATE = 5120
Z_WAKV = 9216
Z_COLS = 9728
Z_TN = 2432

ML_CHUNK = 128
FFT_N2 = 128


def _cp(sem, vmem=V7X_VMEM_LIMIT):
    return pltpu.CompilerParams(dimension_semantics=sem, vmem_limit_bytes=vmem)


def _rms(x, g):
    return x * lax.rsqrt(jnp.mean(x * x, axis=-1, keepdims=True) + EPS) * g


def _mod_kernel(c_ref, w_ref, b_ref, o_ref):
    c = c_ref[...]
    s = c * jax.nn.sigmoid(c)
    o_ref[0] = jnp.dot(s, w_ref[0], preferred_element_type=F32, precision=HIGHEST) + b_ref[0]


def _modulation(cvec, w_mod, b_mod):
    R = cvec.shape[0]
    tn = 1536
    return pl.pallas_call(
        _mod_kernel,
        grid=(DEPTH, 6 * D_MODEL // tn),
        in_specs=[pl.BlockSpec((R, D_MODEL), lambda l, j: (0, 0)),
                  pl.BlockSpec((1, D_MODEL, tn), lambda l, j: (l, 0, j)),
                  pl.BlockSpec((1, 1, tn), lambda l, j: (l, 0, j))],
        out_specs=pl.BlockSpec((1, R, tn), lambda l, j: (l, 0, j)),
        out_shape=jax.ShapeDtypeStruct((DEPTH, R, 6 * D_MODEL), F32),
        compiler_params=_cp(("parallel", "parallel")),
        name="modulation",
    )(cvec, w_mod, b_mod.reshape(DEPTH, 1, 6 * D_MODEL))


def _nmm_kernel(x_ref, g_ref, sc_ref, sh_ref, w_ref, o_ref, h_ref):
    @pl.when(pl.program_id(1) == 0)
    def _():
        y = _rms(x_ref[...], g_ref[...])
        h_ref[...] = (y * (1.0 + sc_ref[0]) + sh_ref[0]).astype(BF16)

    o_ref[...] = jnp.dot(h_ref[...], w_ref[...], preferred_element_type=F32).astype(o_ref.dtype)


def _normmod_matmul(x, g, sc, sh, w, seq_len, tn, out_dtype):
    T, D = x.shape
    N = w.shape[1]
    tm = min(1024, seq_len)
    per = seq_len // tm
    return pl.pallas_call(
        _nmm_kernel,
        grid=(T // tm, N // tn),
        in_specs=[pl.BlockSpec((tm, D), lambda i, j: (i, 0)),
                  pl.BlockSpec((1, D), lambda i, j: (0, 0)),
                  pl.BlockSpec((1, 1, D), lambda i, j: (i // per, 0, 0)),
                  pl.BlockSpec((1, 1, D), lambda i, j: (i // per, 0, 0)),
                  pl.BlockSpec((D, tn), lambda i, j: (0, j))],
        out_specs=pl.BlockSpec((tm, tn), lambda i, j: (i, j)),
        out_shape=jax.ShapeDtypeStruct((T, N), out_dtype),
        scratch_shapes=[pltpu.VMEM((tm, D), BF16)],
        compiler_params=_cp(("parallel", "arbitrary")),
        name="normmod_matmul",
    )(x, g.reshape(1, D), sc, sh, w)


def _conv3(u, prev_row, next_row, w_ref, b_ref, c0, c1):
    tm = u.shape[0]
    row = lax.broadcasted_iota(jnp.int32, u.shape, 0)
    up = jnp.where(row == 0, prev_row, pltpu.roll(u, 1, 0))
    dn = jnp.where(row == tm - 1, next_row, pltpu.roll(u, tm - 1, 0))
    return (w_ref[0:1, c0:c1] * up + w_ref[1:2, c0:c1] * u + w_ref[2:3, c0:c1] * dn + b_ref[0:1, c0:c1])


def _halo_rows(zp_ref, zn_ref, per, c0, c1):
    i = pl.program_id(0)
    first = (i % per) == 0
    last = (i % per) == per - 1
    hp = zp_ref.shape[0]
    prev_row = jnp.where(first, 0.0, zp_ref[hp - 1:hp, c0:c1].astype(F32))
    next_row = jnp.where(last, 0.0, zn_ref[0:1, c0:c1].astype(F32))
    return prev_row, next_row


HALO = 16


def _halo_specs(tm, width, col_block, n_rows):
    nb = n_rows // HALO
    r = tm // HALO
    return [pl.BlockSpec((tm, width), lambda i: (i, col_block)),
            pl.BlockSpec((HALO, width), lambda i: (jnp.maximum(i * r - 1, 0), col_block)),
            pl.BlockSpec((HALO, width), lambda i: (jnp.minimum((i + 1) * r, nb - 1), col_block))]


def _hy_prep_kernel(z_ref, zp_ref, zn_ref, w_ref, b_ref, v_ref, x1_ref, x2_ref, *, per):
    outs = (v_ref, x1_ref, x2_ref)
    for c in range(3):
        c0, c1 = c * HY_CH, (c + 1) * HY_CH
        prev_row, next_row = _halo_rows(zp_ref, zn_ref, per, c0, c1)
        u = z_ref[:, c0:c1].astype(F32)
        outs[c][...] = _conv3(u, prev_row, next_row, w_ref, b_ref, c0, c1).astype(outs[c].dtype)


def _hy_prep(z, conv_w, conv_b, seq_len):
    T = z.shape[0]
    tm = min(512, seq_len)
    W = 3 * HY_CH
    out = jax.ShapeDtypeStruct((T, HY_CH), BF16)
    return pl.pallas_call(
        functools.partial(_hy_prep_kernel, per=seq_len // tm),
        grid=(T // tm,),
        in_specs=_halo_specs(tm, W, Z_HY // W, T) + [
            pl.BlockSpec((3, W), lambda i: (0, 0)), pl.BlockSpec((1, W), lambda i: (0, 0))],
        out_specs=[pl.BlockSpec((tm, HY_CH), lambda i: (i, 0))] * 3,
        out_shape=[out, out, out],
        compiler_params=_cp(("parallel",)),
        name="hyena_prep",
    )(z, z, z, conv_w, conv_b.reshape(1, W))


def _ml_prep_kernel(z_ref, zp_ref, zn_ref, w_ref, b_ref, q_ref, k_ref, *, per):
    W = ML_HEADS * ML_HD
    for c, o_ref, scale in ((0, q_ref, 1.0), (1, k_ref, ML_HD ** -0.5)):
        c0, c1 = c * W, (c + 1) * W
        prev_row, next_row = _halo_rows(zp_ref, zn_ref, per, c0, c1)
        u = z_ref[:, c0:c1].astype(F32)
        y = _conv3(u, prev_row, next_row, w_ref, b_ref, c0, c1)
        y = y * jax.nn.sigmoid(y)
        o_ref[...] = (y * scale).astype(o_ref.dtype)


def _ml_prep(z, conv_w, conv_b, seq_len):
    T = z.shape[0]
    tm = min(512, seq_len)
    W = 2 * ML_HEADS * ML_HD
    out = jax.ShapeDtypeStruct((T, W // 2), BF16)
    return pl.pallas_call(
        functools.partial(_ml_prep_kernel, per=seq_len // tm),
        grid=(T // tm,),
        in_specs=_halo_specs(tm, W, Z_MLQK // W, T) + [
            pl.BlockSpec((3, W), lambda i: (0, 0)), pl.BlockSpec((1, W), lambda i: (0, 0))],
        out_specs=[pl.BlockSpec((tm, W // 2), lambda i: (i, 0))] * 2,
        out_shape=[out, out],
        compiler_params=_cp(("parallel",)),
        name="mlstm_prep",
    )(z, z, z, conv_w, conv_b.reshape(1, W))


def _rope_tables(L, hd):
    quarter = hd // 4
    inv = ROPE_BASE ** (-jnp.arange(quarter, dtype=F32) / quarter)
    t = jnp.arange(L)
    row = (t // GRID_W).astype(F32)
    col = (t % GRID_W).astype(F32)
    lane = jnp.arange(LANES)
    within = lane % hd
    is_col = (within // (hd // 2)) == 1
    second = ((within % (hd // 2)) // quarter) == 1
    j = within % quarter
    pos = jnp.where(is_col[None, :], col[:, None], row[:, None])
    ang = pos * inv[j][None, :]
    return jnp.cos(ang), jnp.where(second[None, :], jnp.sin(ang), -jnp.sin(ang))


def _rope(x, cos, sin, quarter):
    lane = lax.broadcasted_iota(jnp.int32, x.shape, 1)
    first = ((lane % (2 * quarter)) // quarter) == 0
    partner = jnp.where(first, pltpu.roll(x, LANES - quarter, 1), pltpu.roll(x, quarter, 1))
    return x * cos + partner * sin


def _ga_prep_kernel(z_ref, cos_ref, sin_ref, qg_ref, kg_ref, q_ref, k_ref, v_ref, *, rope):
    nq, nk = GA_HEADS, GA_KV
    for h in range(nq + nk):
        x = z_ref[:, h * GA_HD:(h + 1) * GA_HD].astype(F32)
        g = qg_ref[...] if h < nq else kg_ref[...]
        y = _rms(x, g)
        if rope:
            y = _rope(y, cos_ref[...], sin_ref[...], GA_HD // 4)
        if h < nq:
            q_ref[:, h * GA_HD:(h + 1) * GA_HD] = (y * (GA_HD ** -0.5)).astype(q_ref.dtype)
        else:
            k_ref[:, (h - nq) * GA_HD:(h - nq + 1) * GA_HD] = y.astype(k_ref.dtype)
    v_ref[...] = z_ref[:, (nq + nk) * GA_HD:]


def _ga_prep(z, cos, sin, qg, kg, seq_len, rope):
    T = z.shape[0]
    tm = min(512, seq_len)
    per = seq_len // tm
    W = (GA_HEADS + 2 * GA_KV) * GA_HD
    return pl.pallas_call(
        functools.partial(_ga_prep_kernel, rope=rope),
        grid=(T // tm,),
        in_specs=[pl.BlockSpec((tm, W), lambda i: (i, Z_GA // W)),
                  pl.BlockSpec((tm, LANES), lambda i: (i % per, 0)),
                  pl.BlockSpec((tm, LANES), lambda i: (i % per, 0)),
                  pl.BlockSpec((1, GA_HD), lambda i: (0, 0)),
                  pl.BlockSpec((1, GA_HD), lambda i: (0, 0))],
        out_specs=[pl.BlockSpec((tm, GA_HEADS * GA_HD), lambda i: (i, 0)),
                   pl.BlockSpec((tm, GA_KV * GA_HD), lambda i: (i, 0)),
                   pl.BlockSpec((tm, GA_KV * GA_HD), lambda i: (i, 0))],
        out_shape=[jax.ShapeDtypeStruct((T, GA_HEADS * GA_HD), BF16),
                   jax.ShapeDtypeStruct((T, GA_KV * GA_HD), BF16),
                   jax.ShapeDtypeStruct((T, GA_KV * GA_HD), BF16)],
        compiler_params=_cp(("parallel",)),
        name="global_attn_prep",
    )(z, cos, sin, qg.reshape(1, GA_HD), kg.reshape(1, GA_HD))


def _dup_halves(x):
    lane = lax.broadcasted_iota(jnp.int32, x.shape, 1)
    sw = pltpu.roll(x, 64, 1)
    lo = lane < 64
    return jnp.where(lo, x, sw), jnp.where(lo, sw, x)


def _wa_prep_kernel(zq_ref, zkv_ref, cos_ref, sin_ref, q_ref, k_ref, v_ref, *, rope):
    quarter = WA_HD // 4
    for j in range(WA_HEADS * WA_HD // LANES):
        x = zq_ref[:, j * LANES:(j + 1) * LANES].astype(F32)
        if rope:
            x = _rope(x, cos_ref[...], sin_ref[...], quarter)
        q_ref[:, j * LANES:(j + 1) * LANES] = (x * (WA_HD ** -0.5)).astype(q_ref.dtype)
    k = zkv_ref[:, 0:LANES].astype(F32)
    if rope:
        k = _rope(k, cos_ref[...], sin_ref[...], quarter)
    k0, k1 = _dup_halves(k)
    k_ref[:, 0:LANES] = k0.astype(k_ref.dtype)
    k_ref[:, LANES:] = k1.astype(k_ref.dtype)
    v0, v1 = _dup_halves(zkv_ref[:, LANES:].astype(F32))
    v_ref[:, 0:LANES] = v0.astype(v_ref.dtype)
    v_ref[:, LANES:] = v1.astype(v_ref.dtype)


def _wa_prep(z, cos, sin, seq_len, rope):
    T = z.shape[0]
    tm = min(512, seq_len)
    per = seq_len // tm
    WQ = WA_HEADS * WA_HD
    return pl.pallas_call(
        functools.partial(_wa_prep_kernel, rope=rope),
        grid=(T // tm,),
        in_specs=[pl.BlockSpec((tm, WQ), lambda i: (i, Z_WAQ // WQ)),
                  pl.BlockSpec((tm, 2 * LANES), lambda i: (i, Z_WAKV // (2 * LANES))),
                  pl.BlockSpec((tm, LANES), lambda i: (i % per, 0)),
                  pl.BlockSpec((tm, LANES), lambda i: (i % per, 0))],
        out_specs=[pl.BlockSpec((tm, WQ), lambda i: (i, 0)),
                   pl.BlockSpec((tm, 2 * LANES), lambda i: (i, 0)),
                   pl.BlockSpec((tm, 2 * LANES), lambda i: (i, 0))],
        out_shape=[jax.ShapeDtypeStruct((T, WQ), BF16),
                   jax.ShapeDtypeStruct((T, 2 * LANES), BF16),
                   jax.ShapeDtypeStruct((T, 2 * LANES), BF16)],
        compiler_params=_cp(("parallel",)),
        name="window_attn_prep",
    )(z, z, cos, sin)


def _ga_kernel(*refs, tq, tk, n_lat):
    if n_lat:
        q_ref, kl_ref, vl_ref, kc_ref, vc_ref, o_ref = refs
    else:
        q_ref, kc_ref, vc_ref, o_ref = refs
    q = jnp.concatenate([q_ref[0, :, 0:GA_HD], q_ref[0, :, GA_HD:]], axis=0)

    def step(carry, k, v):
        m, l, acc = carry
        s = lax.dot_general(q, k, (((1,), (1,)), ((), ())), preferred_element_type=F32)
        m_new = jnp.maximum(m, jnp.max(s, axis=-1, keepdims=True))
        p = jnp.exp(s - m_new)
        alpha = jnp.exp(m - m_new)
        l = alpha * l + jnp.sum(p, axis=-1, keepdims=True)
        acc = alpha * acc + jnp.dot(p.astype(BF16), v, preferred_element_type=F32)
        return m_new, l, acc

    carry = (jnp.full((2 * tq, 1), NEG, F32), jnp.zeros((2 * tq, 1), F32), jnp.zeros((2 * tq, GA_HD), F32))
    if n_lat:
        def body(j, c):
            off = pl.multiple_of(j * tk, tk)
            return step(c, kl_ref[0, pl.ds(off, tk), :], vl_ref[0, pl.ds(off, tk), :])
        carry = lax.fori_loop(0, n_lat, body, carry)
    m, l, acc = step(carry, kc_ref[0], vc_ref[0])
    o = acc / l
    o_ref[0, :, 0:GA_HD] = o[:tq].astype(o_ref.dtype)
    o_ref[0, :, GA_HD:] = o[tq:].astype(o_ref.dtype)


def _global_attention(q, kc, vc, kl=None, vl=None):
    B, Lq, _ = q.shape
    Lc = kc.shape[1]
    tq = min(256, Lq)
    tk = 512
    n_lat = 0 if kl is None else kl.shape[1] // tk
    W = 2 * GA_HD
    qspec = pl.BlockSpec((1, tq, W), lambda b, g, i: (b, i, g))
    cspec = pl.BlockSpec((1, Lc, GA_HD), lambda b, g, i: (b, 0, g))
    if n_lat:
        Ll = kl.shape[1]
        lspec = pl.BlockSpec((1, Ll, GA_HD), lambda b, g, i: (b, 0, g))
        in_specs, args = [qspec, lspec, lspec, cspec, cspec], (q, kl, vl, kc, vc)
    else:
        in_specs, args = [qspec, cspec, cspec], (q, kc, vc)
    return pl.pallas_call(
        functools.partial(_ga_kernel, tq=tq, tk=tk, n_lat=n_lat),
        grid=(B, GA_KV, Lq // tq),
        in_specs=in_specs,
        out_specs=qspec,
        out_shape=jax.ShapeDtypeStruct(q.shape, BF16),
        compiler_params=_cp(("parallel", "parallel", "parallel")),
        name="global_attention",
    )(*args)


def _wa_kernel(*refs, band, nq):
    if band:
        q_ref, kp_ref, kx_ref, kn_ref, vp_ref, vx_ref, vn_ref, kc_ref, vc_ref, sink_ref, o_ref = refs
    else:
        q_ref, kc_ref, vc_ref, sink_ref, o_ref = refs
    tq = q_ref.shape[1]
    Lc = kc_ref.shape[1]
    qi = pl.program_id(1)
    rows = 4 * tq
    if band:
        nb = 3 * tq
        r = lax.broadcasted_iota(jnp.int32, (rows, nb + Lc), 0) % tq
        c = lax.broadcasted_iota(jnp.int32, (rows, nb + Lc), 1)
        lo = jnp.where(qi == 0, tq, 0)
        hi = jnp.where(qi == nq - 1, 2 * tq, nb)
        valid = ((c >= r) & (c <= r + 2 * WINDOW) & (c >= lo) & (c < hi)) | (c >= nb)
    lane = lax.broadcasted_iota(jnp.int32, (tq, LANES), 1)
    lo_half = lane < 64
    for g in range(WA_KV):
        gs = slice(g * LANES, (g + 1) * LANES)
        if band:
            k = jnp.concatenate([kp_ref[0, :, gs], kx_ref[0, :, gs], kn_ref[0, :, gs], kc_ref[0, :, gs]], axis=0)
            v = jnp.concatenate([vp_ref[0, :, gs], vx_ref[0, :, gs], vn_ref[0, :, gs], vc_ref[0, :, gs]], axis=0)
        else:
            k, v = kc_ref[0, :, gs], vc_ref[0, :, gs]
        parts = []
        for j in range(2):
            qb = q_ref[0, :, (2 * g + j) * LANES:(2 * g + j + 1) * LANES]
            parts.append(jnp.where(lo_half, qb, jnp.zeros_like(qb)))
            parts.append(jnp.where(lo_half, jnp.zeros_like(qb), qb))
        qs = jnp.concatenate(parts, axis=0)
        s = lax.dot_general(qs, k, (((1,), (1,)), ((), ())), preferred_element_type=F32)
        if band:
            s = jnp.where(valid, s, NEG)
        sink = sink_ref[g]
        m = jnp.maximum(jnp.max(s, axis=-1, keepdims=True), sink)
        p = jnp.exp(s - m)
        den = jnp.sum(p, axis=-1, keepdims=True) + jnp.exp(sink - m)
        o = jnp.dot(p.astype(BF16), v, preferred_element_type=F32) / den
        for j in range(2):
            ob = jnp.where(lo_half, o[(2 * j) * tq:(2 * j + 1) * tq], o[(2 * j + 1) * tq:(2 * j + 2) * tq])
            o_ref[0, :, (2 * g + j) * LANES:(2 * g + j + 1) * LANES] = ob.astype(o_ref.dtype)


def _window_attention(q, kc, vc, sink, kl=None, vl=None):
    B, Lq, WQ = q.shape
    Lc = kc.shape[1]
    tq = WINDOW
    nq = Lq // tq
    band = kl is not None
    sink_col = jnp.repeat(sink.astype(F32).reshape(WA_KV, WA_HEADS // WA_KV), tq, axis=1).reshape(WA_KV, 4 * tq, 1)
    qspec = pl.BlockSpec((1, tq, WQ), lambda b, i: (b, i, 0))
    cspec = pl.BlockSpec((1, Lc, 2 * LANES), lambda b, i: (b, 0, 0))
    sspec = pl.BlockSpec((WA_KV, 4 * tq, 1), lambda b, i: (0, 0, 0))
    if band:
        bp = pl.BlockSpec((1, tq, 2 * LANES), lambda b, i: (b, jnp.maximum(i - 1, 0), 0))
        bx = pl.BlockSpec((1, tq, 2 * LANES), lambda b, i: (b, i, 0))
        bn = pl.BlockSpec((1, tq, 2 * LANES), lambda b, i: (b, jnp.minimum(i + 1, nq - 1), 0))
        in_specs = [qspec, bp, bx, bn, bp, bx, bn, cspec, cspec, sspec]
        args = (q, kl, kl, kl, vl, vl, vl, kc, vc, sink_col)
    else:
        in_specs, args = [qspec, cspec, cspec, sspec], (q, kc, vc, sink_col)
    return pl.pallas_call(
        functools.partial(_wa_kernel, band=band, nq=nq),
        grid=(B, nq),
        in_specs=in_specs,
        out_specs=qspec,
        out_shape=jax.ShapeDtypeStruct(q.shape, BF16),
        compiler_params=_cp(("parallel", "parallel")),
        name="window_attention",
    )(*args)


def _mlstm_kernel(*refs, reverse, nc):
    if reverse:
        (q_ref, k_ref, v_ref, g_ref, gb_ref, c0_ref, m0_ref, hf_ref, o_ref, gn_ref,
         y_ref, cf_ref, mf_ref, c_scr, m_scr) = refs
    else:
        (q_ref, k_ref, v_ref, g_ref, gb_ref, c0_ref, m0_ref,
         y_ref, cf_ref, mf_ref, c_scr, m_scr) = refs
    T = ML_CHUNK
    d = 1 if reverse else 0
    step = pl.program_id(1)

    @pl.when(step == 0)
    def _():
        c_scr[...] = c0_ref[0]
        m_scr[...] = m0_ref[0]

    G = g_ref[0] + gb_ref[...]
    LF = jax.nn.log_sigmoid(G)
    tt = lax.broadcasted_iota(jnp.int32, (T, T), 0)
    ss = lax.broadcasted_iota(jnp.int32, (T, T), 1)
    mask = (ss >= tt) if reverse else (ss <= tt)
    tri = mask.astype(F32)
    Bc = jnp.dot(tri, LF, preferred_element_type=F32, precision=HIGHEST)
    Dr = (Bc - pltpu.roll(G, 4, 1)).T
    ones = jnp.ones((T, ML_HD), BF16)
    for h in range(ML_HEADS):
        fl, il = d * 8 + 4 + h, d * 8 + h
        hs = slice(h * ML_HD, (h + 1) * ML_HD)
        b_col = Bc[:, fl:fl + 1]
        i_col = G[:, il:il + 1]
        log_d = jnp.where(mask, b_col - Dr[fl:fl + 1, :], NEG)
        m_prev = m_scr[h, 0:1, 0:1]
        m_inter = b_col + m_prev
        m_t = jnp.maximum(m_inter, jnp.max(log_d, axis=-1, keepdims=True))
        qh, kh = q_ref[0, :, hs], k_ref[0, :, hs]
        s = lax.dot_general(qh, kh, (((1,), (1,)), ((), ())), preferred_element_type=F32)
        wqk = (s * jnp.exp(log_d - m_t)).astype(BF16)
        cs = jnp.exp(m_inter - m_t)
        vaug = jnp.concatenate([v_ref[0, :, hs], ones], axis=1)
        R = (jnp.dot(wqk, vaug, preferred_element_type=F32)
             + cs * jnp.dot(qh, c_scr[h].astype(BF16), preferred_element_type=F32))
        hh = R[:, :ML_HD] / jnp.maximum(jnp.abs(R[:, ML_HD:]), jnp.exp(-m_t))
        if reverse:
            hsum = hf_ref[0, :, hs] + hh
            hn = _rms(hsum, gn_ref[0:1, hs])
            y_ref[0, :, hs] = (jax.nn.sigmoid(o_ref[0, :, hs].astype(F32)) * hn).astype(y_ref.dtype)
        else:
            y_ref[0, :, hs] = hh
        b_end = Bc[0:1, fl:fl + 1] if reverse else Bc[T - 1:T, fl:fl + 1]
        log_w = b_end - b_col + i_col
        m_next = jnp.maximum(b_end + m_prev, jnp.max(log_w, axis=0, keepdims=True))
        w_col = jnp.exp(log_w - m_next)
        decay = jnp.exp(b_end + m_prev - m_next)
        wv = (vaug.astype(F32) * w_col).astype(BF16)
        c_scr[h] = decay * c_scr[h] + lax.dot_general(kh, wv, (((0,), (0,)), ((), ())),
                                                      preferred_element_type=F32)
        m_scr[h] = jnp.broadcast_to(m_next, (8, LANES))

    @pl.when(step == nc - 1)
    def _():
        cf_ref[0] = c_scr[...]
        mf_ref[0] = m_scr[...]


def _mlstm_scan(q, k, z, gates, gate_b, c0, m0, reverse, hf=None, norm_g=None):
    B, L, W = q.shape
    T = ML_CHUNK
    nc = L // T
    idx = (lambda b, j: (b, nc - 1 - j, 0)) if reverse else (lambda b, j: (b, j, 0))
    tok = pl.BlockSpec((1, T, W), idx)
    zv = pl.BlockSpec((1, T, W), (lambda b, j: (b, nc - 1 - j, Z_MLV // W)) if reverse
                      else (lambda b, j: (b, j, Z_MLV // W)))
    cspec = pl.BlockSpec((1, ML_HEADS, ML_HD, 2 * ML_HD), lambda b, j: (b, 0, 0, 0))
    mspec = pl.BlockSpec((1, ML_HEADS, 8, LANES), lambda b, j: (b, 0, 0, 0))
    in_specs = [tok, tok, zv, pl.BlockSpec((1, T, LANES), idx), pl.BlockSpec((1, LANES), lambda b, j: (0, 0)),
                cspec, mspec]
    args = [q, k, z, gates, gate_b, c0, m0]
    if reverse:
        in_specs += [tok, pl.BlockSpec((1, T, W), lambda b, j: (b, nc - 1 - j, Z_MLO // W)),
                     pl.BlockSpec((1, W), lambda b, j: (0, 0))]
        args += [hf, z, norm_g.reshape(1, W)]
    return pl.pallas_call(
        functools.partial(_mlstm_kernel, reverse=reverse, nc=nc),
        grid=(B, nc),
        in_specs=in_specs,
        out_specs=[tok, cspec, mspec],
        out_shape=[jax.ShapeDtypeStruct((B, L, W), BF16 if reverse else F32),
                   jax.ShapeDtypeStruct(c0.shape, F32), jax.ShapeDtypeStruct(m0.shape, F32)],
        scratch_shapes=[pltpu.VMEM((ML_HEADS, ML_HD, 2 * ML_HD), F32), pltpu.VMEM((ML_HEADS, 8, LANES), F32)],
        compiler_params=_cp(("parallel", "arbitrary")),
        name="mlstm_reverse" if reverse else "mlstm_forward",
    )(*args)


def _fft_dims(Lp):
    n1 = 2 * Lp // FFT_N2
    nt1 = Lp // FFT_N2
    nk1 = -(-(n1 // 2 + 1) // 8) * 8
    return n1, nt1, nk1


def _fft_tables(Lp):
    n1, nt1, nk1 = _fft_dims(Lp)
    N = 2 * Lp
    k1 = jnp.arange(nk1)
    t1 = jnp.arange(nt1)
    ang_a = (2.0 * math.pi / n1) * ((k1[:, None] * t1[None, :]) % n1).astype(F32)
    fa = jnp.stack([jnp.cos(ang_a), -jnp.sin(ang_a)], axis=1).reshape(2 * nk1, nt1)
    wgt = jnp.where((k1 == 0) | (k1 == n1 // 2), 1.0, 2.0) * (k1 <= n1 // 2) / N
    fai = jnp.stack([jnp.cos(ang_a) * wgt[:, None], -jnp.sin(ang_a) * wgt[:, None]], axis=1)
    fai = fai.reshape(2 * nk1, nt1).T
    k2 = jnp.arange(FFT_N2)
    t2 = jnp.arange(FFT_N2)
    idx = (t2[None, None, :] * k1[:, None, None] + n1 * t2[None, None, :] * k2[None, :, None]) % N
    phi = (2.0 * math.pi / N) * idx.astype(F32)
    gr, gi = jnp.cos(phi), -jnp.sin(phi)
    gfwd = jnp.concatenate([jnp.concatenate([gr, -gi], axis=2), jnp.concatenate([gi, gr], axis=2)], axis=1)
    grt, git = jnp.swapaxes(gr, 1, 2), jnp.swapaxes(gi, 1, 2)
    ginv = jnp.concatenate([jnp.concatenate([grt, git], axis=2), jnp.concatenate([-git, grt], axis=2)], axis=1)
    return fa.astype(BF16), fai.astype(BF16), gfwd.astype(BF16), ginv.astype(BF16)


def _fa_kernel(fa_ref, x_ref, o_ref):
    o_ref[0] = jnp.dot(fa_ref[...], x_ref[0].astype(BF16), preferred_element_type=F32).astype(o_ref.dtype)


def _fft_stage_a(fa, y, out_dtype):
    B, Lp, C = y.shape
    nt1 = Lp // FFT_N2
    rows = fa.shape[0]
    Wb = 8 * C if C <= 512 else C
    yv = y.reshape(B, nt1, FFT_N2 * C)
    out = pl.pallas_call(
        _fa_kernel,
        grid=(B, FFT_N2 * C // Wb),
        in_specs=[pl.BlockSpec((rows, nt1), lambda b, j: (0, 0)),
                  pl.BlockSpec((1, nt1, Wb), lambda b, j: (b, 0, j))],
        out_specs=pl.BlockSpec((1, rows, Wb), lambda b, j: (b, 0, j)),
        out_shape=jax.ShapeDtypeStruct((B, rows, FFT_N2 * C), out_dtype),
        compiler_params=_cp(("parallel", "parallel")),
        name="fft_stage_a",
    )(fa, yv)
    return out.reshape(B, rows // 2, 2, FFT_N2, C)


def _fc_filter_kernel(g_ref, s_ref, ss_ref, h_ref, *, kb):
    C = HY_CH
    for i in range(kb):
        for n in range(HY_ORDER):
            cf, cb = (2 * n) * C, (2 * n + 1) * C
            scale = lax.rsqrt(ss_ref[0:1, cf:cf + C] + ss_ref[0:1, cb:cb + C] + EPS)
            sf = s_ref[0, i, :, :, cf:cf + C].reshape(2 * FFT_N2, C).astype(BF16)
            sb = s_ref[0, i, :, :, cb:cb + C].reshape(2 * FFT_N2, C).astype(BF16)
            xf = jnp.dot(g_ref[i], sf, preferred_element_type=F32)
            xb = jnp.dot(g_ref[i], sb, preferred_element_type=F32)
            h_ref[n, i, 0] = (xf[:FFT_N2] + xb[:FFT_N2]) * scale
            h_ref[n, i, 1] = (xf[FFT_N2:] - xb[FFT_N2:]) * scale


def _fft_filter_spectrum(gfwd, s_filt, sumsq):
    nk1 = s_filt.shape[1]
    C4 = s_filt.shape[-1]
    kb = 2
    return pl.pallas_call(
        functools.partial(_fc_filter_kernel, kb=kb),
        grid=(nk1 // kb,),
        in_specs=[pl.BlockSpec((kb, 2 * FFT_N2, 2 * FFT_N2), lambda i: (i, 0, 0)),
                  pl.BlockSpec((1, kb, 2, FFT_N2, C4), lambda i: (0, i, 0, 0, 0)),
                  pl.BlockSpec((1, C4), lambda i: (0, 0))],
        out_specs=pl.BlockSpec((HY_ORDER, kb, 2, FFT_N2, HY_CH), lambda i: (0, i, 0, 0, 0)),
        out_shape=jax.ShapeDtypeStruct((HY_ORDER, nk1, 2, FFT_N2, HY_CH), F32),
        compiler_params=_cp(("parallel",)),
        name="fft_filter_spectrum",
    )(gfwd, s_filt, sumsq)


def _fc_kernel(g_ref, gi_ref, h_ref, s_ref, o_ref, *, kb):
    C = s_ref.shape[-1]
    for i in range(kb):
        s = s_ref[0, i].reshape(2 * FFT_N2, C)
        x = jnp.dot(g_ref[i], s, preferred_element_type=F32)
        xr, xi = x[:FFT_N2], x[FFT_N2:]
        hr, hi = h_ref[0, i, 0], h_ref[0, i, 1]
        z = jnp.concatenate([xr * hr - xi * hi, xr * hi + xi * hr], axis=0).astype(BF16)
        bm = jnp.dot(gi_ref[i], z, preferred_element_type=F32)
        o_ref[0, i] = bm.reshape(2, FFT_N2, C).astype(o_ref.dtype)


def _fft_stage_c(gfwd, ginv, hspec, order, s):
    B, nk1, _, _, C = s.shape
    kb = 4
    sspec = pl.BlockSpec((1, kb, 2, FFT_N2, C), lambda i, b: (b, i, 0, 0, 0))
    gspec = pl.BlockSpec((kb, 2 * FFT_N2, 2 * FFT_N2), lambda i, b: (i, 0, 0))
    return pl.pallas_call(
        functools.partial(_fc_kernel, kb=kb),
        grid=(nk1 // kb, B),
        in_specs=[gspec, gspec,
                  pl.BlockSpec((1, kb, 2, FFT_N2, C), lambda i, b: (order, i, 0, 0, 0)), sspec],
        out_specs=sspec,
        out_shape=jax.ShapeDtypeStruct(s.shape, BF16),
        compiler_params=_cp(("parallel", "arbitrary")),
        name="fft_stage_c",
    )(gfwd, ginv, hspec, s)


def _fai_kernel(fai_ref, b_ref, y_ref, gate_ref, skip_ref, o_ref):
    yf = jnp.dot(fai_ref[...], b_ref[0], preferred_element_type=F32)
    o_ref[0] = (gate_ref[0].astype(F32) * (yf + skip_ref[...] * y_ref[0].astype(F32))).astype(o_ref.dtype)


def _fft_stage_a_inv(fai, bm, y, gate, skip):
    B, Lp, C = y.shape
    nt1 = Lp // FFT_N2
    rows = fai.shape[1]
    Wb = 8 * C
    tok = pl.BlockSpec((1, nt1, Wb), lambda b, j: (b, 0, j))
    out = pl.pallas_call(
        _fai_kernel,
        grid=(B, FFT_N2 * C // Wb),
        in_specs=[pl.BlockSpec((nt1, rows), lambda b, j: (0, 0)),
                  pl.BlockSpec((1, rows, Wb), lambda b, j: (b, 0, j)),
                  tok, tok, pl.BlockSpec((1, Wb), lambda b, j: (0, 0))],
        out_specs=tok,
        out_shape=jax.ShapeDtypeStruct((B, nt1, FFT_N2 * C), BF16),
        compiler_params=_cp(("parallel", "parallel")),
        name="fft_stage_a_inv",
    )(fai, bm.reshape(B, rows, FFT_N2 * C), y.reshape(B, nt1, FFT_N2 * C), gate.reshape(B, nt1, FFT_N2 * C),
      jnp.tile(skip.astype(F32), 8).reshape(1, Wb))
    return out.reshape(B, Lp, C)


def _hgen_kernel(z_ref, w1_ref, b1_ref, fr_ref, w2_ref, b2_ref, w3_ref, dec_ref, h_ref, ss_ref):
    i = pl.program_id(0)
    z = z_ref[...]
    tm = z.shape[0]
    h = jnp.sin(fr_ref[0:1, :] * (jnp.dot(z, w1_ref[...], preferred_element_type=F32, precision=HIGHEST)
                                 + b1_ref[...]))
    h = jnp.sin(fr_ref[1:2, :] * (jnp.dot(h, w2_ref[...], preferred_element_type=F32, precision=HIGHEST)
                                 + b2_ref[...]))
    h = jnp.dot(h, w3_ref[...], preferred_element_type=F32, precision=HIGHEST)
    h = h * jnp.exp(-z[:, 0:1] * jnp.abs(dec_ref[...]))
    row = lax.broadcasted_iota(jnp.int32, h.shape, 0) + i * tm
    col = lax.broadcasted_iota(jnp.int32, h.shape, 1)
    h = jnp.where((row == 0) & ((col // HY_CH) % 2 == 1), 0.0, h)
    h_ref[...] = h

    @pl.when(i == 0)
    def _():
        ss_ref[...] = jnp.zeros_like(ss_ref)

    ss_ref[...] += jnp.sum(h * h, axis=0, keepdims=True)


def _hyena_filters(L, p):
    t = jnp.arange(L, dtype=F32)
    tn = t / (L - 1)
    w = 2.0 * math.pi * t / L
    bands = jnp.linspace(1e-4, HY_BANDS - 1, HY_BANDS, dtype=F32)
    ang = w[:, None] * bands[None, :]
    z = jnp.concatenate([tn[:, None], jnp.cos(ang), -jnp.sin(ang)], axis=-1)
    z = jnp.pad(z, ((0, 0), (0, LANES - HY_EMB)))
    w1 = jnp.pad(p['hy_pe_w1'].astype(F32), ((0, LANES - HY_EMB), (0, 0)))
    nf = HY_ORDER * 2 * HY_CH
    tm = min(512, L)
    const = lambda shape: pl.BlockSpec(shape, lambda i: (0,) * len(shape))
    return pl.pallas_call(
        _hgen_kernel,
        grid=(L // tm,),
        in_specs=[pl.BlockSpec((tm, LANES), lambda i: (i, 0)), const((LANES, HY_FFN)), const((1, HY_FFN)),
                  const((2, HY_FFN)), const((HY_FFN, HY_FFN)), const((1, HY_FFN)), const((HY_FFN, nf)),
                  const((1, nf))],
        out_specs=[pl.BlockSpec((tm, nf), lambda i: (i, 0)), const((1, nf))],
        out_shape=[jax.ShapeDtypeStruct((L, nf), F32), jax.ShapeDtypeStruct((1, nf), F32)],
        compiler_params=_cp(("arbitrary",)),
        name="hyena_filter_gen",
    )(z, w1, p['hy_pe_b1'].reshape(1, HY_FFN), p['hy_freq'], p['hy_pe_w2'], p['hy_pe_b2'].reshape(1, HY_FFN),
      p['hy_pe_w3'], p['hy_decay'].reshape(1, nf))


def _hyena_branch(z, p, B, L):
    v, x1, x2 = _hy_prep(z, p['hy_conv_w'], p['hy_conv_b'], L)
    Lp = max(L, 2048)
    fa, fai, gfwd, ginv = _fft_tables(Lp)
    hfilt, sumsq = _hyena_filters(L, p)
    pad3 = lambda a: a.reshape(B, L, HY_CH) if Lp == L else jnp.pad(a.reshape(B, L, HY_CH), ((0, 0), (0, Lp - L), (0, 0)))
    hf = hfilt if Lp == L else jnp.pad(hfilt, ((0, Lp - L), (0, 0)))
    s_filt = _fft_stage_a(fa, hf[None], F32)
    hspec = _fft_filter_spectrum(gfwd, s_filt, sumsq)
    y = pad3(v)
    for n, gate in enumerate((pad3(x1), pad3(x2))):
        s = _fft_stage_a(fa, y, BF16)
        bm = _fft_stage_c(gfwd, ginv, hspec, n, s)
        y = _fft_stage_a_inv(fai, bm, y, gate, p['hy_skip'][n])
    return y[:, :L].reshape(B * L, HY_CH)


def _merge_kernel(ya_ref, yb_ref, yc_ref, yd_ref, g0_ref, g1_ref, g2_ref, g3_ref, wup_ref, wout_ref,
                  x_ref, gate_ref, o_ref):
    acc = None
    for n, (y_ref, g_ref) in enumerate(((ya_ref, g0_ref), (yb_ref, g1_ref), (yc_ref, g2_ref), (yd_ref, g3_ref))):
        t = jax.nn.sigmoid(g_ref[...].astype(F32)) * jnp.dot(y_ref[...], wup_ref[n], preferred_element_type=F32)
        acc = t if acc is None else acc + t
    yl = jnp.dot(acc.astype(BF16), wout_ref[...], preferred_element_type=F32)
    o_ref[...] = x_ref[...] + gate_ref[0] * yl


def _merge(ys, z, w_up, w_out, x, gate, seq_len):
    T, D = x.shape
    tm = min(512, seq_len)
    per = seq_len // tm
    Wy = ys[0].shape[1]
    yspec = pl.BlockSpec((tm, Wy), lambda i: (i, 0))
    gspecs = [pl.BlockSpec((tm, D), functools.partial(lambda i, n: (i, Z_GATE // D + n), n=n)) for n in range(4)]
    return pl.pallas_call(
        _merge_kernel,
        grid=(T // tm,),
        in_specs=[yspec] * 4 + gspecs + [
            pl.BlockSpec((4, Wy, D), lambda i: (0, 0, 0)), pl.BlockSpec((D, D), lambda i: (0, 0)),
            pl.BlockSpec((tm, D), lambda i: (i, 0)), pl.BlockSpec((1, 1, D), lambda i: (i // per, 0, 0))],
        out_specs=pl.BlockSpec((tm, D), lambda i: (i, 0)),
        out_shape=jax.ShapeDtypeStruct((T, D), F32),
        compiler_params=_cp(("parallel",)),
        name="merge_branches",
    )(*ys, z, z, z, z, w_up, w_out, x, gate)


def _mlp_kernel(x_ref, g_ref, sc_ref, sh_ref, w1_ref, b1_ref, w2_ref, b2_ref, gate_ref, fg_ref, o_ref,
                h_ref, acc_ref, *, nk, final):
    k = pl.program_id(1)

    @pl.when(k == 0)
    def _():
        y = _rms(x_ref[...], g_ref[...])
        h_ref[...] = (y * (1.0 + sc_ref[0]) + sh_ref[0]).astype(BF16)
        acc_ref[...] = jnp.zeros_like(acc_ref)

    a = jnp.maximum(jnp.dot(h_ref[...], w1_ref[...], preferred_element_type=F32) + b1_ref[...], 0.0)
    acc_ref[...] += jnp.dot((a * a).astype(BF16), w2_ref[...], preferred_element_type=F32)

    @pl.when(k == nk - 1)
    def _():
        out = x_ref[...] + gate_ref[0] * (acc_ref[...] + b2_ref[...])
        if final:
            out = _rms(out, fg_ref[...])
        o_ref[...] = out


def _mlp(x, g, sc, sh, w1, b1, w2, b2, gate, final_g, seq_len, final):
    T, D = x.shape
    F = w1.shape[1]
    tm = min(1024, seq_len)
    per = seq_len // tm
    tk = 512
    nk = F // tk
    row = lambda i, k: (i // per, 0, 0)
    return pl.pallas_call(
        functools.partial(_mlp_kernel, nk=nk, final=final),
        grid=(T // tm, nk),
        in_specs=[pl.BlockSpec((tm, D), lambda i, k: (i, 0)), pl.BlockSpec((1, D), lambda i, k: (0, 0)),
                  pl.BlockSpec((1, 1, D), row), pl.BlockSpec((1, 1, D), row),
                  pl.BlockSpec((D, tk), lambda i, k: (0, k)), pl.BlockSpec((1, tk), lambda i, k: (0, k)),
                  pl.BlockSpec((tk, D), lambda i, k: (k, 0)), pl.BlockSpec((1, D), lambda i, k: (0, 0)),
                  pl.BlockSpec((1, 1, D), row), pl.BlockSpec((1, D), lambda i, k: (0, 0))],
        out_specs=pl.BlockSpec((tm, D), lambda i, k: (i, 0)),
        out_shape=jax.ShapeDtypeStruct((T, D), F32),
        scratch_shapes=[pltpu.VMEM((tm, D), BF16), pltpu.VMEM((tm, D), F32)],
        compiler_params=_cp(("parallel", "arbitrary")),
        name="mlp",
    )(x, g.reshape(1, D), sc, sh, w1, b1.reshape(1, F), w2, b2.reshape(1, D), gate, final_g.reshape(1, D))


def _pack_w_in(w_in):
    hy_e = 3 * HY_CH
    ga_e = hy_e + (GA_HEADS + 2 * GA_KV) * GA_HD
    mw = ML_HEADS * ML_HD
    ml_e = ga_e + 4 * mw + 16
    wa_e = ml_e + (WA_HEADS + 2 * WA_KV) * WA_HD
    hy, ga = w_in[:, :hy_e], w_in[:, hy_e:ga_e]
    ml = w_in[:, ga_e:ml_e]
    wa = w_in[:, ml_e:wa_e]
    gate = w_in[:, wa_e:]
    waq, wakv = wa[:, :WA_HEADS * WA_HD], wa[:, WA_HEADS * WA_HD:]
    pad = jnp.zeros((w_in.shape[0], Z_COLS - Z_WAKV - wakv.shape[1]), w_in.dtype)
    packed = jnp.concatenate([hy, waq, ga, ml[:, :2 * mw], ml[:, 2 * mw:3 * mw], ml[:, 3 * mw:4 * mw], gate, wakv, pad],
                             axis=1)
    wg = jnp.pad(ml[:, 4 * mw:], ((0, 0), (0, LANES - 16)))
    return packed.astype(BF16), wg


def _token_mixers(zl, zc, gl, gc, p, B, L, Lc, with_ctx_out):
    ya_l = _hyena_branch(zl, p, B, L)
    ya_c = _hyena_branch(zc, p, B, Lc) if with_ctx_out else None
    cos, sin = _rope_tables(L, GA_HD)
    ql, kl, vl = _ga_prep(zl, cos, sin, p['ga_q_g'], p['ga_k_g'], L, True)
    qc, kc, vc = _ga_prep(zc, cos[:Lc], sin[:Lc], p['ga_q_g'], p['ga_k_g'], Lc, False)
    r3 = lambda a, n: a.reshape(B, n, a.shape[-1])
    yb_l = _global_attention(r3(ql, L), r3(kc, Lc), r3(vc, Lc), r3(kl, L), r3(vl, L)).reshape(B * L, -1)
    yb_c = _global_attention(r3(qc, Lc), r3(kc, Lc), r3(vc, Lc)).reshape(B * Lc, -1) if with_ctx_out else None
    mq_l, mk_l = _ml_prep(zl, p['ml_conv_w'], p['ml_conv_b'], L)
    mq_c, mk_c = _ml_prep(zc, p['ml_conv_w'], p['ml_conv_b'], Lc)
    gb = jnp.pad(p['ml_gate_b'].astype(F32), (0, LANES - 16)).reshape(1, LANES)
    c0 = jnp.zeros((B, ML_HEADS, ML_HD, 2 * ML_HD), F32)
    m0 = jnp.zeros((B, ML_HEADS, 8, LANES), F32)
    zl3, zc3, gl3, gc3 = r3(zl, L), r3(zc, Lc), r3(gl, L), r3(gc, Lc)
    h_cf, cf, mf = _mlstm_scan(r3(mq_c, Lc), r3(mk_c, Lc), zc3, gc3, gb, c0, m0, False)
    yc_c, cb, mb = _mlstm_scan(r3(mq_c, Lc), r3(mk_c, Lc), zc3, gc3, gb, c0, m0, True, h_cf, p['ml_norm_g'])
    h_lf, _, _ = _mlstm_scan(r3(mq_l, L), r3(mk_l, L), zl3, gl3, gb, cf, mf, False)
    yc_l, _, _ = _mlstm_scan(r3(mq_l, L), r3(mk_l, L), zl3, gl3, gb, cb, mb, True, h_lf, p['ml_norm_g'])
    yc_l = yc_l.reshape(B * L, -1)
    yc_c = yc_c.reshape(B * Lc, -1)
    cosw, sinw = _rope_tables(L, WA_HD)
    wq_l, wk_l, wv_l = _wa_prep(zl, cosw, sinw, L, True)
    wq_c, wk_c, wv_c = _wa_prep(zc, cosw[:Lc], sinw[:Lc], Lc, False)
    yd_l = _window_attention(r3(wq_l, L), r3(wk_c, Lc), r3(wv_c, Lc), p['wa_sink'], r3(wk_l, L), r3(wv_l, L))
    yd_l = yd_l.reshape(B * L, -1)
    yd_c = (_window_attention(r3(wq_c, Lc), r3(wk_c, Lc), r3(wv_c, Lc), p['wa_sink']).reshape(B * Lc, -1)
            if with_ctx_out else None)
    return (ya_l, yb_l, yc_l, yd_l), (ya_c, yb_c, yc_c, yd_c)


def kernel(x, c, ctx, c_ctx, w_mod, b_mod, ln1_g, ln2_g, w_in, hy_conv_w, hy_conv_b,
           hy_pe_w1, hy_pe_b1, hy_freq, hy_pe_w2, hy_pe_b2, hy_pe_w3, hy_decay, hy_skip,
           ga_q_g, ga_k_g, ml_conv_w, ml_conv_b, ml_gate_b, ml_norm_g, wa_sink, w_up, w_out,
           mlp_w1, mlp_b1, mlp_w2, mlp_b2, final_g):
    B, L, D = x.shape
    Lc = ctx.shape[1]
    R = -(-(B + 1) // 8) * 8
    cvec = jnp.zeros((R, D), F32).at[:B].set(c).at[B].set(c_ctx)
    mod = _modulation(cvec, w_mod, b_mod)
    xl = x.reshape(B * L, D)
    xc = ctx.reshape(B * Lc, D)
    for l in range(DEPTH):
        with_ctx_out = l < DEPTH - 1
        p = dict(hy_conv_w=hy_conv_w[l], hy_conv_b=hy_conv_b[l], hy_pe_w1=hy_pe_w1[l],
                 hy_pe_b1=hy_pe_b1[l], hy_freq=hy_freq[l], hy_pe_w2=hy_pe_w2[l], hy_pe_b2=hy_pe_b2[l],
                 hy_pe_w3=hy_pe_w3[l], hy_decay=hy_decay[l], hy_skip=hy_skip[l],
                 ga_q_g=ga_q_g[l], ga_k_g=ga_k_g[l], ml_conv_w=ml_conv_w[l], ml_conv_b=ml_conv_b[l],
                 ml_gate_b=ml_gate_b[l], ml_norm_g=ml_norm_g[l], wa_sink=wa_sink[l])
        ml_rows = mod[l, :B].reshape(B, 1, 6 * D)
        mc_rows = jnp.broadcast_to(mod[l, B].reshape(1, 1, 6 * D), (B, 1, 6 * D))
        part = lambda m, n: m[:, :, n * D:(n + 1) * D]
        w_pack, w_gate = _pack_w_in(w_in[l])
        zl = _normmod_matmul(xl, ln1_g[l], part(ml_rows, 1), part(ml_rows, 0), w_pack, L, Z_TN, BF16)
        zc = _normmod_matmul(xc, ln1_g[l], part(mc_rows, 1), part(mc_rows, 0), w_pack, Lc, Z_TN, BF16)
        gl = _normmod_matmul(xl, ln1_g[l], part(ml_rows, 1), part(ml_rows, 0), w_gate.astype(BF16), L, LANES, F32)
        gc = _normmod_matmul(xc, ln1_g[l], part(mc_rows, 1), part(mc_rows, 0), w_gate.astype(BF16), Lc, LANES, F32)
        ys_l, ys_c = _token_mixers(zl, zc, gl, gc, p, B, L, Lc, with_ctx_out)
        wup = w_up[l].astype(BF16)
        wout = w_out[l].astype(BF16)
        w1, w2 = mlp_w1[l].astype(BF16), mlp_w2[l].astype(BF16)
        xl = _merge(ys_l, zl, wup, wout, xl, part(ml_rows, 2), L)
        xl = _mlp(xl, ln2_g[l], part(ml_rows, 4), part(ml_rows, 3), w1, mlp_b1[l], w2, mlp_b2[l],
                  part(ml_rows, 5), final_g, L, final=(l == DEPTH - 1))
        if with_ctx_out:
            xc = _merge(ys_c, zc, wup, wout, xc, part(mc_rows, 2), Lc)
            xc = _mlp(xc, ln2_g[l], part(mc_rows, 4), part(mc_rows, 3), w1, mlp_b1[l], w2, mlp_b2[l],
                      part(mc_rows, 5), final_g, Lc, final=False)
    return xl.reshape(B, L, D)
```

```python
import functools
import math

import jax
import jax.numpy as jnp
import numpy as np
from jax import lax
from jax.experimental import pallas as pl
from jax.experimental.pallas import tpu as pltpu

F32 = jnp.float32
BF16 = jnp.bfloat16
HIGHEST = lax.Precision.HIGHEST

D_MODEL = 1024
DEPTH = 2
GRID_W = 64
HY_CH = 512
HY_ORDER = 2
HY_BANDS = 16
HY_EMB = 1 + 2 * HY_BANDS
HY_FFN = 64
GA_HEADS, GA_KV, GA_HD = 4, 2, 128
ML_HEADS, ML_HD = 4, 128
WA_HEADS, WA_KV, WA_HD = 8, 2, 64
WINDOW = 128
ROPE_BASE = 10000.0
D_FF = 4 * D_MODEL
EPS = 1e-6
NEG = -1e30
LOG2E = 1.4426950408889634

LANES = 128
V7X_VMEM_LIMIT = 48 * 1024 * 1024

Z_HY = 0
Z_WAQ = 1536
Z_GA = 2048
Z_MLQK = 3072
Z_MLV = 4096
Z_MLO = 4608
Z_GATE = 5120
Z_WAKV = 9216
Z_COLS = 9728
Z_TN = 2432

ML_CHUNK = 128
FFT_N2 = 128


def _cp(sem, vmem=V7X_VMEM_LIMIT):
    return pltpu.CompilerParams(dimension_semantics=sem, vmem_limit_bytes=vmem)


def _rms(x, g):
    return x * lax.rsqrt(jnp.mean(x * x, axis=-1, keepdims=True) + EPS) * g


def _mod_kernel(c_ref, w_ref, b_ref, o_ref):
    c = c_ref[...]
    s = c * jax.nn.sigmoid(c)
    o_ref[0] = jnp.dot(s, w_ref[0], preferred_element_type=F32, precision=HIGHEST) + b_ref[0]


def _modulation(cvec, w_mod, b_mod):
    R = cvec.shape[0]
    tn = 1536
    return pl.pallas_call(
        _mod_kernel,
        grid=(DEPTH, 6 * D_MODEL // tn),
        in_specs=[pl.BlockSpec((R, D_MODEL), lambda l, j: (0, 0)),
                  pl.BlockSpec((1, D_MODEL, tn), lambda l, j: (l, 0, j)),
                  pl.BlockSpec((1, 1, tn), lambda l, j: (l, 0, j))],
        out_specs=pl.BlockSpec((1, R, tn), lambda l, j: (l, 0, j)),
        out_shape=jax.ShapeDtypeStruct((DEPTH, R, 6 * D_MODEL), F32),
        compiler_params=_cp(("parallel", "parallel")),
        name="modulation",
    )(cvec, w_mod, b_mod.reshape(DEPTH, 1, 6 * D_MODEL))


def _nmm_kernel(x_ref, g_ref, sc_ref, sh_ref, w_ref, o_ref, h_ref):
    @pl.when(pl.program_id(1) == 0)
    def _():
        y = _rms(x_ref[...], g_ref[...])
        h_ref[...] = (y * (1.0 + sc_ref[0]) + sh_ref[0]).astype(BF16)

    o_ref[...] = jnp.dot(h_ref[...], w_ref[...], preferred_element_type=F32).astype(o_ref.dtype)


def _normmod_matmul(x, g, sc, sh, w, seq_len, tn, out_dtype):
    T, D = x.shape
    N = w.shape[1]
    tm = min(1024, seq_len)
    per = seq_len // tm
    return pl.pallas_call(
        _nmm_kernel,
        grid=(T // tm, N // tn),
        in_specs=[pl.BlockSpec((tm, D), lambda i, j: (i, 0)),
                  pl.BlockSpec((1, D), lambda i, j: (0, 0)),
                  pl.BlockSpec((1, 1, D), lambda i, j: (i // per, 0, 0)),
                  pl.BlockSpec((1, 1, D), lambda i, j: (i // per, 0, 0)),
                  pl.BlockSpec((D, tn), lambda i, j: (0, j))],
        out_specs=pl.BlockSpec((tm, tn), lambda i, j: (i, j)),
        out_shape=jax.ShapeDtypeStruct((T, N), out_dtype),
        scratch_shapes=[pltpu.VMEM((tm, D), BF16)],
        compiler_params=_cp(("parallel", "arbitrary")),
        name="normmod_matmul",
    )(x, g.reshape(1, D), sc, sh, w)


def _conv3(u, prev_row, next_row, w_ref, b_ref, c0, c1):
    tm = u.shape[0]
    row = lax.broadcasted_iota(jnp.int32, u.shape, 0)
    up = jnp.where(row == 0, prev_row, pltpu.roll(u, 1, 0))
    dn = jnp.where(row == tm - 1, next_row, pltpu.roll(u, tm - 1, 0))
    return (w_ref[0:1, c0:c1] * up + w_ref[1:2, c0:c1] * u + w_ref[2:3, c0:c1] * dn + b_ref[0:1, c0:c1])


def _halo_rows(zp_ref, zn_ref, per, c0, c1):
    i = pl.program_id(0)
    first = (i % per) == 0
    last = (i % per) == per - 1
    hp = zp_ref.shape[0]
    prev_row = jnp.where(first, 0.0, zp_ref[hp - 1:hp, c0:c1].astype(F32))
    next_row = jnp.where(last, 0.0, zn_ref[0:1, c0:c1].astype(F32))
    return prev_row, next_row


HALO = 16


def _halo_specs(tm, width, col_block, n_rows):
    nb = n_rows // HALO
    r = tm // HALO
    return [pl.BlockSpec((tm, width), lambda i: (i, col_block)),
            pl.BlockSpec((HALO, width), lambda i: (jnp.maximum(i * r - 1, 0), col_block)),
            pl.BlockSpec((HALO, width), lambda i: (jnp.minimum((i + 1) * r, nb - 1), col_block))]


def _hy_prep_kernel(z_ref, zp_ref, zn_ref, w_ref, b_ref, v_ref, x1_ref, x2_ref, *, per):
    outs = (v_ref, x1_ref, x2_ref)
    for c in range(3):
        c0, c1 = c * HY_CH, (c + 1) * HY_CH
        prev_row, next_row = _halo_rows(zp_ref, zn_ref, per, c0, c1)
        u = z_ref[:, c0:c1].astype(F32)
        outs[c][...] = _conv3(u, prev_row, next_row, w_ref, b_ref, c0, c1).astype(outs[c].dtype)


def _hy_prep(z, conv_w, conv_b, seq_len):
    T = z.shape[0]
    tm = min(512, seq_len)
    W = 3 * HY_CH
    out = jax.ShapeDtypeStruct((T, HY_CH), BF16)
    return pl.pallas_call(
        functools.partial(_hy_prep_kernel, per=seq_len // tm),
        grid=(T // tm,),
        in_specs=_halo_specs(tm, W, Z_HY // W, T) + [
            pl.BlockSpec((3, W), lambda i: (0, 0)), pl.BlockSpec((1, W), lambda i: (0, 0))],
        out_specs=[pl.BlockSpec((tm, HY_CH), lambda i: (i, 0))] * 3,
        out_shape=[out, out, out],
        compiler_params=_cp(("parallel",)),
        name="hyena_prep",
    )(z, z, z, conv_w, conv_b.reshape(1, W))


def _ml_prep_kernel(z_ref, zp_ref, zn_ref, w_ref, b_ref, q_ref, k_ref, *, per):
    W = ML_HEADS * ML_HD
    for c, o_ref, scale in ((0, q_ref, 1.0), (1, k_ref, ML_HD ** -0.5)):
        c0, c1 = c * W, (c + 1) * W
        prev_row, next_row = _halo_rows(zp_ref, zn_ref, per, c0, c1)
        u = z_ref[:, c0:c1].astype(F32)
        y = _conv3(u, prev_row, next_row, w_ref, b_ref, c0, c1)
        y = y * jax.nn.sigmoid(y)
        o_ref[...] = (y * scale).astype(o_ref.dtype)


def _ml_prep(z, conv_w, conv_b, seq_len):
    T = z.shape[0]
    tm = min(512, seq_len)
    W = 2 * ML_HEADS * ML_HD
    out = jax.ShapeDtypeStruct((T, W // 2), BF16)
    return pl.pallas_call(
        functools.partial(_ml_prep_kernel, per=seq_len // tm),
        grid=(T // tm,),
        in_specs=_halo_specs(tm, W, Z_MLQK // W, T) + [
            pl.BlockSpec((3, W), lambda i: (0, 0)), pl.BlockSpec((1, W), lambda i: (0, 0))],
        out_specs=[pl.BlockSpec((tm, W // 2), lambda i: (i, 0))] * 2,
        out_shape=[out, out],
        compiler_params=_cp(("parallel",)),
        name="mlstm_prep",
    )(z, z, z, conv_w, conv_b.reshape(1, W))


def _rope_tables(L, hd):
    quarter = hd // 4
    inv = ROPE_BASE ** (-jnp.arange(quarter, dtype=F32) / quarter)
    t = jnp.arange(L)
    row = (t // GRID_W).astype(F32)
    col = (t % GRID_W).astype(F32)
    lane = jnp.arange(LANES)
    within = lane % hd
    is_col = (within // (hd // 2)) == 1
    second = ((within % (hd // 2)) // quarter) == 1
    j = within % quarter
    pos = jnp.where(is_col[None, :], col[:, None], row[:, None])
    ang = pos * inv[j][None, :]
    return jnp.cos(ang), jnp.where(second[None, :], jnp.sin(ang), -jnp.sin(ang))


def _rope(x, cos, sin, quarter):
    lane = lax.broadcasted_iota(jnp.int32, x.shape, 1)
    first = ((lane % (2 * quarter)) // quarter) == 0
    partner = jnp.where(first, pltpu.roll(x, LANES - quarter, 1), pltpu.roll(x, quarter, 1))
    return x * cos + partner * sin


def _ga_prep_kernel(z_ref, cos_ref, sin_ref, qg_ref, kg_ref, q_ref, k_ref, v_ref, *, rope):
    nq, nk = GA_HEADS, GA_KV
    for h in range(nq + nk):
        x = z_ref[:, h * GA_HD:(h + 1) * GA_HD].astype(F32)
        g = qg_ref[...] if h < nq else kg_ref[...]
        y = _rms(x, g)
        if rope:
            y = _rope(y, cos_ref[...], sin_ref[...], GA_HD // 4)
        if h < nq:
            q_ref[0, h * GA_HD:(h + 1) * GA_HD, :] = (y * (GA_HD ** -0.5 * LOG2E)).T.astype(q_ref.dtype)
        else:
            k_ref[:, (h - nq) * GA_HD:(h - nq + 1) * GA_HD] = y.astype(k_ref.dtype)
    for h in range(nk):
        v = z_ref[:, (nq + nk + h) * GA_HD:(nq + nk + h + 1) * GA_HD].astype(F32)
        v_ref[0, h * GA_HD:(h + 1) * GA_HD, :] = v.T.astype(v_ref.dtype)


def _ga_prep(z, cos, sin, qg, kg, B, seq_len, rope):
    T = z.shape[0]
    tm = min(512, seq_len)
    per = seq_len // tm
    W = (GA_HEADS + 2 * GA_KV) * GA_HD
    return pl.pallas_call(
        functools.partial(_ga_prep_kernel, rope=rope),
        grid=(T // tm,),
        in_specs=[pl.BlockSpec((tm, W), lambda i: (i, Z_GA // W)),
                  pl.BlockSpec((tm, LANES), lambda i: (i % per, 0)),
                  pl.BlockSpec((tm, LANES), lambda i: (i % per, 0)),
                  pl.BlockSpec((1, GA_HD), lambda i: (0, 0)),
                  pl.BlockSpec((1, GA_HD), lambda i: (0, 0))],
        out_specs=[pl.BlockSpec((1, GA_HEADS * GA_HD, tm), lambda i: (i // per, 0, i % per)),
                   pl.BlockSpec((tm, GA_KV * GA_HD), lambda i: (i, 0)),
                   pl.BlockSpec((1, GA_KV * GA_HD, tm), lambda i: (i // per, 0, i % per))],
        out_shape=[jax.ShapeDtypeStruct((B, GA_HEADS * GA_HD, seq_len), BF16),
                   jax.ShapeDtypeStruct((T, GA_KV * GA_HD), BF16),
                   jax.ShapeDtypeStruct((B, GA_KV * GA_HD, seq_len), BF16)],
        compiler_params=_cp(("parallel",)),
        name="global_attn_prep",
    )(z, cos, sin, qg.reshape(1, GA_HD), kg.reshape(1, GA_HD))


def _dup_halves(x):
    lane = lax.broadcasted_iota(jnp.int32, x.shape, 1)
    sw = pltpu.roll(x, 64, 1)
    lo = lane < 64
    return jnp.where(lo, x, sw), jnp.where(lo, sw, x)


def _wa_prep_kernel(zq_ref, zkv_ref, cos_ref, sin_ref, q_ref, k_ref, v_ref, *, rope):
    quarter = WA_HD // 4
    for j in range(WA_HEADS * WA_HD // LANES):
        x = zq_ref[:, j * LANES:(j + 1) * LANES].astype(F32)
        if rope:
            x = _rope(x, cos_ref[...], sin_ref[...], quarter)
        q_ref[:, j * LANES:(j + 1) * LANES] = (x * (WA_HD ** -0.5)).astype(q_ref.dtype)
    k = zkv_ref[:, 0:LANES].astype(F32)
    if rope:
        k = _rope(k, cos_ref[...], sin_ref[...], quarter)
    k0, k1 = _dup_halves(k)
    k_ref[:, 0:LANES] = k0.astype(k_ref.dtype)
    k_ref[:, LANES:] = k1.astype(k_ref.dtype)
    v0, v1 = _dup_halves(zkv_ref[:, LANES:].astype(F32))
    v_ref[:, 0:LANES] = v0.astype(v_ref.dtype)
    v_ref[:, LANES:] = v1.astype(v_ref.dtype)


def _wa_prep(z, cos, sin, seq_len, rope):
    T = z.shape[0]
    tm = min(512, seq_len)
    per = seq_len // tm
    WQ = WA_HEADS * WA_HD
    return pl.pallas_call(
        functools.partial(_wa_prep_kernel, rope=rope),
        grid=(T // tm,),
        in_specs=[pl.BlockSpec((tm, WQ), lambda i: (i, Z_WAQ // WQ)),
                  pl.BlockSpec((tm, 2 * LANES), lambda i: (i, Z_WAKV // (2 * LANES))),
                  pl.BlockSpec((tm, LANES), lambda i: (i % per, 0)),
                  pl.BlockSpec((tm, LANES), lambda i: (i % per, 0))],
        out_specs=[pl.BlockSpec((tm, WQ), lambda i: (i, 0)),
                   pl.BlockSpec((tm, 2 * LANES), lambda i: (i, 0)),
                   pl.BlockSpec((tm, 2 * LANES), lambda i: (i, 0))],
        out_shape=[jax.ShapeDtypeStruct((T, WQ), BF16),
                   jax.ShapeDtypeStruct((T, 2 * LANES), BF16),
                   jax.ShapeDtypeStruct((T, 2 * LANES), BF16)],
        compiler_params=_cp(("parallel",)),
        name="window_attn_prep",
    )(z, z, cos, sin)


def _ga_kernel(*refs, n_lat, tk, tq):
    if n_lat:
        q_ref, kl_ref, vl_ref, kc_ref, vc_ref, o_ref, acc_ref, m_ref, l_ref = refs
    else:
        q_ref, kc_ref, vc_ref, o_ref, acc_ref, m_ref, l_ref = refs
    acc_ref[...] = jnp.zeros_like(acc_ref)
    l_ref[...] = jnp.zeros_like(l_ref)
    m_ref[...] = jnp.full_like(m_ref, NEG)
    qt = jnp.concatenate([q_ref[0, 0:GA_HD, :], q_ref[0, GA_HD:, :]], axis=1)

    W = tq

    def scores(k):
        return tuple(jnp.dot(k, qt[:, i * W:(i + 1) * W], preferred_element_type=F32) for i in range(2))

    def softmax_pv(s_blocks, vt):
        for i, s in enumerate(s_blocks):
            cs = slice(i * W, (i + 1) * W)
            m_old = m_ref[:, cs]
            m_new = jnp.maximum(m_old, jnp.max(s, axis=0, keepdims=True))
            p = jnp.exp2(s - m_new)
            alpha = jnp.exp2(m_old - m_new)
            l_ref[:, cs] = alpha * l_ref[:, cs] + jnp.sum(p, axis=0, keepdims=True)
            acc_ref[:, cs] = alpha * acc_ref[:, cs] + jnp.dot(vt, p.astype(BF16), preferred_element_type=F32)
            m_ref[:, cs] = m_new

    if n_lat:
        def body(j, s):
            nxt = pl.multiple_of((j + 1) * tk, tk)
            s_next = scores(kl_ref[0, pl.ds(nxt, tk), :])
            softmax_pv(s, vl_ref[0, :, pl.ds(pl.multiple_of(j * tk, tk), tk)])
            return s_next
        s = lax.fori_loop(0, n_lat - 1, body, scores(kl_ref[0, 0:tk, :]))
        s_ctx = scores(kc_ref[0])
        softmax_pv(s, vl_ref[0, :, (n_lat - 1) * tk:n_lat * tk])
        softmax_pv(s_ctx, vc_ref[0])
    else:
        softmax_pv(scores(kc_ref[0]), vc_ref[0])
    o = acc_ref[...] / l_ref[...]
    for h in range(2):
        o_ref[0, :, h * GA_HD:(h + 1) * GA_HD] = o[:, h * tq:(h + 1) * tq].T.astype(o_ref.dtype)


def _global_attention(qt, kc, vtc, kl=None, vtl=None):
    B, _, Lq = qt.shape
    Lc = kc.shape[1]
    tq = min(512, Lq)
    tk = 256
    n_lat = 0 if kl is None else kl.shape[1] // tk
    W = 2 * GA_HD
    qspec = pl.BlockSpec((1, W, tq), lambda b, g, i: (b, g, i))
    kcspec = pl.BlockSpec((1, Lc, GA_HD), lambda b, g, i: (b, 0, g))
    vcspec = pl.BlockSpec((1, GA_HD, Lc), lambda b, g, i: (b, g, 0))
    if n_lat:
        Ll = kl.shape[1]
        klspec = pl.BlockSpec((1, Ll, GA_HD), lambda b, g, i: (b, 0, g))
        vlspec = pl.BlockSpec((1, GA_HD, Ll), lambda b, g, i: (b, g, 0))
        in_specs, args = [qspec, klspec, vlspec, kcspec, vcspec], (qt, kl, vtl, kc, vtc)
    else:
        in_specs, args = [qspec, kcspec, vcspec], (qt, kc, vtc)
    return pl.pallas_call(
        functools.partial(_ga_kernel, n_lat=n_lat, tk=tk, tq=tq),
        grid=(B, GA_KV, Lq // tq),
        in_specs=in_specs,
        out_specs=pl.BlockSpec((1, tq, W), lambda b, g, i: (b, i, g)),
        out_shape=jax.ShapeDtypeStruct((B, Lq, GA_HEADS * GA_HD), BF16),
        scratch_shapes=[pltpu.VMEM((GA_HD, 2 * tq), F32), pltpu.VMEM((1, 2 * tq), F32),
                        pltpu.VMEM((1, 2 * tq), F32)],
        compiler_params=_cp(("parallel", "parallel", "parallel")),
        name="global_attention",
    )(*args)


def _wa_kernel(*refs, band, nq):
    if band:
        q_ref, kp_ref, kx_ref, kn_ref, vp_ref, vx_ref, vn_ref, kc_ref, vc_ref, sink_ref, o_ref = refs
    else:
        q_ref, kc_ref, vc_ref, sink_ref, o_ref = refs
    tq = q_ref.shape[1]
    Lc = kc_ref.shape[1]
    qi = pl.program_id(1)
    rows = 4 * tq
    if band:
        nb = 3 * tq
        r = lax.broadcasted_iota(jnp.int32, (rows, nb + Lc), 0) % tq
        c = lax.broadcasted_iota(jnp.int32, (rows, nb + Lc), 1)
        lo = jnp.where(qi == 0, tq, 0)
        hi = jnp.where(qi == nq - 1, 2 * tq, nb)
        valid = ((c >= r) & (c <= r + 2 * WINDOW) & (c >= lo) & (c < hi)) | (c >= nb)
    lane = lax.broadcasted_iota(jnp.int32, (tq, LANES), 1)
    lo_half = lane < 64
    for g in range(WA_KV):
        gs = slice(g * LANES, (g + 1) * LANES)
        if band:
            k = jnp.concatenate([kp_ref[0, :, gs], kx_ref[0, :, gs], kn_ref[0, :, gs], kc_ref[0, :, gs]], axis=0)
            v = jnp.concatenate([vp_ref[0, :, gs], vx_ref[0, :, gs], vn_ref[0, :, gs], vc_ref[0, :, gs]], axis=0)
        else:
            k, v = kc_ref[0, :, gs], vc_ref[0, :, gs]
        parts = []
        for j in range(2):
            qb = q_ref[0, :, (2 * g + j) * LANES:(2 * g + j + 1) * LANES]
            parts.append(jnp.where(lo_half, qb, jnp.zeros_like(qb)))
            parts.append(jnp.where(lo_half, jnp.zeros_like(qb), qb))
        qs = jnp.concatenate(parts, axis=0)
        s = lax.dot_general(qs, k, (((1,), (1,)), ((), ())), preferred_element_type=F32)
        if band:
            s = jnp.where(valid, s, NEG)
        sink = sink_ref[g]
        m = jnp.maximum(jnp.max(s, axis=-1, keepdims=True), sink)
        p = jnp.exp(s - m)
        den = jnp.sum(p, axis=-1, keepdims=True) + jnp.exp(sink - m)
        o = jnp.dot(p.astype(BF16), v, preferred_element_type=F32) / den
        for j in range(2):
            ob = jnp.where(lo_half, o[(2 * j) * tq:(2 * j + 1) * tq], o[(2 * j + 1) * tq:(2 * j + 2) * tq])
            o_ref[0, :, (2 * g + j) * LANES:(2 * g + j + 1) * LANES] = ob.astype(o_ref.dtype)


def _window_attention(q, kc, vc, sink, kl=None, vl=None):
    B, Lq, WQ = q.shape
    Lc = kc.shape[1]
    tq = WINDOW
    nq = Lq // tq
    band = kl is not None
    sink_col = jnp.repeat(sink.astype(F32).reshape(WA_KV, WA_HEADS // WA_KV), tq, axis=1).reshape(WA_KV, 4 * tq, 1)
    qspec = pl.BlockSpec((1, tq, WQ), lambda b, i: (b, i, 0))
    cspec = pl.BlockSpec((1, Lc, 2 * LANES), lambda b, i: (b, 0, 0))
    sspec = pl.BlockSpec((WA_KV, 4 * tq, 1), lambda b, i: (0, 0, 0))
    if band:
        bp = pl.BlockSpec((1, tq, 2 * LANES), lambda b, i: (b, jnp.maximum(i - 1, 0), 0))
        bx = pl.BlockSpec((1, tq, 2 * LANES), lambda b, i: (b, i, 0))
        bn = pl.BlockSpec((1, tq, 2 * LANES), lambda b, i: (b, jnp.minimum(i + 1, nq - 1), 0))
        in_specs = [qspec, bp, bx, bn, bp, bx, bn, cspec, cspec, sspec]
        args = (q, kl, kl, kl, vl, vl, vl, kc, vc, sink_col)
    else:
        in_specs, args = [qspec, cspec, cspec, sspec], (q, kc, vc, sink_col)
    return pl.pallas_call(
        functools.partial(_wa_kernel, band=band, nq=nq),
        grid=(B, nq),
        in_specs=in_specs,
        out_specs=qspec,
        out_shape=jax.ShapeDtypeStruct(q.shape, BF16),
        compiler_params=_cp(("parallel", "parallel")),
        name="window_attention",
    )(*args)


def _mlstm_kernel(*refs, reverse, nc):
    if reverse:
        (q_ref, k_ref, v_ref, g_ref, gb_ref, c0_ref, m0_ref, hf_ref, o_ref, gn_ref,
         y_ref, cf_ref, mf_ref, c_scr, m_scr) = refs
    else:
        (q_ref, k_ref, v_ref, g_ref, gb_ref, c0_ref, m0_ref,
         y_ref, cf_ref, mf_ref, c_scr, m_scr) = refs
    T = ML_CHUNK
    d = 1 if reverse else 0
    step = pl.program_id(1)

    @pl.when(step == 0)
    def _():
        c_scr[...] = c0_ref[0]
        m_scr[...] = m0_ref[0]

    G = g_ref[0] + gb_ref[...]
    LF = jax.nn.log_sigmoid(G)
    tt = lax.broadcasted_iota(jnp.int32, (T, T), 0)
    ss = lax.broadcasted_iota(jnp.int32, (T, T), 1)
    mask = (ss >= tt) if reverse else (ss <= tt)
    tri = mask.astype(F32)
    Bc = jnp.dot(tri, LF, preferred_element_type=F32, precision=HIGHEST)
    Dr = (Bc - pltpu.roll(G, 4, 1)).T
    ones = jnp.ones((T, ML_HD), BF16)
    for h in range(ML_HEADS):
        fl, il = d * 8 + 4 + h, d * 8 + h
        hs = slice(h * ML_HD, (h + 1) * ML_HD)
        b_col = Bc[:, fl:fl + 1]
        i_col = G[:, il:il + 1]
        log_d = jnp.where(mask, b_col - Dr[fl:fl + 1, :], NEG)
        m_prev = m_scr[h, 0:1, 0:1]
        m_inter = b_col + m_prev
        m_t = jnp.maximum(m_inter, jnp.max(log_d, axis=-1, keepdims=True))
        qh, kh = q_ref[0, :, hs], k_ref[0, :, hs]
        s = lax.dot_general(qh, kh, (((1,), (1,)), ((), ())), preferred_element_type=F32)
        wqk = (s * jnp.exp(log_d - m_t)).astype(BF16)
        cs = jnp.exp(m_inter - m_t)
        vaug = jnp.concatenate([v_ref[0, :, hs], ones], axis=1)
        R = (jnp.dot(wqk, vaug, preferred_element_type=F32)
             + cs * jnp.dot(qh, c_scr[h].astype(BF16), preferred_element_type=F32))
        hh = R[:, :ML_HD] / jnp.maximum(jnp.abs(R[:, ML_HD:]), jnp.exp(-m_t))
        if reverse:
            hsum = hf_ref[0, :, hs] + hh
            hn = _rms(hsum, gn_ref[0:1, hs])
            y_ref[0, :, hs] = (jax.nn.sigmoid(o_ref[0, :, hs].astype(F32)) * hn).astype(y_ref.dtype)
        else:
            y_ref[0, :, hs] = hh
        b_end = Bc[0:1, fl:fl + 1] if reverse else Bc[T - 1:T, fl:fl + 1]
        log_w = b_end - b_col + i_col
        m_next = jnp.maximum(b_end + m_prev, jnp.max(log_w, axis=0, keepdims=True))
        w_col = jnp.exp(log_w - m_next)
        decay = jnp.exp(b_end + m_prev - m_next)
        wv = (vaug.astype(F32) * w_col).astype(BF16)
        c_scr[h] = decay * c_scr[h] + lax.dot_general(kh, wv, (((0,), (0,)), ((), ())),
                                                      preferred_element_type=F32)
        m_scr[h] = jnp.broadcast_to(m_next, (8, LANES))

    @pl.when(step == nc - 1)
    def _():
        cf_ref[0] = c_scr[...]
        mf_ref[0] = m_scr[...]


def _mlstm_scan(q, k, z, gates, gate_b, c0, m0, reverse, hf=None, norm_g=None):
    B, L, W = q.shape
    T = ML_CHUNK
    nc = L // T
    idx = (lambda b, j: (b, nc - 1 - j, 0)) if reverse else (lambda b, j: (b, j, 0))
    tok = pl.BlockSpec((1, T, W), idx)
    zv = pl.BlockSpec((1, T, W), (lambda b, j: (b, nc - 1 - j, Z_MLV // W)) if reverse
                      else (lambda b, j: (b, j, Z_MLV // W)))
    cspec = pl.BlockSpec((1, ML_HEADS, ML_HD, 2 * ML_HD), lambda b, j: (b, 0, 0, 0))
    mspec = pl.BlockSpec((1, ML_HEADS, 8, LANES), lambda b, j: (b, 0, 0, 0))
    in_specs = [tok, tok, zv, pl.BlockSpec((1, T, LANES), idx), pl.BlockSpec((1, LANES), lambda b, j: (0, 0)),
                cspec, mspec]
    args = [q, k, z, gates, gate_b, c0, m0]
    if reverse:
        in_specs += [tok, pl.BlockSpec((1, T, W), lambda b, j: (b, nc - 1 - j, Z_MLO // W)),
                     pl.BlockSpec((1, W), lambda b, j: (0, 0))]
        args += [hf, z, norm_g.reshape(1, W)]
    return pl.pallas_call(
        functools.partial(_mlstm_kernel, reverse=reverse, nc=nc),
        grid=(B, nc),
        in_specs=in_specs,
        out_specs=[tok, cspec, mspec],
        out_shape=[jax.ShapeDtypeStruct((B, L, W), BF16 if reverse else F32),
                   jax.ShapeDtypeStruct(c0.shape, F32), jax.ShapeDtypeStruct(m0.shape, F32)],
        scratch_shapes=[pltpu.VMEM((ML_HEADS, ML_HD, 2 * ML_HD), F32), pltpu.VMEM((ML_HEADS, 8, LANES), F32)],
        compiler_params=_cp(("parallel", "arbitrary")),
        name="mlstm_reverse" if reverse else "mlstm_forward",
    )(*args)


def _fft_dims(Lp):
    n1 = 2 * Lp // FFT_N2
    nt1 = Lp // FFT_N2
    nk1 = -(-(n1 // 2 + 1) // 8) * 8
    return n1, nt1, nk1


def _fft_tables(Lp):
    n1, nt1, nk1 = _fft_dims(Lp)
    N = 2 * Lp
    k1 = jnp.arange(nk1)
    t1 = jnp.arange(nt1)
    ang_a = (2.0 * math.pi / n1) * ((k1[:, None] * t1[None, :]) % n1).astype(F32)
    fa = jnp.stack([jnp.cos(ang_a), -jnp.sin(ang_a)], axis=1).reshape(2 * nk1, nt1)
    wgt = jnp.where((k1 == 0) | (k1 == n1 // 2), 1.0, 2.0) * (k1 <= n1 // 2) / N
    fai = jnp.stack([jnp.cos(ang_a) * wgt[:, None], -jnp.sin(ang_a) * wgt[:, None]], axis=1)
    fai = fai.reshape(2 * nk1, nt1).T
    k2 = jnp.arange(FFT_N2)
    t2 = jnp.arange(FFT_N2)
    idx = (t2[None, None, :] * k1[:, None, None] + n1 * t2[None, None, :] * k2[None, :, None]) % N
    phi = (2.0 * math.pi / N) * idx.astype(F32)
    gr, gi = jnp.cos(phi), -jnp.sin(phi)
    gfwd = jnp.concatenate([jnp.concatenate([gr, -gi], axis=2), jnp.concatenate([gi, gr], axis=2)], axis=1)
    grt, git = jnp.swapaxes(gr, 1, 2), jnp.swapaxes(gi, 1, 2)
    ginv = jnp.concatenate([jnp.concatenate([grt, git], axis=2), jnp.concatenate([-git, grt], axis=2)], axis=1)
    return fa.astype(BF16), fai.astype(BF16), gfwd.astype(BF16), ginv.astype(BF16)


FFT_TB = 16
FFT_CB = 512


def _fa_kernel(fa_ref, x_ref, o_ref):
    for j in range(FFT_TB):
        xj = x_ref[0, :, j, :].astype(BF16)
        o_ref[0, :, j, :] = jnp.dot(fa_ref[...], xj, preferred_element_type=F32).astype(o_ref.dtype)


def _fft_stage_a(fa, y, out_dtype):
    B, Lp, C = y.shape
    nt1 = Lp // FFT_N2
    rows = fa.shape[0]
    return pl.pallas_call(
        _fa_kernel,
        grid=(B, FFT_N2 // FFT_TB, C // FFT_CB),
        in_specs=[pl.BlockSpec((rows, nt1), lambda b, j, c: (0, 0)),
                  pl.BlockSpec((1, nt1, FFT_TB, FFT_CB), lambda b, j, c: (b, 0, j, c))],
        out_specs=pl.BlockSpec((1, rows, FFT_TB, FFT_CB), lambda b, j, c: (b, 0, j, c)),
        out_shape=jax.ShapeDtypeStruct((B, rows, FFT_N2, C), out_dtype),
        compiler_params=_cp(("parallel", "parallel", "parallel")),
        name="fft_stage_a",
    )(fa, y.reshape(B, nt1, FFT_N2, C))


def _fc_filter_kernel(g_ref, s_ref, ss_ref, h_ref, *, kb):
    C = HY_CH
    for i in range(kb):
        for n in range(HY_ORDER):
            cf, cb = (2 * n) * C, (2 * n + 1) * C
            scale = lax.rsqrt(ss_ref[0:1, cf:cf + C] + ss_ref[0:1, cb:cb + C] + EPS)
            sf = s_ref[0, 2 * i:2 * i + 2, :, cf:cf + C].reshape(2 * FFT_N2, C).astype(BF16)
            sb = s_ref[0, 2 * i:2 * i + 2, :, cb:cb + C].reshape(2 * FFT_N2, C).astype(BF16)
            xf = jnp.dot(g_ref[i], sf, preferred_element_type=F32)
            xb = jnp.dot(g_ref[i], sb, preferred_element_type=F32)
            h_ref[n, 2 * i] = (xf[:FFT_N2] + xb[:FFT_N2]) * scale
            h_ref[n, 2 * i + 1] = (xf[FFT_N2:] - xb[FFT_N2:]) * scale


def _fft_filter_spectrum(gfwd, s_filt, sumsq):
    rows = s_filt.shape[1]
    C4 = s_filt.shape[-1]
    kb = 2
    return pl.pallas_call(
        functools.partial(_fc_filter_kernel, kb=kb),
        grid=(rows // (2 * kb),),
        in_specs=[pl.BlockSpec((kb, 2 * FFT_N2, 2 * FFT_N2), lambda i: (i, 0, 0)),
                  pl.BlockSpec((1, 2 * kb, FFT_N2, C4), lambda i: (0, i, 0, 0)),
                  pl.BlockSpec((1, C4), lambda i: (0, 0))],
        out_specs=pl.BlockSpec((HY_ORDER, 2 * kb, FFT_N2, HY_CH), lambda i: (0, i, 0, 0)),
        out_shape=jax.ShapeDtypeStruct((HY_ORDER, rows, FFT_N2, HY_CH), F32),
        compiler_params=_cp(("parallel",)),
        name="fft_filter_spectrum",
    )(gfwd, s_filt, sumsq)


def _fc_kernel(g_ref, gi_ref, h_ref, s_ref, o_ref, *, kb):
    C = s_ref.shape[-1]
    for i in range(kb):
        s = s_ref[0, 2 * i:2 * i + 2].reshape(2 * FFT_N2, C)
        x = jnp.dot(g_ref[i], s, preferred_element_type=F32)
        xr, xi = x[:FFT_N2], x[FFT_N2:]
        hr, hi = h_ref[0, 2 * i], h_ref[0, 2 * i + 1]
        z = jnp.concatenate([xr * hr - xi * hi, xr * hi + xi * hr], axis=0).astype(BF16)
        bm = jnp.dot(gi_ref[i], z, preferred_element_type=F32)
        o_ref[0, 2 * i:2 * i + 2] = bm.reshape(2, FFT_N2, C).astype(o_ref.dtype)


def _fft_stage_c(gfwd, ginv, hspec, order, s):
    B, rows, _, C = s.shape
    kb = 4
    sspec = pl.BlockSpec((1, 2 * kb, FFT_N2, C), lambda i, b: (b, i, 0, 0))
    gspec = pl.BlockSpec((kb, 2 * FFT_N2, 2 * FFT_N2), lambda i, b: (i, 0, 0))
    return pl.pallas_call(
        functools.partial(_fc_kernel, kb=kb),
        grid=(rows // (2 * kb), B),
        in_specs=[gspec, gspec,
                  pl.BlockSpec((1, 2 * kb, FFT_N2, C), lambda i, b: (order, i, 0, 0)), sspec],
        out_specs=sspec,
        out_shape=jax.ShapeDtypeStruct(s.shape, BF16),
        compiler_params=_cp(("parallel", "arbitrary")),
        name="fft_stage_c",
    )(gfwd, ginv, hspec, s)


def _fai_kernel(fai_ref, b_ref, y_ref, gate_ref, skip_ref, o_ref):
    for j in range(FFT_TB):
        yf = jnp.dot(fai_ref[...], b_ref[0, :, j, :], preferred_element_type=F32)
        yin = y_ref[0, :, j, :].astype(F32)
        o_ref[0, :, j, :] = (gate_ref[0, :, j, :].astype(F32) * (yf + skip_ref[...] * yin)).astype(o_ref.dtype)


def _fft_stage_a_inv(fai, bm, y, gate, skip):
    B, Lp, C = y.shape
    nt1 = Lp // FFT_N2
    rows = fai.shape[1]
    tok = pl.BlockSpec((1, nt1, FFT_TB, FFT_CB), lambda b, j, c: (b, 0, j, c))
    out = pl.pallas_call(
        _fai_kernel,
        grid=(B, FFT_N2 // FFT_TB, C // FFT_CB),
        in_specs=[pl.BlockSpec((nt1, rows), lambda b, j, c: (0, 0)),
                  pl.BlockSpec((1, rows, FFT_TB, FFT_CB), lambda b, j, c: (b, 0, j, c)),
                  tok, tok, pl.BlockSpec((1, FFT_CB), lambda b, j, c: (0, c))],
        out_specs=tok,
        out_shape=jax.ShapeDtypeStruct((B, nt1, FFT_N2, C), BF16),
        compiler_params=_cp(("parallel", "parallel", "parallel")),
        name="fft_stage_a_inv",
    )(fai, bm, y.reshape(B, nt1, FFT_N2, C), gate.reshape(B, nt1, FFT_N2, C), skip.astype(F32).reshape(1, C))
    return out.reshape(B, Lp, C)


def _hgen_kernel(z_ref, w1_ref, b1_ref, fr_ref, w2_ref, b2_ref, w3_ref, dec_ref, h_ref, ss_ref):
    i = pl.program_id(0)
    z = z_ref[...]
    tm = z.shape[0]
    h = jnp.sin(fr_ref[0:1, :] * (jnp.dot(z, w1_ref[...], preferred_element_type=F32, precision=HIGHEST)
                                 + b1_ref[...]))
    h = jnp.sin(fr_ref[1:2, :] * (jnp.dot(h, w2_ref[...], preferred_element_type=F32, precision=HIGHEST)
                                 + b2_ref[...]))
    h = jnp.dot(h, w3_ref[...], preferred_element_type=F32, precision=HIGHEST)
    h = h * jnp.exp(-z[:, 0:1] * jnp.abs(dec_ref[...]))
    row = lax.broadcasted_iota(jnp.int32, h.shape, 0) + i * tm
    col = lax.broadcasted_iota(jnp.int32, h.shape, 1)
    h = jnp.where((row == 0) & ((col // HY_CH) % 2 == 1), 0.0, h)
    h_ref[...] = h

    @pl.when(i == 0)
    def _():
        ss_ref[...] = jnp.zeros_like(ss_ref)

    ss_ref[...] += jnp.sum(h * h, axis=0, keepdims=True)


def _hyena_filters(L, p):
    t = jnp.arange(L, dtype=F32)
    tn = t / (L - 1)
    w = 2.0 * math.pi * t / L
    bands = jnp.linspace(1e-4, HY_BANDS - 1, HY_BANDS, dtype=F32)
    ang = w[:, None] * bands[None, :]
    z = jnp.concatenate([tn[:, None], jnp.cos(ang), -jnp.sin(ang)], axis=-1)
    z = jnp.pad(z, ((0, 0), (0, LANES - HY_EMB)))
    w1 = jnp.pad(p['hy_pe_w1'].astype(F32), ((0, LANES - HY_EMB), (0, 0)))
    nf = HY_ORDER * 2 * HY_CH
    tm = min(512, L)
    const = lambda shape: pl.BlockSpec(shape, lambda i: (0,) * len(shape))
    return pl.pallas_call(
        _hgen_kernel,
        grid=(L // tm,),
        in_specs=[pl.BlockSpec((tm, LANES), lambda i: (i, 0)), const((LANES, HY_FFN)), const((1, HY_FFN)),
                  const((2, HY_FFN)), const((HY_FFN, HY_FFN)), const((1, HY_FFN)), const((HY_FFN, nf)),
                  const((1, nf))],
        out_specs=[pl.BlockSpec((tm, nf), lambda i: (i, 0)), const((1, nf))],
        out_shape=[jax.ShapeDtypeStruct((L, nf), F32), jax.ShapeDtypeStruct((1, nf), F32)],
        compiler_params=_cp(("arbitrary",)),
        name="hyena_filter_gen",
    )(z, w1, p['hy_pe_b1'].reshape(1, HY_FFN), p['hy_freq'], p['hy_pe_w2'], p['hy_pe_b2'].reshape(1, HY_FFN),
      p['hy_pe_w3'], p['hy_decay'].reshape(1, nf))


def _hyena_branch(z, p, B, L):
    v, x1, x2 = _hy_prep(z, p['hy_conv_w'], p['hy_conv_b'], L)
    Lp = max(L, 2048)
    fa, fai, gfwd, ginv = _fft_tables(Lp)
    hfilt, sumsq = _hyena_filters(L, p)
    pad3 = lambda a: a.reshape(B, L, HY_CH) if Lp == L else jnp.pad(a.reshape(B, L, HY_CH), ((0, 0), (0, Lp - L), (0, 0)))
    hf = hfilt if Lp == L else jnp.pad(hfilt, ((0, Lp - L), (0, 0)))
    s_filt = _fft_stage_a(fa, hf[None], F32)
    hspec = _fft_filter_spectrum(gfwd, s_filt, sumsq)
    y = pad3(v)
    for n, gate in enumerate((pad3(x1), pad3(x2))):
        s = _fft_stage_a(fa, y, BF16)
        bm = _fft_stage_c(gfwd, ginv, hspec, n, s)
        y = _fft_stage_a_inv(fai, bm, y, gate, p['hy_skip'][n])
    return y[:, :L].reshape(B * L, HY_CH)


def _merge_kernel(ya_ref, yb_ref, yc_ref, yd_ref, g0_ref, g1_ref, g2_ref, g3_ref, wup_ref, wout_ref,
                  x_ref, gate_ref, o_ref):
    acc = None
    for n, (y_ref, g_ref) in enumerate(((ya_ref, g0_ref), (yb_ref, g1_ref), (yc_ref, g2_ref), (yd_ref, g3_ref))):
        t = jax.nn.sigmoid(g_ref[...].astype(F32)) * jnp.dot(y_ref[...], wup_ref[n], preferred_element_type=F32)
        acc = t if acc is None else acc + t
    yl = jnp.dot(acc.astype(BF16), wout_ref[...], preferred_element_type=F32)
    o_ref[...] = x_ref[...] + gate_ref[0] * yl


def _merge(ys, z, w_up, w_out, x, gate, seq_len):
    T, D = x.shape
    tm = min(512, seq_len)
    per = seq_len // tm
    Wy = ys[0].shape[1]
    yspec = pl.BlockSpec((tm, Wy), lambda i: (i, 0))
    gspecs = [pl.BlockSpec((tm, D), functools.partial(lambda i, n: (i, Z_GATE // D + n), n=n)) for n in range(4)]
    return pl.pallas_call(
        _merge_kernel,
        grid=(T // tm,),
        in_specs=[yspec] * 4 + gspecs + [
            pl.BlockSpec((4, Wy, D), lambda i: (0, 0, 0)), pl.BlockSpec((D, D), lambda i: (0, 0)),
            pl.BlockSpec((tm, D), lambda i: (i, 0)), pl.BlockSpec((1, 1, D), lambda i: (i // per, 0, 0))],
        out_specs=pl.BlockSpec((tm, D), lambda i: (i, 0)),
        out_shape=jax.ShapeDtypeStruct((T, D), F32),
        compiler_params=_cp(("parallel",)),
        name="merge_branches",
    )(*ys, z, z, z, z, w_up, w_out, x, gate)


def _mlp_kernel(x_ref, g_ref, sc_ref, sh_ref, w1_ref, b1_ref, w2_ref, b2_ref, gate_ref, fg_ref, o_ref,
                h_ref, acc_ref, *, nk, final):
    k = pl.program_id(1)

    @pl.when(k == 0)
    def _():
        y = _rms(x_ref[...], g_ref[...])
        h_ref[...] = (y * (1.0 + sc_ref[0]) + sh_ref[0]).astype(BF16)
        acc_ref[...] = jnp.zeros_like(acc_ref)

    a = jnp.maximum(jnp.dot(h_ref[...], w1_ref[...], preferred_element_type=F32) + b1_ref[...], 0.0)
    acc_ref[...] += jnp.dot((a * a).astype(BF16), w2_ref[...], preferred_element_type=F32)

    @pl.when(k == nk - 1)
    def _():
        out = x_ref[...] + gate_ref[0] * (acc_ref[...] + b2_ref[...])
        if final:
            out = _rms(out, fg_ref[...])
        o_ref[...] = out


def _mlp(x, g, sc, sh, w1, b1, w2, b2, gate, final_g, seq_len, final):
    T, D = x.shape
    F = w1.shape[1]
    tm = min(1024, seq_len)
    per = seq_len // tm
    tk = 512
    nk = F // tk
    row = lambda i, k: (i // per, 0, 0)
    return pl.pallas_call(
        functools.partial(_mlp_kernel, nk=nk, final=final),
        grid=(T // tm, nk),
        in_specs=[pl.BlockSpec((tm, D), lambda i, k: (i, 0)), pl.BlockSpec((1, D), lambda i, k: (0, 0)),
                  pl.BlockSpec((1, 1, D), row), pl.BlockSpec((1, 1, D), row),
                  pl.BlockSpec((D, tk), lambda i, k: (0, k)), pl.BlockSpec((1, tk), lambda i, k: (0, k)),
                  pl.BlockSpec((tk, D), lambda i, k: (k, 0)), pl.BlockSpec((1, D), lambda i, k: (0, 0)),
                  pl.BlockSpec((1, 1, D), row), pl.BlockSpec((1, D), lambda i, k: (0, 0))],
        out_specs=pl.BlockSpec((tm, D), lambda i, k: (i, 0)),
        out_shape=jax.ShapeDtypeStruct((T, D), F32),
        scratch_shapes=[pltpu.VMEM((tm, D), BF16), pltpu.VMEM((tm, D), F32)],
        compiler_params=_cp(("parallel", "arbitrary")),
        name="mlp",
    )(x, g.reshape(1, D), sc, sh, w1, b1.reshape(1, F), w2, b2.reshape(1, D), gate, final_g.reshape(1, D))


def _pack_w_in(w_in):
    hy_e = 3 * HY_CH
    ga_e = hy_e + (GA_HEADS + 2 * GA_KV) * GA_HD
    mw = ML_HEADS * ML_HD
    ml_e = ga_e + 4 * mw + 16
    wa_e = ml_e + (WA_HEADS + 2 * WA_KV) * WA_HD
    hy, ga = w_in[:, :hy_e], w_in[:, hy_e:ga_e]
    ml = w_in[:, ga_e:ml_e]
    wa = w_in[:, ml_e:wa_e]
    gate = w_in[:, wa_e:]
    waq, wakv = wa[:, :WA_HEADS * WA_HD], wa[:, WA_HEADS * WA_HD:]
    pad = jnp.zeros((w_in.shape[0], Z_COLS - Z_WAKV - wakv.shape[1]), w_in.dtype)
    packed = jnp.concatenate([hy, waq, ga, ml[:, :2 * mw], ml[:, 2 * mw:3 * mw], ml[:, 3 * mw:4 * mw], gate, wakv, pad],
                             axis=1)
    wg = jnp.pad(ml[:, 4 * mw:], ((0, 0), (0, LANES - 16)))
    return packed.astype(BF16), wg


def _token_mixers(zl, zc, gl, gc, p, B, L, Lc, with_ctx_out):
    ya_l = _hyena_branch(zl, p, B, L)
    ya_c = _hyena_branch(zc, p, B, Lc) if with_ctx_out else None
    cos, sin = _rope_tables(L, GA_HD)
    ql, kl, vl = _ga_prep(zl, cos, sin, p['ga_q_g'], p['ga_k_g'], B, L, True)
    qc, kc, vc = _ga_prep(zc, cos[:Lc], sin[:Lc], p['ga_q_g'], p['ga_k_g'], B, Lc, False)
    r3 = lambda a, n: a.reshape(B, n, a.shape[-1])
    yb_l = _global_attention(ql, r3(kc, Lc), vc, r3(kl, L), vl).reshape(B * L, -1)
    yb_c = _global_attention(qc, r3(kc, Lc), vc).reshape(B * Lc, -1) if with_ctx_out else None
    mq_l, mk_l = _ml_prep(zl, p['ml_conv_w'], p['ml_conv_b'], L)
    mq_c, mk_c = _ml_prep(zc, p['ml_conv_w'], p['ml_conv_b'], Lc)
    gb = jnp.pad(p['ml_gate_b'].astype(F32), (0, LANES - 16)).reshape(1, LANES)
    c0 = jnp.zeros((B, ML_HEADS, ML_HD, 2 * ML_HD), F32)
    m0 = jnp.zeros((B, ML_HEADS, 8, LANES), F32)
    zl3, zc3, gl3, gc3 = r3(zl, L), r3(zc, Lc), r3(gl, L), r3(gc, Lc)
    h_cf, cf, mf = _mlstm_scan(r3(mq_c, Lc), r3(mk_c, Lc), zc3, gc3, gb, c0, m0, False)
    yc_c, cb, mb = _mlstm_scan(r3(mq_c, Lc), r3(mk_c, Lc), zc3, gc3, gb, c0, m0, True, h_cf, p['ml_norm_g'])
    h_lf, _, _ = _mlstm_scan(r3(mq_l, L), r3(mk_l, L), zl3, gl3, gb, cf, mf, False)
    yc_l, _, _ = _mlstm_scan(r3(mq_l, L), r3(mk_l, L), zl3, gl3, gb, cb, mb, True, h_lf, p['ml_norm_g'])
    yc_l = yc_l.reshape(B * L, -1)
    yc_c = yc_c.reshape(B * Lc, -1)
    cosw, sinw = _rope_tables(L, WA_HD)
    wq_l, wk_l, wv_l = _wa_prep(zl, cosw, sinw, L, True)
    wq_c, wk_c, wv_c = _wa_prep(zc, cosw[:Lc], sinw[:Lc], Lc, False)
    yd_l = _window_attention(r3(wq_l, L), r3(wk_c, Lc), r3(wv_c, Lc), p['wa_sink'], r3(wk_l, L), r3(wv_l, L))
    yd_l = yd_l.reshape(B * L, -1)
    yd_c = (_window_attention(r3(wq_c, Lc), r3(wk_c, Lc), r3(wv_c, Lc), p['wa_sink']).reshape(B * Lc, -1)
            if with_ctx_out else None)
    return (ya_l, yb_l, yc_l, yd_l), (ya_c, yb_c, yc_c, yd_c)


def kernel(x, c, ctx, c_ctx, w_mod, b_mod, ln1_g, ln2_g, w_in, hy_conv_w, hy_conv_b,
           hy_pe_w1, hy_pe_b1, hy_freq, hy_pe_w2, hy_pe_b2, hy_pe_w3, hy_decay, hy_skip,
           ga_q_g, ga_k_g, ml_conv_w, ml_conv_b, ml_gate_b, ml_norm_g, wa_sink, w_up, w_out,
           mlp_w1, mlp_b1, mlp_w2, mlp_b2, final_g):
    B, L, D = x.shape
    Lc = ctx.shape[1]
    R = -(-(B + 1) // 8) * 8
    cvec = jnp.zeros((R, D), F32).at[:B].set(c).at[B].set(c_ctx)
    mod = _modulation(cvec, w_mod, b_mod)
    xl = x.reshape(B * L, D)
    xc = ctx.reshape(B * Lc, D)
    for l in range(DEPTH):
        with_ctx_out = l < DEPTH - 1
        p = dict(hy_conv_w=hy_conv_w[l], hy_conv_b=hy_conv_b[l], hy_pe_w1=hy_pe_w1[l],
                 hy_pe_b1=hy_pe_b1[l], hy_freq=hy_freq[l], hy_pe_w2=hy_pe_w2[l], hy_pe_b2=hy_pe_b2[l],
                 hy_pe_w3=hy_pe_w3[l], hy_decay=hy_decay[l], hy_skip=hy_skip[l],
                 ga_q_g=ga_q_g[l], ga_k_g=ga_k_g[l], ml_conv_w=ml_conv_w[l], ml_conv_b=ml_conv_b[l],
                 ml_gate_b=ml_gate_b[l], ml_norm_g=ml_norm_g[l], wa_sink=wa_sink[l])
        ml_rows = mod[l, :B].reshape(B, 1, 6 * D)
        mc_rows = jnp.broadcast_to(mod[l, B].reshape(1, 1, 6 * D), (B, 1, 6 * D))
        part = lambda m, n: m[:, :, n * D:(n + 1) * D]
        w_pack, w_gate = _pack_w_in(w_in[l])
        zl = _normmod_matmul(xl, ln1_g[l], part(ml_rows, 1), part(ml_rows, 0), w_pack, L, Z_TN, BF16)
        zc = _normmod_matmul(xc, ln1_g[l], part(mc_rows, 1), part(mc_rows, 0), w_pack, Lc, Z_TN, BF16)
        gl = _normmod_matmul(xl, ln1_g[l], part(ml_rows, 1), part(ml_rows, 0), w_gate.astype(BF16), L, LANES, F32)
        gc = _normmod_matmul(xc, ln1_g[l], part(mc_rows, 1), part(mc_rows, 0), w_gate.astype(BF16), Lc, LANES, F32)
        ys_l, ys_c = _token_mixers(zl, zc, gl, gc, p, B, L, Lc, with_ctx_out)
        wup = w_up[l].astype(BF16)
        wout = w_out[l].astype(BF16)
        w1, w2 = mlp_w1[l].astype(BF16), mlp_w2[l].astype(BF16)
        xl = _merge(ys_l, zl, wup, wout, xl, part(ml_rows, 2), L)
        xl = _mlp(xl, ln2_g[l], part(ml_rows, 4), part(ml_rows, 3), w1, mlp_b1[l], w2, mlp_b2[l],
                  part(ml_rows, 5), final_g, L, final=(l == DEPTH - 1))
        if with_ctx_out:
            xc = _merge(ys_c, zc, wup, wout, xc, part(mc_rows, 2), Lc)
            xc = _mlp(xc, ln2_g[l], part(mc_rows, 4), part(mc_rows, 3), w1, mlp_b1[l], w2, mlp_b2[l],
                      part(mc_rows, 5), final_g, Lc, final=False)
    return xl.reshape(B, L, D)
```

```python
import functools
import math

import jax
import jax.numpy as jnp
import numpy as np
from jax import lax
from jax.experimental import pallas as pl
from jax.experimental.pallas import tpu as pltpu

F32 = jnp.float32
BF16 = jnp.bfloat16
HIGHEST = lax.Precision.HIGHEST

D_MODEL = 1024
DEPTH = 2
GRID_W = 64
HY_CH = 512
HY_ORDER = 2
HY_BANDS = 16
HY_EMB = 1 + 2 * HY_BANDS
HY_FFN = 64
GA_HEADS, GA_KV, GA_HD = 4, 2, 128
ML_HEADS, ML_HD = 4, 128
WA_HEADS, WA_KV, WA_HD = 8, 2, 64
WINDOW = 128
ROPE_BASE = 10000.0
D_FF = 4 * D_MODEL
EPS = 1e-6
NEG = -1e30
LOG2E = 1.4426950408889634

LANES = 128
V7X_VMEM_LIMIT = 48 * 1024 * 1024

Z_HY = 0
Z_WAQ = 1536
Z_GA = 2048
Z_MLQK = 3072
Z_MLV = 4096
Z_MLO = 4608
Z_GATE = 5120
Z_WAKV = 9216
Z_COLS = 9728
Z_TN = 2432

ML_CHUNK = 128
FFT_N2 = 128
GA_CB = 256


def _cp(sem, vmem=V7X_VMEM_LIMIT):
    return pltpu.CompilerParams(dimension_semantics=sem, vmem_limit_bytes=vmem)


def _rms(x, g):
    return x * lax.rsqrt(jnp.mean(x * x, axis=-1, keepdims=True) + EPS) * g


def _mod_kernel(c_ref, w_ref, b_ref, o_ref):
    c = c_ref[...]
    s = c * jax.nn.sigmoid(c)
    o_ref[0] = jnp.dot(s, w_ref[0], preferred_element_type=F32, precision=HIGHEST) + b_ref[0]


def _modulation(cvec, w_mod, b_mod):
    R = cvec.shape[0]
    tn = 1536
    return pl.pallas_call(
        _mod_kernel,
        grid=(DEPTH, 6 * D_MODEL // tn),
        in_specs=[pl.BlockSpec((R, D_MODEL), lambda l, j: (0, 0)),
                  pl.BlockSpec((1, D_MODEL, tn), lambda l, j: (l, 0, j)),
                  pl.BlockSpec((1, 1, tn), lambda l, j: (l, 0, j))],
        out_specs=pl.BlockSpec((1, R, tn), lambda l, j: (l, 0, j)),
        out_shape=jax.ShapeDtypeStruct((DEPTH, R, 6 * D_MODEL), F32),
        compiler_params=_cp(("parallel", "parallel")),
        name="modulation",
    )(cvec, w_mod, b_mod.reshape(DEPTH, 1, 6 * D_MODEL))


def _nmm_kernel(x_ref, g_ref, sc_ref, sh_ref, w_ref, o_ref, h_ref):
    @pl.when(pl.program_id(1) == 0)
    def _():
        y = _rms(x_ref[...], g_ref[...])
        h_ref[...] = (y * (1.0 + sc_ref[0]) + sh_ref[0]).astype(BF16)

    o_ref[...] = jnp.dot(h_ref[...], w_ref[...], preferred_element_type=F32).astype(o_ref.dtype)


def _normmod_matmul(x, g, sc, sh, w, seq_len, tn, out_dtype):
    T, D = x.shape
    N = w.shape[1]
    tm = min(1024, seq_len)
    per = seq_len // tm
    return pl.pallas_call(
        _nmm_kernel,
        grid=(T // tm, N // tn),
        in_specs=[pl.BlockSpec((tm, D), lambda i, j: (i, 0)),
                  pl.BlockSpec((1, D), lambda i, j: (0, 0)),
                  pl.BlockSpec((1, 1, D), lambda i, j: (i // per, 0, 0)),
                  pl.BlockSpec((1, 1, D), lambda i, j: (i // per, 0, 0)),
                  pl.BlockSpec((D, tn), lambda i, j: (0, j))],
        out_specs=pl.BlockSpec((tm, tn), lambda i, j: (i, j)),
        out_shape=jax.ShapeDtypeStruct((T, N), out_dtype),
        scratch_shapes=[pltpu.VMEM((tm, D), BF16)],
        compiler_params=_cp(("parallel", "arbitrary")),
        name="normmod_matmul",
    )(x, g.reshape(1, D), sc, sh, w)


def _conv3(u, prev_row, next_row, w_ref, b_ref, c0, c1):
    tm = u.shape[0]
    row = lax.broadcasted_iota(jnp.int32, u.shape, 0)
    up = jnp.where(row == 0, prev_row, pltpu.roll(u, 1, 0))
    dn = jnp.where(row == tm - 1, next_row, pltpu.roll(u, tm - 1, 0))
    return (w_ref[0:1, c0:c1] * up + w_ref[1:2, c0:c1] * u + w_ref[2:3, c0:c1] * dn + b_ref[0:1, c0:c1])


def _halo_rows(zp_ref, zn_ref, per, c0, c1):
    i = pl.program_id(0)
    first = (i % per) == 0
    last = (i % per) == per - 1
    hp = zp_ref.shape[0]
    prev_row = jnp.where(first, 0.0, zp_ref[hp - 1:hp, c0:c1].astype(F32))
    next_row = jnp.where(last, 0.0, zn_ref[0:1, c0:c1].astype(F32))
    return prev_row, next_row


HALO = 16


def _halo_specs(tm, width, col_block, n_rows):
    nb = n_rows // HALO
    r = tm // HALO
    return [pl.BlockSpec((tm, width), lambda i: (i, col_block)),
            pl.BlockSpec((HALO, width), lambda i: (jnp.maximum(i * r - 1, 0), col_block)),
            pl.BlockSpec((HALO, width), lambda i: (jnp.minimum((i + 1) * r, nb - 1), col_block))]


def _hy_prep_kernel(z_ref, zp_ref, zn_ref, w_ref, b_ref, v_ref, x1_ref, x2_ref, *, per):
    outs = (v_ref, x1_ref, x2_ref)
    for c in range(3):
        c0, c1 = c * HY_CH, (c + 1) * HY_CH
        prev_row, next_row = _halo_rows(zp_ref, zn_ref, per, c0, c1)
        u = z_ref[:, c0:c1].astype(F32)
        outs[c][...] = _conv3(u, prev_row, next_row, w_ref, b_ref, c0, c1).astype(outs[c].dtype)


def _hy_prep(z, conv_w, conv_b, seq_len):
    T = z.shape[0]
    tm = min(512, seq_len)
    W = 3 * HY_CH
    out = jax.ShapeDtypeStruct((T, HY_CH), BF16)
    return pl.pallas_call(
        functools.partial(_hy_prep_kernel, per=seq_len // tm),
        grid=(T // tm,),
        in_specs=_halo_specs(tm, W, Z_HY // W, T) + [
            pl.BlockSpec((3, W), lambda i: (0, 0)), pl.BlockSpec((1, W), lambda i: (0, 0))],
        out_specs=[pl.BlockSpec((tm, HY_CH), lambda i: (i, 0))] * 3,
        out_shape=[out, out, out],
        compiler_params=_cp(("parallel",)),
        name="hyena_prep",
    )(z, z, z, conv_w, conv_b.reshape(1, W))


def _ml_prep_kernel(z_ref, zp_ref, zn_ref, zv_ref, w_ref, b_ref, q_ref, k_ref, v_ref, *, per):
    W = ML_HEADS * ML_HD
    for c in range(2):
        c0, c1 = c * W, (c + 1) * W
        prev_row, next_row = _halo_rows(zp_ref, zn_ref, per, c0, c1)
        u = z_ref[:, c0:c1].astype(F32)
        y = _conv3(u, prev_row, next_row, w_ref, b_ref, c0, c1)
        y = y * jax.nn.sigmoid(y)
        if c == 0:
            for h in range(ML_HEADS):
                q_ref[0, h * ML_HD:(h + 1) * ML_HD, :] = y[:, h * ML_HD:(h + 1) * ML_HD].T.astype(q_ref.dtype)
        else:
            k_ref[...] = (y * (ML_HD ** -0.5)).astype(k_ref.dtype)
    for h in range(ML_HEADS):
        v_ref[0, h * ML_HD:(h + 1) * ML_HD, :] = zv_ref[:, h * ML_HD:(h + 1) * ML_HD].astype(F32).T.astype(v_ref.dtype)


def _ml_prep(z, conv_w, conv_b, B, seq_len):
    T = z.shape[0]
    tm = min(512, seq_len)
    per = seq_len // tm
    W = 2 * ML_HEADS * ML_HD
    Wh = W // 2
    tspec = pl.BlockSpec((1, Wh, tm), lambda i: (i // per, 0, i % per))
    tshape = jax.ShapeDtypeStruct((B, Wh, seq_len), BF16)
    return pl.pallas_call(
        functools.partial(_ml_prep_kernel, per=per),
        grid=(T // tm,),
        in_specs=_halo_specs(tm, W, Z_MLQK // W, T) + [
            pl.BlockSpec((tm, Wh), lambda i: (i, Z_MLV // Wh)),
            pl.BlockSpec((3, W), lambda i: (0, 0)), pl.BlockSpec((1, W), lambda i: (0, 0))],
        out_specs=[tspec, pl.BlockSpec((tm, Wh), lambda i: (i, 0)), tspec],
        out_shape=[tshape, jax.ShapeDtypeStruct((T, Wh), BF16), tshape],
        compiler_params=_cp(("parallel",)),
        name="mlstm_prep",
    )(z, z, z, z, conv_w, conv_b.reshape(1, W))


def _rope_tables(L, hd):
    quarter = hd // 4
    inv = ROPE_BASE ** (-jnp.arange(quarter, dtype=F32) / quarter)
    t = jnp.arange(L)
    row = (t // GRID_W).astype(F32)
    col = (t % GRID_W).astype(F32)
    lane = jnp.arange(LANES)
    within = lane % hd
    is_col = (within // (hd // 2)) == 1
    second = ((within % (hd // 2)) // quarter) == 1
    j = within % quarter
    pos = jnp.where(is_col[None, :], col[:, None], row[:, None])
    ang = pos * inv[j][None, :]
    return jnp.cos(ang), jnp.where(second[None, :], jnp.sin(ang), -jnp.sin(ang))


def _rope(x, cos, sin, quarter):
    lane = lax.broadcasted_iota(jnp.int32, x.shape, 1)
    first = ((lane % (2 * quarter)) // quarter) == 0
    partner = jnp.where(first, pltpu.roll(x, LANES - quarter, 1), pltpu.roll(x, quarter, 1))
    return x * cos + partner * sin


def _ga_prep_kernel(z_ref, cos_ref, sin_ref, qg_ref, kg_ref, q_ref, k_ref, v_ref, *, rope):
    nq, nk = GA_HEADS, GA_KV
    for h in range(nq + nk):
        x = z_ref[:, h * GA_HD:(h + 1) * GA_HD].astype(F32)
        g = qg_ref[...] if h < nq else kg_ref[...]
        y = _rms(x, g)
        if rope:
            y = _rope(y, cos_ref[...], sin_ref[...], GA_HD // 4)
        if h < nq:
            q_ref[0, h * GA_HD:(h + 1) * GA_HD, :] = (y * (GA_HD ** -0.5 * LOG2E)).T.astype(q_ref.dtype)
        else:
            k_ref[:, (h - nq) * GA_HD:(h - nq + 1) * GA_HD] = y.astype(k_ref.dtype)
    for h in range(nk):
        v = z_ref[:, (nq + nk + h) * GA_HD:(nq + nk + h + 1) * GA_HD].astype(F32)
        v_ref[0, h * GA_HD:(h + 1) * GA_HD, :] = v.T.astype(v_ref.dtype)


def _ga_prep(z, cos, sin, qg, kg, B, seq_len, rope):
    T = z.shape[0]
    tm = min(512, seq_len)
    per = seq_len // tm
    W = (GA_HEADS + 2 * GA_KV) * GA_HD
    return pl.pallas_call(
        functools.partial(_ga_prep_kernel, rope=rope),
        grid=(T // tm,),
        in_specs=[pl.BlockSpec((tm, W), lambda i: (i, Z_GA // W)),
                  pl.BlockSpec((tm, LANES), lambda i: (i % per, 0)),
                  pl.BlockSpec((tm, LANES), lambda i: (i % per, 0)),
                  pl.BlockSpec((1, GA_HD), lambda i: (0, 0)),
                  pl.BlockSpec((1, GA_HD), lambda i: (0, 0))],
        out_specs=[pl.BlockSpec((1, GA_HEADS * GA_HD, tm), lambda i: (i // per, 0, i % per)),
                   pl.BlockSpec((tm, GA_KV * GA_HD), lambda i: (i, 0)),
                   pl.BlockSpec((1, GA_KV * GA_HD, tm), lambda i: (i // per, 0, i % per))],
        out_shape=[jax.ShapeDtypeStruct((B, GA_HEADS * GA_HD, seq_len), BF16),
                   jax.ShapeDtypeStruct((T, GA_KV * GA_HD), BF16),
                   jax.ShapeDtypeStruct((B, GA_KV * GA_HD, seq_len), BF16)],
        compiler_params=_cp(("parallel",)),
        name="global_attn_prep",
    )(z, cos, sin, qg.reshape(1, GA_HD), kg.reshape(1, GA_HD))


def _dup_halves(x):
    lane = lax.broadcasted_iota(jnp.int32, x.shape, 1)
    sw = pltpu.roll(x, 64, 1)
    lo = lane < 64
    return jnp.where(lo, x, sw), jnp.where(lo, sw, x)


def _wa_prep_kernel(zq_ref, zkv_ref, cos_ref, sin_ref, q_ref, k_ref, v_ref, *, rope):
    quarter = WA_HD // 4
    for j in range(WA_HEADS * WA_HD // LANES):
        x = zq_ref[:, j * LANES:(j + 1) * LANES].astype(F32)
        if rope:
            x = _rope(x, cos_ref[...], sin_ref[...], quarter)
        q_ref[:, j * LANES:(j + 1) * LANES] = (x * (WA_HD ** -0.5)).astype(q_ref.dtype)
    k = zkv_ref[:, 0:LANES].astype(F32)
    if rope:
        k = _rope(k, cos_ref[...], sin_ref[...], quarter)
    k0, k1 = _dup_halves(k)
    k_ref[:, 0:LANES] = k0.astype(k_ref.dtype)
    k_ref[:, LANES:] = k1.astype(k_ref.dtype)
    v0, v1 = _dup_halves(zkv_ref[:, LANES:].astype(F32))
    v_ref[:, 0:LANES] = v0.astype(v_ref.dtype)
    v_ref[:, LANES:] = v1.astype(v_ref.dtype)


def _wa_prep(z, cos, sin, seq_len, rope):
    T = z.shape[0]
    tm = min(512, seq_len)
    per = seq_len // tm
    WQ = WA_HEADS * WA_HD
    return pl.pallas_call(
        functools.partial(_wa_prep_kernel, rope=rope),
        grid=(T // tm,),
        in_specs=[pl.BlockSpec((tm, WQ), lambda i: (i, Z_WAQ // WQ)),
                  pl.BlockSpec((tm, 2 * LANES), lambda i: (i, Z_WAKV // (2 * LANES))),
                  pl.BlockSpec((tm, LANES), lambda i: (i % per, 0)),
                  pl.BlockSpec((tm, LANES), lambda i: (i % per, 0))],
        out_specs=[pl.BlockSpec((tm, WQ), lambda i: (i, 0)),
                   pl.BlockSpec((tm, 2 * LANES), lambda i: (i, 0)),
                   pl.BlockSpec((tm, 2 * LANES), lambda i: (i, 0))],
        out_shape=[jax.ShapeDtypeStruct((T, WQ), BF16),
                   jax.ShapeDtypeStruct((T, 2 * LANES), BF16),
                   jax.ShapeDtypeStruct((T, 2 * LANES), BF16)],
        compiler_params=_cp(("parallel",)),
        name="window_attn_prep",
    )(z, z, cos, sin)


def _ga_kernel(q_ref, k_ref, vt_ref, o_ref, acc_ref, m_ref, l_ref, *, nchunks, tk, tq):
    acc_ref[...] = jnp.zeros_like(acc_ref)
    l_ref[...] = jnp.zeros_like(l_ref)
    m_ref[...] = jnp.full_like(m_ref, NEG)
    qt = jnp.concatenate([q_ref[0, 0:GA_HD, :], q_ref[0, GA_HD:, :]], axis=1)

    W = GA_CB
    nblk = 2 * tq // W

    def scores(k):
        return tuple(jnp.dot(k, qt[:, i * W:(i + 1) * W], preferred_element_type=F32) for i in range(nblk))

    def softmax_pv(s_blocks, vt):
        for i, s in enumerate(s_blocks):
            cs = slice(i * W, (i + 1) * W)
            m_old = m_ref[:, cs]
            m_new = jnp.maximum(m_old, jnp.max(s, axis=0, keepdims=True))
            p = jnp.exp2(s - m_new)
            alpha = jnp.exp2(m_old - m_new)
            l_ref[:, cs] = alpha * l_ref[:, cs] + jnp.sum(p, axis=0, keepdims=True)
            acc_ref[:, cs] = alpha * acc_ref[:, cs] + jnp.dot(vt, p.astype(BF16), preferred_element_type=F32)
            m_ref[:, cs] = m_new

    def body(j, s):
        s_next = scores(k_ref[0, pl.ds(pl.multiple_of((j + 1) * tk, tk), tk), :])
        softmax_pv(s, vt_ref[0, :, pl.ds(pl.multiple_of(j * tk, tk), tk)])
        return s_next
    s = scores(k_ref[0, 0:tk, :])
    if nchunks > 1:
        s = lax.fori_loop(0, nchunks - 1, body, s, unroll=4)
    softmax_pv(s, vt_ref[0, :, (nchunks - 1) * tk:nchunks * tk])
    o = acc_ref[...] / l_ref[...]
    for h in range(2):
        o_ref[0, :, h * GA_HD:(h + 1) * GA_HD] = o[:, h * tq:(h + 1) * tq].T.astype(o_ref.dtype)


def _global_attention(qt, k, vt):
    B, _, Lq = qt.shape
    Lk = k.shape[1]
    tq = min(512, Lq)
    tk = 256
    W = 2 * GA_HD
    return pl.pallas_call(
        functools.partial(_ga_kernel, nchunks=Lk // tk, tk=tk, tq=tq),
        grid=(B, GA_KV, Lq // tq),
        in_specs=[pl.BlockSpec((1, W, tq), lambda b, g, i: (b, g, i)),
                  pl.BlockSpec((1, Lk, GA_HD), lambda b, g, i: (b, 0, g)),
                  pl.BlockSpec((1, GA_HD, Lk), lambda b, g, i: (b, g, 0))],
        out_specs=pl.BlockSpec((1, tq, W), lambda b, g, i: (b, i, g)),
        out_shape=jax.ShapeDtypeStruct((B, Lq, GA_HEADS * GA_HD), BF16),
        scratch_shapes=[pltpu.VMEM((GA_HD, 2 * tq), F32), pltpu.VMEM((1, 2 * tq), F32),
                        pltpu.VMEM((1, 2 * tq), F32)],
        compiler_params=_cp(("parallel", "parallel", "parallel")),
        name="global_attention",
    )(qt, k, vt)


def _wa_kernel(*refs, band, nq):
    if band:
        q_ref, kp_ref, kx_ref, kn_ref, vp_ref, vx_ref, vn_ref, kc_ref, vc_ref, sink_ref, o_ref = refs
    else:
        q_ref, kc_ref, vc_ref, sink_ref, o_ref = refs
    tq = q_ref.shape[1]
    Lc = kc_ref.shape[1]
    qi = pl.program_id(1)
    rows = 4 * tq
    if band:
        nb = 3 * tq
        r = lax.broadcasted_iota(jnp.int32, (rows, nb + Lc), 0) % tq
        c = lax.broadcasted_iota(jnp.int32, (rows, nb + Lc), 1)
        lo = jnp.where(qi == 0, tq, 0)
        hi = jnp.where(qi == nq - 1, 2 * tq, nb)
        valid = ((c >= r) & (c <= r + 2 * WINDOW) & (c >= lo) & (c < hi)) | (c >= nb)
    lane = lax.broadcasted_iota(jnp.int32, (tq, LANES), 1)
    lo_half = lane < 64
    for g in range(WA_KV):
        gs = slice(g * LANES, (g + 1) * LANES)
        if band:
            k = jnp.concatenate([kp_ref[0, :, gs], kx_ref[0, :, gs], kn_ref[0, :, gs], kc_ref[0, :, gs]], axis=0)
            v = jnp.concatenate([vp_ref[0, :, gs], vx_ref[0, :, gs], vn_ref[0, :, gs], vc_ref[0, :, gs]], axis=0)
        else:
            k, v = kc_ref[0, :, gs], vc_ref[0, :, gs]
        parts = []
        for j in range(2):
            qb = q_ref[0, :, (2 * g + j) * LANES:(2 * g + j + 1) * LANES]
            parts.append(jnp.where(lo_half, qb, jnp.zeros_like(qb)))
            parts.append(jnp.where(lo_half, jnp.zeros_like(qb), qb))
        qs = jnp.concatenate(parts, axis=0)
        s = lax.dot_general(qs, k, (((1,), (1,)), ((), ())), preferred_element_type=F32)
        if band:
            s = jnp.where(valid, s, NEG)
        sink = sink_ref[g]
        m = jnp.maximum(jnp.max(s, axis=-1, keepdims=True), sink)
        p = jnp.exp(s - m)
        den = jnp.sum(p, axis=-1, keepdims=True) + jnp.exp(sink - m)
        o = jnp.dot(p.astype(BF16), v, preferred_element_type=F32) / den
        for j in range(2):
            ob = jnp.where(lo_half, o[(2 * j) * tq:(2 * j + 1) * tq], o[(2 * j + 1) * tq:(2 * j + 2) * tq])
            o_ref[0, :, (2 * g + j) * LANES:(2 * g + j + 1) * LANES] = ob.astype(o_ref.dtype)


def _window_attention(q, kc, vc, sink, kl=None, vl=None):
    B, Lq, WQ = q.shape
    Lc = kc.shape[1]
    tq = WINDOW
    nq = Lq // tq
    band = kl is not None
    sink_col = jnp.repeat(sink.astype(F32).reshape(WA_KV, WA_HEADS // WA_KV), tq, axis=1).reshape(WA_KV, 4 * tq, 1)
    qspec = pl.BlockSpec((1, tq, WQ), lambda b, i: (b, i, 0))
    cspec = pl.BlockSpec((1, Lc, 2 * LANES), lambda b, i: (b, 0, 0))
    sspec = pl.BlockSpec((WA_KV, 4 * tq, 1), lambda b, i: (0, 0, 0))
    if band:
        bp = pl.BlockSpec((1, tq, 2 * LANES), lambda b, i: (b, jnp.maximum(i - 1, 0), 0))
        bx = pl.BlockSpec((1, tq, 2 * LANES), lambda b, i: (b, i, 0))
        bn = pl.BlockSpec((1, tq, 2 * LANES), lambda b, i: (b, jnp.minimum(i + 1, nq - 1), 0))
        in_specs = [qspec, bp, bx, bn, bp, bx, bn, cspec, cspec, sspec]
        args = (q, kl, kl, kl, vl, vl, vl, kc, vc, sink_col)
    else:
        in_specs, args = [qspec, cspec, cspec, sspec], (q, kc, vc, sink_col)
    return pl.pallas_call(
        functools.partial(_wa_kernel, band=band, nq=nq),
        grid=(B, nq),
        in_specs=in_specs,
        out_specs=qspec,
        out_shape=jax.ShapeDtypeStruct(q.shape, BF16),
        compiler_params=_cp(("parallel", "parallel")),
        name="window_attention",
    )(*args)


def _mlstm_kernel(*refs, reverse, nc, bb):
    if reverse:
        (qt_ref, k_ref, vt_ref, g_ref, gb_ref, c0_ref, m0_ref, hft_ref, o_ref, gn_ref,
         y_ref, cf_ref, mf_ref, c_scr, m_scr) = refs
    else:
        (qt_ref, k_ref, vt_ref, g_ref, gb_ref, c0_ref, m0_ref,
         y_ref, cf_ref, mf_ref, c_scr, m_scr) = refs
    T = ML_CHUNK
    d = 1 if reverse else 0
    step = pl.program_id(1)

    @pl.when(step == 0)
    def _():
        c_scr[...] = c0_ref[...]
        m_scr[...] = m0_ref[...]

    si = lax.broadcasted_iota(jnp.int32, (T, T), 0)
    ti = lax.broadcasted_iota(jnp.int32, (T, T), 1)
    tri = ((ti >= si) if reverse else (ti <= si)).astype(F32)
    mask_t = (si >= ti) if reverse else (si <= ti)
    ones_rows = jnp.ones((ML_HD, T), BF16)
    e_last = 0 if reverse else T - 1
    for bi in range(bb):
        G = g_ref[bi] + gb_ref[...]
        LF = jax.nn.log_sigmoid(G)
        Bc = jnp.dot(tri, LF, preferred_element_type=F32, precision=HIGHEST)
        GT = G.T
        BT = Bc.T
        Dc = Bc - pltpu.roll(G, 4, 1)
        for h in range(ML_HEADS):
            fl, il = d * 8 + 4 + h, d * 8 + h
            hs = slice(h * ML_HD, (h + 1) * ML_HD)
            b_row, i_row = BT[fl:fl + 1, :], GT[il:il + 1, :]
            log_d = jnp.where(mask_t, b_row - Dc[:, fl:fl + 1], NEG)
            m_prev = m_scr[bi, h, 0:1, 0:1]
            m_inter = b_row + m_prev
            m_t = jnp.maximum(m_inter, jnp.max(log_d, axis=0, keepdims=True))
            kh, qt = k_ref[bi, :, hs], qt_ref[bi, hs, :]
            st = jnp.dot(kh, qt, preferred_element_type=F32)
            wqk = (st * jnp.exp(log_d - m_t)).astype(BF16)
            cs = jnp.exp(m_inter - m_t)
            vaug = jnp.concatenate([vt_ref[bi, hs, :], ones_rows], axis=0)
            R = (jnp.dot(vaug, wqk, preferred_element_type=F32)
                 + cs * jnp.dot(c_scr[bi, h].astype(BF16), qt, preferred_element_type=F32))
            hh = R[:ML_HD] / jnp.maximum(jnp.abs(R[ML_HD:]), jnp.exp(-m_t))
            if reverse:
                hsum = hft_ref[bi, hs, :] + hh
                hn = hsum * lax.rsqrt(jnp.mean(hsum * hsum, axis=0, keepdims=True) + EPS) * gn_ref[hs, :]
                y_ref[bi, :, hs] = (jax.nn.sigmoid(o_ref[bi, :, hs].astype(F32)) * hn.T).astype(y_ref.dtype)
            else:
                y_ref[bi, hs, :] = hh
            b_end = BT[fl:fl + 1, e_last:e_last + 1]
            log_w = b_end - b_row + i_row
            m_next = jnp.maximum(b_end + m_prev, jnp.max(log_w, axis=1, keepdims=True))
            w_row = jnp.exp(log_w - m_next)
            decay = jnp.exp(b_end + m_prev - m_next)
            wv = (vaug.astype(F32) * w_row).astype(BF16)
            c_scr[bi, h] = decay * c_scr[bi, h] + jnp.dot(wv, kh, preferred_element_type=F32)
            m_scr[bi, h] = jnp.broadcast_to(m_next, (8, LANES))

    @pl.when(step == nc - 1)
    def _():
        cf_ref[...] = c_scr[...]
        mf_ref[...] = m_scr[...]


def _mlstm_scan(qt, k, vt, z, gates, gate_b, c0, m0, reverse, hft=None, norm_g=None):
    B, L, W = k.shape
    T = ML_CHUNK
    nc = L // T
    bb = 2
    cj = (lambda j: nc - 1 - j) if reverse else (lambda j: j)
    tok = pl.BlockSpec((bb, T, W), lambda b, j: (b, cj(j), 0))
    ttok = pl.BlockSpec((bb, W, T), lambda b, j: (b, 0, cj(j)))
    cspec = pl.BlockSpec((bb, ML_HEADS, 2 * ML_HD, ML_HD), lambda b, j: (b, 0, 0, 0))
    mspec = pl.BlockSpec((bb, ML_HEADS, 8, LANES), lambda b, j: (b, 0, 0, 0))
    in_specs = [ttok, tok, ttok, pl.BlockSpec((bb, T, LANES), lambda b, j: (b, cj(j), 0)),
                pl.BlockSpec((1, LANES), lambda b, j: (0, 0)), cspec, mspec]
    args = [qt, k, vt, gates, gate_b, c0, m0]
    if reverse:
        in_specs += [ttok, pl.BlockSpec((bb, T, W), lambda b, j: (b, cj(j), Z_MLO // W)),
                     pl.BlockSpec((W, T), lambda b, j: (0, 0))]
        args += [hft, z, jnp.broadcast_to(norm_g.astype(F32).reshape(W, 1), (W, T))]
    return pl.pallas_call(
        functools.partial(_mlstm_kernel, reverse=reverse, nc=nc, bb=bb),
        grid=(B // bb, nc),
        in_specs=in_specs,
        out_specs=[tok if reverse else ttok, cspec, mspec],
        out_shape=[jax.ShapeDtypeStruct((B, L, W), BF16) if reverse else jax.ShapeDtypeStruct((B, W, L), F32),
                   jax.ShapeDtypeStruct(c0.shape, F32), jax.ShapeDtypeStruct(m0.shape, F32)],
        scratch_shapes=[pltpu.VMEM((bb, ML_HEADS, 2 * ML_HD, ML_HD), F32), pltpu.VMEM((bb, ML_HEADS, 8, LANES), F32)],
        compiler_params=_cp(("parallel", "arbitrary")),
        name="mlstm_reverse" if reverse else "mlstm_forward",
    )(*args)


def _fft_dims(Lp):
    n1 = 2 * Lp // FFT_N2
    nt1 = Lp // FFT_N2
    nk1 = -(-(n1 // 2 + 1) // 8) * 8
    return n1, nt1, nk1


def _fft_tables(Lp):
    n1, nt1, nk1 = _fft_dims(Lp)
    N = 2 * Lp
    k1 = jnp.arange(nk1)
    t1 = jnp.arange(nt1)
    ang_a = (2.0 * math.pi / n1) * ((k1[:, None] * t1[None, :]) % n1).astype(F32)
    fa = jnp.stack([jnp.cos(ang_a), -jnp.sin(ang_a)], axis=1).reshape(2 * nk1, nt1)
    wgt = jnp.where((k1 == 0) | (k1 == n1 // 2), 1.0, 2.0) * (k1 <= n1 // 2) / N
    fai = jnp.stack([jnp.cos(ang_a) * wgt[:, None], -jnp.sin(ang_a) * wgt[:, None]], axis=1)
    fai = fai.reshape(2 * nk1, nt1).T
    k2 = jnp.arange(FFT_N2)
    t2 = jnp.arange(FFT_N2)
    idx = (t2[None, None, :] * k1[:, None, None] + n1 * t2[None, None, :] * k2[None, :, None]) % N
    phi = (2.0 * math.pi / N) * idx.astype(F32)
    gr, gi = jnp.cos(phi), -jnp.sin(phi)
    gfwd = jnp.concatenate([jnp.concatenate([gr, -gi], axis=2), jnp.concatenate([gi, gr], axis=2)], axis=1)
    grt, git = jnp.swapaxes(gr, 1, 2), jnp.swapaxes(gi, 1, 2)
    ginv = jnp.concatenate([jnp.concatenate([grt, git], axis=2), jnp.concatenate([-git, grt], axis=2)], axis=1)
    eye = jnp.eye(FFT_TB, dtype=F32)
    return (jnp.kron(fa, eye).astype(BF16), jnp.kron(fai, eye).astype(BF16), gfwd.astype(BF16), ginv.astype(BF16))


FFT_TB = 16
FFT_CB = 512


def _fa_kernel(fa_ref, x_ref, o_ref):
    nt1, tb, cb = x_ref.shape[1:]
    x = x_ref[0].reshape(nt1 * tb, cb).astype(BF16)
    r = jnp.dot(fa_ref[...], x, preferred_element_type=F32)
    o_ref[0] = r.reshape(o_ref.shape[1], tb, cb).astype(o_ref.dtype)


def _fft_stage_a(fa, y, out_dtype):
    B, Lp, C = y.shape
    nt1 = Lp // FFT_N2
    rows = fa.shape[0] // FFT_TB
    return pl.pallas_call(
        _fa_kernel,
        grid=(B, FFT_N2 // FFT_TB, C // FFT_CB),
        in_specs=[pl.BlockSpec(fa.shape, lambda b, j, c: (0, 0)),
                  pl.BlockSpec((1, nt1, FFT_TB, FFT_CB), lambda b, j, c: (b, 0, j, c))],
        out_specs=pl.BlockSpec((1, rows, FFT_TB, FFT_CB), lambda b, j, c: (b, 0, j, c)),
        out_shape=jax.ShapeDtypeStruct((B, rows, FFT_N2, C), out_dtype),
        compiler_params=_cp(("parallel", "parallel", "parallel")),
        name="fft_stage_a",
    )(fa, y.reshape(B, nt1, FFT_N2, C))


def _fc_filter_kernel(g_ref, s_ref, ss_ref, h_ref, *, kb):
    C = HY_CH
    for i in range(kb):
        for n in range(HY_ORDER):
            cf, cb = (2 * n) * C, (2 * n + 1) * C
            scale = lax.rsqrt(ss_ref[0:1, cf:cf + C] + ss_ref[0:1, cb:cb + C] + EPS)
            sf = s_ref[0, 2 * i:2 * i + 2, :, cf:cf + C].reshape(2 * FFT_N2, C).astype(BF16)
            sb = s_ref[0, 2 * i:2 * i + 2, :, cb:cb + C].reshape(2 * FFT_N2, C).astype(BF16)
            xf = jnp.dot(g_ref[i], sf, preferred_element_type=F32)
            xb = jnp.dot(g_ref[i], sb, preferred_element_type=F32)
            h_ref[n, 2 * i] = (xf[:FFT_N2] + xb[:FFT_N2]) * scale
            h_ref[n, 2 * i + 1] = (xf[FFT_N2:] - xb[FFT_N2:]) * scale


def _fft_filter_spectrum(gfwd, s_filt, sumsq):
    rows = s_filt.shape[1]
    C4 = s_filt.shape[-1]
    kb = 2
    return pl.pallas_call(
        functools.partial(_fc_filter_kernel, kb=kb),
        grid=(rows // (2 * kb),),
        in_specs=[pl.BlockSpec((kb, 2 * FFT_N2, 2 * FFT_N2), lambda i: (i, 0, 0)),
                  pl.BlockSpec((1, 2 * kb, FFT_N2, C4), lambda i: (0, i, 0, 0)),
                  pl.BlockSpec((1, C4), lambda i: (0, 0))],
        out_specs=pl.BlockSpec((HY_ORDER, 2 * kb, FFT_N2, HY_CH), lambda i: (0, i, 0, 0)),
        out_shape=jax.ShapeDtypeStruct((HY_ORDER, rows, FFT_N2, HY_CH), F32),
        compiler_params=_cp(("parallel",)),
        name="fft_filter_spectrum",
    )(gfwd, s_filt, sumsq)


def _fc_kernel(g_ref, gi_ref, h_ref, s_ref, o_ref, *, kb):
    C = s_ref.shape[-1]
    for i in range(kb):
        s = s_ref[0, 2 * i:2 * i + 2].reshape(2 * FFT_N2, C)
        x = jnp.dot(g_ref[i], s, preferred_element_type=F32)
        xr, xi = x[:FFT_N2], x[FFT_N2:]
        hr, hi = h_ref[0, 2 * i], h_ref[0, 2 * i + 1]
        z = jnp.concatenate([xr * hr - xi * hi, xr * hi + xi * hr], axis=0).astype(BF16)
        bm = jnp.dot(gi_ref[i], z, preferred_element_type=F32)
        o_ref[0, 2 * i:2 * i + 2] = bm.reshape(2, FFT_N2, C).astype(o_ref.dtype)


def _fft_stage_c(gfwd, ginv, hspec, order, s):
    B, rows, _, C = s.shape
    kb = 4
    sspec = pl.BlockSpec((1, 2 * kb, FFT_N2, C), lambda i, b: (b, i, 0, 0))
    gspec = pl.BlockSpec((kb, 2 * FFT_N2, 2 * FFT_N2), lambda i, b: (i, 0, 0))
    return pl.pallas_call(
        functools.partial(_fc_kernel, kb=kb),
        grid=(rows // (2 * kb), B),
        in_specs=[gspec, gspec,
                  pl.BlockSpec((1, 2 * kb, FFT_N2, C), lambda i, b: (order, i, 0, 0)), sspec],
        out_specs=sspec,
        out_shape=jax.ShapeDtypeStruct(s.shape, BF16),
        compiler_params=_cp(("parallel", "arbitrary")),
        name="fft_stage_c",
    )(gfwd, ginv, hspec, s)


def _fai_kernel(fai_ref, b_ref, y_ref, gate_ref, skip_ref, o_ref):
    nt1, tb, cb = y_ref.shape[1:]
    bm = b_ref[0].reshape(b_ref.shape[1] * tb, cb)
    yf = jnp.dot(fai_ref[...], bm, preferred_element_type=F32)
    yin = y_ref[0].reshape(nt1 * tb, cb).astype(F32)
    gate = gate_ref[0].reshape(nt1 * tb, cb).astype(F32)
    o_ref[0] = (gate * (yf + skip_ref[...] * yin)).reshape(nt1, tb, cb).astype(o_ref.dtype)


def _fft_stage_a_inv(fai, bm, y, gate, skip):
    B, Lp, C = y.shape
    nt1 = Lp // FFT_N2
    rows = fai.shape[1] // FFT_TB
    tok = pl.BlockSpec((1, nt1, FFT_TB, FFT_CB), lambda b, j, c: (b, 0, j, c))
    out = pl.pallas_call(
        _fai_kernel,
        grid=(B, FFT_N2 // FFT_TB, C // FFT_CB),
        in_specs=[pl.BlockSpec(fai.shape, lambda b, j, c: (0, 0)),
                  pl.BlockSpec((1, rows, FFT_TB, FFT_CB), lambda b, j, c: (b, 0, j, c)),
                  tok, tok, pl.BlockSpec((1, FFT_CB), lambda b, j, c: (0, c))],
        out_specs=tok,
        out_shape=jax.ShapeDtypeStruct((B, nt1, FFT_N2, C), BF16),
        compiler_params=_cp(("parallel", "parallel", "parallel")),
        name="fft_stage_a_inv",
    )(fai, bm, y.reshape(B, nt1, FFT_N2, C), gate.reshape(B, nt1, FFT_N2, C), skip.astype(F32).reshape(1, C))
    return out.reshape(B, Lp, C)


def _hgen_kernel(z_ref, w1_ref, b1_ref, fr_ref, w2_ref, b2_ref, w3_ref, dec_ref, h_ref, ss_ref):
    i = pl.program_id(0)
    z = z_ref[...]
    tm = z.shape[0]
    h = jnp.sin(fr_ref[0:1, :] * (jnp.dot(z, w1_ref[...], preferred_element_type=F32, precision=HIGHEST)
                                 + b1_ref[...]))
    h = jnp.sin(fr_ref[1:2, :] * (jnp.dot(h, w2_ref[...], preferred_element_type=F32, precision=HIGHEST)
                                 + b2_ref[...]))
    h = jnp.dot(h, w3_ref[...], preferred_element_type=F32, precision=HIGHEST)
    h = h * jnp.exp(-z[:, 0:1] * jnp.abs(dec_ref[...]))
    row = lax.broadcasted_iota(jnp.int32, h.shape, 0) + i * tm
    col = lax.broadcasted_iota(jnp.int32, h.shape, 1)
    h = jnp.where((row == 0) & ((col // HY_CH) % 2 == 1), 0.0, h)
    h_ref[...] = h

    @pl.when(i == 0)
    def _():
        ss_ref[...] = jnp.zeros_like(ss_ref)

    ss_ref[...] += jnp.sum(h * h, axis=0, keepdims=True)


def _hyena_filters(L, p):
    t = jnp.arange(L, dtype=F32)
    tn = t / (L - 1)
    w = 2.0 * math.pi * t / L
    bands = jnp.linspace(1e-4, HY_BANDS - 1, HY_BANDS, dtype=F32)
    ang = w[:, None] * bands[None, :]
    z = jnp.concatenate([tn[:, None], jnp.cos(ang), -jnp.sin(ang)], axis=-1)
    z = jnp.pad(z, ((0, 0), (0, LANES - HY_EMB)))
    w1 = jnp.pad(p['hy_pe_w1'].astype(F32), ((0, LANES - HY_EMB), (0, 0)))
    nf = HY_ORDER * 2 * HY_CH
    tm = min(512, L)
    const = lambda shape: pl.BlockSpec(shape, lambda i: (0,) * len(shape))
    return pl.pallas_call(
        _hgen_kernel,
        grid=(L // tm,),
        in_specs=[pl.BlockSpec((tm, LANES), lambda i: (i, 0)), const((LANES, HY_FFN)), const((1, HY_FFN)),
                  const((2, HY_FFN)), const((HY_FFN, HY_FFN)), const((1, HY_FFN)), const((HY_FFN, nf)),
                  const((1, nf))],
        out_specs=[pl.BlockSpec((tm, nf), lambda i: (i, 0)), const((1, nf))],
        out_shape=[jax.ShapeDtypeStruct((L, nf), F32), jax.ShapeDtypeStruct((1, nf), F32)],
        compiler_params=_cp(("arbitrary",)),
        name="hyena_filter_gen",
    )(z, w1, p['hy_pe_b1'].reshape(1, HY_FFN), p['hy_freq'], p['hy_pe_w2'], p['hy_pe_b2'].reshape(1, HY_FFN),
      p['hy_pe_w3'], p['hy_decay'].reshape(1, nf))


def _hyena_branch(z, p, B, L):
    v, x1, x2 = _hy_prep(z, p['hy_conv_w'], p['hy_conv_b'], L)
    Lp = max(L, 2048)
    fa, fai, gfwd, ginv = _fft_tables(Lp)
    hfilt, sumsq = _hyena_filters(L, p)
    pad3 = lambda a: a.reshape(B, L, HY_CH) if Lp == L else jnp.pad(a.reshape(B, L, HY_CH), ((0, 0), (0, Lp - L), (0, 0)))
    hf = hfilt if Lp == L else jnp.pad(hfilt, ((0, Lp - L), (0, 0)))
    s_filt = _fft_stage_a(fa, hf[None], F32)
    hspec = _fft_filter_spectrum(gfwd, s_filt, sumsq)
    y = pad3(v)
    for n, gate in enumerate((pad3(x1), pad3(x2))):
        s = _fft_stage_a(fa, y, BF16)
        bm = _fft_stage_c(gfwd, ginv, hspec, n, s)
        y = _fft_stage_a_inv(fai, bm, y, gate, p['hy_skip'][n])
    return y[:, :L].reshape(B * L, HY_CH)


def _merge_kernel(ya_ref, yb_ref, yc_ref, yd_ref, g0_ref, g1_ref, g2_ref, g3_ref, wup_ref, wout_ref,
                  x_ref, gate_ref, o_ref):
    acc = None
    for n, (y_ref, g_ref) in enumerate(((ya_ref, g0_ref), (yb_ref, g1_ref), (yc_ref, g2_ref), (yd_ref, g3_ref))):
        t = jax.nn.sigmoid(g_ref[...].astype(F32)) * jnp.dot(y_ref[...], wup_ref[n], preferred_element_type=F32)
        acc = t if acc is None else acc + t
    yl = jnp.dot(acc.astype(BF16), wout_ref[...], preferred_element_type=F32)
    o_ref[...] = x_ref[...] + gate_ref[0] * yl


def _merge(ys, z, w_up, w_out, x, gate, seq_len):
    T, D = x.shape
    tm = min(512, seq_len)
    per = seq_len // tm
    Wy = ys[0].shape[1]
    yspec = pl.BlockSpec((tm, Wy), lambda i: (i, 0))
    gspecs = [pl.BlockSpec((tm, D), functools.partial(lambda i, n: (i, Z_GATE // D + n), n=n)) for n in range(4)]
    return pl.pallas_call(
        _merge_kernel,
        grid=(T // tm,),
        in_specs=[yspec] * 4 + gspecs + [
            pl.BlockSpec((4, Wy, D), lambda i: (0, 0, 0)), pl.BlockSpec((D, D), lambda i: (0, 0)),
            pl.BlockSpec((tm, D), lambda i: (i, 0)), pl.BlockSpec((1, 1, D), lambda i: (i // per, 0, 0))],
        out_specs=pl.BlockSpec((tm, D), lambda i: (i, 0)),
        out_shape=jax.ShapeDtypeStruct((T, D), F32),
        compiler_params=_cp(("parallel",)),
        name="merge_branches",
    )(*ys, z, z, z, z, w_up, w_out, x, gate)


def _mlp_kernel(x_ref, g_ref, sc_ref, sh_ref, w1_ref, b1_ref, w2_ref, b2_ref, gate_ref, fg_ref, o_ref,
                h_ref, acc_ref, *, nk, final):
    k = pl.program_id(1)

    @pl.when(k == 0)
    def _():
        y = _rms(x_ref[...], g_ref[...])
        h_ref[...] = (y * (1.0 + sc_ref[0]) + sh_ref[0]).astype(BF16)
        acc_ref[...] = jnp.zeros_like(acc_ref)

    a = jnp.maximum(jnp.dot(h_ref[...], w1_ref[...], preferred_element_type=F32) + b1_ref[...], 0.0)
    acc_ref[...] += jnp.dot((a * a).astype(BF16), w2_ref[...], preferred_element_type=F32)

    @pl.when(k == nk - 1)
    def _():
        out = x_ref[...] + gate_ref[0] * (acc_ref[...] + b2_ref[...])
        if final:
            out = _rms(out, fg_ref[...])
        o_ref[...] = out


def _mlp(x, g, sc, sh, w1, b1, w2, b2, gate, final_g, seq_len, final):
    T, D = x.shape
    F = w1.shape[1]
    tm = min(1024, seq_len)
    per = seq_len // tm
    tk = 512
    nk = F // tk
    row = lambda i, k: (i // per, 0, 0)
    return pl.pallas_call(
        functools.partial(_mlp_kernel, nk=nk, final=final),
        grid=(T // tm, nk),
        in_specs=[pl.BlockSpec((tm, D), lambda i, k: (i, 0)), pl.BlockSpec((1, D), lambda i, k: (0, 0)),
                  pl.BlockSpec((1, 1, D), row), pl.BlockSpec((1, 1, D), row),
                  pl.BlockSpec((D, tk), lambda i, k: (0, k)), pl.BlockSpec((1, tk), lambda i, k: (0, k)),
                  pl.BlockSpec((tk, D), lambda i, k: (k, 0)), pl.BlockSpec((1, D), lambda i, k: (0, 0)),
                  pl.BlockSpec((1, 1, D), row), pl.BlockSpec((1, D), lambda i, k: (0, 0))],
        out_specs=pl.BlockSpec((tm, D), lambda i, k: (i, 0)),
        out_shape=jax.ShapeDtypeStruct((T, D), F32),
        scratch_shapes=[pltpu.VMEM((tm, D), BF16), pltpu.VMEM((tm, D), F32)],
        compiler_params=_cp(("parallel", "arbitrary")),
        name="mlp",
    )(x, g.reshape(1, D), sc, sh, w1, b1.reshape(1, F), w2, b2.reshape(1, D), gate, final_g.reshape(1, D))


def _pack_w_in(w_in):
    hy_e = 3 * HY_CH
    ga_e = hy_e + (GA_HEADS + 2 * GA_KV) * GA_HD
    mw = ML_HEADS * ML_HD
    ml_e = ga_e + 4 * mw + 16
    wa_e = ml_e + (WA_HEADS + 2 * WA_KV) * WA_HD
    hy, ga = w_in[:, :hy_e], w_in[:, hy_e:ga_e]
    ml = w_in[:, ga_e:ml_e]
    wa = w_in[:, ml_e:wa_e]
    gate = w_in[:, wa_e:]
    waq, wakv = wa[:, :WA_HEADS * WA_HD], wa[:, WA_HEADS * WA_HD:]
    pad = jnp.zeros((w_in.shape[0], Z_COLS - Z_WAKV - wakv.shape[1]), w_in.dtype)
    packed = jnp.concatenate([hy, waq, ga, ml[:, :2 * mw], ml[:, 2 * mw:3 * mw], ml[:, 3 * mw:4 * mw], gate, wakv, pad],
                             axis=1)
    wg = jnp.pad(ml[:, 4 * mw:], ((0, 0), (0, LANES - 16)))
    return packed.astype(BF16), wg


def _token_mixers(zl, zc, gl, gc, p, B, L, Lc, with_ctx_out):
    ya_l = _hyena_branch(zl, p, B, L)
    ya_c = _hyena_branch(zc, p, B, Lc) if with_ctx_out else None
    cos, sin = _rope_tables(L, GA_HD)
    ql, kl, vl = _ga_prep(zl, cos, sin, p['ga_q_g'], p['ga_k_g'], B, L, True)
    qc, kc, vc = _ga_prep(zc, cos[:Lc], sin[:Lc], p['ga_q_g'], p['ga_k_g'], B, Lc, False)
    r3 = lambda a, n: a.reshape(B, n, a.shape[-1])
    k_all = jnp.concatenate([r3(kl, L), r3(kc, Lc)], axis=1)
    vt_all = jnp.concatenate([vl, vc], axis=2)
    yb_l = _global_attention(ql, k_all, vt_all).reshape(B * L, -1)
    yb_c = _global_attention(qc, r3(kc, Lc), vc).reshape(B * Lc, -1) if with_ctx_out else None
    mq_l, mk_l, mv_l = _ml_prep(zl, p['ml_conv_w'], p['ml_conv_b'], B, L)
    mq_c, mk_c, mv_c = _ml_prep(zc, p['ml_conv_w'], p['ml_conv_b'], B, Lc)
    gb = jnp.pad(p['ml_gate_b'].astype(F32), (0, LANES - 16)).reshape(1, LANES)
    c0 = jnp.zeros((B, ML_HEADS, 2 * ML_HD, ML_HD), F32)
    m0 = jnp.zeros((B, ML_HEADS, 8, LANES), F32)
    zl3, zc3, gl3, gc3 = r3(zl, L), r3(zc, Lc), r3(gl, L), r3(gc, Lc)
    h_cf, cf, mf = _mlstm_scan(mq_c, r3(mk_c, Lc), mv_c, zc3, gc3, gb, c0, m0, False)
    yc_c, cb, mb = _mlstm_scan(mq_c, r3(mk_c, Lc), mv_c, zc3, gc3, gb, c0, m0, True, h_cf, p['ml_norm_g'])
    h_lf, _, _ = _mlstm_scan(mq_l, r3(mk_l, L), mv_l, zl3, gl3, gb, cf, mf, False)
    yc_l, _, _ = _mlstm_scan(mq_l, r3(mk_l, L), mv_l, zl3, gl3, gb, cb, mb, True, h_lf, p['ml_norm_g'])
    yc_l = yc_l.reshape(B * L, -1)
    yc_c = yc_c.reshape(B * Lc, -1)
    cosw, sinw = _rope_tables(L, WA_HD)
    wq_l, wk_l, wv_l = _wa_prep(zl, cosw, sinw, L, True)
    wq_c, wk_c, wv_c = _wa_prep(zc, cosw[:Lc], sinw[:Lc], Lc, False)
    yd_l = _window_attention(r3(wq_l, L), r3(wk_c, Lc), r3(wv_c, Lc), p['wa_sink'], r3(wk_l, L), r3(wv_l, L))
    yd_l = yd_l.reshape(B * L, -1)
    yd_c = (_window_attention(r3(wq_c, Lc), r3(wk_c, Lc), r3(wv_c, Lc), p['wa_sink']).reshape(B * Lc, -1)
            if with_ctx_out else None)
    return (ya_l, yb_l, yc_l, yd_l), (ya_c, yb_c, yc_c, yd_c)


def kernel(x, c, ctx, c_ctx, w_mod, b_mod, ln1_g, ln2_g, w_in, hy_conv_w, hy_conv_b,
           hy_pe_w1, hy_pe_b1, hy_freq, hy_pe_w2, hy_pe_b2, hy_pe_w3, hy_decay, hy_skip,
           ga_q_g, ga_k_g, ml_conv_w, ml_conv_b, ml_gate_b, ml_norm_g, wa_sink, w_up, w_out,
           mlp_w1, mlp_b1, mlp_w2, mlp_b2, final_g):
    B, L, D = x.shape
    Lc = ctx.shape[1]
    R = -(-(B + 1) // 8) * 8
    cvec = jnp.zeros((R, D), F32).at[:B].set(c).at[B].set(c_ctx)
    mod = _modulation(cvec, w_mod, b_mod)
    xl = x.reshape(B * L, D)
    xc = ctx.reshape(B * Lc, D)
    for l in range(DEPTH):
        with_ctx_out = l < DEPTH - 1
        p = dict(hy_conv_w=hy_conv_w[l], hy_conv_b=hy_conv_b[l], hy_pe_w1=hy_pe_w1[l],
                 hy_pe_b1=hy_pe_b1[l], hy_freq=hy_freq[l], hy_pe_w2=hy_pe_w2[l], hy_pe_b2=hy_pe_b2[l],
                 hy_pe_w3=hy_pe_w3[l], hy_decay=hy_decay[l], hy_skip=hy_skip[l],
                 ga_q_g=ga_q_g[l], ga_k_g=ga_k_g[l], ml_conv_w=ml_conv_w[l], ml_conv_b=ml_conv_b[l],
                 ml_gate_b=ml_gate_b[l], ml_norm_g=ml_norm_g[l], wa_sink=wa_sink[l])
        ml_rows = mod[l, :B].reshape(B, 1, 6 * D)
        mc_rows = jnp.broadcast_to(mod[l, B].reshape(1, 1, 6 * D), (B, 1, 6 * D))
        part = lambda m, n: m[:, :, n * D:(n + 1) * D]
        w_pack, w_gate = _pack_w_in(w_in[l])
        zl = _normmod_matmul(xl, ln1_g[l], part(ml_rows, 1), part(ml_rows, 0), w_pack, L, Z_TN, BF16)
        zc = _normmod_matmul(xc, ln1_g[l], part(mc_rows, 1), part(mc_rows, 0), w_pack, Lc, Z_TN, BF16)
        gl = _normmod_matmul(xl, ln1_g[l], part(ml_rows, 1), part(ml_rows, 0), w_gate.astype(BF16), L, LANES, F32)
        gc = _normmod_matmul(xc, ln1_g[l], part(mc_rows, 1), part(mc_rows, 0), w_gate.astype(BF16), Lc, LANES, F32)
        ys_l, ys_c = _token_mixers(zl, zc, gl, gc, p, B, L, Lc, with_ctx_out)
        wup = w_up[l].astype(BF16)
        wout = w_out[l].astype(BF16)
        w1, w2 = mlp_w1[l].astype(BF16), mlp_w2[l].astype(BF16)
        xl = _merge(ys_l, zl, wup, wout, xl, part(ml_rows, 2), L)
        xl = _mlp(xl, ln2_g[l], part(ml_rows, 4), part(ml_rows, 3), w1, mlp_b1[l], w2, mlp_b2[l],
                  part(ml_rows, 5), final_g, L, final=(l == DEPTH - 1))
        if with_ctx_out:
            xc = _merge(ys_c, zc, wup, wout, xc, part(mc_rows, 2), Lc)
            xc = _mlp(xc, ln2_g[l], part(mc_rows, 4), part(mc_rows, 3), w1, mlp_b1[l], w2, mlp_b2[l],
                      part(mc_rows, 5), final_g, Lc, final=False)
    return xl.reshape(B, L, D)
```

```python
import functools
import math

import jax
import jax.numpy as jnp
import numpy as np
from jax import lax
from jax.experimental import pallas as pl
from jax.experimental.pallas import tpu as pltpu

F32 = jnp.float32
BF16 = jnp.bfloat16
HIGHEST = lax.Precision.HIGHEST

D_MODEL = 1024
DEPTH = 2
GRID_W = 64
HY_CH = 512
HY_ORDER = 2
HY_BANDS = 16
HY_EMB = 1 + 2 * HY_BANDS
HY_FFN = 64
GA_HEADS, GA_KV, GA_HD = 4, 2, 128
ML_HEADS, ML_HD = 4, 128
WA_HEADS, WA_KV, WA_HD = 8, 2, 64
WINDOW = 128
ROPE_BASE = 10000.0
D_FF = 4 * D_MODEL
EPS = 1e-6
NEG = -1e30
LOG2E = 1.4426950408889634

LANES = 128
V7X_VMEM_LIMIT = 48 * 1024 * 1024

Z_HY = 0
Z_WAQ = 1536
Z_GA = 2048
Z_MLQK = 3072
Z_MLV = 4096
Z_MLO = 4608
Z_GATE = 5120
Z_WAKV = 9216
Z_COLS = 9728
Z_TN = 2432

ML_CHUNK = 128
FFT_N2 = 128
GA_CB = 256
WA_PAIR = 4
GA_UNROLL = 8


def _cp(sem, vmem=V7X_VMEM_LIMIT):
    return pltpu.CompilerParams(dimension_semantics=sem, vmem_limit_bytes=vmem)


def _rms(x, g):
    return x * lax.rsqrt(jnp.mean(x * x, axis=-1, keepdims=True) + EPS) * g


def _mod_kernel(c_ref, w_ref, b_ref, o_ref):
    c = c_ref[...]
    s = c * jax.nn.sigmoid(c)
    o_ref[0] = jnp.dot(s, w_ref[0], preferred_element_type=F32, precision=HIGHEST) + b_ref[0]


def _modulation(cvec, w_mod, b_mod):
    R = cvec.shape[0]
    tn = 1536
    return pl.pallas_call(
        _mod_kernel,
        grid=(DEPTH, 6 * D_MODEL // tn),
        in_specs=[pl.BlockSpec((R, D_MODEL), lambda l, j: (0, 0)),
                  pl.BlockSpec((1, D_MODEL, tn), lambda l, j: (l, 0, j)),
                  pl.BlockSpec((1, 1, tn), lambda l, j: (l, 0, j))],
        out_specs=pl.BlockSpec((1, R, tn), lambda l, j: (l, 0, j)),
        out_shape=jax.ShapeDtypeStruct((DEPTH, R, 6 * D_MODEL), F32),
        compiler_params=_cp(("parallel", "parallel")),
        name="modulation",
    )(cvec, w_mod, b_mod.reshape(DEPTH, 1, 6 * D_MODEL))


def _nmm_kernel(x_ref, g_ref, sc_ref, sh_ref, w_ref, o_ref, h_ref):
    @pl.when(pl.program_id(1) == 0)
    def _():
        y = _rms(x_ref[...], g_ref[...])
        h_ref[...] = (y * (1.0 + sc_ref[0]) + sh_ref[0]).astype(BF16)

    o_ref[...] = jnp.dot(h_ref[...], w_ref[...], preferred_element_type=F32).astype(o_ref.dtype)


def _normmod_matmul(x, g, sc, sh, w, seq_len, tn, out_dtype):
    T, D = x.shape
    N = w.shape[1]
    tm = min(1024, seq_len)
    per = seq_len // tm
    return pl.pallas_call(
        _nmm_kernel,
        grid=(T // tm, N // tn),
        in_specs=[pl.BlockSpec((tm, D), lambda i, j: (i, 0)),
                  pl.BlockSpec((1, D), lambda i, j: (0, 0)),
                  pl.BlockSpec((1, 1, D), lambda i, j: (i // per, 0, 0)),
                  pl.BlockSpec((1, 1, D), lambda i, j: (i // per, 0, 0)),
                  pl.BlockSpec((D, tn), lambda i, j: (0, j))],
        out_specs=pl.BlockSpec((tm, tn), lambda i, j: (i, j)),
        out_shape=jax.ShapeDtypeStruct((T, N), out_dtype),
        scratch_shapes=[pltpu.VMEM((tm, D), BF16)],
        compiler_params=_cp(("parallel", "arbitrary")),
        name="normmod_matmul",
    )(x, g.reshape(1, D), sc, sh, w)


def _conv3(u, prev_row, next_row, w_ref, b_ref, c0, c1):
    tm = u.shape[0]
    row = lax.broadcasted_iota(jnp.int32, u.shape, 0)
    up = jnp.where(row == 0, prev_row, pltpu.roll(u, 1, 0))
    dn = jnp.where(row == tm - 1, next_row, pltpu.roll(u, tm - 1, 0))
    return (w_ref[0:1, c0:c1] * up + w_ref[1:2, c0:c1] * u + w_ref[2:3, c0:c1] * dn + b_ref[0:1, c0:c1])


def _halo_rows(zp_ref, zn_ref, per, c0, c1):
    i = pl.program_id(0)
    first = (i % per) == 0
    last = (i % per) == per - 1
    hp = zp_ref.shape[0]
    prev_row = jnp.where(first, 0.0, zp_ref[hp - 1:hp, c0:c1].astype(F32))
    next_row = jnp.where(last, 0.0, zn_ref[0:1, c0:c1].astype(F32))
    return prev_row, next_row


HALO = 16


def _halo_specs(tm, width, col_block, n_rows):
    nb = n_rows // HALO
    r = tm // HALO
    return [pl.BlockSpec((tm, width), lambda i: (i, col_block)),
            pl.BlockSpec((HALO, width), lambda i: (jnp.maximum(i * r - 1, 0), col_block)),
            pl.BlockSpec((HALO, width), lambda i: (jnp.minimum((i + 1) * r, nb - 1), col_block))]


def _hy_prep_kernel(z_ref, zp_ref, zn_ref, w_ref, b_ref, v_ref, x1_ref, x2_ref, *, per):
    outs = (v_ref, x1_ref, x2_ref)
    for c in range(3):
        c0, c1 = c * HY_CH, (c + 1) * HY_CH
        prev_row, next_row = _halo_rows(zp_ref, zn_ref, per, c0, c1)
        u = z_ref[:, c0:c1].astype(F32)
        outs[c][...] = _conv3(u, prev_row, next_row, w_ref, b_ref, c0, c1).astype(outs[c].dtype)


def _hy_prep(z, conv_w, conv_b, seq_len):
    T = z.shape[0]
    tm = min(512, seq_len)
    W = 3 * HY_CH
    out = jax.ShapeDtypeStruct((T, HY_CH), BF16)
    return pl.pallas_call(
        functools.partial(_hy_prep_kernel, per=seq_len // tm),
        grid=(T // tm,),
        in_specs=_halo_specs(tm, W, Z_HY // W, T) + [
            pl.BlockSpec((3, W), lambda i: (0, 0)), pl.BlockSpec((1, W), lambda i: (0, 0))],
        out_specs=[pl.BlockSpec((tm, HY_CH), lambda i: (i, 0))] * 3,
        out_shape=[out, out, out],
        compiler_params=_cp(("parallel",)),
        name="hyena_prep",
    )(z, z, z, conv_w, conv_b.reshape(1, W))


def _ml_prep_kernel(z_ref, zp_ref, zn_ref, zv_ref, w_ref, b_ref, q_ref, k_ref, v_ref, *, per):
    W = ML_HEADS * ML_HD
    for c in range(2):
        c0, c1 = c * W, (c + 1) * W
        prev_row, next_row = _halo_rows(zp_ref, zn_ref, per, c0, c1)
        u = z_ref[:, c0:c1].astype(F32)
        y = _conv3(u, prev_row, next_row, w_ref, b_ref, c0, c1)
        y = y * jax.nn.sigmoid(y)
        if c == 0:
            for h in range(ML_HEADS):
                q_ref[0, h * ML_HD:(h + 1) * ML_HD, :] = y[:, h * ML_HD:(h + 1) * ML_HD].T.astype(q_ref.dtype)
        else:
            k_ref[...] = (y * (ML_HD ** -0.5)).astype(k_ref.dtype)
    for h in range(ML_HEADS):
        v_ref[0, h * ML_HD:(h + 1) * ML_HD, :] = zv_ref[:, h * ML_HD:(h + 1) * ML_HD].astype(F32).T.astype(v_ref.dtype)


def _ml_prep(z, conv_w, conv_b, B, seq_len):
    T = z.shape[0]
    tm = min(512, seq_len)
    per = seq_len // tm
    W = 2 * ML_HEADS * ML_HD
    Wh = W // 2
    tspec = pl.BlockSpec((1, Wh, tm), lambda i: (i // per, 0, i % per))
    tshape = jax.ShapeDtypeStruct((B, Wh, seq_len), BF16)
    return pl.pallas_call(
        functools.partial(_ml_prep_kernel, per=per),
        grid=(T // tm,),
        in_specs=_halo_specs(tm, W, Z_MLQK // W, T) + [
            pl.BlockSpec((tm, Wh), lambda i: (i, Z_MLV // Wh)),
            pl.BlockSpec((3, W), lambda i: (0, 0)), pl.BlockSpec((1, W), lambda i: (0, 0))],
        out_specs=[tspec, pl.BlockSpec((tm, Wh), lambda i: (i, 0)), tspec],
        out_shape=[tshape, jax.ShapeDtypeStruct((T, Wh), BF16), tshape],
        compiler_params=_cp(("parallel",)),
        name="mlstm_prep",
    )(z, z, z, z, conv_w, conv_b.reshape(1, W))


def _rope_tables(L, hd):
    quarter = hd // 4
    inv = ROPE_BASE ** (-jnp.arange(quarter, dtype=F32) / quarter)
    t = jnp.arange(L)
    row = (t // GRID_W).astype(F32)
    col = (t % GRID_W).astype(F32)
    lane = jnp.arange(LANES)
    within = lane % hd
    is_col = (within // (hd // 2)) == 1
    second = ((within % (hd // 2)) // quarter) == 1
    j = within % quarter
    pos = jnp.where(is_col[None, :], col[:, None], row[:, None])
    ang = pos * inv[j][None, :]
    return jnp.cos(ang), jnp.where(second[None, :], jnp.sin(ang), -jnp.sin(ang))


def _rope(x, cos, sin, quarter):
    lane = lax.broadcasted_iota(jnp.int32, x.shape, 1)
    first = ((lane % (2 * quarter)) // quarter) == 0
    partner = jnp.where(first, pltpu.roll(x, LANES - quarter, 1), pltpu.roll(x, quarter, 1))
    return x * cos + partner * sin


def _ga_prep_kernel(z_ref, cos_ref, sin_ref, qg_ref, kg_ref, q_ref, k_ref, v_ref, *, rope):
    nq, nk = GA_HEADS, GA_KV
    for h in range(nq + nk):
        x = z_ref[:, h * GA_HD:(h + 1) * GA_HD].astype(F32)
        g = qg_ref[...] if h < nq else kg_ref[...]
        y = _rms(x, g)
        if rope:
            y = _rope(y, cos_ref[...], sin_ref[...], GA_HD // 4)
        if h < nq:
            q_ref[0, h * GA_HD:(h + 1) * GA_HD, :] = (y * (GA_HD ** -0.5 * LOG2E)).T.astype(q_ref.dtype)
        else:
            k_ref[:, (h - nq) * GA_HD:(h - nq + 1) * GA_HD] = y.astype(k_ref.dtype)
    for h in range(nk):
        v = z_ref[:, (nq + nk + h) * GA_HD:(nq + nk + h + 1) * GA_HD].astype(F32)
        v_ref[0, h * GA_HD:(h + 1) * GA_HD, :] = v.T.astype(v_ref.dtype)


def _ga_prep(z, cos, sin, qg, kg, B, seq_len, rope):
    T = z.shape[0]
    tm = min(512, seq_len)
    per = seq_len // tm
    W = (GA_HEADS + 2 * GA_KV) * GA_HD
    return pl.pallas_call(
        functools.partial(_ga_prep_kernel, rope=rope),
        grid=(T // tm,),
        in_specs=[pl.BlockSpec((tm, W), lambda i: (i, Z_GA // W)),
                  pl.BlockSpec((tm, LANES), lambda i: (i % per, 0)),
                  pl.BlockSpec((tm, LANES), lambda i: (i % per, 0)),
                  pl.BlockSpec((1, GA_HD), lambda i: (0, 0)),
                  pl.BlockSpec((1, GA_HD), lambda i: (0, 0))],
        out_specs=[pl.BlockSpec((1, GA_HEADS * GA_HD, tm), lambda i: (i // per, 0, i % per)),
                   pl.BlockSpec((tm, GA_KV * GA_HD), lambda i: (i, 0)),
                   pl.BlockSpec((1, GA_KV * GA_HD, tm), lambda i: (i // per, 0, i % per))],
        out_shape=[jax.ShapeDtypeStruct((B, GA_HEADS * GA_HD, seq_len), BF16),
                   jax.ShapeDtypeStruct((T, GA_KV * GA_HD), BF16),
                   jax.ShapeDtypeStruct((B, GA_KV * GA_HD, seq_len), BF16)],
        compiler_params=_cp(("parallel",)),
        name="global_attn_prep",
    )(z, cos, sin, qg.reshape(1, GA_HD), kg.reshape(1, GA_HD))


def _wa_prep_kernel(zq_ref, zkv_ref, cos_ref, sin_ref, q_ref, k_ref, v_ref, *, rope):
    quarter = WA_HD // 4
    for j in range(WA_HEADS * WA_HD // LANES):
        x = zq_ref[:, j * LANES:(j + 1) * LANES].astype(F32)
        if rope:
            x = _rope(x, cos_ref[...], sin_ref[...], quarter)
        q_ref[0, j * LANES:(j + 1) * LANES, :] = (x * (WA_HD ** -0.5 * LOG2E)).T.astype(q_ref.dtype)
    k = zkv_ref[:, 0:LANES].astype(F32)
    if rope:
        k = _rope(k, cos_ref[...], sin_ref[...], quarter)
    k_ref[...] = k.astype(k_ref.dtype)
    v_ref[0] = zkv_ref[:, LANES:].astype(F32).T.astype(v_ref.dtype)


def _wa_prep(z, cos, sin, B, seq_len, rope):
    T = z.shape[0]
    tm = min(512, seq_len)
    per = seq_len // tm
    WQ = WA_HEADS * WA_HD
    return pl.pallas_call(
        functools.partial(_wa_prep_kernel, rope=rope),
        grid=(T // tm,),
        in_specs=[pl.BlockSpec((tm, WQ), lambda i: (i, Z_WAQ // WQ)),
                  pl.BlockSpec((tm, 2 * LANES), lambda i: (i, Z_WAKV // (2 * LANES))),
                  pl.BlockSpec((tm, LANES), lambda i: (i % per, 0)),
                  pl.BlockSpec((tm, LANES), lambda i: (i % per, 0))],
        out_specs=[pl.BlockSpec((1, WQ, tm), lambda i: (i // per, 0, i % per)),
                   pl.BlockSpec((tm, LANES), lambda i: (i, 0)),
                   pl.BlockSpec((1, LANES, tm), lambda i: (i // per, 0, i % per))],
        out_shape=[jax.ShapeDtypeStruct((B, WQ, seq_len), BF16),
                   jax.ShapeDtypeStruct((T, LANES), BF16),
                   jax.ShapeDtypeStruct((B, LANES, seq_len), BF16)],
        compiler_params=_cp(("parallel",)),
        name="window_attn_prep",
    )(z, z, cos, sin)


def _ga_kernel(q_ref, k_ref, vt_ref, o_ref, acc_ref, m_ref, l_ref, *, nchunks, tk, tq):
    acc_ref[...] = jnp.zeros_like(acc_ref)
    l_ref[...] = jnp.zeros_like(l_ref)
    m_ref[...] = jnp.full_like(m_ref, NEG)
    qt = jnp.concatenate([q_ref[0, 0:GA_HD, :], q_ref[0, GA_HD:, :]], axis=1)

    W = GA_CB
    nblk = 2 * tq // W

    def scores(k):
        return tuple(jnp.dot(k, qt[:, i * W:(i + 1) * W], preferred_element_type=F32) for i in range(nblk))

    def softmax_pv(s_blocks, vt):
        for i, s in enumerate(s_blocks):
            cs = slice(i * W, (i + 1) * W)
            m_old = m_ref[:, cs]
            m_new = jnp.maximum(m_old, jnp.max(s, axis=0, keepdims=True))
            p = jnp.exp2(s - m_new)
            alpha = jnp.exp2(m_old - m_new)
            l_ref[:, cs] = alpha * l_ref[:, cs] + jnp.sum(p, axis=0, keepdims=True)
            acc_ref[:, cs] = alpha * acc_ref[:, cs] + jnp.dot(vt, p.astype(BF16), preferred_element_type=F32)
            m_ref[:, cs] = m_new

    def body(j, s):
        s_next = scores(k_ref[0, pl.ds(pl.multiple_of((j + 1) * tk, tk), tk), :])
        softmax_pv(s, vt_ref[0, :, pl.ds(pl.multiple_of(j * tk, tk), tk)])
        return s_next
    s = scores(k_ref[0, 0:tk, :])
    if nchunks > 1:
        s = lax.fori_loop(0, nchunks - 1, body, s, unroll=GA_UNROLL if (nchunks - 1) % GA_UNROLL == 0 else 1)
    softmax_pv(s, vt_ref[0, :, (nchunks - 1) * tk:nchunks * tk])
    o = acc_ref[...] / l_ref[...]
    for h in range(2):
        o_ref[0, :, h * GA_HD:(h + 1) * GA_HD] = o[:, h * tq:(h + 1) * tq].T.astype(o_ref.dtype)


def _global_attention(qt, k, vt):
    B, _, Lq = qt.shape
    Lk = k.shape[1]
    tq = min(512, Lq)
    tk = 256
    W = 2 * GA_HD
    return pl.pallas_call(
        functools.partial(_ga_kernel, nchunks=Lk // tk, tk=tk, tq=tq),
        grid=(B, GA_KV, Lq // tq),
        in_specs=[pl.BlockSpec((1, W, tq), lambda b, g, i: (b, g, i)),
                  pl.BlockSpec((1, Lk, GA_HD), lambda b, g, i: (b, 0, g)),
                  pl.BlockSpec((1, GA_HD, Lk), lambda b, g, i: (b, g, 0))],
        out_specs=pl.BlockSpec((1, tq, W), lambda b, g, i: (b, i, g)),
        out_shape=jax.ShapeDtypeStruct((B, Lq, GA_HEADS * GA_HD), BF16),
        scratch_shapes=[pltpu.VMEM((GA_HD, 2 * tq), F32), pltpu.VMEM((1, 2 * tq), F32),
                        pltpu.VMEM((1, 2 * tq), F32)],
        compiler_params=_cp(("parallel", "parallel", "parallel")),
        name="global_attention",
    )(qt, k, vt)


def _wa_kernel(*refs, band, nq):
    if band:
        q_ref, kp_ref, kx_ref, kn_ref, vp_ref, vx_ref, vn_ref, kc_ref, vc_ref, sink_ref, o_ref = refs
    else:
        q_ref, kc_ref, vc_ref, sink_ref, o_ref = refs
    tq = q_ref.shape[2]
    Lc = kc_ref.shape[1]
    qi = pl.program_id(1)
    G = WA_HEADS // WA_KV
    cols = WA_PAIR * tq
    if band:
        nb = 3 * tq
        k = jnp.concatenate([kp_ref[0], kx_ref[0], kn_ref[0], kc_ref[0]], axis=0)
        vt = jnp.concatenate([vp_ref[0], vx_ref[0], vn_ref[0], vc_ref[0]], axis=1)
        c = lax.broadcasted_iota(jnp.int32, (nb + Lc, cols), 0)
        r = lax.broadcasted_iota(jnp.int32, (nb + Lc, cols), 1) % tq
        lo = jnp.where(qi == 0, tq, 0)
        hi = jnp.where(qi == nq - 1, 2 * tq, nb)
        valid = ((c >= r) & (c <= r + 2 * WINDOW) & (c >= lo) & (c < hi)) | (c >= nb)
    else:
        k, vt = kc_ref[0], vc_ref[0]
    nkeys = k.shape[0]
    ones_rows = jnp.ones((16, nkeys), BF16)
    zeros = jnp.zeros((WA_HD, cols), BF16)
    for pr in range(WA_HEADS // WA_PAIR):
        g = (pr * WA_PAIR) // G
        qg = jnp.concatenate([q_ref[0, (WA_PAIR * pr + h) * WA_HD:(WA_PAIR * pr + h + 1) * WA_HD, :]
                              for h in range(WA_PAIR)], axis=1)
        qpad = jnp.concatenate([qg, zeros] if g == 0 else [zeros, qg], axis=0)
        s = jnp.dot(k, qpad, preferred_element_type=F32)
        if band:
            s = jnp.where(valid, s, NEG)
        sink = sink_ref[pr]
        m = jnp.maximum(jnp.max(s, axis=0, keepdims=True), sink)
        p = jnp.exp2(s - m)
        vaug = jnp.concatenate([vt[g * WA_HD:(g + 1) * WA_HD, :], ones_rows], axis=0)
        R = jnp.dot(vaug, p.astype(BF16), preferred_element_type=F32)
        o = R[:WA_HD] / (R[WA_HD:WA_HD + 1] + jnp.exp2(sink - m))
        ot = jnp.concatenate([o[:, h * tq:(h + 1) * tq] for h in range(WA_PAIR)], axis=0)
        wo = WA_PAIR * WA_HD
        o_ref[0, :, pr * wo:(pr + 1) * wo] = ot.T.astype(o_ref.dtype)


def _window_attention(qt, kc, vtc, sink, kl=None, vtl=None):
    B, WQ, Lq = qt.shape
    Lc = kc.shape[1]
    tq = WINDOW
    nq = Lq // tq
    band = kl is not None
    npair = WA_HEADS // WA_PAIR
    sink_row = jnp.repeat(sink.astype(F32).reshape(npair, WA_PAIR) * LOG2E, tq, axis=1).reshape(npair, 1, WA_PAIR * tq)
    qspec = pl.BlockSpec((1, WQ, tq), lambda b, i: (b, 0, i))
    kcspec = pl.BlockSpec((1, Lc, LANES), lambda b, i: (b, 0, 0))
    vcspec = pl.BlockSpec((1, LANES, Lc), lambda b, i: (b, 0, 0))
    sspec = pl.BlockSpec((npair, 1, WA_PAIR * tq), lambda b, i: (0, 0, 0))
    if band:
        prev = lambda i: jnp.maximum(i - 1, 0)
        nxt = lambda i: jnp.minimum(i + 1, nq - 1)
        same = lambda i: i
        kspec = lambda f: pl.BlockSpec((1, tq, LANES), lambda b, i: (b, f(i), 0))
        vspec = lambda f: pl.BlockSpec((1, LANES, tq), lambda b, i: (b, 0, f(i)))
        in_specs = [qspec, kspec(prev), kspec(same), kspec(nxt), vspec(prev), vspec(same), vspec(nxt),
                    kcspec, vcspec, sspec]
        args = (qt, kl, kl, kl, vtl, vtl, vtl, kc, vtc, sink_row)
    else:
        in_specs, args = [qspec, kcspec, vcspec, sspec], (qt, kc, vtc, sink_row)
    return pl.pallas_call(
        functools.partial(_wa_kernel, band=band, nq=nq),
        grid=(B, nq),
        in_specs=in_specs,
        out_specs=pl.BlockSpec((1, tq, WQ), lambda b, i: (b, i, 0)),
        out_shape=jax.ShapeDtypeStruct((B, Lq, WQ), BF16),
        compiler_params=_cp(("parallel", "parallel")),
        name="window_attention",
    )(*args)


def _mlstm_kernel(*refs, reverse, nc, bb):
    if reverse:
        (qt_ref, k_ref, vt_ref, g_ref, gb_ref, c0_ref, m0_ref, hft_ref, o_ref, gn_ref,
         y_ref, cf_ref, mf_ref, c_scr, m_scr) = refs
    else:
        (qt_ref, k_ref, vt_ref, g_ref, gb_ref, c0_ref, m0_ref,
         y_ref, cf_ref, mf_ref, c_scr, m_scr) = refs
    T = ML_CHUNK
    d = 1 if reverse else 0
    step = pl.program_id(1)

    @pl.when(step == 0)
    def _():
        c_scr[...] = c0_ref[...]
        m_scr[...] = m0_ref[...]

    si = lax.broadcasted_iota(jnp.int32, (T, T), 0)
    ti = lax.broadcasted_iota(jnp.int32, (T, T), 1)
    tri = ((ti >= si) if reverse else (ti <= si)).astype(F32)
    mask_t = (si >= ti) if reverse else (si <= ti)
    ones_rows = jnp.ones((ML_HD, T), BF16)
    e_last = 0 if reverse else T - 1
    for bi in range(bb):
        G = g_ref[bi] + gb_ref[...]
        LF = jax.nn.log_sigmoid(G)
        Bc = jnp.dot(tri, LF, preferred_element_type=F32, precision=HIGHEST)
        GT = G.T
        BT = Bc.T
        Dc = Bc - pltpu.roll(G, 4, 1)
        for h in range(ML_HEADS):
            fl, il = d * 8 + 4 + h, d * 8 + h
            hs = slice(h * ML_HD, (h + 1) * ML_HD)
            b_row, i_row = BT[fl:fl + 1, :], GT[il:il + 1, :]
            log_d = jnp.where(mask_t, b_row - Dc[:, fl:fl + 1], NEG)
            m_prev = m_scr[bi, h, 0:1, 0:1]
            m_inter = b_row + m_prev
            m_t = jnp.maximum(m_inter, jnp.max(log_d, axis=0, keepdims=True))
            kh, qt = k_ref[bi, :, hs], qt_ref[bi, hs, :]
            st = jnp.dot(kh, qt, preferred_element_type=F32)
            wqk = (st * jnp.exp(log_d - m_t)).astype(BF16)
            cs = jnp.exp(m_inter - m_t)
            vaug = jnp.concatenate([vt_ref[bi, hs, :], ones_rows], axis=0)
            R = (jnp.dot(vaug, wqk, preferred_element_type=F32)
                 + cs * jnp.dot(c_scr[bi, h].astype(BF16), qt, preferred_element_type=F32))
            hh = R[:ML_HD] / jnp.maximum(jnp.abs(R[ML_HD:]), jnp.exp(-m_t))
            if reverse:
                hsum = hft_ref[bi, hs, :] + hh
                hn = hsum * lax.rsqrt(jnp.mean(hsum * hsum, axis=0, keepdims=True) + EPS) * gn_ref[hs, :]
                y_ref[bi, :, hs] = (jax.nn.sigmoid(o_ref[bi, :, hs].astype(F32)) * hn.T).astype(y_ref.dtype)
            else:
                y_ref[bi, hs, :] = hh
            b_end = BT[fl:fl + 1, e_last:e_last + 1]
            log_w = b_end - b_row + i_row
            m_next = jnp.maximum(b_end + m_prev, jnp.max(log_w, axis=1, keepdims=True))
            w_row = jnp.exp(log_w - m_next)
            decay = jnp.exp(b_end + m_prev - m_next)
            wv = (vaug.astype(F32) * w_row).astype(BF16)
            c_scr[bi, h] = decay * c_scr[bi, h] + jnp.dot(wv, kh, preferred_element_type=F32)
            m_scr[bi, h] = jnp.broadcast_to(m_next, (8, LANES))

    @pl.when(step == nc - 1)
    def _():
        cf_ref[...] = c_scr[...]
        mf_ref[...] = m_scr[...]


def _mlstm_scan(qt, k, vt, z, gates, gate_b, c0, m0, reverse, hft=None, norm_g=None):
    B, L, W = k.shape
    T = ML_CHUNK
    nc = L // T
    bb = 2
    cj = (lambda j: nc - 1 - j) if reverse else (lambda j: j)
    tok = pl.BlockSpec((bb, T, W), lambda b, j: (b, cj(j), 0))
    ttok = pl.BlockSpec((bb, W, T), lambda b, j: (b, 0, cj(j)))
    cspec = pl.BlockSpec((bb, ML_HEADS, 2 * ML_HD, ML_HD), lambda b, j: (b, 0, 0, 0))
    mspec = pl.BlockSpec((bb, ML_HEADS, 8, LANES), lambda b, j: (b, 0, 0, 0))
    in_specs = [ttok, tok, ttok, pl.BlockSpec((bb, T, LANES), lambda b, j: (b, cj(j), 0)),
                pl.BlockSpec((1, LANES), lambda b, j: (0, 0)), cspec, mspec]
    args = [qt, k, vt, gates, gate_b, c0, m0]
    if reverse:
        in_specs += [ttok, pl.BlockSpec((bb, T, W), lambda b, j: (b, cj(j), Z_MLO // W)),
                     pl.BlockSpec((W, T), lambda b, j: (0, 0))]
        args += [hft, z, jnp.broadcast_to(norm_g.astype(F32).reshape(W, 1), (W, T))]
    return pl.pallas_call(
        functools.partial(_mlstm_kernel, reverse=reverse, nc=nc, bb=bb),
        grid=(B // bb, nc),
        in_specs=in_specs,
        out_specs=[tok if reverse else ttok, cspec, mspec],
        out_shape=[jax.ShapeDtypeStruct((B, L, W), BF16) if reverse else jax.ShapeDtypeStruct((B, W, L), F32),
                   jax.ShapeDtypeStruct(c0.shape, F32), jax.ShapeDtypeStruct(m0.shape, F32)],
        scratch_shapes=[pltpu.VMEM((bb, ML_HEADS, 2 * ML_HD, ML_HD), F32), pltpu.VMEM((bb, ML_HEADS, 8, LANES), F32)],
        compiler_params=_cp(("parallel", "arbitrary")),
        name="mlstm_reverse" if reverse else "mlstm_forward",
    )(*args)


def _fft_dims(Lp):
    n1 = 2 * Lp // FFT_N2
    nt1 = Lp // FFT_N2
    nk1 = -(-(n1 // 2 + 1) // 8) * 8
    return n1, nt1, nk1


def _fft_tables(Lp):
    n1, nt1, nk1 = _fft_dims(Lp)
    N = 2 * Lp
    k1 = jnp.arange(nk1)
    t1 = jnp.arange(nt1)
    ang_a = (2.0 * math.pi / n1) * ((k1[:, None] * t1[None, :]) % n1).astype(F32)
    fa = jnp.stack([jnp.cos(ang_a), -jnp.sin(ang_a)], axis=1).reshape(2 * nk1, nt1)
    wgt = jnp.where((k1 == 0) | (k1 == n1 // 2), 1.0, 2.0) * (k1 <= n1 // 2) / N
    fai = jnp.stack([jnp.cos(ang_a) * wgt[:, None], -jnp.sin(ang_a) * wgt[:, None]], axis=1)
    fai = fai.reshape(2 * nk1, nt1).T
    k2 = jnp.arange(FFT_N2)
    t2 = jnp.arange(FFT_N2)
    idx = (t2[None, None, :] * k1[:, None, None] + n1 * t2[None, None, :] * k2[None, :, None]) % N
    phi = (2.0 * math.pi / N) * idx.astype(F32)
    gr, gi = jnp.cos(phi), -jnp.sin(phi)
    gfwd = jnp.concatenate([jnp.concatenate([gr, -gi], axis=2), jnp.concatenate([gi, gr], axis=2)], axis=1)
    grt, git = jnp.swapaxes(gr, 1, 2), jnp.swapaxes(gi, 1, 2)
    ginv = jnp.concatenate([jnp.concatenate([grt, git], axis=2), jnp.concatenate([-git, grt], axis=2)], axis=1)
    eye = jnp.eye(FFT_TB, dtype=F32)
    return (jnp.kron(fa, eye).astype(BF16), jnp.kron(fai, eye).astype(BF16), gfwd.astype(BF16), ginv.astype(BF16))


FFT_TB = 16
FFT_CB = 512


def _fa_kernel(fa_ref, x_ref, o_ref):
    nt1, tb, cb = x_ref.shape[1:]
    x = x_ref[0].reshape(nt1 * tb, cb).astype(BF16)
    r = jnp.dot(fa_ref[...], x, preferred_element_type=F32)
    o_ref[0] = r.reshape(o_ref.shape[1], tb, cb).astype(o_ref.dtype)


def _fft_stage_a(fa, y, out_dtype):
    B, Lp, C = y.shape
    nt1 = Lp // FFT_N2
    rows = fa.shape[0] // FFT_TB
    return pl.pallas_call(
        _fa_kernel,
        grid=(B, FFT_N2 // FFT_TB, C // FFT_CB),
        in_specs=[pl.BlockSpec(fa.shape, lambda b, j, c: (0, 0)),
                  pl.BlockSpec((1, nt1, FFT_TB, FFT_CB), lambda b, j, c: (b, 0, j, c))],
        out_specs=pl.BlockSpec((1, rows, FFT_TB, FFT_CB), lambda b, j, c: (b, 0, j, c)),
        out_shape=jax.ShapeDtypeStruct((B, rows, FFT_N2, C), out_dtype),
        compiler_params=_cp(("parallel", "parallel", "parallel")),
        name="fft_stage_a",
    )(fa, y.reshape(B, nt1, FFT_N2, C))


def _fc_filter_kernel(g_ref, s_ref, ss_ref, h_ref, *, kb):
    C = HY_CH
    for i in range(kb):
        for n in range(HY_ORDER):
            cf, cb = (2 * n) * C, (2 * n + 1) * C
            scale = lax.rsqrt(ss_ref[0:1, cf:cf + C] + ss_ref[0:1, cb:cb + C] + EPS)
            sf = s_ref[0, 2 * i:2 * i + 2, :, cf:cf + C].reshape(2 * FFT_N2, C).astype(BF16)
            sb = s_ref[0, 2 * i:2 * i + 2, :, cb:cb + C].reshape(2 * FFT_N2, C).astype(BF16)
            xf = jnp.dot(g_ref[i], sf, preferred_element_type=F32)
            xb = jnp.dot(g_ref[i], sb, preferred_element_type=F32)
            h_ref[n, 2 * i] = (xf[:FFT_N2] + xb[:FFT_N2]) * scale
            h_ref[n, 2 * i + 1] = (xf[FFT_N2:] - xb[FFT_N2:]) * scale


def _fft_filter_spectrum(gfwd, s_filt, sumsq):
    rows = s_filt.shape[1]
    C4 = s_filt.shape[-1]
    kb = 2
    return pl.pallas_call(
        functools.partial(_fc_filter_kernel, kb=kb),
        grid=(rows // (2 * kb),),
        in_specs=[pl.BlockSpec((kb, 2 * FFT_N2, 2 * FFT_N2), lambda i: (i, 0, 0)),
                  pl.BlockSpec((1, 2 * kb, FFT_N2, C4), lambda i: (0, i, 0, 0)),
                  pl.BlockSpec((1, C4), lambda i: (0, 0))],
        out_specs=pl.BlockSpec((HY_ORDER, 2 * kb, FFT_N2, HY_CH), lambda i: (0, i, 0, 0)),
        out_shape=jax.ShapeDtypeStruct((HY_ORDER, rows, FFT_N2, HY_CH), F32),
        compiler_params=_cp(("parallel",)),
        name="fft_filter_spectrum",
    )(gfwd, s_filt, sumsq)


def _fc_kernel(g_ref, gi_ref, h_ref, s_ref, o_ref, *, kb):
    C = s_ref.shape[-1]
    for i in range(kb):
        s = s_ref[0, 2 * i:2 * i + 2].reshape(2 * FFT_N2, C)
        x = jnp.dot(g_ref[i], s, preferred_element_type=F32)
        xr, xi = x[:FFT_N2], x[FFT_N2:]
        hr, hi = h_ref[0, 2 * i], h_ref[0, 2 * i + 1]
        z = jnp.concatenate([xr * hr - xi * hi, xr * hi + xi * hr], axis=0).astype(BF16)
        bm = jnp.dot(gi_ref[i], z, preferred_element_type=F32)
        o_ref[0, 2 * i:2 * i + 2] = bm.reshape(2, FFT_N2, C).astype(o_ref.dtype)


def _fft_stage_c(gfwd, ginv, hspec, order, s):
    B, rows, _, C = s.shape
    kb = 4
    sspec = pl.BlockSpec((1, 2 * kb, FFT_N2, C), lambda i, b: (b, i, 0, 0))
    gspec = pl.BlockSpec((kb, 2 * FFT_N2, 2 * FFT_N2), lambda i, b: (i, 0, 0))
    return pl.pallas_call(
        functools.partial(_fc_kernel, kb=kb),
        grid=(rows // (2 * kb), B),
        in_specs=[gspec, gspec,
                  pl.BlockSpec((1, 2 * kb, FFT_N2, C), lambda i, b: (order, i, 0, 0)), sspec],
        out_specs=sspec,
        out_shape=jax.ShapeDtypeStruct(s.shape, BF16),
        compiler_params=_cp(("parallel", "arbitrary")),
        name="fft_stage_c",
    )(gfwd, ginv, hspec, s)


def _fai_kernel(fai_ref, b_ref, y_ref, gate_ref, skip_ref, o_ref):
    nt1, tb, cb = y_ref.shape[1:]
    bm = b_ref[0].reshape(b_ref.shape[1] * tb, cb)
    yf = jnp.dot(fai_ref[...], bm, preferred_element_type=F32)
    yin = y_ref[0].reshape(nt1 * tb, cb).astype(F32)
    gate = gate_ref[0].reshape(nt1 * tb, cb).astype(F32)
    o_ref[0] = (gate * (yf + skip_ref[...] * yin)).reshape(nt1, tb, cb).astype(o_ref.dtype)


def _fft_stage_a_inv(fai, bm, y, gate, skip):
    B, Lp, C = y.shape
    nt1 = Lp // FFT_N2
    rows = fai.shape[1] // FFT_TB
    tok = pl.BlockSpec((1, nt1, FFT_TB, FFT_CB), lambda b, j, c: (b, 0, j, c))
    out = pl.pallas_call(
        _fai_kernel,
        grid=(B, FFT_N2 // FFT_TB, C // FFT_CB),
        in_specs=[pl.BlockSpec(fai.shape, lambda b, j, c: (0, 0)),
                  pl.BlockSpec((1, rows, FFT_TB, FFT_CB), lambda b, j, c: (b, 0, j, c)),
                  tok, tok, pl.BlockSpec((1, FFT_CB), lambda b, j, c: (0, c))],
        out_specs=tok,
        out_shape=jax.ShapeDtypeStruct((B, nt1, FFT_N2, C), BF16),
        compiler_params=_cp(("parallel", "parallel", "parallel")),
        name="fft_stage_a_inv",
    )(fai, bm, y.reshape(B, nt1, FFT_N2, C), gate.reshape(B, nt1, FFT_N2, C), skip.astype(F32).reshape(1, C))
    return out.reshape(B, Lp, C)


def _hgen_kernel(z_ref, w1_ref, b1_ref, fr_ref, w2_ref, b2_ref, w3_ref, dec_ref, h_ref, ss_ref):
    i = pl.program_id(0)
    z = z_ref[...]
    tm = z.shape[0]
    h = jnp.sin(fr_ref[0:1, :] * (jnp.dot(z, w1_ref[...], preferred_element_type=F32, precision=HIGHEST)
                                 + b1_ref[...]))
    h = jnp.sin(fr_ref[1:2, :] * (jnp.dot(h, w2_ref[...], preferred_element_type=F32, precision=HIGHEST)
                                 + b2_ref[...]))
    h = jnp.dot(h, w3_ref[...], preferred_element_type=F32, precision=HIGHEST)
    h = h * jnp.exp(-z[:, 0:1] * jnp.abs(dec_ref[...]))
    row = lax.broadcasted_iota(jnp.int32, h.shape, 0) + i * tm
    col = lax.broadcasted_iota(jnp.int32, h.shape, 1)
    h = jnp.where((row == 0) & ((col // HY_CH) % 2 == 1), 0.0, h)
    h_ref[...] = h

    @pl.when(i == 0)
    def _():
        ss_ref[...] = jnp.zeros_like(ss_ref)

    ss_ref[...] += jnp.sum(h * h, axis=0, keepdims=True)


def _hyena_filters(L, p):
    t = jnp.arange(L, dtype=F32)
    tn = t / (L - 1)
    w = 2.0 * math.pi * t / L
    bands = jnp.linspace(1e-4, HY_BANDS - 1, HY_BANDS, dtype=F32)
    ang = w[:, None] * bands[None, :]
    z = jnp.concatenate([tn[:, None], jnp.cos(ang), -jnp.sin(ang)], axis=-1)
    z = jnp.pad(z, ((0, 0), (0, LANES - HY_EMB)))
    w1 = jnp.pad(p['hy_pe_w1'].astype(F32), ((0, LANES - HY_EMB), (0, 0)))
    nf = HY_ORDER * 2 * HY_CH
    tm = min(512, L)
    const = lambda shape: pl.BlockSpec(shape, lambda i: (0,) * len(shape))
    return pl.pallas_call(
        _hgen_kernel,
        grid=(L // tm,),
        in_specs=[pl.BlockSpec((tm, LANES), lambda i: (i, 0)), const((LANES, HY_FFN)), const((1, HY_FFN)),
                  const((2, HY_FFN)), const((HY_FFN, HY_FFN)), const((1, HY_FFN)), const((HY_FFN, nf)),
                  const((1, nf))],
        out_specs=[pl.BlockSpec((tm, nf), lambda i: (i, 0)), const((1, nf))],
        out_shape=[jax.ShapeDtypeStruct((L, nf), F32), jax.ShapeDtypeStruct((1, nf), F32)],
        compiler_params=_cp(("arbitrary",)),
        name="hyena_filter_gen",
    )(z, w1, p['hy_pe_b1'].reshape(1, HY_FFN), p['hy_freq'], p['hy_pe_w2'], p['hy_pe_b2'].reshape(1, HY_FFN),
      p['hy_pe_w3'], p['hy_decay'].reshape(1, nf))


def _hyena_branch(z, p, B, L):
    v, x1, x2 = _hy_prep(z, p['hy_conv_w'], p['hy_conv_b'], L)
    Lp = max(L, 2048)
    fa, fai, gfwd, ginv = _fft_tables(Lp)
    hfilt, sumsq = _hyena_filters(L, p)
    pad3 = lambda a: a.reshape(B, L, HY_CH) if Lp == L else jnp.pad(a.reshape(B, L, HY_CH), ((0, 0), (0, Lp - L), (0, 0)))
    hf = hfilt if Lp == L else jnp.pad(hfilt, ((0, Lp - L), (0, 0)))
    s_filt = _fft_stage_a(fa, hf[None], F32)
    hspec = _fft_filter_spectrum(gfwd, s_filt, sumsq)
    y = pad3(v)
    for n, gate in enumerate((pad3(x1), pad3(x2))):
        s = _fft_stage_a(fa, y, BF16)
        bm = _fft_stage_c(gfwd, ginv, hspec, n, s)
        y = _fft_stage_a_inv(fai, bm, y, gate, p['hy_skip'][n])
    return y[:, :L].reshape(B * L, HY_CH)


def _merge_kernel(ya_ref, yb_ref, yc_ref, yd_ref, g0_ref, g1_ref, g2_ref, g3_ref, wup_ref, wout_ref,
                  x_ref, gate_ref, o_ref):
    acc = None
    for n, (y_ref, g_ref) in enumerate(((ya_ref, g0_ref), (yb_ref, g1_ref), (yc_ref, g2_ref), (yd_ref, g3_ref))):
        t = jax.nn.sigmoid(g_ref[...].astype(F32)) * jnp.dot(y_ref[...], wup_ref[n], preferred_element_type=F32)
        acc = t if acc is None else acc + t
    yl = jnp.dot(acc.astype(BF16), wout_ref[...], preferred_element_type=F32)
    o_ref[...] = x_ref[...] + gate_ref[0] * yl


def _merge(ys, z, w_up, w_out, x, gate, seq_len):
    T, D = x.shape
    tm = min(512, seq_len)
    per = seq_len // tm
    Wy = ys[0].shape[1]
    yspec = pl.BlockSpec((tm, Wy), lambda i: (i, 0))
    gspecs = [pl.BlockSpec((tm, D), functools.partial(lambda i, n: (i, Z_GATE // D + n), n=n)) for n in range(4)]
    return pl.pallas_call(
        _merge_kernel,
        grid=(T // tm,),
        in_specs=[yspec] * 4 + gspecs + [
            pl.BlockSpec((4, Wy, D), lambda i: (0, 0, 0)), pl.BlockSpec((D, D), lambda i: (0, 0)),
            pl.BlockSpec((tm, D), lambda i: (i, 0)), pl.BlockSpec((1, 1, D), lambda i: (i // per, 0, 0))],
        out_specs=pl.BlockSpec((tm, D), lambda i: (i, 0)),
        out_shape=jax.ShapeDtypeStruct((T, D), F32),
        compiler_params=_cp(("parallel",)),
        name="merge_branches",
    )(*ys, z, z, z, z, w_up, w_out, x, gate)


def _mlp_kernel(x_ref, g_ref, sc_ref, sh_ref, w1_ref, b1_ref, w2_ref, b2_ref, gate_ref, fg_ref, o_ref,
                h_ref, acc_ref, *, nk, final):
    k = pl.program_id(1)

    @pl.when(k == 0)
    def _():
        y = _rms(x_ref[...], g_ref[...])
        h_ref[...] = (y * (1.0 + sc_ref[0]) + sh_ref[0]).astype(BF16)
        acc_ref[...] = jnp.zeros_like(acc_ref)

    a = jnp.maximum(jnp.dot(h_ref[...], w1_ref[...], preferred_element_type=F32) + b1_ref[...], 0.0)
    acc_ref[...] += jnp.dot((a * a).astype(BF16), w2_ref[...], preferred_element_type=F32)

    @pl.when(k == nk - 1)
    def _():
        out = x_ref[...] + gate_ref[0] * (acc_ref[...] + b2_ref[...])
        if final:
            out = _rms(out, fg_ref[...])
        o_ref[...] = out


def _mlp(x, g, sc, sh, w1, b1, w2, b2, gate, final_g, seq_len, final):
    T, D = x.shape
    F = w1.shape[1]
    tm = min(1024, seq_len)
    per = seq_len // tm
    tk = 512
    nk = F // tk
    row = lambda i, k: (i // per, 0, 0)
    return pl.pallas_call(
        functools.partial(_mlp_kernel, nk=nk, final=final),
        grid=(T // tm, nk),
        in_specs=[pl.BlockSpec((tm, D), lambda i, k: (i, 0)), pl.BlockSpec((1, D), lambda i, k: (0, 0)),
                  pl.BlockSpec((1, 1, D), row), pl.BlockSpec((1, 1, D), row),
                  pl.BlockSpec((D, tk), lambda i, k: (0, k)), pl.BlockSpec((1, tk), lambda i, k: (0, k)),
                  pl.BlockSpec((tk, D), lambda i, k: (k, 0)), pl.BlockSpec((1, D), lambda i, k: (0, 0)),
                  pl.BlockSpec((1, 1, D), row), pl.BlockSpec((1, D), lambda i, k: (0, 0))],
        out_specs=pl.BlockSpec((tm, D), lambda i, k: (i, 0)),
        out_shape=jax.ShapeDtypeStruct((T, D), F32),
        scratch_shapes=[pltpu.VMEM((tm, D), BF16), pltpu.VMEM((tm, D), F32)],
        compiler_params=_cp(("parallel", "arbitrary")),
        name="mlp",
    )(x, g.reshape(1, D), sc, sh, w1, b1.reshape(1, F), w2, b2.reshape(1, D), gate, final_g.reshape(1, D))


def _pack_w_in(w_in):
    hy_e = 3 * HY_CH
    ga_e = hy_e + (GA_HEADS + 2 * GA_KV) * GA_HD
    mw = ML_HEADS * ML_HD
    ml_e = ga_e + 4 * mw + 16
    wa_e = ml_e + (WA_HEADS + 2 * WA_KV) * WA_HD
    hy, ga = w_in[:, :hy_e], w_in[:, hy_e:ga_e]
    ml = w_in[:, ga_e:ml_e]
    wa = w_in[:, ml_e:wa_e]
    gate = w_in[:, wa_e:]
    waq, wakv = wa[:, :WA_HEADS * WA_HD], wa[:, WA_HEADS * WA_HD:]
    pad = jnp.zeros((w_in.shape[0], Z_COLS - Z_WAKV - wakv.shape[1]), w_in.dtype)
    packed = jnp.concatenate([hy, waq, ga, ml[:, :2 * mw], ml[:, 2 * mw:3 * mw], ml[:, 3 * mw:4 * mw], gate, wakv, pad],
                             axis=1)
    wg = jnp.pad(ml[:, 4 * mw:], ((0, 0), (0, LANES - 16)))
    return packed.astype(BF16), wg


def _token_mixers(zl, zc, gl, gc, p, B, L, Lc, with_ctx_out):
    ya_l = _hyena_branch(zl, p, B, L)
    ya_c = _hyena_branch(zc, p, B, Lc) if with_ctx_out else None
    cos, sin = _rope_tables(L, GA_HD)
    ql, kl, vl = _ga_prep(zl, cos, sin, p['ga_q_g'], p['ga_k_g'], B, L, True)
    qc, kc, vc = _ga_prep(zc, cos[:Lc], sin[:Lc], p['ga_q_g'], p['ga_k_g'], B, Lc, False)
    r3 = lambda a, n: a.reshape(B, n, a.shape[-1])
    k_all = jnp.concatenate([r3(kl, L), r3(kc, Lc)], axis=1)
    vt_all = jnp.concatenate([vl, vc], axis=2)
    yb_l = _global_attention(ql, k_all, vt_all).reshape(B * L, -1)
    yb_c = _global_attention(qc, r3(kc, Lc), vc).reshape(B * Lc, -1) if with_ctx_out else None
    mq_l, mk_l, mv_l = _ml_prep(zl, p['ml_conv_w'], p['ml_conv_b'], B, L)
    mq_c, mk_c, mv_c = _ml_prep(zc, p['ml_conv_w'], p['ml_conv_b'], B, Lc)
    gb = jnp.pad(p['ml_gate_b'].astype(F32), (0, LANES - 16)).reshape(1, LANES)
    c0 = jnp.zeros((B, ML_HEADS, 2 * ML_HD, ML_HD), F32)
    m0 = jnp.zeros((B, ML_HEADS, 8, LANES), F32)
    zl3, zc3, gl3, gc3 = r3(zl, L), r3(zc, Lc), r3(gl, L), r3(gc, Lc)
    h_cf, cf, mf = _mlstm_scan(mq_c, r3(mk_c, Lc), mv_c, zc3, gc3, gb, c0, m0, False)
    yc_c, cb, mb = _mlstm_scan(mq_c, r3(mk_c, Lc), mv_c, zc3, gc3, gb, c0, m0, True, h_cf, p['ml_norm_g'])
    h_lf, _, _ = _mlstm_scan(mq_l, r3(mk_l, L), mv_l, zl3, gl3, gb, cf, mf, False)
    yc_l, _, _ = _mlstm_scan(mq_l, r3(mk_l, L), mv_l, zl3, gl3, gb, cb, mb, True, h_lf, p['ml_norm_g'])
    yc_l = yc_l.reshape(B * L, -1)
    yc_c = yc_c.reshape(B * Lc, -1)
    cosw, sinw = _rope_tables(L, WA_HD)
    wq_l, wk_l, wv_l = _wa_prep(zl, cosw, sinw, B, L, True)
    wq_c, wk_c, wv_c = _wa_prep(zc, cosw[:Lc], sinw[:Lc], B, Lc, False)
    yd_l = _window_attention(wq_l, r3(wk_c, Lc), wv_c, p['wa_sink'], r3(wk_l, L), wv_l).reshape(B * L, -1)
    yd_c = (_window_attention(wq_c, r3(wk_c, Lc), wv_c, p['wa_sink']).reshape(B * Lc, -1)
            if with_ctx_out else None)
    return (ya_l, yb_l, yc_l, yd_l), (ya_c, yb_c, yc_c, yd_c)


def kernel(x, c, ctx, c_ctx, w_mod, b_mod, ln1_g, ln2_g, w_in, hy_conv_w, hy_conv_b,
           hy_pe_w1, hy_pe_b1, hy_freq, hy_pe_w2, hy_pe_b2, hy_pe_w3, hy_decay, hy_skip,
           ga_q_g, ga_k_g, ml_conv_w, ml_conv_b, ml_gate_b, ml_norm_g, wa_sink, w_up, w_out,
           mlp_w1, mlp_b1, mlp_w2, mlp_b2, final_g):
    B, L, D = x.shape
    Lc = ctx.shape[1]
    R = -(-(B + 1) // 8) * 8
    cvec = jnp.zeros((R, D), F32).at[:B].set(c).at[B].set(c_ctx)
    mod = _modulation(cvec, w_mod, b_mod)
    xl = x.reshape(B * L, D)
    xc = ctx.reshape(B * Lc, D)
    for l in range(DEPTH):
        with_ctx_out = l < DEPTH - 1
        p = dict(hy_conv_w=hy_conv_w[l], hy_conv_b=hy_conv_b[l], hy_pe_w1=hy_pe_w1[l],
                 hy_pe_b1=hy_pe_b1[l], hy_freq=hy_freq[l], hy_pe_w2=hy_pe_w2[l], hy_pe_b2=hy_pe_b2[l],
                 hy_pe_w3=hy_pe_w3[l], hy_decay=hy_decay[l], hy_skip=hy_skip[l],
                 ga_q_g=ga_q_g[l], ga_k_g=ga_k_g[l], ml_conv_w=ml_conv_w[l], ml_conv_b=ml_conv_b[l],
                 ml_gate_b=ml_gate_b[l], ml_norm_g=ml_norm_g[l], wa_sink=wa_sink[l])
        ml_rows = mod[l, :B].reshape(B, 1, 6 * D)
        mc_rows = jnp.broadcast_to(mod[l, B].reshape(1, 1, 6 * D), (B, 1, 6 * D))
        part = lambda m, n: m[:, :, n * D:(n + 1) * D]
        w_pack, w_gate = _pack_w_in(w_in[l])
        zl = _normmod_matmul(xl, ln1_g[l], part(ml_rows, 1), part(ml_rows, 0), w_pack, L, Z_TN, BF16)
        zc = _normmod_matmul(xc, ln1_g[l], part(mc_rows, 1), part(mc_rows, 0), w_pack, Lc, Z_TN, BF16)
        gl = _normmod_matmul(xl, ln1_g[l], part(ml_rows, 1), part(ml_rows, 0), w_gate.astype(BF16), L, LANES, F32)
        gc = _normmod_matmul(xc, ln1_g[l], part(mc_rows, 1), part(mc_rows, 0), w_gate.astype(BF16), Lc, LANES, F32)
        ys_l, ys_c = _token_mixers(zl, zc, gl, gc, p, B, L, Lc, with_ctx_out)
        wup = w_up[l].astype(BF16)
        wout = w_out[l].astype(BF16)
        w1, w2 = mlp_w1[l].astype(BF16), mlp_w2[l].astype(BF16)
        xl = _merge(ys_l, zl, wup, wout, xl, part(ml_rows, 2), L)
        xl = _mlp(xl, ln2_g[l], part(ml_rows, 4), part(ml_rows, 3), w1, mlp_b1[l], w2, mlp_b2[l],
                  part(ml_rows, 5), final_g, L, final=(l == DEPTH - 1))
        if with_ctx_out:
            xc = _merge(ys_c, zc, wup, wout, xc, part(mc_rows, 2), Lc)
            xc = _mlp(xc, ln2_g[l], part(mc_rows, 4), part(mc_rows, 3), w1, mlp_b1[l], w2, mlp_b2[l],
                      part(mc_rows, 5), final_g, Lc, final=False)
    return xl.reshape(B, L, D)
```

```python
import functools
import math

import jax
import jax.numpy as jnp
import numpy as np
from jax import lax
from jax.experimental import pallas as pl
from jax.experimental.pallas import tpu as pltpu

F32 = jnp.float32
BF16 = jnp.bfloat16
HIGHEST = lax.Precision.HIGHEST

D_MODEL = 1024
DEPTH = 2
GRID_W = 64
HY_CH = 512
HY_ORDER = 2
HY_BANDS = 16
HY_EMB = 1 + 2 * HY_BANDS
HY_FFN = 64
GA_HEADS, GA_KV, GA_HD = 4, 2, 128
ML_HEADS, ML_HD = 4, 128
WA_HEADS, WA_KV, WA_HD = 8, 2, 64
WINDOW = 128
ROPE_BASE = 10000.0
D_FF = 4 * D_MODEL
EPS = 1e-6
NEG = -1e30
LOG2E = 1.4426950408889634

LANES = 128
V7X_VMEM_LIMIT = 48 * 1024 * 1024

Z_HY = 0
Z_WAQ = 1536
Z_GA = 2048
Z_MLQK = 3072
Z_MLV = 4096
Z_MLO = 4608
Z_GATE = 5120
Z_WAKV = 9216
Z_COLS = 9728
Z_TN = 2432

ML_CHUNK = 128
FFT_N2 = 128
GA_CB = 256
WA_PAIR = 2
GA_UNROLL = 8


def _cp(sem, vmem=V7X_VMEM_LIMIT):
    return pltpu.CompilerParams(dimension_semantics=sem, vmem_limit_bytes=vmem)


def _rms(x, g):
    return x * lax.rsqrt(jnp.mean(x * x, axis=-1, keepdims=True) + EPS) * g


def _mod_kernel(c_ref, w_ref, b_ref, o_ref):
    c = c_ref[...]
    s = c * jax.nn.sigmoid(c)
    o_ref[0] = jnp.dot(s, w_ref[0], preferred_element_type=F32, precision=HIGHEST) + b_ref[0]


def _modulation(cvec, w_mod, b_mod):
    R = cvec.shape[0]
    tn = 1536
    return pl.pallas_call(
        _mod_kernel,
        grid=(DEPTH, 6 * D_MODEL // tn),
        in_specs=[pl.BlockSpec((R, D_MODEL), lambda l, j: (0, 0)),
                  pl.BlockSpec((1, D_MODEL, tn), lambda l, j: (l, 0, j)),
                  pl.BlockSpec((1, 1, tn), lambda l, j: (l, 0, j))],
        out_specs=pl.BlockSpec((1, R, tn), lambda l, j: (l, 0, j)),
        out_shape=jax.ShapeDtypeStruct((DEPTH, R, 6 * D_MODEL), F32),
        compiler_params=_cp(("parallel", "parallel")),
        name="modulation",
    )(cvec, w_mod, b_mod.reshape(DEPTH, 1, 6 * D_MODEL))


def _nmm_kernel(x_ref, g_ref, sc_ref, sh_ref, w_ref, o_ref, h_ref):
    @pl.when(pl.program_id(1) == 0)
    def _():
        y = _rms(x_ref[...], g_ref[...])
        h_ref[...] = (y * (1.0 + sc_ref[0]) + sh_ref[0]).astype(BF16)

    o_ref[...] = jnp.dot(h_ref[...], w_ref[...], preferred_element_type=F32).astype(o_ref.dtype)


def _normmod_matmul(x, g, sc, sh, w, seq_len, tn, out_dtype):
    T, D = x.shape
    N = w.shape[1]
    tm = min(1024, seq_len)
    per = seq_len // tm
    return pl.pallas_call(
        _nmm_kernel,
        grid=(T // tm, N // tn),
        in_specs=[pl.BlockSpec((tm, D), lambda i, j: (i, 0)),
                  pl.BlockSpec((1, D), lambda i, j: (0, 0)),
                  pl.BlockSpec((1, 1, D), lambda i, j: (i // per, 0, 0)),
                  pl.BlockSpec((1, 1, D), lambda i, j: (i // per, 0, 0)),
                  pl.BlockSpec((D, tn), lambda i, j: (0, j))],
        out_specs=pl.BlockSpec((tm, tn), lambda i, j: (i, j)),
        out_shape=jax.ShapeDtypeStruct((T, N), out_dtype),
        scratch_shapes=[pltpu.VMEM((tm, D), BF16)],
        compiler_params=_cp(("parallel", "arbitrary")),
        name="normmod_matmul",
    )(x, g.reshape(1, D), sc, sh, w)


def _conv3(u, prev_row, next_row, w_ref, b_ref, c0, c1):
    tm = u.shape[0]
    row = lax.broadcasted_iota(jnp.int32, u.shape, 0)
    up = jnp.where(row == 0, prev_row, pltpu.roll(u, 1, 0))
    dn = jnp.where(row == tm - 1, next_row, pltpu.roll(u, tm - 1, 0))
    return (w_ref[0:1, c0:c1] * up + w_ref[1:2, c0:c1] * u + w_ref[2:3, c0:c1] * dn + b_ref[0:1, c0:c1])


def _halo_rows(zp_ref, zn_ref, per, c0, c1):
    i = pl.program_id(0)
    first = (i % per) == 0
    last = (i % per) == per - 1
    hp = zp_ref.shape[0]
    prev_row = jnp.where(first, 0.0, zp_ref[hp - 1:hp, c0:c1].astype(F32))
    next_row = jnp.where(last, 0.0, zn_ref[0:1, c0:c1].astype(F32))
    return prev_row, next_row


HALO = 16


def _halo_specs(tm, width, col_block, n_rows):
    nb = n_rows // HALO
    r = tm // HALO
    return [pl.BlockSpec((tm, width), lambda i: (i, col_block)),
            pl.BlockSpec((HALO, width), lambda i: (jnp.maximum(i * r - 1, 0), col_block)),
            pl.BlockSpec((HALO, width), lambda i: (jnp.minimum((i + 1) * r, nb - 1), col_block))]


def _hy_prep_kernel(z_ref, zp_ref, zn_ref, w_ref, b_ref, v_ref, x1_ref, x2_ref, *, per):
    outs = (v_ref, x1_ref, x2_ref)
    for c in range(3):
        c0, c1 = c * HY_CH, (c + 1) * HY_CH
        prev_row, next_row = _halo_rows(zp_ref, zn_ref, per, c0, c1)
        u = z_ref[:, c0:c1].astype(F32)
        outs[c][...] = _conv3(u, prev_row, next_row, w_ref, b_ref, c0, c1).astype(outs[c].dtype)


def _hy_prep(z, conv_w, conv_b, seq_len):
    T = z.shape[0]
    tm = min(512, seq_len)
    W = 3 * HY_CH
    out = jax.ShapeDtypeStruct((T, HY_CH), BF16)
    return pl.pallas_call(
        functools.partial(_hy_prep_kernel, per=seq_len // tm),
        grid=(T // tm,),
        in_specs=_halo_specs(tm, W, Z_HY // W, T) + [
            pl.BlockSpec((3, W), lambda i: (0, 0)), pl.BlockSpec((1, W), lambda i: (0, 0))],
        out_specs=[pl.BlockSpec((tm, HY_CH), lambda i: (i, 0))] * 3,
        out_shape=[out, out, out],
        compiler_params=_cp(("parallel",)),
        name="hyena_prep",
    )(z, z, z, conv_w, conv_b.reshape(1, W))


def _ml_prep_kernel(z_ref, zp_ref, zn_ref, zv_ref, w_ref, b_ref, q_ref, k_ref, v_ref, *, per):
    W = ML_HEADS * ML_HD
    for c in range(2):
        c0, c1 = c * W, (c + 1) * W
        prev_row, next_row = _halo_rows(zp_ref, zn_ref, per, c0, c1)
        u = z_ref[:, c0:c1].astype(F32)
        y = _conv3(u, prev_row, next_row, w_ref, b_ref, c0, c1)
        y = y * jax.nn.sigmoid(y)
        if c == 0:
            for h in range(ML_HEADS):
                q_ref[0, h * ML_HD:(h + 1) * ML_HD, :] = y[:, h * ML_HD:(h + 1) * ML_HD].T.astype(q_ref.dtype)
        else:
            k_ref[...] = (y * (ML_HD ** -0.5)).astype(k_ref.dtype)
    for h in range(ML_HEADS):
        v_ref[0, h * ML_HD:(h + 1) * ML_HD, :] = zv_ref[:, h * ML_HD:(h + 1) * ML_HD].astype(F32).T.astype(v_ref.dtype)


def _ml_prep(z, conv_w, conv_b, B, seq_len):
    T = z.shape[0]
    tm = min(512, seq_len)
    per = seq_len // tm
    W = 2 * ML_HEADS * ML_HD
    Wh = W // 2
    tspec = pl.BlockSpec((1, Wh, tm), lambda i: (i // per, 0, i % per))
    tshape = jax.ShapeDtypeStruct((B, Wh, seq_len), BF16)
    return pl.pallas_call(
        functools.partial(_ml_prep_kernel, per=per),
        grid=(T // tm,),
        in_specs=_halo_specs(tm, W, Z_MLQK // W, T) + [
            pl.BlockSpec((tm, Wh), lambda i: (i, Z_MLV // Wh)),
            pl.BlockSpec((3, W), lambda i: (0, 0)), pl.BlockSpec((1, W), lambda i: (0, 0))],
        out_specs=[tspec, pl.BlockSpec((tm, Wh), lambda i: (i, 0)), tspec],
        out_shape=[tshape, jax.ShapeDtypeStruct((T, Wh), BF16), tshape],
        compiler_params=_cp(("parallel",)),
        name="mlstm_prep",
    )(z, z, z, z, conv_w, conv_b.reshape(1, W))


def _rope_tables(L, hd):
    quarter = hd // 4
    inv = ROPE_BASE ** (-jnp.arange(quarter, dtype=F32) / quarter)
    t = jnp.arange(L)
    row = (t // GRID_W).astype(F32)
    col = (t % GRID_W).astype(F32)
    lane = jnp.arange(LANES)
    within = lane % hd
    is_col = (within // (hd // 2)) == 1
    second = ((within % (hd // 2)) // quarter) == 1
    j = within % quarter
    pos = jnp.where(is_col[None, :], col[:, None], row[:, None])
    ang = pos * inv[j][None, :]
    return jnp.cos(ang), jnp.where(second[None, :], jnp.sin(ang), -jnp.sin(ang))


def _rope(x, cos, sin, quarter):
    lane = lax.broadcasted_iota(jnp.int32, x.shape, 1)
    first = ((lane % (2 * quarter)) // quarter) == 0
    partner = jnp.where(first, pltpu.roll(x, LANES - quarter, 1), pltpu.roll(x, quarter, 1))
    return x * cos + partner * sin


def _ga_prep_kernel(z_ref, cos_ref, sin_ref, qg_ref, kg_ref, q_ref, k_ref, v_ref, *, rope):
    nq, nk = GA_HEADS, GA_KV
    for h in range(nq + nk):
        x = z_ref[:, h * GA_HD:(h + 1) * GA_HD].astype(F32)
        g = qg_ref[...] if h < nq else kg_ref[...]
        y = _rms(x, g)
        if rope:
            y = _rope(y, cos_ref[...], sin_ref[...], GA_HD // 4)
        if h < nq:
            q_ref[0, h * GA_HD:(h + 1) * GA_HD, :] = (y * (GA_HD ** -0.5 * LOG2E)).T.astype(q_ref.dtype)
        else:
            k_ref[:, (h - nq) * GA_HD:(h - nq + 1) * GA_HD] = y.astype(k_ref.dtype)
    for h in range(nk):
        v = z_ref[:, (nq + nk + h) * GA_HD:(nq + nk + h + 1) * GA_HD].astype(F32)
        v_ref[0, h * GA_HD:(h + 1) * GA_HD, :] = v.T.astype(v_ref.dtype)


def _ga_prep(z, cos, sin, qg, kg, B, seq_len, rope):
    T = z.shape[0]
    tm = min(512, seq_len)
    per = seq_len // tm
    W = (GA_HEADS + 2 * GA_KV) * GA_HD
    return pl.pallas_call(
        functools.partial(_ga_prep_kernel, rope=rope),
        grid=(T // tm,),
        in_specs=[pl.BlockSpec((tm, W), lambda i: (i, Z_GA // W)),
                  pl.BlockSpec((tm, LANES), lambda i: (i % per, 0)),
                  pl.BlockSpec((tm, LANES), lambda i: (i % per, 0)),
                  pl.BlockSpec((1, GA_HD), lambda i: (0, 0)),
                  pl.BlockSpec((1, GA_HD), lambda i: (0, 0))],
        out_specs=[pl.BlockSpec((1, GA_HEADS * GA_HD, tm), lambda i: (i // per, 0, i % per)),
                   pl.BlockSpec((tm, GA_KV * GA_HD), lambda i: (i, 0)),
                   pl.BlockSpec((1, GA_KV * GA_HD, tm), lambda i: (i // per, 0, i % per))],
        out_shape=[jax.ShapeDtypeStruct((B, GA_HEADS * GA_HD, seq_len), BF16),
                   jax.ShapeDtypeStruct((T, GA_KV * GA_HD), BF16),
                   jax.ShapeDtypeStruct((B, GA_KV * GA_HD, seq_len), BF16)],
        compiler_params=_cp(("parallel",)),
        name="global_attn_prep",
    )(z, cos, sin, qg.reshape(1, GA_HD), kg.reshape(1, GA_HD))


def _wa_prep_kernel(zq_ref, zkv_ref, cos_ref, sin_ref, q_ref, k_ref, v_ref, *, rope):
    quarter = WA_HD // 4
    for j in range(WA_HEADS * WA_HD // LANES):
        x = zq_ref[:, j * LANES:(j + 1) * LANES].astype(F32)
        if rope:
            x = _rope(x, cos_ref[...], sin_ref[...], quarter)
        q_ref[0, j * LANES:(j + 1) * LANES, :] = (x * (WA_HD ** -0.5 * LOG2E)).T.astype(q_ref.dtype)
    k = zkv_ref[:, 0:LANES].astype(F32)
    if rope:
        k = _rope(k, cos_ref[...], sin_ref[...], quarter)
    k_ref[...] = k.astype(k_ref.dtype)
    v_ref[0] = zkv_ref[:, LANES:].astype(F32).T.astype(v_ref.dtype)


def _wa_prep(z, cos, sin, B, seq_len, rope):
    T = z.shape[0]
    tm = min(512, seq_len)
    per = seq_len // tm
    WQ = WA_HEADS * WA_HD
    return pl.pallas_call(
        functools.partial(_wa_prep_kernel, rope=rope),
        grid=(T // tm,),
        in_specs=[pl.BlockSpec((tm, WQ), lambda i: (i, Z_WAQ // WQ)),
                  pl.BlockSpec((tm, 2 * LANES), lambda i: (i, Z_WAKV // (2 * LANES))),
                  pl.BlockSpec((tm, LANES), lambda i: (i % per, 0)),
                  pl.BlockSpec((tm, LANES), lambda i: (i % per, 0))],
        out_specs=[pl.BlockSpec((1, WQ, tm), lambda i: (i // per, 0, i % per)),
                   pl.BlockSpec((tm, LANES), lambda i: (i, 0)),
                   pl.BlockSpec((1, LANES, tm), lambda i: (i // per, 0, i % per))],
        out_shape=[jax.ShapeDtypeStruct((B, WQ, seq_len), BF16),
                   jax.ShapeDtypeStruct((T, LANES), BF16),
                   jax.ShapeDtypeStruct((B, LANES, seq_len), BF16)],
        compiler_params=_cp(("parallel",)),
        name="window_attn_prep",
    )(z, z, cos, sin)


def _ga_kernel(q_ref, k_ref, vt_ref, o_ref, acc_ref, m_ref, l_ref, *, nchunks, tk, tq):
    acc_ref[...] = jnp.zeros_like(acc_ref)
    l_ref[...] = jnp.zeros_like(l_ref)
    m_ref[...] = jnp.full_like(m_ref, NEG)
    qt = jnp.concatenate([q_ref[0, 0:GA_HD, :], q_ref[0, GA_HD:, :]], axis=1)

    W = GA_CB
    nblk = 2 * tq // W

    def scores(k):
        return tuple(jnp.dot(k, qt[:, i * W:(i + 1) * W], preferred_element_type=F32) for i in range(nblk))

    def softmax_pv(s_blocks, vt):
        cols = [slice(i * W, (i + 1) * W) for i in range(nblk)]
        m_old = [m_ref[:, cs] for cs in cols]
        m_new = [jnp.maximum(mo, jnp.max(s, axis=0, keepdims=True)) for mo, s in zip(m_old, s_blocks)]
        p = [jnp.exp2(s - mn) for s, mn in zip(s_blocks, m_new)]
        alpha = [jnp.exp2(mo - mn) for mo, mn in zip(m_old, m_new)]
        pv = [jnp.dot(vt, pi.astype(BF16), preferred_element_type=F32) for pi in p]
        for i, cs in enumerate(cols):
            l_ref[:, cs] = alpha[i] * l_ref[:, cs] + jnp.sum(p[i], axis=0, keepdims=True)
            acc_ref[:, cs] = alpha[i] * acc_ref[:, cs] + pv[i]
            m_ref[:, cs] = m_new[i]

    def body(j, s):
        s_next = scores(k_ref[0, pl.ds(pl.multiple_of((j + 1) * tk, tk), tk), :])
        softmax_pv(s, vt_ref[0, :, pl.ds(pl.multiple_of(j * tk, tk), tk)])
        return s_next
    s = scores(k_ref[0, 0:tk, :])
    if nchunks > 1:
        s = lax.fori_loop(0, nchunks - 1, body, s, unroll=GA_UNROLL if (nchunks - 1) % GA_UNROLL == 0 else 1)
    softmax_pv(s, vt_ref[0, :, (nchunks - 1) * tk:nchunks * tk])
    o = acc_ref[...] / l_ref[...]
    for h in range(2):
        o_ref[0, :, h * GA_HD:(h + 1) * GA_HD] = o[:, h * tq:(h + 1) * tq].T.astype(o_ref.dtype)


def _global_attention(qt, k, vt):
    B, _, Lq = qt.shape
    Lk = k.shape[1]
    tq = min(512, Lq)
    tk = 256
    W = 2 * GA_HD
    return pl.pallas_call(
        functools.partial(_ga_kernel, nchunks=Lk // tk, tk=tk, tq=tq),
        grid=(B, GA_KV, Lq // tq),
        in_specs=[pl.BlockSpec((1, W, tq), lambda b, g, i: (b, g, i)),
                  pl.BlockSpec((1, Lk, GA_HD), lambda b, g, i: (b, 0, g)),
                  pl.BlockSpec((1, GA_HD, Lk), lambda b, g, i: (b, g, 0))],
        out_specs=pl.BlockSpec((1, tq, W), lambda b, g, i: (b, i, g)),
        out_shape=jax.ShapeDtypeStruct((B, Lq, GA_HEADS * GA_HD), BF16),
        scratch_shapes=[pltpu.VMEM((GA_HD, 2 * tq), F32), pltpu.VMEM((1, 2 * tq), F32),
                        pltpu.VMEM((1, 2 * tq), F32)],
        compiler_params=_cp(("parallel", "parallel", "parallel")),
        name="global_attention",
    )(qt, k, vt)


def _wa_kernel(*refs, band, nq):
    if band:
        q_ref, kp_ref, kx_ref, kn_ref, vp_ref, vx_ref, vn_ref, kc_ref, vc_ref, sink_ref, o_ref = refs
    else:
        q_ref, kc_ref, vc_ref, sink_ref, o_ref = refs
    tq = q_ref.shape[2]
    Lc = kc_ref.shape[1]
    qi = pl.program_id(1)
    G = WA_HEADS // WA_KV
    cols = WA_PAIR * tq
    if band:
        nb = 3 * tq
        k = jnp.concatenate([kp_ref[0], kx_ref[0], kn_ref[0], kc_ref[0]], axis=0)
        vt = jnp.concatenate([vp_ref[0], vx_ref[0], vn_ref[0], vc_ref[0]], axis=1)
        c = lax.broadcasted_iota(jnp.int32, (nb + Lc, cols), 0)
        r = lax.broadcasted_iota(jnp.int32, (nb + Lc, cols), 1) % tq
        lo = jnp.where(qi == 0, tq, 0)
        hi = jnp.where(qi == nq - 1, 2 * tq, nb)
        valid = ((c >= r) & (c <= r + 2 * WINDOW) & (c >= lo) & (c < hi)) | (c >= nb)
    else:
        k, vt = kc_ref[0], vc_ref[0]
    nkeys = k.shape[0]
    ones_rows = jnp.ones((16, nkeys), BF16)
    zeros = jnp.zeros((WA_HD, cols), BF16)
    npr = WA_HEADS // WA_PAIR
    grp = [(pr * WA_PAIR) // G for pr in range(npr)]
    s = []
    for pr in range(npr):
        qg = jnp.concatenate([q_ref[0, (WA_PAIR * pr + h) * WA_HD:(WA_PAIR * pr + h + 1) * WA_HD, :]
                              for h in range(WA_PAIR)], axis=1)
        qpad = jnp.concatenate([qg, zeros] if grp[pr] == 0 else [zeros, qg], axis=0)
        sp = jnp.dot(k, qpad, preferred_element_type=F32)
        s.append(jnp.where(valid, sp, NEG) if band else sp)
    sink = [sink_ref[pr] for pr in range(npr)]
    m = [jnp.maximum(jnp.max(s[pr], axis=0, keepdims=True), sink[pr]) for pr in range(npr)]
    p = [jnp.exp2(s[pr] - m[pr]).astype(BF16) for pr in range(npr)]
    R = []
    for pr in range(npr):
        vaug = jnp.concatenate([vt[grp[pr] * WA_HD:(grp[pr] + 1) * WA_HD, :], ones_rows], axis=0)
        R.append(jnp.dot(vaug, p[pr], preferred_element_type=F32))
    for pr in range(npr):
        o = R[pr][:WA_HD] / (R[pr][WA_HD:WA_HD + 1] + jnp.exp2(sink[pr] - m[pr]))
        ot = jnp.concatenate([o[:, h * tq:(h + 1) * tq] for h in range(WA_PAIR)], axis=0)
        wo = WA_PAIR * WA_HD
        o_ref[0, :, pr * wo:(pr + 1) * wo] = ot.T.astype(o_ref.dtype)


def _window_attention(qt, kc, vtc, sink, kl=None, vtl=None):
    B, WQ, Lq = qt.shape
    Lc = kc.shape[1]
    tq = WINDOW
    nq = Lq // tq
    band = kl is not None
    npair = WA_HEADS // WA_PAIR
    sink_row = jnp.repeat(sink.astype(F32).reshape(npair, WA_PAIR) * LOG2E, tq, axis=1).reshape(npair, 1, WA_PAIR * tq)
    qspec = pl.BlockSpec((1, WQ, tq), lambda b, i: (b, 0, i))
    kcspec = pl.BlockSpec((1, Lc, LANES), lambda b, i: (b, 0, 0))
    vcspec = pl.BlockSpec((1, LANES, Lc), lambda b, i: (b, 0, 0))
    sspec = pl.BlockSpec((npair, 1, WA_PAIR * tq), lambda b, i: (0, 0, 0))
    if band:
        prev = lambda i: jnp.maximum(i - 1, 0)
        nxt = lambda i: jnp.minimum(i + 1, nq - 1)
        same = lambda i: i
        kspec = lambda f: pl.BlockSpec((1, tq, LANES), lambda b, i: (b, f(i), 0))
        vspec = lambda f: pl.BlockSpec((1, LANES, tq), lambda b, i: (b, 0, f(i)))
        in_specs = [qspec, kspec(prev), kspec(same), kspec(nxt), vspec(prev), vspec(same), vspec(nxt),
                    kcspec, vcspec, sspec]
        args = (qt, kl, kl, kl, vtl, vtl, vtl, kc, vtc, sink_row)
    else:
        in_specs, args = [qspec, kcspec, vcspec, sspec], (qt, kc, vtc, sink_row)
    return pl.pallas_call(
        functools.partial(_wa_kernel, band=band, nq=nq),
        grid=(B, nq),
        in_specs=in_specs,
        out_specs=pl.BlockSpec((1, tq, WQ), lambda b, i: (b, i, 0)),
        out_shape=jax.ShapeDtypeStruct((B, Lq, WQ), BF16),
        compiler_params=_cp(("parallel", "parallel")),
        name="window_attention",
    )(*args)


def _mlstm_kernel(*refs, reverse, nc, bb):
    if reverse:
        (qt_ref, k_ref, vt_ref, g_ref, gb_ref, c0_ref, m0_ref, hft_ref, o_ref, gn_ref,
         y_ref, cf_ref, mf_ref, c_scr, m_scr) = refs
    else:
        (qt_ref, k_ref, vt_ref, g_ref, gb_ref, c0_ref, m0_ref,
         y_ref, cf_ref, mf_ref, c_scr, m_scr) = refs
    T = ML_CHUNK
    d = 1 if reverse else 0
    step = pl.program_id(1)

    @pl.when(step == 0)
    def _():
        c_scr[...] = c0_ref[...]
        m_scr[...] = m0_ref[...]

    si = lax.broadcasted_iota(jnp.int32, (T, T), 0)
    ti = lax.broadcasted_iota(jnp.int32, (T, T), 1)
    tri = ((ti >= si) if reverse else (ti <= si)).astype(F32)
    mask_t = (si >= ti) if reverse else (si <= ti)
    ones_rows = jnp.ones((ML_HD, T), BF16)
    e_last = 0 if reverse else T - 1
    chains = [(bi, h) for bi in range(bb) for h in range(ML_HEADS)]
    gates = []
    for bi in range(bb):
        G = g_ref[bi] + gb_ref[...]
        LF = jax.nn.log_sigmoid(G)
        Bc = jnp.dot(tri, LF, preferred_element_type=F32, precision=HIGHEST)
        gates.append((G.T, Bc.T, Bc - pltpu.roll(G, 4, 1)))
    st, ph = {}, {}
    for bi, h in chains:
        hs = slice(h * ML_HD, (h + 1) * ML_HD)
        st[bi, h] = jnp.dot(k_ref[bi, :, hs], qt_ref[bi, hs, :], preferred_element_type=F32)
    for bi, h in chains:
        GT, BT, Dc = gates[bi]
        fl, il = d * 8 + 4 + h, d * 8 + h
        b_row, i_row = BT[fl:fl + 1, :], GT[il:il + 1, :]
        log_d = jnp.where(mask_t, b_row - Dc[:, fl:fl + 1], NEG)
        m_prev = m_scr[bi, h, 0:1, 0:1]
        m_inter = b_row + m_prev
        m_t = jnp.maximum(m_inter, jnp.max(log_d, axis=0, keepdims=True))
        wqk = (st[bi, h] * jnp.exp(log_d - m_t)).astype(BF16)
        b_end = BT[fl:fl + 1, e_last:e_last + 1]
        log_w = b_end - b_row + i_row
        m_next = jnp.maximum(b_end + m_prev, jnp.max(log_w, axis=1, keepdims=True))
        ph[bi, h] = (wqk, jnp.exp(m_inter - m_t), jnp.exp(-m_t), jnp.exp(log_w - m_next),
                     jnp.exp(b_end + m_prev - m_next), m_next)
    for bi, h in chains:
        hs = slice(h * ML_HD, (h + 1) * ML_HD)
        wqk, cs, em, w_row, decay, m_next = ph[bi, h]
        kh, qt = k_ref[bi, :, hs], qt_ref[bi, hs, :]
        vaug = jnp.concatenate([vt_ref[bi, hs, :], ones_rows], axis=0)
        R = (jnp.dot(vaug, wqk, preferred_element_type=F32)
             + cs * jnp.dot(c_scr[bi, h].astype(BF16), qt, preferred_element_type=F32))
        hh = R[:ML_HD] / jnp.maximum(jnp.abs(R[ML_HD:]), em)
        if reverse:
            hsum = hft_ref[bi, hs, :] + hh
            hn = hsum * lax.rsqrt(jnp.mean(hsum * hsum, axis=0, keepdims=True) + EPS) * gn_ref[hs, :]
            y_ref[bi, :, hs] = (jax.nn.sigmoid(o_ref[bi, :, hs].astype(F32)) * hn.T).astype(y_ref.dtype)
        else:
            y_ref[bi, hs, :] = hh
        wv = (vaug.astype(F32) * w_row).astype(BF16)
        c_scr[bi, h] = decay * c_scr[bi, h] + jnp.dot(wv, kh, preferred_element_type=F32)
        m_scr[bi, h] = jnp.broadcast_to(m_next, (8, LANES))

    @pl.when(step == nc - 1)
    def _():
        cf_ref[...] = c_scr[...]
        mf_ref[...] = m_scr[...]


def _mlstm_scan(qt, k, vt, z, gates, gate_b, c0, m0, reverse, hft=None, norm_g=None):
    B, L, W = k.shape
    T = ML_CHUNK
    nc = L // T
    bb = 4 if B % 4 == 0 else 2
    cj =(lambda j: nc - 1 - j) if reverse else (lambda j: j)
    tok = pl.BlockSpec((bb, T, W), lambda b, j: (b, cj(j), 0))
    ttok = pl.BlockSpec((bb, W, T), lambda b, j: (b, 0, cj(j)))
    cspec = pl.BlockSpec((bb, ML_HEADS, 2 * ML_HD, ML_HD), lambda b, j: (b, 0, 0, 0))
    mspec = pl.BlockSpec((bb, ML_HEADS, 8, LANES), lambda b, j: (b, 0, 0, 0))
    in_specs = [ttok, tok, ttok, pl.BlockSpec((bb, T, LANES), lambda b, j: (b, cj(j), 0)),
                pl.BlockSpec((1, LANES), lambda b, j: (0, 0)), cspec, mspec]
    args = [qt, k, vt, gates, gate_b, c0, m0]
    if reverse:
        in_specs += [ttok, pl.BlockSpec((bb, T, W), lambda b, j: (b, cj(j), Z_MLO // W)),
                     pl.BlockSpec((W, T), lambda b, j: (0, 0))]
        args += [hft, z, jnp.broadcast_to(norm_g.astype(F32).reshape(W, 1), (W, T))]
    return pl.pallas_call(
        functools.partial(_mlstm_kernel, reverse=reverse, nc=nc, bb=bb),
        grid=(B // bb, nc),
        in_specs=in_specs,
        out_specs=[tok if reverse else ttok, cspec, mspec],
        out_shape=[jax.ShapeDtypeStruct((B, L, W), BF16) if reverse else jax.ShapeDtypeStruct((B, W, L), F32),
                   jax.ShapeDtypeStruct(c0.shape, F32), jax.ShapeDtypeStruct(m0.shape, F32)],
        scratch_shapes=[pltpu.VMEM((bb, ML_HEADS, 2 * ML_HD, ML_HD), F32), pltpu.VMEM((bb, ML_HEADS, 8, LANES), F32)],
        compiler_params=_cp(("parallel", "arbitrary")),
        name="mlstm_reverse" if reverse else "mlstm_forward",
    )(*args)


def _fft_dims(Lp):
    n1 = 2 * Lp // FFT_N2
    nt1 = Lp // FFT_N2
    nk1 = -(-(n1 // 2 + 1) // 8) * 8
    return n1, nt1, nk1


def _fft_tables(Lp):
    n1, nt1, nk1 = _fft_dims(Lp)
    N = 2 * Lp
    k1 = jnp.arange(nk1)
    t1 = jnp.arange(nt1)
    ang_a = (2.0 * math.pi / n1) * ((k1[:, None] * t1[None, :]) % n1).astype(F32)
    fa = jnp.stack([jnp.cos(ang_a), -jnp.sin(ang_a)], axis=1).reshape(2 * nk1, nt1)
    wgt = jnp.where((k1 == 0) | (k1 == n1 // 2), 1.0, 2.0) * (k1 <= n1 // 2) / N
    fai = jnp.stack([jnp.cos(ang_a) * wgt[:, None], -jnp.sin(ang_a) * wgt[:, None]], axis=1)
    fai = fai.reshape(2 * nk1, nt1).T
    k2 = jnp.arange(FFT_N2)
    t2 = jnp.arange(FFT_N2)
    idx = (t2[None, None, :] * k1[:, None, None] + n1 * t2[None, None, :] * k2[None, :, None]) % N
    phi = (2.0 * math.pi / N) * idx.astype(F32)
    gr, gi = jnp.cos(phi), -jnp.sin(phi)
    gfwd = jnp.concatenate([jnp.concatenate([gr, -gi], axis=2), jnp.concatenate([gi, gr], axis=2)], axis=1)
    grt, git = jnp.swapaxes(gr, 1, 2), jnp.swapaxes(gi, 1, 2)
    ginv = jnp.concatenate([jnp.concatenate([grt, git], axis=2), jnp.concatenate([-git, grt], axis=2)], axis=1)
    eye = jnp.eye(FFT_TB, dtype=F32)
    return (jnp.kron(fa, eye).astype(BF16), jnp.kron(fai, eye).astype(BF16), gfwd.astype(BF16), ginv.astype(BF16))


FFT_TB = 8
FFT_CB = 512


def _fa_kernel(fa_ref, x_ref, o_ref):
    nt1, tb, cb = x_ref.shape[1:]
    x = x_ref[0].astype(F32).reshape(nt1 * tb, cb).astype(BF16)
    r = jnp.dot(fa_ref[...], x, preferred_element_type=F32)
    o_ref[0] = r.reshape(o_ref.shape[1], tb, cb).astype(o_ref.dtype)


def _fft_stage_a(fa, y, out_dtype):
    B, Lp, C = y.shape
    nt1 = Lp // FFT_N2
    rows = fa.shape[0] // FFT_TB
    return pl.pallas_call(
        _fa_kernel,
        grid=(B, FFT_N2 // FFT_TB, C // FFT_CB),
        in_specs=[pl.BlockSpec(fa.shape, lambda b, j, c: (0, 0)),
                  pl.BlockSpec((1, nt1, FFT_TB, FFT_CB), lambda b, j, c: (b, 0, j, c))],
        out_specs=pl.BlockSpec((1, rows, FFT_TB, FFT_CB), lambda b, j, c: (b, 0, j, c)),
        out_shape=jax.ShapeDtypeStruct((B, rows, FFT_N2, C), out_dtype),
        compiler_params=_cp(("parallel", "parallel", "parallel")),
        name="fft_stage_a",
    )(fa, y.reshape(B, nt1, FFT_N2, C))


def _fc_filter_kernel(g_ref, s_ref, ss_ref, h_ref, *, kb):
    C = HY_CH
    for i in range(kb):
        for n in range(HY_ORDER):
            cf, cb = (2 * n) * C, (2 * n + 1) * C
            scale = lax.rsqrt(ss_ref[0:1, cf:cf + C] + ss_ref[0:1, cb:cb + C] + EPS)
            sf = s_ref[0, 2 * i:2 * i + 2, :, cf:cf + C].reshape(2 * FFT_N2, C).astype(BF16)
            sb = s_ref[0, 2 * i:2 * i + 2, :, cb:cb + C].reshape(2 * FFT_N2, C).astype(BF16)
            xf = jnp.dot(g_ref[i], sf, preferred_element_type=F32)
            xb = jnp.dot(g_ref[i], sb, preferred_element_type=F32)
            h_ref[n, 2 * i] = (xf[:FFT_N2] + xb[:FFT_N2]) * scale
            h_ref[n, 2 * i + 1] = (xf[FFT_N2:] - xb[FFT_N2:]) * scale


def _fft_filter_spectrum(gfwd, s_filt, sumsq):
    rows = s_filt.shape[1]
    C4 = s_filt.shape[-1]
    kb = 2
    return pl.pallas_call(
        functools.partial(_fc_filter_kernel, kb=kb),
        grid=(rows // (2 * kb),),
        in_specs=[pl.BlockSpec((kb, 2 * FFT_N2, 2 * FFT_N2), lambda i: (i, 0, 0)),
                  pl.BlockSpec((1, 2 * kb, FFT_N2, C4), lambda i: (0, i, 0, 0)),
                  pl.BlockSpec((1, C4), lambda i: (0, 0))],
        out_specs=pl.BlockSpec((HY_ORDER, 2 * kb, FFT_N2, HY_CH), lambda i: (0, i, 0, 0)),
        out_shape=jax.ShapeDtypeStruct((HY_ORDER, rows, FFT_N2, HY_CH), F32),
        compiler_params=_cp(("parallel",)),
        name="fft_filter_spectrum",
    )(gfwd, s_filt, sumsq)


def _fc_kernel(g_ref, gi_ref, h_ref, s_ref, o_ref, *, kb):
    C = s_ref.shape[-1]
    x = [jnp.dot(g_ref[i], s_ref[0, 2 * i:2 * i + 2].reshape(2 * FFT_N2, C), preferred_element_type=F32)
         for i in range(kb)]
    z = []
    for i in range(kb):
        xr, xi = x[i][:FFT_N2], x[i][FFT_N2:]
        hr, hi = h_ref[0, 2 * i], h_ref[0, 2 * i + 1]
        z.append(jnp.concatenate([xr * hr - xi * hi, xr * hi + xi * hr], axis=0).astype(BF16))
    bm = [jnp.dot(gi_ref[i], z[i], preferred_element_type=F32) for i in range(kb)]
    for i in range(kb):
        o_ref[0, 2 * i:2 * i + 2] = bm[i].reshape(2, FFT_N2, C).astype(o_ref.dtype)


def _fft_stage_c(gfwd, ginv, hspec, order, s):
    B, rows, _, C = s.shape
    kb = 4
    sspec = pl.BlockSpec((1, 2 * kb, FFT_N2, C), lambda i, b: (b, i, 0, 0))
    gspec = pl.BlockSpec((kb, 2 * FFT_N2, 2 * FFT_N2), lambda i, b: (i, 0, 0))
    return pl.pallas_call(
        functools.partial(_fc_kernel, kb=kb),
        grid=(rows // (2 * kb), B),
        in_specs=[gspec, gspec,
                  pl.BlockSpec((1, 2 * kb, FFT_N2, C), lambda i, b: (order, i, 0, 0)), sspec],
        out_specs=sspec,
        out_shape=jax.ShapeDtypeStruct(s.shape, BF16),
        compiler_params=_cp(("parallel", "arbitrary")),
        name="fft_stage_c",
    )(gfwd, ginv, hspec, s)


def _fai_kernel(fai_ref, b_ref, y_ref, gate_ref, skip_ref, o_ref):
    nt1, tb, cb = y_ref.shape[1:]
    bm = b_ref[0].astype(F32).reshape(b_ref.shape[1] * tb, cb).astype(BF16)
    yf = jnp.dot(fai_ref[...], bm, preferred_element_type=F32)
    yin = y_ref[0].astype(F32).reshape(nt1 * tb, cb)
    gate = gate_ref[0].astype(F32).reshape(nt1 * tb, cb)
    o_ref[0] = (gate * (yf + skip_ref[...] * yin)).reshape(nt1, tb, cb).astype(o_ref.dtype)


def _fft_stage_a_inv(fai, bm, y, gate, skip):
    B, Lp, C = y.shape
    nt1 = Lp // FFT_N2
    rows = fai.shape[1] // FFT_TB
    tok = pl.BlockSpec((1, nt1, FFT_TB, FFT_CB), lambda b, j, c: (b, 0, j, c))
    out = pl.pallas_call(
        _fai_kernel,
        grid=(B, FFT_N2 // FFT_TB, C // FFT_CB),
        in_specs=[pl.BlockSpec(fai.shape, lambda b, j, c: (0, 0)),
                  pl.BlockSpec((1, rows, FFT_TB, FFT_CB), lambda b, j, c: (b, 0, j, c)),
                  tok, tok, pl.BlockSpec((1, FFT_CB), lambda b, j, c: (0, c))],
        out_specs=tok,
        out_shape=jax.ShapeDtypeStruct((B, nt1, FFT_N2, C), BF16),
        compiler_params=_cp(("parallel", "parallel", "parallel")),
        name="fft_stage_a_inv",
    )(fai, bm, y.reshape(B, nt1, FFT_N2, C), gate.reshape(B, nt1, FFT_N2, C), skip.astype(F32).reshape(1, C))
    return out.reshape(B, Lp, C)


def _hgen_kernel(z_ref, w1_ref, b1_ref, fr_ref, w2_ref, b2_ref, w3_ref, dec_ref, h_ref, ss_ref):
    i = pl.program_id(0)
    z = z_ref[...]
    tm = z.shape[0]
    h = jnp.sin(fr_ref[0:1, :] * (jnp.dot(z, w1_ref[...], preferred_element_type=F32, precision=HIGHEST)
                                 + b1_ref[...]))
    h = jnp.sin(fr_ref[1:2, :] * (jnp.dot(h, w2_ref[...], preferred_element_type=F32, precision=HIGHEST)
                                 + b2_ref[...]))
    h = jnp.dot(h, w3_ref[...], preferred_element_type=F32, precision=HIGHEST)
    h = h * jnp.exp(-z[:, 0:1] * jnp.abs(dec_ref[...]))
    row = lax.broadcasted_iota(jnp.int32, h.shape, 0) + i * tm
    col = lax.broadcasted_iota(jnp.int32, h.shape, 1)
    h = jnp.where((row == 0) & ((col // HY_CH) % 2 == 1), 0.0, h)
    h_ref[...] = h

    @pl.when(i == 0)
    def _():
        ss_ref[...] = jnp.zeros_like(ss_ref)

    ss_ref[...] += jnp.sum(h * h, axis=0, keepdims=True)


def _hyena_filters(L, p):
    t = jnp.arange(L, dtype=F32)
    tn = t / (L - 1)
    w = 2.0 * math.pi * t / L
    bands = jnp.linspace(1e-4, HY_BANDS - 1, HY_BANDS, dtype=F32)
    ang = w[:, None] * bands[None, :]
    z = jnp.concatenate([tn[:, None], jnp.cos(ang), -jnp.sin(ang)], axis=-1)
    z = jnp.pad(z, ((0, 0), (0, LANES - HY_EMB)))
    w1 = jnp.pad(p['hy_pe_w1'].astype(F32), ((0, LANES - HY_EMB), (0, 0)))
    nf = HY_ORDER * 2 * HY_CH
    tm = min(512, L)
    const = lambda shape: pl.BlockSpec(shape, lambda i: (0,) * len(shape))
    return pl.pallas_call(
        _hgen_kernel,
        grid=(L // tm,),
        in_specs=[pl.BlockSpec((tm, LANES), lambda i: (i, 0)), const((LANES, HY_FFN)), const((1, HY_FFN)),
                  const((2, HY_FFN)), const((HY_FFN, HY_FFN)), const((1, HY_FFN)), const((HY_FFN, nf)),
                  const((1, nf))],
        out_specs=[pl.BlockSpec((tm, nf), lambda i: (i, 0)), const((1, nf))],
        out_shape=[jax.ShapeDtypeStruct((L, nf), F32), jax.ShapeDtypeStruct((1, nf), F32)],
        compiler_params=_cp(("arbitrary",)),
        name="hyena_filter_gen",
    )(z, w1, p['hy_pe_b1'].reshape(1, HY_FFN), p['hy_freq'], p['hy_pe_w2'], p['hy_pe_b2'].reshape(1, HY_FFN),
      p['hy_pe_w3'], p['hy_decay'].reshape(1, nf))


def _hyena_branch(z, p, B, L):
    v, x1, x2 = _hy_prep(z, p['hy_conv_w'], p['hy_conv_b'], L)
    Lp = max(L, 2048)
    fa, fai, gfwd, ginv = _fft_tables(Lp)
    hfilt, sumsq = _hyena_filters(L, p)
    pad3 = lambda a: a.reshape(B, L, HY_CH) if Lp == L else jnp.pad(a.reshape(B, L, HY_CH), ((0, 0), (0, Lp - L), (0, 0)))
    hf = hfilt if Lp == L else jnp.pad(hfilt, ((0, Lp - L), (0, 0)))
    s_filt = _fft_stage_a(fa, hf[None], F32)
    hspec = _fft_filter_spectrum(gfwd, s_filt, sumsq)
    y = pad3(v)
    for n, gate in enumerate((pad3(x1), pad3(x2))):
        s = _fft_stage_a(fa, y, BF16)
        bm = _fft_stage_c(gfwd, ginv, hspec, n, s)
        y = _fft_stage_a_inv(fai, bm, y, gate, p['hy_skip'][n])
    return y[:, :L].reshape(B * L, HY_CH)


def _merge_kernel(ya_ref, yb_ref, yc_ref, yd_ref, g0_ref, g1_ref, g2_ref, g3_ref, wup_ref, wout_ref,
                  x_ref, gate_ref, o_ref):
    acc = None
    for n, (y_ref, g_ref) in enumerate(((ya_ref, g0_ref), (yb_ref, g1_ref), (yc_ref, g2_ref), (yd_ref, g3_ref))):
        t = jax.nn.sigmoid(g_ref[...].astype(F32)) * jnp.dot(y_ref[...], wup_ref[n], preferred_element_type=F32)
        acc = t if acc is None else acc + t
    yl = jnp.dot(acc.astype(BF16), wout_ref[...], preferred_element_type=F32)
    o_ref[...] = x_ref[...] + gate_ref[0] * yl


def _merge(ys, z, w_up, w_out, x, gate, seq_len):
    T, D = x.shape
    tm = min(512, seq_len)
    per = seq_len // tm
    Wy = ys[0].shape[1]
    yspec = pl.BlockSpec((tm, Wy), lambda i: (i, 0))
    gspecs = [pl.BlockSpec((tm, D), functools.partial(lambda i, n: (i, Z_GATE // D + n), n=n)) for n in range(4)]
    return pl.pallas_call(
        _merge_kernel,
        grid=(T // tm,),
        in_specs=[yspec] * 4 + gspecs + [
            pl.BlockSpec((4, Wy, D), lambda i: (0, 0, 0)), pl.BlockSpec((D, D), lambda i: (0, 0)),
            pl.BlockSpec((tm, D), lambda i: (i, 0)), pl.BlockSpec((1, 1, D), lambda i: (i // per, 0, 0))],
        out_specs=pl.BlockSpec((tm, D), lambda i: (i, 0)),
        out_shape=jax.ShapeDtypeStruct((T, D), F32),
        compiler_params=_cp(("parallel",)),
        name="merge_branches",
    )(*ys, z, z, z, z, w_up, w_out, x, gate)


def _mlp_kernel(x_ref, g_ref, sc_ref, sh_ref, w1_ref, b1_ref, w2_ref, b2_ref, gate_ref, fg_ref, o_ref,
                h_ref, acc_ref, *, nk, final):
    k = pl.program_id(1)

    @pl.when(k == 0)
    def _():
        y = _rms(x_ref[...], g_ref[...])
        h_ref[...] = (y * (1.0 + sc_ref[0]) + sh_ref[0]).astype(BF16)
        acc_ref[...] = jnp.zeros_like(acc_ref)

    a = jnp.maximum(jnp.dot(h_ref[...], w1_ref[...], preferred_element_type=F32) + b1_ref[...], 0.0)
    acc_ref[...] += jnp.dot((a * a).astype(BF16), w2_ref[...], preferred_element_type=F32)

    @pl.when(k == nk - 1)
    def _():
        out = x_ref[...] + gate_ref[0] * (acc_ref[...] + b2_ref[...])
        if final:
            out = _rms(out, fg_ref[...])
        o_ref[...] = out


def _mlp(x, g, sc, sh, w1, b1, w2, b2, gate, final_g, seq_len, final):
    T, D = x.shape
    F = w1.shape[1]
    tm = min(1024, seq_len)
    per = seq_len // tm
    tk = 512
    nk = F // tk
    row = lambda i, k: (i // per, 0, 0)
    return pl.pallas_call(
        functools.partial(_mlp_kernel, nk=nk, final=final),
        grid=(T // tm, nk),
        in_specs=[pl.BlockSpec((tm, D), lambda i, k: (i, 0)), pl.BlockSpec((1, D), lambda i, k: (0, 0)),
                  pl.BlockSpec((1, 1, D), row), pl.BlockSpec((1, 1, D), row),
                  pl.BlockSpec((D, tk), lambda i, k: (0, k)), pl.BlockSpec((1, tk), lambda i, k: (0, k)),
                  pl.BlockSpec((tk, D), lambda i, k: (k, 0)), pl.BlockSpec((1, D), lambda i, k: (0, 0)),
                  pl.BlockSpec((1, 1, D), row), pl.BlockSpec((1, D), lambda i, k: (0, 0))],
        out_specs=pl.BlockSpec((tm, D), lambda i, k: (i, 0)),
        out_shape=jax.ShapeDtypeStruct((T, D), F32),
        scratch_shapes=[pltpu.VMEM((tm, D), BF16), pltpu.VMEM((tm, D), F32)],
        compiler_params=_cp(("parallel", "arbitrary")),
        name="mlp",
    )(x, g.reshape(1, D), sc, sh, w1, b1.reshape(1, F), w2, b2.reshape(1, D), gate, final_g.reshape(1, D))


def _pack_w_in(w_in):
    hy_e = 3 * HY_CH
    ga_e = hy_e + (GA_HEADS + 2 * GA_KV) * GA_HD
    mw = ML_HEADS * ML_HD
    ml_e = ga_e + 4 * mw + 16
    wa_e = ml_e + (WA_HEADS + 2 * WA_KV) * WA_HD
    hy, ga = w_in[:, :hy_e], w_in[:, hy_e:ga_e]
    ml = w_in[:, ga_e:ml_e]
    wa = w_in[:, ml_e:wa_e]
    gate = w_in[:, wa_e:]
    waq, wakv = wa[:, :WA_HEADS * WA_HD], wa[:, WA_HEADS * WA_HD:]
    pad = jnp.zeros((w_in.shape[0], Z_COLS - Z_WAKV - wakv.shape[1]), w_in.dtype)
    packed = jnp.concatenate([hy, waq, ga, ml[:, :2 * mw], ml[:, 2 * mw:3 * mw], ml[:, 3 * mw:4 * mw], gate, wakv, pad],
                             axis=1)
    wg = jnp.pad(ml[:, 4 * mw:], ((0, 0), (0, LANES - 16)))
    return packed.astype(BF16), wg


def _token_mixers(zl, zc, gl, gc, p, B, L, Lc, with_ctx_out):
    ya_l = _hyena_branch(zl, p, B, L)
    ya_c = _hyena_branch(zc, p, B, Lc) if with_ctx_out else None
    cos, sin = _rope_tables(L, GA_HD)
    ql, kl, vl = _ga_prep(zl, cos, sin, p['ga_q_g'], p['ga_k_g'], B, L, True)
    qc, kc, vc = _ga_prep(zc, cos[:Lc], sin[:Lc], p['ga_q_g'], p['ga_k_g'], B, Lc, False)
    r3 = lambda a, n: a.reshape(B, n, a.shape[-1])
    k_all = jnp.concatenate([r3(kl, L), r3(kc, Lc)], axis=1)
    vt_all = jnp.concatenate([vl, vc], axis=2)
    yb_l = _global_attention(ql, k_all, vt_all).reshape(B * L, -1)
    yb_c = _global_attention(qc, r3(kc, Lc), vc).reshape(B * Lc, -1) if with_ctx_out else None
    mq_l, mk_l, mv_l = _ml_prep(zl, p['ml_conv_w'], p['ml_conv_b'], B, L)
    mq_c, mk_c, mv_c = _ml_prep(zc, p['ml_conv_w'], p['ml_conv_b'], B, Lc)
    gb = jnp.pad(p['ml_gate_b'].astype(F32), (0, LANES - 16)).reshape(1, LANES)
    c0 = jnp.zeros((B, ML_HEADS, 2 * ML_HD, ML_HD), F32)
    m0 = jnp.zeros((B, ML_HEADS, 8, LANES), F32)
    zl3, zc3, gl3, gc3 = r3(zl, L), r3(zc, Lc), r3(gl, L), r3(gc, Lc)
    h_cf, cf, mf = _mlstm_scan(mq_c, r3(mk_c, Lc), mv_c, zc3, gc3, gb, c0, m0, False)
    yc_c, cb, mb = _mlstm_scan(mq_c, r3(mk_c, Lc), mv_c, zc3, gc3, gb, c0, m0, True, h_cf, p['ml_norm_g'])
    h_lf, _, _ = _mlstm_scan(mq_l, r3(mk_l, L), mv_l, zl3, gl3, gb, cf, mf, False)
    yc_l, _, _ = _mlstm_scan(mq_l, r3(mk_l, L), mv_l, zl3, gl3, gb, cb, mb, True, h_lf, p['ml_norm_g'])
    yc_l = yc_l.reshape(B * L, -1)
    yc_c = yc_c.reshape(B * Lc, -1)
    cosw, sinw = _rope_tables(L, WA_HD)
    wq_l, wk_l, wv_l = _wa_prep(zl, cosw, sinw, B, L, True)
    wq_c, wk_c, wv_c = _wa_prep(zc, cosw[:Lc], sinw[:Lc], B, Lc, False)
    yd_l = _window_attention(wq_l, r3(wk_c, Lc), wv_c, p['wa_sink'], r3(wk_l, L), wv_l).reshape(B * L, -1)
    yd_c = (_window_attention(wq_c, r3(wk_c, Lc), wv_c, p['wa_sink']).reshape(B * Lc, -1)
            if with_ctx_out else None)
    return (ya_l, yb_l, yc_l, yd_l), (ya_c, yb_c, yc_c, yd_c)


def kernel(x, c, ctx, c_ctx, w_mod, b_mod, ln1_g, ln2_g, w_in, hy_conv_w, hy_conv_b,
           hy_pe_w1, hy_pe_b1, hy_freq, hy_pe_w2, hy_pe_b2, hy_pe_w3, hy_decay, hy_skip,
           ga_q_g, ga_k_g, ml_conv_w, ml_conv_b, ml_gate_b, ml_norm_g, wa_sink, w_up, w_out,
           mlp_w1, mlp_b1, mlp_w2, mlp_b2, final_g):
    B, L, D = x.shape
    Lc = ctx.shape[1]
    R = -(-(B + 1) // 8) * 8
    cvec = jnp.zeros((R, D), F32).at[:B].set(c).at[B].set(c_ctx)
    mod = _modulation(cvec, w_mod, b_mod)
    xl = x.reshape(B * L, D)
    xc = ctx.reshape(B * Lc, D)
    for l in range(DEPTH):
        with_ctx_out = l < DEPTH - 1
        p = dict(hy_conv_w=hy_conv_w[l], hy_conv_b=hy_conv_b[l], hy_pe_w1=hy_pe_w1[l],
                 hy_pe_b1=hy_pe_b1[l], hy_freq=hy_freq[l], hy_pe_w2=hy_pe_w2[l], hy_pe_b2=hy_pe_b2[l],
                 hy_pe_w3=hy_pe_w3[l], hy_decay=hy_decay[l], hy_skip=hy_skip[l],
                 ga_q_g=ga_q_g[l], ga_k_g=ga_k_g[l], ml_conv_w=ml_conv_w[l], ml_conv_b=ml_conv_b[l],
                 ml_gate_b=ml_gate_b[l], ml_norm_g=ml_norm_g[l], wa_sink=wa_sink[l])
        ml_rows = mod[l, :B].reshape(B, 1, 6 * D)
        mc_rows = jnp.broadcast_to(mod[l, B].reshape(1, 1, 6 * D), (B, 1, 6 * D))
        part = lambda m, n: m[:, :, n * D:(n + 1) * D]
        w_pack, w_gate = _pack_w_in(w_in[l])
        zl = _normmod_matmul(xl, ln1_g[l], part(ml_rows, 1), part(ml_rows, 0), w_pack, L, Z_TN, BF16)
        zc = _normmod_matmul(xc, ln1_g[l], part(mc_rows, 1), part(mc_rows, 0), w_pack, Lc, Z_TN, BF16)
        gl = _normmod_matmul(xl, ln1_g[l], part(ml_rows, 1), part(ml_rows, 0), w_gate.astype(BF16), L, LANES, F32)
        gc = _normmod_matmul(xc, ln1_g[l], part(mc_rows, 1), part(mc_rows, 0), w_gate.astype(BF16), Lc, LANES, F32)
        ys_l, ys_c = _token_mixers(zl, zc, gl, gc, p, B, L, Lc, with_ctx_out)
        wup = w_up[l].astype(BF16)
        wout = w_out[l].astype(BF16)
        w1, w2 = mlp_w1[l].astype(BF16), mlp_w2[l].astype(BF16)
        xl = _merge(ys_l, zl, wup, wout, xl, part(ml_rows, 2), L)
        xl = _mlp(xl, ln2_g[l], part(ml_rows, 4), part(ml_rows, 3), w1, mlp_b1[l], w2, mlp_b2[l],
                  part(ml_rows, 5), final_g, L, final=(l == DEPTH - 1))
        if with_ctx_out:
            xc = _merge(ys_c, zc, wup, wout, xc, part(mc_rows, 2), Lc)
            xc = _mlp(xc, ln2_g[l], part(mc_rows, 4), part(mc_rows, 3), w1, mlp_b1[l], w2, mlp_b2[l],
                      part(mc_rows, 5), final_g, Lc, final=False)
    return xl.reshape(B, L, D)
```

```python
import functools
import math

import jax
import jax.numpy as jnp
import numpy as np
from jax import lax
from jax.experimental import pallas as pl
from jax.experimental.pallas import tpu as pltpu

F32 = jnp.float32
BF16 = jnp.bfloat16
HIGHEST = lax.Precision.HIGHEST

D_MODEL = 1024
DEPTH = 2
GRID_W = 64
HY_CH = 512
HY_ORDER = 2
HY_BANDS = 16
HY_EMB = 1 + 2 * HY_BANDS
HY_FFN = 64
GA_HEADS, GA_KV, GA_HD = 4, 2, 128
ML_HEADS, ML_HD = 4, 128
WA_HEADS, WA_KV, WA_HD = 8, 2, 64
WINDOW = 128
ROPE_BASE = 10000.0
D_FF = 4 * D_MODEL
EPS = 1e-6
NEG = -1e30
LOG2E = 1.4426950408889634

LANES = 128
V7X_VMEM_LIMIT = 48 * 1024 * 1024

Z_HY = 0
Z_WAQ = 1536
Z_GA = 2048
Z_MLQK = 3072
Z_MLV = 4096
Z_MLO = 4608
Z_GATE = 5120
Z_WAKV = 9216
Z_COLS = 9728
Z_TN = 2432

ML_CHUNK = 128
FFT_N2 = 128
GA_CB = 256
WA_PAIR = 2
GA_UNROLL = 8


def _cp(sem, vmem=V7X_VMEM_LIMIT):
    return pltpu.CompilerParams(dimension_semantics=sem, vmem_limit_bytes=vmem)


def _rms(x, g):
    return x * lax.rsqrt(jnp.mean(x * x, axis=-1, keepdims=True) + EPS) * g


def _mod_kernel(c_ref, w_ref, b_ref, o_ref):
    c = c_ref[...]
    s = c * jax.nn.sigmoid(c)
    o_ref[0] = jnp.dot(s, w_ref[0], preferred_element_type=F32, precision=HIGHEST) + b_ref[0]


def _modulation(cvec, w_mod, b_mod):
    R = cvec.shape[0]
    tn = 1536
    return pl.pallas_call(
        _mod_kernel,
        grid=(DEPTH, 6 * D_MODEL // tn),
        in_specs=[pl.BlockSpec((R, D_MODEL), lambda l, j: (0, 0)),
                  pl.BlockSpec((1, D_MODEL, tn), lambda l, j: (l, 0, j)),
                  pl.BlockSpec((1, 1, tn), lambda l, j: (l, 0, j))],
        out_specs=pl.BlockSpec((1, R, tn), lambda l, j: (l, 0, j)),
        out_shape=jax.ShapeDtypeStruct((DEPTH, R, 6 * D_MODEL), F32),
        compiler_params=_cp(("parallel", "parallel")),
        name="modulation",
    )(cvec, w_mod, b_mod.reshape(DEPTH, 1, 6 * D_MODEL))


def _nmm_kernel(x_ref, g_ref, sc_ref, sh_ref, w_ref, o_ref, h_ref):
    @pl.when(pl.program_id(1) == 0)
    def _():
        y = _rms(x_ref[...], g_ref[...])
        h_ref[...] = (y * (1.0 + sc_ref[0]) + sh_ref[0]).astype(BF16)

    o_ref[...] = jnp.dot(h_ref[...], w_ref[...], preferred_element_type=F32).astype(o_ref.dtype)


def _normmod_matmul(x, g, sc, sh, w, seq_len, tn, out_dtype):
    T, D = x.shape
    N = w.shape[1]
    tm = min(1024, seq_len)
    per = seq_len // tm
    return pl.pallas_call(
        _nmm_kernel,
        grid=(T // tm, N // tn),
        in_specs=[pl.BlockSpec((tm, D), lambda i, j: (i, 0)),
                  pl.BlockSpec((1, D), lambda i, j: (0, 0)),
                  pl.BlockSpec((1, 1, D), lambda i, j: (i // per, 0, 0)),
                  pl.BlockSpec((1, 1, D), lambda i, j: (i // per, 0, 0)),
                  pl.BlockSpec((D, tn), lambda i, j: (0, j))],
        out_specs=pl.BlockSpec((tm, tn), lambda i, j: (i, j)),
        out_shape=jax.ShapeDtypeStruct((T, N), out_dtype),
        scratch_shapes=[pltpu.VMEM((tm, D), BF16)],
        compiler_params=_cp(("parallel", "arbitrary")),
        name="normmod_matmul",
    )(x, g.reshape(1, D), sc, sh, w)


def _conv3(u, prev_row, next_row, w_ref, b_ref, c0, c1):
    tm = u.shape[0]
    row = lax.broadcasted_iota(jnp.int32, u.shape, 0)
    up = jnp.where(row == 0, prev_row, pltpu.roll(u, 1, 0))
    dn = jnp.where(row == tm - 1, next_row, pltpu.roll(u, tm - 1, 0))
    return (w_ref[0:1, c0:c1] * up + w_ref[1:2, c0:c1] * u + w_ref[2:3, c0:c1] * dn + b_ref[0:1, c0:c1])


def _halo_rows(zp_ref, zn_ref, per, c0, c1):
    i = pl.program_id(0)
    first = (i % per) == 0
    last = (i % per) == per - 1
    hp = zp_ref.shape[0]
    prev_row = jnp.where(first, 0.0, zp_ref[hp - 1:hp, c0:c1].astype(F32))
    next_row = jnp.where(last, 0.0, zn_ref[0:1, c0:c1].astype(F32))
    return prev_row, next_row


HALO = 16


def _halo_specs(tm, width, col_block, n_rows):
    nb = n_rows // HALO
    r = tm // HALO
    return [pl.BlockSpec((tm, width), lambda i: (i, col_block)),
            pl.BlockSpec((HALO, width), lambda i: (jnp.maximum(i * r - 1, 0), col_block)),
            pl.BlockSpec((HALO, width), lambda i: (jnp.minimum((i + 1) * r, nb - 1), col_block))]


def _hy_prep_kernel(z_ref, zp_ref, zn_ref, w_ref, b_ref, v_ref, x1_ref, x2_ref, *, per):
    outs = (v_ref, x1_ref, x2_ref)
    for c in range(3):
        c0, c1 = c * HY_CH, (c + 1) * HY_CH
        prev_row, next_row = _halo_rows(zp_ref, zn_ref, per, c0, c1)
        u = z_ref[:, c0:c1].astype(F32)
        outs[c][...] = _conv3(u, prev_row, next_row, w_ref, b_ref, c0, c1).astype(outs[c].dtype)


def _hy_prep(z, conv_w, conv_b, seq_len):
    T = z.shape[0]
    tm = min(512, seq_len)
    W = 3 * HY_CH
    out = jax.ShapeDtypeStruct((T, HY_CH), BF16)
    return pl.pallas_call(
        functools.partial(_hy_prep_kernel, per=seq_len // tm),
        grid=(T // tm,),
        in_specs=_halo_specs(tm, W, Z_HY // W, T) + [
            pl.BlockSpec((3, W), lambda i: (0, 0)), pl.BlockSpec((1, W), lambda i: (0, 0))],
        out_specs=[pl.BlockSpec((tm, HY_CH), lambda i: (i, 0))] * 3,
        out_shape=[out, out, out],
        compiler_params=_cp(("parallel",)),
        name="hyena_prep",
    )(z, z, z, conv_w, conv_b.reshape(1, W))


def _ml_prep_kernel(z_ref, zp_ref, zn_ref, zv_ref, w_ref, b_ref, q_ref, k_ref, v_ref, *, per):
    W = ML_HEADS * ML_HD
    for c in range(2):
        c0, c1 = c * W, (c + 1) * W
        prev_row, next_row = _halo_rows(zp_ref, zn_ref, per, c0, c1)
        u = z_ref[:, c0:c1].astype(F32)
        y = _conv3(u, prev_row, next_row, w_ref, b_ref, c0, c1)
        y = y * jax.nn.sigmoid(y)
        if c == 0:
            for h in range(ML_HEADS):
                q_ref[0, h * ML_HD:(h + 1) * ML_HD, :] = y[:, h * ML_HD:(h + 1) * ML_HD].T.astype(q_ref.dtype)
        else:
            k_ref[...] = (y * (ML_HD ** -0.5)).astype(k_ref.dtype)
    for h in range(ML_HEADS):
        v_ref[0, h * ML_HD:(h + 1) * ML_HD, :] = zv_ref[:, h * ML_HD:(h + 1) * ML_HD].astype(F32).T.astype(v_ref.dtype)


def _ml_prep(z, conv_w, conv_b, B, seq_len):
    T = z.shape[0]
    tm = min(512, seq_len)
    per = seq_len // tm
    W = 2 * ML_HEADS * ML_HD
    Wh = W // 2
    tspec = pl.BlockSpec((1, Wh, tm), lambda i: (i // per, 0, i % per))
    tshape = jax.ShapeDtypeStruct((B, Wh, seq_len), BF16)
    return pl.pallas_call(
        functools.partial(_ml_prep_kernel, per=per),
        grid=(T // tm,),
        in_specs=_halo_specs(tm, W, Z_MLQK // W, T) + [
            pl.BlockSpec((tm, Wh), lambda i: (i, Z_MLV // Wh)),
            pl.BlockSpec((3, W), lambda i: (0, 0)), pl.BlockSpec((1, W), lambda i: (0, 0))],
        out_specs=[tspec, pl.BlockSpec((tm, Wh), lambda i: (i, 0)), tspec],
        out_shape=[tshape, jax.ShapeDtypeStruct((T, Wh), BF16), tshape],
        compiler_params=_cp(("parallel",)),
        name="mlstm_prep",
    )(z, z, z, z, conv_w, conv_b.reshape(1, W))


def _rope_tables(L, hd):
    quarter = hd // 4
    inv = ROPE_BASE ** (-jnp.arange(quarter, dtype=F32) / quarter)
    t = jnp.arange(L)
    row = (t // GRID_W).astype(F32)
    col = (t % GRID_W).astype(F32)
    lane = jnp.arange(LANES)
    within = lane % hd
    is_col = (within // (hd // 2)) == 1
    second = ((within % (hd // 2)) // quarter) == 1
    j = within % quarter
    pos = jnp.where(is_col[None, :], col[:, None], row[:, None])
    ang = pos * inv[j][None, :]
    return jnp.cos(ang), jnp.where(second[None, :], jnp.sin(ang), -jnp.sin(ang))


def _rope(x, cos, sin, quarter):
    lane = lax.broadcasted_iota(jnp.int32, x.shape, 1)
    first = ((lane % (2 * quarter)) // quarter) == 0
    partner = jnp.where(first, pltpu.roll(x, LANES - quarter, 1), pltpu.roll(x, quarter, 1))
    return x * cos + partner * sin


def _ga_prep_kernel(z_ref, cos_ref, sin_ref, qg_ref, kg_ref, q_ref, k_ref, v_ref, *, rope):
    nq, nk = GA_HEADS, GA_KV
    for h in range(nq + nk):
        x = z_ref[:, h * GA_HD:(h + 1) * GA_HD].astype(F32)
        g = qg_ref[...] if h < nq else kg_ref[...]
        y = _rms(x, g)
        if rope:
            y = _rope(y, cos_ref[...], sin_ref[...], GA_HD // 4)
        if h < nq:
            q_ref[0, h * GA_HD:(h + 1) * GA_HD, :] = (y * (GA_HD ** -0.5 * LOG2E)).T.astype(q_ref.dtype)
        else:
            k_ref[:, (h - nq) * GA_HD:(h - nq + 1) * GA_HD] = y.astype(k_ref.dtype)
    for h in range(nk):
        v = z_ref[:, (nq + nk + h) * GA_HD:(nq + nk + h + 1) * GA_HD].astype(F32)
        v_ref[0, h * GA_HD:(h + 1) * GA_HD, :] = v.T.astype(v_ref.dtype)


def _ga_prep(z, cos, sin, qg, kg, B, seq_len, rope):
    T = z.shape[0]
    tm = min(512, seq_len)
    per = seq_len // tm
    W = (GA_HEADS + 2 * GA_KV) * GA_HD
    return pl.pallas_call(
        functools.partial(_ga_prep_kernel, rope=rope),
        grid=(T // tm,),
        in_specs=[pl.BlockSpec((tm, W), lambda i: (i, Z_GA // W)),
                  pl.BlockSpec((tm, LANES), lambda i: (i % per, 0)),
                  pl.BlockSpec((tm, LANES), lambda i: (i % per, 0)),
                  pl.BlockSpec((1, GA_HD), lambda i: (0, 0)),
                  pl.BlockSpec((1, GA_HD), lambda i: (0, 0))],
        out_specs=[pl.BlockSpec((1, GA_HEADS * GA_HD, tm), lambda i: (i // per, 0, i % per)),
                   pl.BlockSpec((tm, GA_KV * GA_HD), lambda i: (i, 0)),
                   pl.BlockSpec((1, GA_KV * GA_HD, tm), lambda i: (i // per, 0, i % per))],
        out_shape=[jax.ShapeDtypeStruct((B, GA_HEADS * GA_HD, seq_len), BF16),
                   jax.ShapeDtypeStruct((T, GA_KV * GA_HD), BF16),
                   jax.ShapeDtypeStruct((B, GA_KV * GA_HD, seq_len), BF16)],
        compiler_params=_cp(("parallel",)),
        name="global_attn_prep",
    )(z, cos, sin, qg.reshape(1, GA_HD), kg.reshape(1, GA_HD))


def _wa_prep_kernel(zq_ref, zkv_ref, cos_ref, sin_ref, q_ref, k_ref, v_ref, *, rope):
    quarter = WA_HD // 4
    for j in range(WA_HEADS * WA_HD // LANES):
        x = zq_ref[:, j * LANES:(j + 1) * LANES].astype(F32)
        if rope:
            x = _rope(x, cos_ref[...], sin_ref[...], quarter)
        q_ref[0, j * LANES:(j + 1) * LANES, :] = (x * (WA_HD ** -0.5 * LOG2E)).T.astype(q_ref.dtype)
    k = zkv_ref[:, 0:LANES].astype(F32)
    if rope:
        k = _rope(k, cos_ref[...], sin_ref[...], quarter)
    k_ref[...] = k.astype(k_ref.dtype)
    v_ref[0] = zkv_ref[:, LANES:].astype(F32).T.astype(v_ref.dtype)


def _wa_prep(z, cos, sin, B, seq_len, rope):
    T = z.shape[0]
    tm = min(512, seq_len)
    per = seq_len // tm
    WQ = WA_HEADS * WA_HD
    return pl.pallas_call(
        functools.partial(_wa_prep_kernel, rope=rope),
        grid=(T // tm,),
        in_specs=[pl.BlockSpec((tm, WQ), lambda i: (i, Z_WAQ // WQ)),
                  pl.BlockSpec((tm, 2 * LANES), lambda i: (i, Z_WAKV // (2 * LANES))),
                  pl.BlockSpec((tm, LANES), lambda i: (i % per, 0)),
                  pl.BlockSpec((tm, LANES), lambda i: (i % per, 0))],
        out_specs=[pl.BlockSpec((1, WQ, tm), lambda i: (i // per, 0, i % per)),
                   pl.BlockSpec((tm, LANES), lambda i: (i, 0)),
                   pl.BlockSpec((1, LANES, tm), lambda i: (i // per, 0, i % per))],
        out_shape=[jax.ShapeDtypeStruct((B, WQ, seq_len), BF16),
                   jax.ShapeDtypeStruct((T, LANES), BF16),
                   jax.ShapeDtypeStruct((B, LANES, seq_len), BF16)],
        compiler_params=_cp(("parallel",)),
        name="window_attn_prep",
    )(z, z, cos, sin)


def _ga_kernel(q_ref, k_ref, vt_ref, o_ref, acc_ref, m_ref, *, nchunks, tk, tq):
    acc_ref[...] = jnp.zeros_like(acc_ref)
    m_ref[...] = jnp.full_like(m_ref, NEG)
    qt = jnp.concatenate([q_ref[0, 0:GA_HD, :], q_ref[0, GA_HD:, :]], axis=1)

    W = GA_CB
    nblk = 2 * tq // W

    def scores(k):
        return tuple(jnp.dot(k, qt[:, i * W:(i + 1) * W], preferred_element_type=F32) for i in range(nblk))

    def softmax_pv(s_blocks, vt):
        cols = [slice(i * W, (i + 1) * W) for i in range(nblk)]
        m_old = [m_ref[:, cs] for cs in cols]
        m_new = [jnp.maximum(mo, jnp.max(s, axis=0, keepdims=True)) for mo, s in zip(m_old, s_blocks)]
        p = [jnp.exp2((s - mn).astype(BF16)) for s, mn in zip(s_blocks, m_new)]
        alpha = [jnp.exp2(mo - mn) for mo, mn in zip(m_old, m_new)]
        vaug = jnp.concatenate([vt, jnp.ones((16, vt.shape[1]), BF16)], axis=0)
        pv = [jnp.dot(vaug, pi, preferred_element_type=F32) for pi in p]
        for i, cs in enumerate(cols):
            acc_ref[:, cs] = alpha[i] * acc_ref[:, cs] + pv[i]
            m_ref[:, cs] = m_new[i]

    def body(j, s):
        s_next = scores(k_ref[0, pl.ds(pl.multiple_of((j + 1) * tk, tk), tk), :])
        softmax_pv(s, vt_ref[0, :, pl.ds(pl.multiple_of(j * tk, tk), tk)])
        return s_next
    s = scores(k_ref[0, 0:tk, :])
    if nchunks > 1:
        s = lax.fori_loop(0, nchunks - 1, body, s, unroll=GA_UNROLL if (nchunks - 1) % GA_UNROLL == 0 else 1)
    softmax_pv(s, vt_ref[0, :, (nchunks - 1) * tk:nchunks * tk])
    o = acc_ref[0:GA_HD, :] / acc_ref[GA_HD:GA_HD + 1, :]
    for h in range(2):
        o_ref[0, :, h * GA_HD:(h + 1) * GA_HD] = o[:, h * tq:(h + 1) * tq].T.astype(o_ref.dtype)


def _global_attention(qt, k, vt):
    B, _, Lq = qt.shape
    Lk = k.shape[1]
    tq = min(512, Lq)
    tk = 256
    W = 2 * GA_HD
    return pl.pallas_call(
        functools.partial(_ga_kernel, nchunks=Lk // tk, tk=tk, tq=tq),
        grid=(B, GA_KV, Lq // tq),
        in_specs=[pl.BlockSpec((1, W, tq), lambda b, g, i: (b, g, i)),
                  pl.BlockSpec((1, Lk, GA_HD), lambda b, g, i: (b, 0, g)),
                  pl.BlockSpec((1, GA_HD, Lk), lambda b, g, i: (b, g, 0))],
        out_specs=pl.BlockSpec((1, tq, W), lambda b, g, i: (b, i, g)),
        out_shape=jax.ShapeDtypeStruct((B, Lq, GA_HEADS * GA_HD), BF16),
        scratch_shapes=[pltpu.VMEM((GA_HD + 16, 2 * tq), F32), pltpu.VMEM((1, 2 * tq), F32)],
        compiler_params=_cp(("parallel", "parallel", "parallel")),
        name="global_attention",
    )(qt, k, vt)


def _wa_kernel(*refs, band, nq):
    if band:
        q_ref, kp_ref, kx_ref, kn_ref, vp_ref, vx_ref, vn_ref, kc_ref, vc_ref, sink_ref, o_ref = refs
    else:
        q_ref, kc_ref, vc_ref, sink_ref, o_ref = refs
    tq = q_ref.shape[2]
    Lc = kc_ref.shape[1]
    qi = pl.program_id(1)
    G = WA_HEADS // WA_KV
    cols = WA_PAIR * tq
    if band:
        nb = 3 * tq
        k = jnp.concatenate([kp_ref[0], kx_ref[0], kn_ref[0], kc_ref[0]], axis=0)
        vt = jnp.concatenate([vp_ref[0], vx_ref[0], vn_ref[0], vc_ref[0]], axis=1)
        c = lax.broadcasted_iota(jnp.int32, (nb + Lc, cols), 0)
        r = lax.broadcasted_iota(jnp.int32, (nb + Lc, cols), 1) % tq
        lo = jnp.where(qi == 0, tq, 0)
        hi = jnp.where(qi == nq - 1, 2 * tq, nb)
        valid = ((c >= r) & (c <= r + 2 * WINDOW) & (c >= lo) & (c < hi)) | (c >= nb)
    else:
        k, vt = kc_ref[0], vc_ref[0]
    nkeys = k.shape[0]
    ones_rows = jnp.ones((16, nkeys), BF16)
    zeros = jnp.zeros((WA_HD, cols), BF16)
    npr = WA_HEADS // WA_PAIR
    grp = [(pr * WA_PAIR) // G for pr in range(npr)]
    s = []
    for pr in range(npr):
        qg = jnp.concatenate([q_ref[0, (WA_PAIR * pr + h) * WA_HD:(WA_PAIR * pr + h + 1) * WA_HD, :]
                              for h in range(WA_PAIR)], axis=1)
        qpad = jnp.concatenate([qg, zeros] if grp[pr] == 0 else [zeros, qg], axis=0)
        sp = jnp.dot(k, qpad, preferred_element_type=F32)
        s.append(jnp.where(valid, sp, NEG) if band else sp)
    sink = [sink_ref[pr] for pr in range(npr)]
    m = [jnp.maximum(jnp.max(s[pr], axis=0, keepdims=True), sink[pr]) for pr in range(npr)]
    p = [jnp.exp2(s[pr] - m[pr]).astype(BF16) for pr in range(npr)]
    R = []
    for pr in range(npr):
        vaug = jnp.concatenate([vt[grp[pr] * WA_HD:(grp[pr] + 1) * WA_HD, :], ones_rows], axis=0)
        R.append(jnp.dot(vaug, p[pr], preferred_element_type=F32))
    for pr in range(npr):
        o = R[pr][:WA_HD] / (R[pr][WA_HD:WA_HD + 1] + jnp.exp2(sink[pr] - m[pr]))
        ot = jnp.concatenate([o[:, h * tq:(h + 1) * tq] for h in range(WA_PAIR)], axis=0)
        wo = WA_PAIR * WA_HD
        o_ref[0, :, pr * wo:(pr + 1) * wo] = ot.T.astype(o_ref.dtype)


def _window_attention(qt, kc, vtc, sink, kl=None, vtl=None):
    B, WQ, Lq = qt.shape
    Lc = kc.shape[1]
    tq = WINDOW
    nq = Lq // tq
    band = kl is not None
    npair = WA_HEADS // WA_PAIR
    sink_row = jnp.repeat(sink.astype(F32).reshape(npair, WA_PAIR) * LOG2E, tq, axis=1).reshape(npair, 1, WA_PAIR * tq)
    qspec = pl.BlockSpec((1, WQ, tq), lambda b, i: (b, 0, i))
    kcspec = pl.BlockSpec((1, Lc, LANES), lambda b, i: (b, 0, 0))
    vcspec = pl.BlockSpec((1, LANES, Lc), lambda b, i: (b, 0, 0))
    sspec = pl.BlockSpec((npair, 1, WA_PAIR * tq), lambda b, i: (0, 0, 0))
    if band:
        prev = lambda i: jnp.maximum(i - 1, 0)
        nxt = lambda i: jnp.minimum(i + 1, nq - 1)
        same = lambda i: i
        kspec = lambda f: pl.BlockSpec((1, tq, LANES), lambda b, i: (b, f(i), 0))
        vspec = lambda f: pl.BlockSpec((1, LANES, tq), lambda b, i: (b, 0, f(i)))
        in_specs = [qspec, kspec(prev), kspec(same), kspec(nxt), vspec(prev), vspec(same), vspec(nxt),
                    kcspec, vcspec, sspec]
        args = (qt, kl, kl, kl, vtl, vtl, vtl, kc, vtc, sink_row)
    else:
        in_specs, args = [qspec, kcspec, vcspec, sspec], (qt, kc, vtc, sink_row)
    return pl.pallas_call(
        functools.partial(_wa_kernel, band=band, nq=nq),
        grid=(B, nq),
        in_specs=in_specs,
        out_specs=pl.BlockSpec((1, tq, WQ), lambda b, i: (b, i, 0)),
        out_shape=jax.ShapeDtypeStruct((B, Lq, WQ), BF16),
        compiler_params=_cp(("parallel", "parallel")),
        name="window_attention",
    )(*args)


def _mlstm_kernel(*refs, reverse, nc, bb):
    if reverse:
        (qt_ref, k_ref, vt_ref, g_ref, gb_ref, c0_ref, m0_ref, hft_ref, o_ref, gn_ref,
         y_ref, cf_ref, mf_ref, c_scr, m_scr) = refs
    else:
        (qt_ref, k_ref, vt_ref, g_ref, gb_ref, c0_ref, m0_ref,
         y_ref, cf_ref, mf_ref, c_scr, m_scr) = refs
    T = ML_CHUNK
    d = 1 if reverse else 0
    step = pl.program_id(1)

    @pl.when(step == 0)
    def _():
        c_scr[...] = c0_ref[...]
        m_scr[...] = m0_ref[...]

    si = lax.broadcasted_iota(jnp.int32, (T, T), 0)
    ti = lax.broadcasted_iota(jnp.int32, (T, T), 1)
    tri = ((ti >= si) if reverse else (ti <= si)).astype(F32)
    mask_t = (si >= ti) if reverse else (si <= ti)
    ones_rows = jnp.ones((ML_HD, T), BF16)
    e_last = 0 if reverse else T - 1
    chains = [(bi, h) for bi in range(bb) for h in range(ML_HEADS)]
    gates = []
    for bi in range(bb):
        G = g_ref[bi] + gb_ref[...]
        LF = jax.nn.log_sigmoid(G)
        Bc = jnp.dot(tri, LF, preferred_element_type=F32, precision=HIGHEST)
        gates.append((G.T, Bc.T, Bc - pltpu.roll(G, 4, 1)))
    st, ph = {}, {}
    for bi, h in chains:
        hs = slice(h * ML_HD, (h + 1) * ML_HD)
        st[bi, h] = jnp.dot(k_ref[bi, :, hs], qt_ref[bi, hs, :], preferred_element_type=F32)
    for bi, h in chains:
        GT, BT, Dc = gates[bi]
        fl, il = d * 8 + 4 + h, d * 8 + h
        b_row, i_row = BT[fl:fl + 1, :], GT[il:il + 1, :]
        log_d = jnp.where(mask_t, b_row - Dc[:, fl:fl + 1], NEG)
        m_prev = m_scr[bi, h, 0:1, 0:1]
        m_inter = b_row + m_prev
        m_t = jnp.maximum(m_inter, jnp.max(log_d, axis=0, keepdims=True))
        wqk = (st[bi, h] * jnp.exp(log_d - m_t)).astype(BF16)
        b_end = BT[fl:fl + 1, e_last:e_last + 1]
        log_w = b_end - b_row + i_row
        m_next = jnp.maximum(b_end + m_prev, jnp.max(log_w, axis=1, keepdims=True))
        ph[bi, h] = (wqk, jnp.exp(m_inter - m_t), jnp.exp(-m_t), jnp.exp(log_w - m_next),
                     jnp.exp(b_end + m_prev - m_next), m_next)
    for bi, h in chains:
        hs = slice(h * ML_HD, (h + 1) * ML_HD)
        wqk, cs, em, w_row, decay, m_next = ph[bi, h]
        kh, qt = k_ref[bi, :, hs], qt_ref[bi, hs, :]
        vaug = jnp.concatenate([vt_ref[bi, hs, :], ones_rows], axis=0)
        R = (jnp.dot(vaug, wqk, preferred_element_type=F32)
             + cs * jnp.dot(c_scr[bi, h].astype(BF16), qt, preferred_element_type=F32))
        hh = R[:ML_HD] / jnp.maximum(jnp.abs(R[ML_HD:]), em)
        if reverse:
            hsum = hft_ref[bi, hs, :] + hh
            hn = hsum * lax.rsqrt(jnp.mean(hsum * hsum, axis=0, keepdims=True) + EPS) * gn_ref[hs, :]
            y_ref[bi, :, hs] = (jax.nn.sigmoid(o_ref[bi, :, hs].astype(F32)) * hn.T).astype(y_ref.dtype)
        else:
            y_ref[bi, hs, :] = hh
        wv = (vaug.astype(F32) * w_row).astype(BF16)
        c_scr[bi, h] = decay * c_scr[bi, h] + jnp.dot(wv, kh, preferred_element_type=F32)
        m_scr[bi, h] = jnp.broadcast_to(m_next, (8, LANES))

    @pl.when(step == nc - 1)
    def _():
        cf_ref[...] = c_scr[...]
        mf_ref[...] = m_scr[...]


def _mlstm_scan(qt, k, vt, z, gates, gate_b, c0, m0, reverse, hft=None, norm_g=None):
    B, L, W = k.shape
    T = ML_CHUNK
    nc = L // T
    bb = 4 if B % 4 == 0 else 2
    cj =(lambda j: nc - 1 - j) if reverse else (lambda j: j)
    tok = pl.BlockSpec((bb, T, W), lambda b, j: (b, cj(j), 0))
    ttok = pl.BlockSpec((bb, W, T), lambda b, j: (b, 0, cj(j)))
    cspec = pl.BlockSpec((bb, ML_HEADS, 2 * ML_HD, ML_HD), lambda b, j: (b, 0, 0, 0))
    mspec = pl.BlockSpec((bb, ML_HEADS, 8, LANES), lambda b, j: (b, 0, 0, 0))
    in_specs = [ttok, tok, ttok, pl.BlockSpec((bb, T, LANES), lambda b, j: (b, cj(j), 0)),
                pl.BlockSpec((1, LANES), lambda b, j: (0, 0)), cspec, mspec]
    args = [qt, k, vt, gates, gate_b, c0, m0]
    if reverse:
        in_specs += [ttok, pl.BlockSpec((bb, T, W), lambda b, j: (b, cj(j), Z_MLO // W)),
                     pl.BlockSpec((W, T), lambda b, j: (0, 0))]
        args += [hft, z, jnp.broadcast_to(norm_g.astype(F32).reshape(W, 1), (W, T))]
    return pl.pallas_call(
        functools.partial(_mlstm_kernel, reverse=reverse, nc=nc, bb=bb),
        grid=(B // bb, nc),
        in_specs=in_specs,
        out_specs=[tok if reverse else ttok, cspec, mspec],
        out_shape=[jax.ShapeDtypeStruct((B, L, W), BF16) if reverse else jax.ShapeDtypeStruct((B, W, L), F32),
                   jax.ShapeDtypeStruct(c0.shape, F32), jax.ShapeDtypeStruct(m0.shape, F32)],
        scratch_shapes=[pltpu.VMEM((bb, ML_HEADS, 2 * ML_HD, ML_HD), F32), pltpu.VMEM((bb, ML_HEADS, 8, LANES), F32)],
        compiler_params=_cp(("parallel", "arbitrary")),
        name="mlstm_reverse" if reverse else "mlstm_forward",
    )(*args)


def _fft_dims(Lp):
    n1 = 2 * Lp // FFT_N2
    nt1 = Lp // FFT_N2
    nk1 = -(-(n1 // 2 + 1) // 8) * 8
    return n1, nt1, nk1


def _fft_tables(Lp):
    n1, nt1, nk1 = _fft_dims(Lp)
    N = 2 * Lp
    k1 = jnp.arange(nk1)
    t1 = jnp.arange(nt1)
    ang_a = (2.0 * math.pi / n1) * ((k1[:, None] * t1[None, :]) % n1).astype(F32)
    fa = jnp.stack([jnp.cos(ang_a), -jnp.sin(ang_a)], axis=1).reshape(2 * nk1, nt1)
    wgt = jnp.where((k1 == 0) | (k1 == n1 // 2), 1.0, 2.0) * (k1 <= n1 // 2) / N
    fai = jnp.stack([jnp.cos(ang_a) * wgt[:, None], -jnp.sin(ang_a) * wgt[:, None]], axis=1)
    fai = fai.reshape(2 * nk1, nt1).T
    k2 = jnp.arange(FFT_N2)
    t2 = jnp.arange(FFT_N2)
    idx = (t2[None, None, :] * k1[:, None, None] + n1 * t2[None, None, :] * k2[None, :, None]) % N
    phi = (2.0 * math.pi / N) * idx.astype(F32)
    gr, gi = jnp.cos(phi), -jnp.sin(phi)
    gfwd = jnp.concatenate([jnp.concatenate([gr, -gi], axis=2), jnp.concatenate([gi, gr], axis=2)], axis=1)
    grt, git = jnp.swapaxes(gr, 1, 2), jnp.swapaxes(gi, 1, 2)
    ginv = jnp.concatenate([jnp.concatenate([grt, git], axis=2), jnp.concatenate([-git, grt], axis=2)], axis=1)
    eye = jnp.eye(FFT_TK, dtype=F32)
    return (jnp.kron(fa, eye).astype(BF16), jnp.kron(fai, eye).astype(BF16), gfwd.astype(BF16), ginv.astype(BF16))


FFT_TB = 16
FFT_TK = 8
FFT_CB = 512


def _fa_kernel(fa_ref, x_ref, o_ref):
    nt1, tb, cb = x_ref.shape[1:]
    rows = o_ref.shape[1]
    x = x_ref[0].astype(F32).reshape(nt1, tb // FFT_TK, FFT_TK, cb)
    parts = []
    for j in range(tb // FFT_TK):
        xj = x[:, j].reshape(nt1 * FFT_TK, cb).astype(BF16)
        r = jnp.dot(fa_ref[...], xj, preferred_element_type=F32)
        parts.append(r.reshape(rows, 1, FFT_TK, cb))
    o_ref[0] = jnp.concatenate(parts, axis=1).reshape(rows, tb, cb).astype(o_ref.dtype)


def _fft_stage_a(fa, y, out_dtype):
    B, Lp, C = y.shape
    nt1 = Lp // FFT_N2
    rows = fa.shape[0] // FFT_TK
    return pl.pallas_call(
        _fa_kernel,
        grid=(B, FFT_N2 // FFT_TB, C // FFT_CB),
        in_specs=[pl.BlockSpec(fa.shape, lambda b, j, c: (0, 0)),
                  pl.BlockSpec((1, nt1, FFT_TB, FFT_CB), lambda b, j, c: (b, 0, j, c))],
        out_specs=pl.BlockSpec((1, rows, FFT_TB, FFT_CB), lambda b, j, c: (b, 0, j, c)),
        out_shape=jax.ShapeDtypeStruct((B, rows, FFT_N2, C), out_dtype),
        compiler_params=_cp(("parallel", "parallel", "parallel")),
        name="fft_stage_a",
    )(fa, y.reshape(B, nt1, FFT_N2, C))


def _fc_filter_kernel(g_ref, s_ref, ss_ref, h_ref, *, kb):
    C = HY_CH
    for i in range(kb):
        for n in range(HY_ORDER):
            cf, cb = (2 * n) * C, (2 * n + 1) * C
            scale = lax.rsqrt(ss_ref[0:1, cf:cf + C] + ss_ref[0:1, cb:cb + C] + EPS)
            sf = s_ref[0, 2 * i:2 * i + 2, :, cf:cf + C].reshape(2 * FFT_N2, C).astype(BF16)
            sb = s_ref[0, 2 * i:2 * i + 2, :, cb:cb + C].reshape(2 * FFT_N2, C).astype(BF16)
            xf = jnp.dot(g_ref[i], sf, preferred_element_type=F32)
            xb = jnp.dot(g_ref[i], sb, preferred_element_type=F32)
            h_ref[n, 2 * i] = (xf[:FFT_N2] + xb[:FFT_N2]) * scale
            h_ref[n, 2 * i + 1] = (xf[FFT_N2:] - xb[FFT_N2:]) * scale


def _fft_filter_spectrum(gfwd, s_filt, sumsq):
    rows = s_filt.shape[1]
    C4 = s_filt.shape[-1]
    kb = 2
    return pl.pallas_call(
        functools.partial(_fc_filter_kernel, kb=kb),
        grid=(rows // (2 * kb),),
        in_specs=[pl.BlockSpec((kb, 2 * FFT_N2, 2 * FFT_N2), lambda i: (i, 0, 0)),
                  pl.BlockSpec((1, 2 * kb, FFT_N2, C4), lambda i: (0, i, 0, 0)),
                  pl.BlockSpec((1, C4), lambda i: (0, 0))],
        out_specs=pl.BlockSpec((HY_ORDER, 2 * kb, FFT_N2, HY_CH), lambda i: (0, i, 0, 0)),
        out_shape=jax.ShapeDtypeStruct((HY_ORDER, rows, FFT_N2, HY_CH), F32),
        compiler_params=_cp(("parallel",)),
        name="fft_filter_spectrum",
    )(gfwd, s_filt, sumsq)


def _fc_kernel(g_ref, gi_ref, h_ref, s_ref, o_ref, *, kb):
    C = s_ref.shape[-1]
    x = [jnp.dot(g_ref[i], s_ref[0, 2 * i:2 * i + 2].reshape(2 * FFT_N2, C), preferred_element_type=F32)
         for i in range(kb)]
    z = []
    for i in range(kb):
        xr, xi = x[i][:FFT_N2], x[i][FFT_N2:]
        hr, hi = h_ref[0, 2 * i], h_ref[0, 2 * i + 1]
        z.append(jnp.concatenate([xr * hr - xi * hi, xr * hi + xi * hr], axis=0).astype(BF16))
    bm = [jnp.dot(gi_ref[i], z[i], preferred_element_type=F32) for i in range(kb)]
    for i in range(kb):
        o_ref[0, 2 * i:2 * i + 2] = bm[i].reshape(2, FFT_N2, C).astype(o_ref.dtype)


def _fft_stage_c(gfwd, ginv, hspec, order, s):
    B, rows, _, C = s.shape
    kb = 4
    sspec = pl.BlockSpec((1, 2 * kb, FFT_N2, C), lambda i, b: (b, i, 0, 0))
    gspec = pl.BlockSpec((kb, 2 * FFT_N2, 2 * FFT_N2), lambda i, b: (i, 0, 0))
    return pl.pallas_call(
        functools.partial(_fc_kernel, kb=kb),
        grid=(rows // (2 * kb), B),
        in_specs=[gspec, gspec,
                  pl.BlockSpec((1, 2 * kb, FFT_N2, C), lambda i, b: (order, i, 0, 0)), sspec],
        out_specs=sspec,
        out_shape=jax.ShapeDtypeStruct(s.shape, BF16),
        compiler_params=_cp(("parallel", "arbitrary")),
        name="fft_stage_c",
    )(gfwd, ginv, hspec, s)


def _fai_kernel(fai_ref, b_ref, y_ref, gate_ref, skip_ref, o_ref):
    nt1, tb, cb = y_ref.shape[1:]
    rows = b_ref.shape[1]
    bm = b_ref[0].astype(F32).reshape(rows, tb // FFT_TK, FFT_TK, cb)
    parts = []
    for j in range(tb // FFT_TK):
        bj = bm[:, j].reshape(rows * FFT_TK, cb).astype(BF16)
        yf = jnp.dot(fai_ref[...], bj, preferred_element_type=F32)
        parts.append(yf.reshape(nt1, 1, FFT_TK, cb))
    yf = jnp.concatenate(parts, axis=1).reshape(nt1, tb, cb)
    out = gate_ref[0].astype(F32) * (yf + skip_ref[...].reshape(1, 1, cb) * y_ref[0].astype(F32))
    o_ref[0] = out.astype(o_ref.dtype)


def _fft_stage_a_inv(fai, bm, y, gate, skip):
    B, Lp, C = y.shape
    nt1 = Lp // FFT_N2
    rows = fai.shape[1] // FFT_TK
    tok = pl.BlockSpec((1, nt1, FFT_TB, FFT_CB), lambda b, j, c: (b, 0, j, c))
    out = pl.pallas_call(
        _fai_kernel,
        grid=(B, FFT_N2 // FFT_TB, C // FFT_CB),
        in_specs=[pl.BlockSpec(fai.shape, lambda b, j, c: (0, 0)),
                  pl.BlockSpec((1, rows, FFT_TB, FFT_CB), lambda b, j, c: (b, 0, j, c)),
                  tok, tok, pl.BlockSpec((1, FFT_CB), lambda b, j, c: (0, c))],
        out_specs=tok,
        out_shape=jax.ShapeDtypeStruct((B, nt1, FFT_N2, C), BF16),
        compiler_params=_cp(("parallel", "parallel", "parallel")),
        name="fft_stage_a_inv",
    )(fai, bm, y.reshape(B, nt1, FFT_N2, C), gate.reshape(B, nt1, FFT_N2, C), skip.astype(F32).reshape(1, C))
    return out.reshape(B, Lp, C)


def _hgen_kernel(z_ref, w1_ref, b1_ref, fr_ref, w2_ref, b2_ref, w3_ref, dec_ref, h_ref, ss_ref):
    i = pl.program_id(0)
    z = z_ref[...]
    tm = z.shape[0]
    h = jnp.sin(fr_ref[0:1, :] * (jnp.dot(z, w1_ref[...], preferred_element_type=F32, precision=HIGHEST)
                                 + b1_ref[...]))
    h = jnp.sin(fr_ref[1:2, :] * (jnp.dot(h, w2_ref[...], preferred_element_type=F32, precision=HIGHEST)
                                 + b2_ref[...]))
    h = jnp.dot(h, w3_ref[...], preferred_element_type=F32, precision=HIGHEST)
    h = h * jnp.exp(-z[:, 0:1] * jnp.abs(dec_ref[...]))
    row = lax.broadcasted_iota(jnp.int32, h.shape, 0) + i * tm
    col = lax.broadcasted_iota(jnp.int32, h.shape, 1)
    h = jnp.where((row == 0) & ((col // HY_CH) % 2 == 1), 0.0, h)
    h_ref[...] = h

    @pl.when(i == 0)
    def _():
        ss_ref[...] = jnp.zeros_like(ss_ref)

    ss_ref[...] += jnp.sum(h * h, axis=0, keepdims=True)


def _hyena_filters(L, p):
    t = jnp.arange(L, dtype=F32)
    tn = t / (L - 1)
    w = 2.0 * math.pi * t / L
    bands = jnp.linspace(1e-4, HY_BANDS - 1, HY_BANDS, dtype=F32)
    ang = w[:, None] * bands[None, :]
    z = jnp.concatenate([tn[:, None], jnp.cos(ang), -jnp.sin(ang)], axis=-1)
    z = jnp.pad(z, ((0, 0), (0, LANES - HY_EMB)))
    w1 = jnp.pad(p['hy_pe_w1'].astype(F32), ((0, LANES - HY_EMB), (0, 0)))
    nf = HY_ORDER * 2 * HY_CH
    tm = min(512, L)
    const = lambda shape: pl.BlockSpec(shape, lambda i: (0,) * len(shape))
    return pl.pallas_call(
        _hgen_kernel,
        grid=(L // tm,),
        in_specs=[pl.BlockSpec((tm, LANES), lambda i: (i, 0)), const((LANES, HY_FFN)), const((1, HY_FFN)),
                  const((2, HY_FFN)), const((HY_FFN, HY_FFN)), const((1, HY_FFN)), const((HY_FFN, nf)),
                  const((1, nf))],
        out_specs=[pl.BlockSpec((tm, nf), lambda i: (i, 0)), const((1, nf))],
        out_shape=[jax.ShapeDtypeStruct((L, nf), F32), jax.ShapeDtypeStruct((1, nf), F32)],
        compiler_params=_cp(("arbitrary",)),
        name="hyena_filter_gen",
    )(z, w1, p['hy_pe_b1'].reshape(1, HY_FFN), p['hy_freq'], p['hy_pe_w2'], p['hy_pe_b2'].reshape(1, HY_FFN),
      p['hy_pe_w3'], p['hy_decay'].reshape(1, nf))


def _hyena_branch(z, p, B, L):
    v, x1, x2 = _hy_prep(z, p['hy_conv_w'], p['hy_conv_b'], L)
    Lp = max(L, 2048)
    fa, fai, gfwd, ginv = _fft_tables(Lp)
    hfilt, sumsq = _hyena_filters(L, p)
    pad3 = lambda a: a.reshape(B, L, HY_CH) if Lp == L else jnp.pad(a.reshape(B, L, HY_CH), ((0, 0), (0, Lp - L), (0, 0)))
    hf = hfilt if Lp == L else jnp.pad(hfilt, ((0, Lp - L), (0, 0)))
    s_filt = _fft_stage_a(fa, hf[None], F32)
    hspec = _fft_filter_spectrum(gfwd, s_filt, sumsq)
    y = pad3(v)
    for n, gate in enumerate((pad3(x1), pad3(x2))):
        s = _fft_stage_a(fa, y, BF16)
        bm = _fft_stage_c(gfwd, ginv, hspec, n, s)
        y = _fft_stage_a_inv(fai, bm, y, gate, p['hy_skip'][n])
    return y[:, :L].reshape(B * L, HY_CH)


def _merge_kernel(ya_ref, yb_ref, yc_ref, yd_ref, g0_ref, g1_ref, g2_ref, g3_ref, wup_ref, wout_ref,
                  x_ref, gate_ref, o_ref):
    acc = None
    for n, (y_ref, g_ref) in enumerate(((ya_ref, g0_ref), (yb_ref, g1_ref), (yc_ref, g2_ref), (yd_ref, g3_ref))):
        t = jax.nn.sigmoid(g_ref[...].astype(F32)) * jnp.dot(y_ref[...], wup_ref[n], preferred_element_type=F32)
        acc = t if acc is None else acc + t
    yl = jnp.dot(acc.astype(BF16), wout_ref[...], preferred_element_type=F32)
    o_ref[...] = x_ref[...] + gate_ref[0] * yl


def _merge(ys, z, w_up, w_out, x, gate, seq_len):
    T, D = x.shape
    tm = min(512, seq_len)
    per = seq_len // tm
    Wy = ys[0].shape[1]
    yspec = pl.BlockSpec((tm, Wy), lambda i: (i, 0))
    gspecs = [pl.BlockSpec((tm, D), functools.partial(lambda i, n: (i, Z_GATE // D + n), n=n)) for n in range(4)]
    return pl.pallas_call(
        _merge_kernel,
        grid=(T // tm,),
        in_specs=[yspec] * 4 + gspecs + [
            pl.BlockSpec((4, Wy, D), lambda i: (0, 0, 0)), pl.BlockSpec((D, D), lambda i: (0, 0)),
            pl.BlockSpec((tm, D), lambda i: (i, 0)), pl.BlockSpec((1, 1, D), lambda i: (i // per, 0, 0))],
        out_specs=pl.BlockSpec((tm, D), lambda i: (i, 0)),
        out_shape=jax.ShapeDtypeStruct((T, D), F32),
        compiler_params=_cp(("parallel",)),
        name="merge_branches",
    )(*ys, z, z, z, z, w_up, w_out, x, gate)


def _mlp_kernel(x_ref, g_ref, sc_ref, sh_ref, w1_ref, b1_ref, w2_ref, b2_ref, gate_ref, fg_ref, o_ref,
                h_ref, acc_ref, *, nk, final):
    k = pl.program_id(1)

    @pl.when(k == 0)
    def _():
        y = _rms(x_ref[...], g_ref[...])
        h_ref[...] = (y * (1.0 + sc_ref[0]) + sh_ref[0]).astype(BF16)
        acc_ref[...] = jnp.zeros_like(acc_ref)

    a = jnp.maximum(jnp.dot(h_ref[...], w1_ref[...], preferred_element_type=F32) + b1_ref[...], 0.0)
    acc_ref[...] += jnp.dot((a * a).astype(BF16), w2_ref[...], preferred_element_type=F32)

    @pl.when(k == nk - 1)
    def _():
        out = x_ref[...] + gate_ref[0] * (acc_ref[...] + b2_ref[...])
        if final:
            out = _rms(out, fg_ref[...])
        o_ref[...] = out


def _mlp(x, g, sc, sh, w1, b1, w2, b2, gate, final_g, seq_len, final):
    T, D = x.shape
    F = w1.shape[1]
    tm = min(1024, seq_len)
    per = seq_len // tm
    tk = 1024
    nk = F // tk
    row = lambda i, k: (i // per, 0, 0)
    return pl.pallas_call(
        functools.partial(_mlp_kernel, nk=nk, final=final),
        grid=(T // tm, nk),
        in_specs=[pl.BlockSpec((tm, D), lambda i, k: (i, 0)), pl.BlockSpec((1, D), lambda i, k: (0, 0)),
                  pl.BlockSpec((1, 1, D), row), pl.BlockSpec((1, 1, D), row),
                  pl.BlockSpec((D, tk), lambda i, k: (0, k)), pl.BlockSpec((1, tk), lambda i, k: (0, k)),
                  pl.BlockSpec((tk, D), lambda i, k: (k, 0)), pl.BlockSpec((1, D), lambda i, k: (0, 0)),
                  pl.BlockSpec((1, 1, D), row), pl.BlockSpec((1, D), lambda i, k: (0, 0))],
        out_specs=pl.BlockSpec((tm, D), lambda i, k: (i, 0)),
        out_shape=jax.ShapeDtypeStruct((T, D), F32),
        scratch_shapes=[pltpu.VMEM((tm, D), BF16), pltpu.VMEM((tm, D), F32)],
        compiler_params=_cp(("parallel", "arbitrary")),
        name="mlp",
    )(x, g.reshape(1, D), sc, sh, w1, b1.reshape(1, F), w2, b2.reshape(1, D), gate, final_g.reshape(1, D))


def _pack_w_in(w_in):
    hy_e = 3 * HY_CH
    ga_e = hy_e + (GA_HEADS + 2 * GA_KV) * GA_HD
    mw = ML_HEADS * ML_HD
    ml_e = ga_e + 4 * mw + 16
    wa_e = ml_e + (WA_HEADS + 2 * WA_KV) * WA_HD
    hy, ga = w_in[:, :hy_e], w_in[:, hy_e:ga_e]
    ml = w_in[:, ga_e:ml_e]
    wa = w_in[:, ml_e:wa_e]
    gate = w_in[:, wa_e:]
    waq, wakv = wa[:, :WA_HEADS * WA_HD], wa[:, WA_HEADS * WA_HD:]
    pad = jnp.zeros((w_in.shape[0], Z_COLS - Z_WAKV - wakv.shape[1]), w_in.dtype)
    packed = jnp.concatenate([hy, waq, ga, ml[:, :2 * mw], ml[:, 2 * mw:3 * mw], ml[:, 3 * mw:4 * mw], gate, wakv, pad],
                             axis=1)
    wg = jnp.pad(ml[:, 4 * mw:], ((0, 0), (0, LANES - 16)))
    return packed.astype(BF16), wg


def _token_mixers(zl, zc, gl, gc, p, B, L, Lc, with_ctx_out):
    ya_l = _hyena_branch(zl, p, B, L)
    ya_c = _hyena_branch(zc, p, B, Lc) if with_ctx_out else None
    cos, sin = _rope_tables(L, GA_HD)
    ql, kl, vl = _ga_prep(zl, cos, sin, p['ga_q_g'], p['ga_k_g'], B, L, True)
    qc, kc, vc = _ga_prep(zc, cos[:Lc], sin[:Lc], p['ga_q_g'], p['ga_k_g'], B, Lc, False)
    r3 = lambda a, n: a.reshape(B, n, a.shape[-1])
    k_all = jnp.concatenate([r3(kl, L), r3(kc, Lc)], axis=1)
    vt_all = jnp.concatenate([vl, vc], axis=2)
    yb_l = _global_attention(ql, k_all, vt_all).reshape(B * L, -1)
    yb_c = _global_attention(qc, r3(kc, Lc), vc).reshape(B * Lc, -1) if with_ctx_out else None
    mq_l, mk_l, mv_l = _ml_prep(zl, p['ml_conv_w'], p['ml_conv_b'], B, L)
    mq_c, mk_c, mv_c = _ml_prep(zc, p['ml_conv_w'], p['ml_conv_b'], B, Lc)
    gb = jnp.pad(p['ml_gate_b'].astype(F32), (0, LANES - 16)).reshape(1, LANES)
    c0 = jnp.zeros((B, ML_HEADS, 2 * ML_HD, ML_HD), F32)
    m0 = jnp.zeros((B, ML_HEADS, 8, LANES), F32)
    zl3, zc3, gl3, gc3 = r3(zl, L), r3(zc, Lc), r3(gl, L), r3(gc, Lc)
    h_cf, cf, mf = _mlstm_scan(mq_c, r3(mk_c, Lc), mv_c, zc3, gc3, gb, c0, m0, False)
    yc_c, cb, mb = _mlstm_scan(mq_c, r3(mk_c, Lc), mv_c, zc3, gc3, gb, c0, m0, True, h_cf, p['ml_norm_g'])
    h_lf, _, _ = _mlstm_scan(mq_l, r3(mk_l, L), mv_l, zl3, gl3, gb, cf, mf, False)
    yc_l, _, _ = _mlstm_scan(mq_l, r3(mk_l, L), mv_l, zl3, gl3, gb, cb, mb, True, h_lf, p['ml_norm_g'])
    yc_l = yc_l.reshape(B * L, -1)
    yc_c = yc_c.reshape(B * Lc, -1)
    cosw, sinw = _rope_tables(L, WA_HD)
    wq_l, wk_l, wv_l = _wa_prep(zl, cosw, sinw, B, L, True)
    wq_c, wk_c, wv_c = _wa_prep(zc, cosw[:Lc], sinw[:Lc], B, Lc, False)
    yd_l = _window_attention(wq_l, r3(wk_c, Lc), wv_c, p['wa_sink'], r3(wk_l, L), wv_l).reshape(B * L, -1)
    yd_c = (_window_attention(wq_c, r3(wk_c, Lc), wv_c, p['wa_sink']).reshape(B * Lc, -1)
            if with_ctx_out else None)
    return (ya_l, yb_l, yc_l, yd_l), (ya_c, yb_c, yc_c, yd_c)


def kernel(x, c, ctx, c_ctx, w_mod, b_mod, ln1_g, ln2_g, w_in, hy_conv_w, hy_conv_b,
           hy_pe_w1, hy_pe_b1, hy_freq, hy_pe_w2, hy_pe_b2, hy_pe_w3, hy_decay, hy_skip,
           ga_q_g, ga_k_g, ml_conv_w, ml_conv_b, ml_gate_b, ml_norm_g, wa_sink, w_up, w_out,
           mlp_w1, mlp_b1, mlp_w2, mlp_b2, final_g):
    B, L, D = x.shape
    Lc = ctx.shape[1]
    R = -(-(B + 1) // 8) * 8
    cvec = jnp.zeros((R, D), F32).at[:B].set(c).at[B].set(c_ctx)
    mod = _modulation(cvec, w_mod, b_mod)
    xl = x.reshape(B * L, D)
    xc = ctx.reshape(B * Lc, D)
    for l in range(DEPTH):
        with_ctx_out = l < DEPTH - 1
        p = dict(hy_conv_w=hy_conv_w[l], hy_conv_b=hy_conv_b[l], hy_pe_w1=hy_pe_w1[l],
                 hy_pe_b1=hy_pe_b1[l], hy_freq=hy_freq[l], hy_pe_w2=hy_pe_w2[l], hy_pe_b2=hy_pe_b2[l],
                 hy_pe_w3=hy_pe_w3[l], hy_decay=hy_decay[l], hy_skip=hy_skip[l],
                 ga_q_g=ga_q_g[l], ga_k_g=ga_k_g[l], ml_conv_w=ml_conv_w[l], ml_conv_b=ml_conv_b[l],
                 ml_gate_b=ml_gate_b[l], ml_norm_g=ml_norm_g[l], wa_sink=wa_sink[l])
        ml_rows = mod[l, :B].reshape(B, 1, 6 * D)
        mc_rows = jnp.broadcast_to(mod[l, B].reshape(1, 1, 6 * D), (B, 1, 6 * D))
        part = lambda m, n: m[:, :, n * D:(n + 1) * D]
        w_pack, w_gate = _pack_w_in(w_in[l])
        zl = _normmod_matmul(xl, ln1_g[l], part(ml_rows, 1), part(ml_rows, 0), w_pack, L, Z_TN, BF16)
        zc = _normmod_matmul(xc, ln1_g[l], part(mc_rows, 1), part(mc_rows, 0), w_pack, Lc, Z_TN, BF16)
        gl = _normmod_matmul(xl, ln1_g[l], part(ml_rows, 1), part(ml_rows, 0), w_gate.astype(BF16), L, LANES, F32)
        gc = _normmod_matmul(xc, ln1_g[l], part(mc_rows, 1), part(mc_rows, 0), w_gate.astype(BF16), Lc, LANES, F32)
        ys_l, ys_c = _token_mixers(zl, zc, gl, gc, p, B, L, Lc, with_ctx_out)
        wup = w_up[l].astype(BF16)
        wout = w_out[l].astype(BF16)
        w1, w2 = mlp_w1[l].astype(BF16), mlp_w2[l].astype(BF16)
        xl = _merge(ys_l, zl, wup, wout, xl, part(ml_rows, 2), L)
        xl = _mlp(xl, ln2_g[l], part(ml_rows, 4), part(ml_rows, 3), w1, mlp_b1[l], w2, mlp_b2[l],
                  part(ml_rows, 5), final_g, L, final=(l == DEPTH - 1))
        if with_ctx_out:
            xc = _merge(ys_c, zc, wup, wout, xc, part(mc_rows, 2), Lc)
            xc = _mlp(xc, ln2_g[l], part(mc_rows, 4), part(mc_rows, 3), w1, mlp_b1[l], w2, mlp_b2[l],
                      part(mc_rows, 5), final_g, Lc, final=False)
    return xl.reshape(B, L, D)
```

```python
import functools
import math

import jax
import jax.numpy as jnp
import numpy as np
from jax import lax
from jax.experimental import pallas as pl
from jax.experimental.pallas import tpu as pltpu

F32 = jnp.float32
BF16 = jnp.bfloat16
HIGHEST = lax.Precision.HIGHEST

D_MODEL = 1024
DEPTH = 2
GRID_W = 64
HY_CH = 512
HY_ORDER = 2
HY_BANDS = 16
HY_EMB = 1 + 2 * HY_BANDS
HY_FFN = 64
GA_HEADS, GA_KV, GA_HD = 4, 2, 128
ML_HEADS, ML_HD = 4, 128
WA_HEADS, WA_KV, WA_HD = 8, 2, 64
WINDOW = 128
ROPE_BASE = 10000.0
D_FF = 4 * D_MODEL
EPS = 1e-6
NEG = -1e30
LOG2E = 1.4426950408889634

LANES = 128
V7X_VMEM_LIMIT = 48 * 1024 * 1024

Z_HY = 0
Z_WAQ = 1536
Z_GA = 2048
Z_MLQK = 3072
Z_MLV = 4096
Z_MLO = 4608
Z_GATE = 5120
Z_WAKV = 9216
Z_COLS = 9728
Z_TN = 2432

ML_CHUNK = 256
FFT_N2 = 128
GA_CB = 256
WA_PAIR = 2
GA_TK = 256
GA_UNROLL = 16


def _cp(sem, vmem=V7X_VMEM_LIMIT):
    return pltpu.CompilerParams(dimension_semantics=sem, vmem_limit_bytes=vmem)


def _rms(x, g):
    return x * lax.rsqrt(jnp.mean(x * x, axis=-1, keepdims=True) + EPS) * g


def _mod_kernel(c_ref, w_ref, b_ref, o_ref):
    c = c_ref[...]
    s = c * jax.nn.sigmoid(c)
    o_ref[0] = jnp.dot(s, w_ref[0], preferred_element_type=F32, precision=HIGHEST) + b_ref[0]


def _modulation(cvec, w_mod, b_mod):
    R = cvec.shape[0]
    tn = 1536
    return pl.pallas_call(
        _mod_kernel,
        grid=(DEPTH, 6 * D_MODEL // tn),
        in_specs=[pl.BlockSpec((R, D_MODEL), lambda l, j: (0, 0)),
                  pl.BlockSpec((1, D_MODEL, tn), lambda l, j: (l, 0, j)),
                  pl.BlockSpec((1, 1, tn), lambda l, j: (l, 0, j))],
        out_specs=pl.BlockSpec((1, R, tn), lambda l, j: (l, 0, j)),
        out_shape=jax.ShapeDtypeStruct((DEPTH, R, 6 * D_MODEL), F32),
        compiler_params=_cp(("parallel", "parallel")),
        name="modulation",
    )(cvec, w_mod, b_mod.reshape(DEPTH, 1, 6 * D_MODEL))


def _nmm_kernel(x_ref, g_ref, sc_ref, sh_ref, w_ref, o_ref, h_ref):
    @pl.when(pl.program_id(1) == 0)
    def _():
        y = _rms(x_ref[...], g_ref[...])
        h_ref[...] = (y * (1.0 + sc_ref[0]) + sh_ref[0]).astype(BF16)

    o_ref[...] = jnp.dot(h_ref[...], w_ref[...], preferred_element_type=F32).astype(o_ref.dtype)


def _normmod_matmul(x, g, sc, sh, w, seq_len, tn, out_dtype):
    T, D = x.shape
    N = w.shape[1]
    tm = min(1024, seq_len)
    per = seq_len // tm
    return pl.pallas_call(
        _nmm_kernel,
        grid=(T // tm, N // tn),
        in_specs=[pl.BlockSpec((tm, D), lambda i, j: (i, 0)),
                  pl.BlockSpec((1, D), lambda i, j: (0, 0)),
                  pl.BlockSpec((1, 1, D), lambda i, j: (i // per, 0, 0)),
                  pl.BlockSpec((1, 1, D), lambda i, j: (i // per, 0, 0)),
                  pl.BlockSpec((D, tn), lambda i, j: (0, j))],
        out_specs=pl.BlockSpec((tm, tn), lambda i, j: (i, j)),
        out_shape=jax.ShapeDtypeStruct((T, N), out_dtype),
        scratch_shapes=[pltpu.VMEM((tm, D), BF16)],
        compiler_params=_cp(("parallel", "arbitrary")),
        name="normmod_matmul",
    )(x, g.reshape(1, D), sc, sh, w)


def _conv3(u, prev_row, next_row, w_ref, b_ref, c0, c1):
    tm = u.shape[0]
    row = lax.broadcasted_iota(jnp.int32, u.shape, 0)
    up = jnp.where(row == 0, prev_row, pltpu.roll(u, 1, 0))
    dn = jnp.where(row == tm - 1, next_row, pltpu.roll(u, tm - 1, 0))
    return (w_ref[0:1, c0:c1] * up + w_ref[1:2, c0:c1] * u + w_ref[2:3, c0:c1] * dn + b_ref[0:1, c0:c1])


def _halo_rows(zp_ref, zn_ref, per, c0, c1):
    i = pl.program_id(0)
    first = (i % per) == 0
    last = (i % per) == per - 1
    hp = zp_ref.shape[0]
    prev_row = jnp.where(first, 0.0, zp_ref[hp - 1:hp, c0:c1].astype(F32))
    next_row = jnp.where(last, 0.0, zn_ref[0:1, c0:c1].astype(F32))
    return prev_row, next_row


HALO = 16


def _halo_specs(tm, width, col_block, n_rows):
    nb = n_rows // HALO
    r = tm // HALO
    return [pl.BlockSpec((tm, width), lambda i: (i, col_block)),
            pl.BlockSpec((HALO, width), lambda i: (jnp.maximum(i * r - 1, 0), col_block)),
            pl.BlockSpec((HALO, width), lambda i: (jnp.minimum((i + 1) * r, nb - 1), col_block))]


def _hy_prep_kernel(z_ref, zp_ref, zn_ref, w_ref, b_ref, v_ref, x1_ref, x2_ref, *, per):
    outs = (v_ref, x1_ref, x2_ref)
    for c in range(3):
        c0, c1 = c * HY_CH, (c + 1) * HY_CH
        prev_row, next_row = _halo_rows(zp_ref, zn_ref, per, c0, c1)
        u = z_ref[:, c0:c1].astype(F32)
        outs[c][...] = _conv3(u, prev_row, next_row, w_ref, b_ref, c0, c1).astype(outs[c].dtype)


def _hy_prep(z, conv_w, conv_b, seq_len):
    T = z.shape[0]
    tm = min(512, seq_len)
    W = 3 * HY_CH
    out = jax.ShapeDtypeStruct((T, HY_CH), BF16)
    return pl.pallas_call(
        functools.partial(_hy_prep_kernel, per=seq_len // tm),
        grid=(T // tm,),
        in_specs=_halo_specs(tm, W, Z_HY // W, T) + [
            pl.BlockSpec((3, W), lambda i: (0, 0)), pl.BlockSpec((1, W), lambda i: (0, 0))],
        out_specs=[pl.BlockSpec((tm, HY_CH), lambda i: (i, 0))] * 3,
        out_shape=[out, out, out],
        compiler_params=_cp(("parallel",)),
        name="hyena_prep",
    )(z, z, z, conv_w, conv_b.reshape(1, W))


def _ml_prep_kernel(z_ref, zp_ref, zn_ref, zv_ref, w_ref, b_ref, q_ref, k_ref, v_ref, *, per):
    W = ML_HEADS * ML_HD
    for c in range(2):
        c0, c1 = c * W, (c + 1) * W
        prev_row, next_row = _halo_rows(zp_ref, zn_ref, per, c0, c1)
        u = z_ref[:, c0:c1].astype(F32)
        y = _conv3(u, prev_row, next_row, w_ref, b_ref, c0, c1)
        y = y * jax.nn.sigmoid(y)
        if c == 0:
            for h in range(ML_HEADS):
                q_ref[0, h * ML_HD:(h + 1) * ML_HD, :] = y[:, h * ML_HD:(h + 1) * ML_HD].T.astype(q_ref.dtype)
        else:
            k_ref[...] = (y * (ML_HD ** -0.5)).astype(k_ref.dtype)
    for h in range(ML_HEADS):
        v_ref[0, h * ML_HD:(h + 1) * ML_HD, :] = zv_ref[:, h * ML_HD:(h + 1) * ML_HD].astype(F32).T.astype(v_ref.dtype)


def _ml_prep(z, conv_w, conv_b, B, seq_len):
    T = z.shape[0]
    tm = min(512, seq_len)
    per = seq_len // tm
    W = 2 * ML_HEADS * ML_HD
    Wh = W // 2
    tspec = pl.BlockSpec((1, Wh, tm), lambda i: (i // per, 0, i % per))
    tshape = jax.ShapeDtypeStruct((B, Wh, seq_len), BF16)
    return pl.pallas_call(
        functools.partial(_ml_prep_kernel, per=per),
        grid=(T // tm,),
        in_specs=_halo_specs(tm, W, Z_MLQK // W, T) + [
            pl.BlockSpec((tm, Wh), lambda i: (i, Z_MLV // Wh)),
            pl.BlockSpec((3, W), lambda i: (0, 0)), pl.BlockSpec((1, W), lambda i: (0, 0))],
        out_specs=[tspec, pl.BlockSpec((tm, Wh), lambda i: (i, 0)), tspec],
        out_shape=[tshape, jax.ShapeDtypeStruct((T, Wh), BF16), tshape],
        compiler_params=_cp(("parallel",)),
        name="mlstm_prep",
    )(z, z, z, z, conv_w, conv_b.reshape(1, W))


def _rope_tables(L, hd):
    quarter = hd // 4
    inv = ROPE_BASE ** (-jnp.arange(quarter, dtype=F32) / quarter)
    t = jnp.arange(L)
    row = (t // GRID_W).astype(F32)
    col = (t % GRID_W).astype(F32)
    lane = jnp.arange(LANES)
    within = lane % hd
    is_col = (within // (hd // 2)) == 1
    second = ((within % (hd // 2)) // quarter) == 1
    j = within % quarter
    pos = jnp.where(is_col[None, :], col[:, None], row[:, None])
    ang = pos * inv[j][None, :]
    return jnp.cos(ang), jnp.where(second[None, :], jnp.sin(ang), -jnp.sin(ang))


def _rope(x, cos, sin, quarter):
    lane = lax.broadcasted_iota(jnp.int32, x.shape, 1)
    first = ((lane % (2 * quarter)) // quarter) == 0
    partner = jnp.where(first, pltpu.roll(x, LANES - quarter, 1), pltpu.roll(x, quarter, 1))
    return x * cos + partner * sin


def _ga_prep_kernel(z_ref, cos_ref, sin_ref, qg_ref, kg_ref, q_ref, k_ref, v_ref, *, rope):
    nq, nk = GA_HEADS, GA_KV
    for h in range(nq + nk):
        x = z_ref[:, h * GA_HD:(h + 1) * GA_HD].astype(F32)
        g = qg_ref[...] if h < nq else kg_ref[...]
        y = _rms(x, g)
        if rope:
            y = _rope(y, cos_ref[...], sin_ref[...], GA_HD // 4)
        if h < nq:
            q_ref[0, h * GA_HD:(h + 1) * GA_HD, :] = (y * (GA_HD ** -0.5 * LOG2E)).T.astype(q_ref.dtype)
        else:
            k_ref[:, (h - nq) * GA_HD:(h - nq + 1) * GA_HD] = y.astype(k_ref.dtype)
    for h in range(nk):
        v = z_ref[:, (nq + nk + h) * GA_HD:(nq + nk + h + 1) * GA_HD].astype(F32)
        v_ref[0, h * GA_HD:(h + 1) * GA_HD, :] = v.T.astype(v_ref.dtype)


def _ga_prep(z, cos, sin, qg, kg, B, seq_len, rope):
    T = z.shape[0]
    tm = min(512, seq_len)
    per = seq_len // tm
    W = (GA_HEADS + 2 * GA_KV) * GA_HD
    return pl.pallas_call(
        functools.partial(_ga_prep_kernel, rope=rope),
        grid=(T // tm,),
        in_specs=[pl.BlockSpec((tm, W), lambda i: (i, Z_GA // W)),
                  pl.BlockSpec((tm, LANES), lambda i: (i % per, 0)),
                  pl.BlockSpec((tm, LANES), lambda i: (i % per, 0)),
                  pl.BlockSpec((1, GA_HD), lambda i: (0, 0)),
                  pl.BlockSpec((1, GA_HD), lambda i: (0, 0))],
        out_specs=[pl.BlockSpec((1, GA_HEADS * GA_HD, tm), lambda i: (i // per, 0, i % per)),
                   pl.BlockSpec((tm, GA_KV * GA_HD), lambda i: (i, 0)),
                   pl.BlockSpec((1, GA_KV * GA_HD, tm), lambda i: (i // per, 0, i % per))],
        out_shape=[jax.ShapeDtypeStruct((B, GA_HEADS * GA_HD, seq_len), BF16),
                   jax.ShapeDtypeStruct((T, GA_KV * GA_HD), BF16),
                   jax.ShapeDtypeStruct((B, GA_KV * GA_HD, seq_len), BF16)],
        compiler_params=_cp(("parallel",)),
        name="global_attn_prep",
    )(z, cos, sin, qg.reshape(1, GA_HD), kg.reshape(1, GA_HD))


def _wa_prep_kernel(zq_ref, zkv_ref, cos_ref, sin_ref, q_ref, k_ref, v_ref, *, rope):
    quarter = WA_HD // 4
    for j in range(WA_HEADS * WA_HD // LANES):
        x = zq_ref[:, j * LANES:(j + 1) * LANES].astype(F32)
        if rope:
            x = _rope(x, cos_ref[...], sin_ref[...], quarter)
        q_ref[0, j * LANES:(j + 1) * LANES, :] = (x * (WA_HD ** -0.5 * LOG2E)).T.astype(q_ref.dtype)
    k = zkv_ref[:, 0:LANES].astype(F32)
    if rope:
        k = _rope(k, cos_ref[...], sin_ref[...], quarter)
    k_ref[...] = k.astype(k_ref.dtype)
    v_ref[0] = zkv_ref[:, LANES:].astype(F32).T.astype(v_ref.dtype)


def _wa_prep(z, cos, sin, B, seq_len, rope):
    T = z.shape[0]
    tm = min(512, seq_len)
    per = seq_len // tm
    WQ = WA_HEADS * WA_HD
    return pl.pallas_call(
        functools.partial(_wa_prep_kernel, rope=rope),
        grid=(T // tm,),
        in_specs=[pl.BlockSpec((tm, WQ), lambda i: (i, Z_WAQ // WQ)),
                  pl.BlockSpec((tm, 2 * LANES), lambda i: (i, Z_WAKV // (2 * LANES))),
                  pl.BlockSpec((tm, LANES), lambda i: (i % per, 0)),
                  pl.BlockSpec((tm, LANES), lambda i: (i % per, 0))],
        out_specs=[pl.BlockSpec((1, WQ, tm), lambda i: (i // per, 0, i % per)),
                   pl.BlockSpec((tm, LANES), lambda i: (i, 0)),
                   pl.BlockSpec((1, LANES, tm), lambda i: (i // per, 0, i % per))],
        out_shape=[jax.ShapeDtypeStruct((B, WQ, seq_len), BF16),
                   jax.ShapeDtypeStruct((T, LANES), BF16),
                   jax.ShapeDtypeStruct((B, LANES, seq_len), BF16)],
        compiler_params=_cp(("parallel",)),
        name="window_attn_prep",
    )(z, z, cos, sin)


def _ga_kernel(q_ref, k_ref, vt_ref, o_ref, acc_ref, m_ref, *, nchunks, tk, tq):
    acc_ref[...] = jnp.zeros_like(acc_ref)
    m_ref[...] = jnp.full_like(m_ref, NEG)
    qt = jnp.concatenate([q_ref[0, 0:GA_HD, :], q_ref[0, GA_HD:, :]], axis=1)

    W = GA_CB
    nblk = 2 * tq // W

    def scores(k):
        return tuple(jnp.dot(k, qt[:, i * W:(i + 1) * W], preferred_element_type=F32) for i in range(nblk))

    def softmax_pv(s_blocks, vt):
        cols = [slice(i * W, (i + 1) * W) for i in range(nblk)]
        m_old = [m_ref[:, cs] for cs in cols]
        m_new = [jnp.maximum(mo, jnp.max(s, axis=0, keepdims=True)) for mo, s in zip(m_old, s_blocks)]
        alpha = [jnp.exp2(mo - mn) for mo, mn in zip(m_old, m_new)]
        vaug = jnp.concatenate([vt, jnp.ones((16, vt.shape[1]), BF16)], axis=0)
        pv = [jnp.dot(vaug, jnp.exp2((s - mn).astype(BF16)), preferred_element_type=F32)
              for s, mn in zip(s_blocks, m_new)]
        for i, cs in enumerate(cols):
            acc_ref[:, cs] = alpha[i] * acc_ref[:, cs] + pv[i]
            m_ref[:, cs] = m_new[i]

    def body(j, s):
        s_next = scores(k_ref[0, pl.ds(pl.multiple_of((j + 1) * tk, tk), tk), :])
        softmax_pv(s, vt_ref[0, :, pl.ds(pl.multiple_of(j * tk, tk), tk)])
        return s_next
    s = scores(k_ref[0, 0:tk, :])
    if nchunks > 1:
        s = lax.fori_loop(0, nchunks - 1, body, s, unroll=GA_UNROLL if (nchunks - 1) % GA_UNROLL == 0 else 1)
    Lk = k_ref.shape[1]
    if Lk > nchunks * tk:
        s_tail = scores(k_ref[0, nchunks * tk:, :])
    softmax_pv(s, vt_ref[0, :, (nchunks - 1) * tk:nchunks * tk])
    if Lk > nchunks * tk:
        softmax_pv(s_tail, vt_ref[0, :, nchunks * tk:])
    o = acc_ref[0:GA_HD, :] / acc_ref[GA_HD:GA_HD + 1, :]
    for h in range(2):
        o_ref[0, :, h * GA_HD:(h + 1) * GA_HD] = o[:, h * tq:(h + 1) * tq].T.astype(o_ref.dtype)


def _global_attention(qt, k, vt):
    B, _, Lq = qt.shape
    Lk = k.shape[1]
    tq = min(512, Lq)
    tk = min(GA_TK, Lk)
    W = 2 * GA_HD
    return pl.pallas_call(
        functools.partial(_ga_kernel, nchunks=Lk // tk, tk=tk, tq=tq),
        grid=(B, GA_KV, Lq // tq),
        in_specs=[pl.BlockSpec((1, W, tq), lambda b, g, i: (b, g, i)),
                  pl.BlockSpec((1, Lk, GA_HD), lambda b, g, i: (b, 0, g)),
                  pl.BlockSpec((1, GA_HD, Lk), lambda b, g, i: (b, g, 0))],
        out_specs=pl.BlockSpec((1, tq, W), lambda b, g, i: (b, i, g)),
        out_shape=jax.ShapeDtypeStruct((B, Lq, GA_HEADS * GA_HD), BF16),
        scratch_shapes=[pltpu.VMEM((GA_HD + 16, 2 * tq), F32), pltpu.VMEM((1, 2 * tq), F32)],
        compiler_params=_cp(("parallel", "parallel", "parallel")),
        name="global_attention",
    )(qt, k, vt)


def _wa_kernel(*refs, band, nq):
    if band:
        q_ref, kp_ref, kx_ref, kn_ref, vp_ref, vx_ref, vn_ref, kc_ref, vc_ref, sink_ref, o_ref = refs
    else:
        q_ref, kc_ref, vc_ref, sink_ref, o_ref = refs
    tq = q_ref.shape[2]
    Lc = kc_ref.shape[1]
    qi = pl.program_id(1)
    G = WA_HEADS // WA_KV
    cols = WA_PAIR * tq
    if band:
        nb = 3 * tq
        k = jnp.concatenate([kp_ref[0], kx_ref[0], kn_ref[0], kc_ref[0]], axis=0)
        vt = jnp.concatenate([vp_ref[0], vx_ref[0], vn_ref[0], vc_ref[0]], axis=1)
        c = lax.broadcasted_iota(jnp.int32, (nb + Lc, cols), 0)
        r = lax.broadcasted_iota(jnp.int32, (nb + Lc, cols), 1) % tq
        lo = jnp.where(qi == 0, tq, 0)
        hi = jnp.where(qi == nq - 1, 2 * tq, nb)
        valid = ((c >= r) & (c <= r + 2 * WINDOW) & (c >= lo) & (c < hi)) | (c >= nb)
    else:
        k, vt = kc_ref[0], vc_ref[0]
    nkeys = k.shape[0]
    ones_rows = jnp.ones((16, nkeys), BF16)
    zeros = jnp.zeros((WA_HD, cols), BF16)
    npr = WA_HEADS // WA_PAIR
    grp = [(pr * WA_PAIR) // G for pr in range(npr)]
    s = []
    for pr in range(npr):
        qg = jnp.concatenate([q_ref[0, (WA_PAIR * pr + h) * WA_HD:(WA_PAIR * pr + h + 1) * WA_HD, :]
                              for h in range(WA_PAIR)], axis=1)
        qpad = jnp.concatenate([qg, zeros] if grp[pr] == 0 else [zeros, qg], axis=0)
        sp = jnp.dot(k, qpad, preferred_element_type=F32)
        s.append(jnp.where(valid, sp, NEG) if band else sp)
    sink = [sink_ref[pr] for pr in range(npr)]
    m = [jnp.maximum(jnp.max(s[pr], axis=0, keepdims=True), sink[pr]) for pr in range(npr)]
    p = [jnp.exp2(s[pr] - m[pr]).astype(BF16) for pr in range(npr)]
    R = []
    for pr in range(npr):
        vaug = jnp.concatenate([vt[grp[pr] * WA_HD:(grp[pr] + 1) * WA_HD, :], ones_rows], axis=0)
        R.append(jnp.dot(vaug, p[pr], preferred_element_type=F32))
    for pr in range(npr):
        o = R[pr][:WA_HD] / (R[pr][WA_HD:WA_HD + 1] + jnp.exp2(sink[pr] - m[pr]))
        ot = jnp.concatenate([o[:, h * tq:(h + 1) * tq] for h in range(WA_PAIR)], axis=0)
        wo = WA_PAIR * WA_HD
        o_ref[0, :, pr * wo:(pr + 1) * wo] = ot.T.astype(o_ref.dtype)


def _window_attention(qt, kc, vtc, sink, kl=None, vtl=None):
    B, WQ, Lq = qt.shape
    Lc = kc.shape[1]
    tq = WINDOW
    nq = Lq // tq
    band = kl is not None
    npair = WA_HEADS // WA_PAIR
    sink_row = jnp.repeat(sink.astype(F32).reshape(npair, WA_PAIR) * LOG2E, tq, axis=1).reshape(npair, 1, WA_PAIR * tq)
    qspec = pl.BlockSpec((1, WQ, tq), lambda b, i: (b, 0, i))
    kcspec = pl.BlockSpec((1, Lc, LANES), lambda b, i: (b, 0, 0))
    vcspec = pl.BlockSpec((1, LANES, Lc), lambda b, i: (b, 0, 0))
    sspec = pl.BlockSpec((npair, 1, WA_PAIR * tq), lambda b, i: (0, 0, 0))
    if band:
        prev = lambda i: jnp.maximum(i - 1, 0)
        nxt = lambda i: jnp.minimum(i + 1, nq - 1)
        same = lambda i: i
        kspec = lambda f: pl.BlockSpec((1, tq, LANES), lambda b, i: (b, f(i), 0))
        vspec = lambda f: pl.BlockSpec((1, LANES, tq), lambda b, i: (b, 0, f(i)))
        in_specs = [qspec, kspec(prev), kspec(same), kspec(nxt), vspec(prev), vspec(same), vspec(nxt),
                    kcspec, vcspec, sspec]
        args = (qt, kl, kl, kl, vtl, vtl, vtl, kc, vtc, sink_row)
    else:
        in_specs, args = [qspec, kcspec, vcspec, sspec], (qt, kc, vtc, sink_row)
    return pl.pallas_call(
        functools.partial(_wa_kernel, band=band, nq=nq),
        grid=(B, nq),
        in_specs=in_specs,
        out_specs=pl.BlockSpec((1, tq, WQ), lambda b, i: (b, i, 0)),
        out_shape=jax.ShapeDtypeStruct((B, Lq, WQ), BF16),
        compiler_params=_cp(("parallel", "parallel")),
        name="window_attention",
    )(*args)


def _mlstm_kernel(*refs, reverse, nc, bb):
    if reverse:
        (qt_ref, k_ref, vt_ref, g_ref, gb_ref, c0_ref, m0_ref, hft_ref, o_ref, gn_ref,
         y_ref, cf_ref, mf_ref, c_scr, m_scr) = refs
    else:
        (qt_ref, k_ref, vt_ref, g_ref, gb_ref, c0_ref, m0_ref,
         y_ref, cf_ref, mf_ref, c_scr, m_scr) = refs
    T = ML_CHUNK
    d = 1 if reverse else 0
    step = pl.program_id(1)

    @pl.when(step == 0)
    def _():
        c_scr[...] = c0_ref[...]
        m_scr[...] = m0_ref[...]

    si = lax.broadcasted_iota(jnp.int32, (T, T), 0)
    ti = lax.broadcasted_iota(jnp.int32, (T, T), 1)
    tri = ((ti >= si) if reverse else (ti <= si)).astype(F32)
    mask_t = (si >= ti) if reverse else (si <= ti)
    ones_rows = jnp.ones((ML_HD, T), BF16)
    e_last = 0 if reverse else T - 1
    chains = [(bi, h) for bi in range(bb) for h in range(ML_HEADS)]
    gates = []
    for bi in range(bb):
        G = g_ref[bi] + gb_ref[...]
        LF = jax.nn.log_sigmoid(G)
        Bc = jnp.dot(tri, LF, preferred_element_type=F32, precision=HIGHEST)
        gates.append((G.T, Bc.T, Bc - pltpu.roll(G, 4, 1)))
    st, ph = {}, {}
    for bi, h in chains:
        hs = slice(h * ML_HD, (h + 1) * ML_HD)
        st[bi, h] = jnp.dot(k_ref[bi, :, hs], qt_ref[bi, hs, :], preferred_element_type=F32)
    for bi, h in chains:
        GT, BT, Dc = gates[bi]
        fl, il = d * 8 + 4 + h, d * 8 + h
        b_row, i_row = BT[fl:fl + 1, :], GT[il:il + 1, :]
        log_d = jnp.where(mask_t, b_row - Dc[:, fl:fl + 1], NEG)
        m_prev = m_scr[bi, h, 0:1, 0:1]
        m_inter = b_row + m_prev
        m_t = jnp.maximum(m_inter, jnp.max(log_d, axis=0, keepdims=True))
        wqk = (st[bi, h] * jnp.exp(log_d - m_t)).astype(BF16)
        b_end = BT[fl:fl + 1, e_last:e_last + 1]
        log_w = b_end - b_row + i_row
        m_next = jnp.maximum(b_end + m_prev, jnp.max(log_w, axis=1, keepdims=True))
        ph[bi, h] = (wqk, jnp.exp(m_inter - m_t), jnp.exp(-m_t), jnp.exp(log_w - m_next),
                     jnp.exp(b_end + m_prev - m_next), m_next)
    for bi, h in chains:
        hs = slice(h * ML_HD, (h + 1) * ML_HD)
        wqk, cs, em, w_row, decay, m_next = ph[bi, h]
        kh, qt = k_ref[bi, :, hs], qt_ref[bi, hs, :]
        vaug = jnp.concatenate([vt_ref[bi, hs, :], ones_rows], axis=0)
        R = (jnp.dot(vaug, wqk, preferred_element_type=F32)
             + cs * jnp.dot(c_scr[bi, h].astype(BF16), qt, preferred_element_type=F32))
        hh = R[:ML_HD] / jnp.maximum(jnp.abs(R[ML_HD:]), em)
        if reverse:
            hsum = hft_ref[bi, hs, :] + hh
            hn = hsum * lax.rsqrt(jnp.mean(hsum * hsum, axis=0, keepdims=True) + EPS) * gn_ref[hs, :]
            y_ref[bi, :, hs] = (jax.nn.sigmoid(o_ref[bi, :, hs].astype(F32)) * hn.T).astype(y_ref.dtype)
        else:
            y_ref[bi, hs, :] = hh
        wv = (vaug.astype(F32) * w_row).astype(BF16)
        c_scr[bi, h] = decay * c_scr[bi, h] + jnp.dot(wv, kh, preferred_element_type=F32)
        m_scr[bi, h] = jnp.broadcast_to(m_next, (8, LANES))

    @pl.when(step == nc - 1)
    def _():
        cf_ref[...] = c_scr[...]
        mf_ref[...] = m_scr[...]


def _mlstm_scan(qt, k, vt, z, gates, gate_b, c0, m0, reverse, hft=None, norm_g=None):
    B, L, W = k.shape
    T = ML_CHUNK
    nc = L // T
    bb = 4 if B % 4 == 0 else 2
    cj =(lambda j: nc - 1 - j) if reverse else (lambda j: j)
    tok = pl.BlockSpec((bb, T, W), lambda b, j: (b, cj(j), 0))
    ttok = pl.BlockSpec((bb, W, T), lambda b, j: (b, 0, cj(j)))
    cspec = pl.BlockSpec((bb, ML_HEADS, 2 * ML_HD, ML_HD), lambda b, j: (b, 0, 0, 0))
    mspec = pl.BlockSpec((bb, ML_HEADS, 8, LANES), lambda b, j: (b, 0, 0, 0))
    in_specs = [ttok, tok, ttok, pl.BlockSpec((bb, T, LANES), lambda b, j: (b, cj(j), 0)),
                pl.BlockSpec((1, LANES), lambda b, j: (0, 0)), cspec, mspec]
    args = [qt, k, vt, gates, gate_b, c0, m0]
    if reverse:
        in_specs += [ttok, pl.BlockSpec((bb, T, W), lambda b, j: (b, cj(j), Z_MLO // W)),
                     pl.BlockSpec((W, T), lambda b, j: (0, 0))]
        args += [hft, z, jnp.broadcast_to(norm_g.astype(F32).reshape(W, 1), (W, T))]
    return pl.pallas_call(
        functools.partial(_mlstm_kernel, reverse=reverse, nc=nc, bb=bb),
        grid=(B // bb, nc),
        in_specs=in_specs,
        out_specs=[tok if reverse else ttok, cspec, mspec],
        out_shape=[jax.ShapeDtypeStruct((B, L, W), BF16) if reverse else jax.ShapeDtypeStruct((B, W, L), F32),
                   jax.ShapeDtypeStruct(c0.shape, F32), jax.ShapeDtypeStruct(m0.shape, F32)],
        scratch_shapes=[pltpu.VMEM((bb, ML_HEADS, 2 * ML_HD, ML_HD), F32), pltpu.VMEM((bb, ML_HEADS, 8, LANES), F32)],
        compiler_params=_cp(("parallel", "arbitrary")),
        name="mlstm_reverse" if reverse else "mlstm_forward",
    )(*args)


def _fft_dims(Lp):
    n1 = 2 * Lp // FFT_N2
    nt1 = Lp // FFT_N2
    nk1 = -(-(n1 // 2 + 1) // 8) * 8
    return n1, nt1, nk1


def _fft_tables(Lp):
    n1, nt1, nk1 = _fft_dims(Lp)
    N = 2 * Lp
    k1 = jnp.arange(nk1)
    t1 = jnp.arange(nt1)
    ang_a = (2.0 * math.pi / n1) * ((k1[:, None] * t1[None, :]) % n1).astype(F32)
    fa = jnp.stack([jnp.cos(ang_a), -jnp.sin(ang_a)], axis=1).reshape(2 * nk1, nt1)
    wgt = jnp.where((k1 == 0) | (k1 == n1 // 2), 1.0, 2.0) * (k1 <= n1 // 2) / N
    fai = jnp.stack([jnp.cos(ang_a) * wgt[:, None], -jnp.sin(ang_a) * wgt[:, None]], axis=1)
    fai = fai.reshape(2 * nk1, nt1).T
    k2 = jnp.arange(FFT_N2)
    t2 = jnp.arange(FFT_N2)
    idx = (t2[None, None, :] * k1[:, None, None] + n1 * t2[None, None, :] * k2[None, :, None]) % N
    phi = (2.0 * math.pi / N) * idx.astype(F32)
    gr, gi = jnp.cos(phi), -jnp.sin(phi)
    gfwd = jnp.concatenate([jnp.concatenate([gr, -gi], axis=2), jnp.concatenate([gi, gr], axis=2)], axis=1)
    grt, git = jnp.swapaxes(gr, 1, 2), jnp.swapaxes(gi, 1, 2)
    ginv = jnp.concatenate([jnp.concatenate([grt, git], axis=2), jnp.concatenate([-git, grt], axis=2)], axis=1)
    eye = jnp.eye(FFT_TK, dtype=F32)
    return (jnp.kron(fa, eye).astype(BF16), jnp.kron(fai, eye).astype(BF16), gfwd.astype(BF16), ginv.astype(BF16))


FFT_TB = 16
FFT_TK = 8
FFT_CB = 512


def _fa_kernel(fa_ref, x_ref, o_ref):
    nt1, tb, cb = x_ref.shape[1:]
    rows = o_ref.shape[1]
    x = x_ref[0].astype(F32).reshape(nt1, tb // FFT_TK, FFT_TK, cb)
    parts = []
    for j in range(tb // FFT_TK):
        xj = x[:, j].reshape(nt1 * FFT_TK, cb).astype(BF16)
        r = jnp.dot(fa_ref[...], xj, preferred_element_type=F32)
        parts.append(r.reshape(rows, 1, FFT_TK, cb))
    o_ref[0] = jnp.concatenate(parts, axis=1).reshape(rows, tb, cb).astype(o_ref.dtype)


def _fft_stage_a(fa, y, out_dtype):
    B, Lp, C = y.shape
    nt1 = Lp // FFT_N2
    rows = fa.shape[0] // FFT_TK
    return pl.pallas_call(
        _fa_kernel,
        grid=(B, FFT_N2 // FFT_TB, C // FFT_CB),
        in_specs=[pl.BlockSpec(fa.shape, lambda b, j, c: (0, 0)),
                  pl.BlockSpec((1, nt1, FFT_TB, FFT_CB), lambda b, j, c: (b, 0, j, c))],
        out_specs=pl.BlockSpec((1, rows, FFT_TB, FFT_CB), lambda b, j, c: (b, 0, j, c)),
        out_shape=jax.ShapeDtypeStruct((B, rows, FFT_N2, C), out_dtype),
        compiler_params=_cp(("parallel", "parallel", "parallel")),
        name="fft_stage_a",
    )(fa, y.reshape(B, nt1, FFT_N2, C))


def _fc_filter_kernel(g_ref, s_ref, ss_ref, h_ref, *, kb):
    C = HY_CH
    for i in range(kb):
        for n in range(HY_ORDER):
            cf, cb = (2 * n) * C, (2 * n + 1) * C
            scale = lax.rsqrt(ss_ref[0:1, cf:cf + C] + ss_ref[0:1, cb:cb + C] + EPS)
            sf = s_ref[0, 2 * i:2 * i + 2, :, cf:cf + C].reshape(2 * FFT_N2, C).astype(BF16)
            sb = s_ref[0, 2 * i:2 * i + 2, :, cb:cb + C].reshape(2 * FFT_N2, C).astype(BF16)
            xf = jnp.dot(g_ref[i], sf, preferred_element_type=F32)
            xb = jnp.dot(g_ref[i], sb, preferred_element_type=F32)
            h_ref[n, 2 * i] = (xf[:FFT_N2] + xb[:FFT_N2]) * scale
            h_ref[n, 2 * i + 1] = (xf[FFT_N2:] - xb[FFT_N2:]) * scale


def _fft_filter_spectrum(gfwd, s_filt, sumsq):
    rows = s_filt.shape[1]
    C4 = s_filt.shape[-1]
    kb = 2
    return pl.pallas_call(
        functools.partial(_fc_filter_kernel, kb=kb),
        grid=(rows // (2 * kb),),
        in_specs=[pl.BlockSpec((kb, 2 * FFT_N2, 2 * FFT_N2), lambda i: (i, 0, 0)),
                  pl.BlockSpec((1, 2 * kb, FFT_N2, C4), lambda i: (0, i, 0, 0)),
                  pl.BlockSpec((1, C4), lambda i: (0, 0))],
        out_specs=pl.BlockSpec((HY_ORDER, 2 * kb, FFT_N2, HY_CH), lambda i: (0, i, 0, 0)),
        out_shape=jax.ShapeDtypeStruct((HY_ORDER, rows, FFT_N2, HY_CH), F32),
        compiler_params=_cp(("parallel",)),
        name="fft_filter_spectrum",
    )(gfwd, s_filt, sumsq)


def _fc_kernel(g_ref, gi_ref, h_ref, s_ref, o_ref, *, kb):
    C = s_ref.shape[-1]
    x = [jnp.dot(g_ref[i], s_ref[0, 2 * i:2 * i + 2].reshape(2 * FFT_N2, C), preferred_element_type=F32)
         for i in range(kb)]
    z = []
    for i in range(kb):
        xr, xi = x[i][:FFT_N2], x[i][FFT_N2:]
        hr, hi = h_ref[0, 2 * i], h_ref[0, 2 * i + 1]
        z.append(jnp.concatenate([xr * hr - xi * hi, xr * hi + xi * hr], axis=0).astype(BF16))
    bm = [jnp.dot(gi_ref[i], z[i], preferred_element_type=F32) for i in range(kb)]
    for i in range(kb):
        o_ref[0, 2 * i:2 * i + 2] = bm[i].reshape(2, FFT_N2, C).astype(o_ref.dtype)


def _fft_stage_c(gfwd, ginv, hspec, order, s):
    B, rows, _, C = s.shape
    kb = 4
    sspec = pl.BlockSpec((1, 2 * kb, FFT_N2, C), lambda i, b: (b, i, 0, 0))
    gspec = pl.BlockSpec((kb, 2 * FFT_N2, 2 * FFT_N2), lambda i, b: (i, 0, 0))
    return pl.pallas_call(
        functools.partial(_fc_kernel, kb=kb),
        grid=(rows // (2 * kb), B),
        in_specs=[gspec, gspec,
                  pl.BlockSpec((1, 2 * kb, FFT_N2, C), lambda i, b: (order, i, 0, 0)), sspec],
        out_specs=sspec,
        out_shape=jax.ShapeDtypeStruct(s.shape, BF16),
        compiler_params=_cp(("parallel", "arbitrary")),
        name="fft_stage_c",
    )(gfwd, ginv, hspec, s)


def _fai_kernel(fai_ref, b_ref, y_ref, gate_ref, skip_ref, o_ref):
    nt1, tb, cb = y_ref.shape[1:]
    rows = b_ref.shape[1]
    bm = b_ref[0].astype(F32).reshape(rows, tb // FFT_TK, FFT_TK, cb)
    parts = []
    for j in range(tb // FFT_TK):
        bj = bm[:, j].reshape(rows * FFT_TK, cb).astype(BF16)
        yf = jnp.dot(fai_ref[...], bj, preferred_element_type=F32)
        parts.append(yf.reshape(nt1, 1, FFT_TK, cb))
    yf = jnp.concatenate(parts, axis=1).reshape(nt1, tb, cb)
    out = gate_ref[0].astype(F32) * (yf + skip_ref[...].reshape(1, 1, cb) * y_ref[0].astype(F32))
    o_ref[0] = out.astype(o_ref.dtype)


def _fft_stage_a_inv(fai, bm, y, gate, skip):
    B, Lp, C = y.shape
    nt1 = Lp // FFT_N2
    rows = fai.shape[1] // FFT_TK
    tok = pl.BlockSpec((1, nt1, FFT_TB, FFT_CB), lambda b, j, c: (b, 0, j, c))
    out = pl.pallas_call(
        _fai_kernel,
        grid=(B, FFT_N2 // FFT_TB, C // FFT_CB),
        in_specs=[pl.BlockSpec(fai.shape, lambda b, j, c: (0, 0)),
                  pl.BlockSpec((1, rows, FFT_TB, FFT_CB), lambda b, j, c: (b, 0, j, c)),
                  tok, tok, pl.BlockSpec((1, FFT_CB), lambda b, j, c: (0, c))],
        out_specs=tok,
        out_shape=jax.ShapeDtypeStruct((B, nt1, FFT_N2, C), BF16),
        compiler_params=_cp(("parallel", "parallel", "parallel")),
        name="fft_stage_a_inv",
    )(fai, bm, y.reshape(B, nt1, FFT_N2, C), gate.reshape(B, nt1, FFT_N2, C), skip.astype(F32).reshape(1, C))
    return out.reshape(B, Lp, C)


def _hgen_kernel(z_ref, w1_ref, b1_ref, fr_ref, w2_ref, b2_ref, w3_ref, dec_ref, h_ref, ss_ref):
    i = pl.program_id(0)
    z = z_ref[...]
    tm = z.shape[0]
    h = jnp.sin(fr_ref[0:1, :] * (jnp.dot(z, w1_ref[...], preferred_element_type=F32, precision=HIGHEST)
                                 + b1_ref[...]))
    h = jnp.sin(fr_ref[1:2, :] * (jnp.dot(h, w2_ref[...], preferred_element_type=F32, precision=HIGHEST)
                                 + b2_ref[...]))
    h = jnp.dot(h, w3_ref[...], preferred_element_type=F32, precision=HIGHEST)
    h = h * jnp.exp(-z[:, 0:1] * jnp.abs(dec_ref[...]))
    row = lax.broadcasted_iota(jnp.int32, h.shape, 0) + i * tm
    col = lax.broadcasted_iota(jnp.int32, h.shape, 1)
    h = jnp.where((row == 0) & ((col // HY_CH) % 2 == 1), 0.0, h)
    h_ref[...] = h

    @pl.when(i == 0)
    def _():
        ss_ref[...] = jnp.zeros_like(ss_ref)

    ss_ref[...] += jnp.sum(h * h, axis=0, keepdims=True)


def _hyena_filters(L, p):
    t = jnp.arange(L, dtype=F32)
    tn = t / (L - 1)
    w = 2.0 * math.pi * t / L
    bands = jnp.linspace(1e-4, HY_BANDS - 1, HY_BANDS, dtype=F32)
    ang = w[:, None] * bands[None, :]
    z = jnp.concatenate([tn[:, None], jnp.cos(ang), -jnp.sin(ang)], axis=-1)
    z = jnp.pad(z, ((0, 0), (0, LANES - HY_EMB)))
    w1 = jnp.pad(p['hy_pe_w1'].astype(F32), ((0, LANES - HY_EMB), (0, 0)))
    nf = HY_ORDER * 2 * HY_CH
    tm = min(512, L)
    const = lambda shape: pl.BlockSpec(shape, lambda i: (0,) * len(shape))
    return pl.pallas_call(
        _hgen_kernel,
        grid=(L // tm,),
        in_specs=[pl.BlockSpec((tm, LANES), lambda i: (i, 0)), const((LANES, HY_FFN)), const((1, HY_FFN)),
                  const((2, HY_FFN)), const((HY_FFN, HY_FFN)), const((1, HY_FFN)), const((HY_FFN, nf)),
                  const((1, nf))],
        out_specs=[pl.BlockSpec((tm, nf), lambda i: (i, 0)), const((1, nf))],
        out_shape=[jax.ShapeDtypeStruct((L, nf), F32), jax.ShapeDtypeStruct((1, nf), F32)],
        compiler_params=_cp(("arbitrary",)),
        name="hyena_filter_gen",
    )(z, w1, p['hy_pe_b1'].reshape(1, HY_FFN), p['hy_freq'], p['hy_pe_w2'], p['hy_pe_b2'].reshape(1, HY_FFN),
      p['hy_pe_w3'], p['hy_decay'].reshape(1, nf))


def _hyena_branch(z, p, B, L):
    v, x1, x2 = _hy_prep(z, p['hy_conv_w'], p['hy_conv_b'], L)
    Lp = max(L, 2048)
    fa, fai, gfwd, ginv = _fft_tables(Lp)
    hfilt, sumsq = _hyena_filters(L, p)
    pad3 = lambda a: a.reshape(B, L, HY_CH) if Lp == L else jnp.pad(a.reshape(B, L, HY_CH), ((0, 0), (0, Lp - L), (0, 0)))
    hf = hfilt if Lp == L else jnp.pad(hfilt, ((0, Lp - L), (0, 0)))
    s_filt = _fft_stage_a(fa, hf[None], F32)
    hspec = _fft_filter_spectrum(gfwd, s_filt, sumsq)
    y = pad3(v)
    for n, gate in enumerate((pad3(x1), pad3(x2))):
        s = _fft_stage_a(fa, y, BF16)
        bm = _fft_stage_c(gfwd, ginv, hspec, n, s)
        y = _fft_stage_a_inv(fai, bm, y, gate, p['hy_skip'][n])
    return y[:, :L].reshape(B * L, HY_CH)


def _merge_kernel(ya_ref, yb_ref, yc_ref, yd_ref, g0_ref, g1_ref, g2_ref, g3_ref, wup_ref, wout_ref,
                  x_ref, gate_ref, o_ref):
    acc = None
    for n, (y_ref, g_ref) in enumerate(((ya_ref, g0_ref), (yb_ref, g1_ref), (yc_ref, g2_ref), (yd_ref, g3_ref))):
        t = jax.nn.sigmoid(g_ref[...].astype(F32)) * jnp.dot(y_ref[...], wup_ref[n], preferred_element_type=F32)
        acc = t if acc is None else acc + t
    yl = jnp.dot(acc.astype(BF16), wout_ref[...], preferred_element_type=F32)
    o_ref[...] = x_ref[...] + gate_ref[0] * yl


def _merge(ys, z, w_up, w_out, x, gate, seq_len):
    T, D = x.shape
    tm = min(512, seq_len)
    per = seq_len // tm
    Wy = ys[0].shape[1]
    yspec = pl.BlockSpec((tm, Wy), lambda i: (i, 0))
    gspecs = [pl.BlockSpec((tm, D), functools.partial(lambda i, n: (i, Z_GATE // D + n), n=n)) for n in range(4)]
    return pl.pallas_call(
        _merge_kernel,
        grid=(T // tm,),
        in_specs=[yspec] * 4 + gspecs + [
            pl.BlockSpec((4, Wy, D), lambda i: (0, 0, 0)), pl.BlockSpec((D, D), lambda i: (0, 0)),
            pl.BlockSpec((tm, D), lambda i: (i, 0)), pl.BlockSpec((1, 1, D), lambda i: (i // per, 0, 0))],
        out_specs=pl.BlockSpec((tm, D), lambda i: (i, 0)),
        out_shape=jax.ShapeDtypeStruct((T, D), F32),
        compiler_params=_cp(("parallel",)),
        name="merge_branches",
    )(*ys, z, z, z, z, w_up, w_out, x, gate)


def _mlp_kernel(x_ref, g_ref, sc_ref, sh_ref, w1_ref, b1_ref, w2_ref, b2_ref, gate_ref, fg_ref, o_ref,
                h_ref, acc_ref, *, nk, final):
    k = pl.program_id(1)

    @pl.when(k == 0)
    def _():
        y = _rms(x_ref[...], g_ref[...])
        h_ref[...] = (y * (1.0 + sc_ref[0]) + sh_ref[0]).astype(BF16)
        acc_ref[...] = jnp.zeros_like(acc_ref)

    a = jnp.maximum(jnp.dot(h_ref[...], w1_ref[...], preferred_element_type=F32) + b1_ref[...], 0.0)
    acc_ref[...] += jnp.dot((a * a).astype(BF16), w2_ref[...], preferred_element_type=F32)

    @pl.when(k == nk - 1)
    def _():
        out = x_ref[...] + gate_ref[0] * (acc_ref[...] + b2_ref[...])
        if final:
            out = _rms(out, fg_ref[...])
        o_ref[...] = out


def _mlp(x, g, sc, sh, w1, b1, w2, b2, gate, final_g, seq_len, final):
    T, D = x.shape
    F = w1.shape[1]
    tm = min(1024, seq_len)
    per = seq_len // tm
    tk = 1024
    nk = F // tk
    row = lambda i, k: (i // per, 0, 0)
    return pl.pallas_call(
        functools.partial(_mlp_kernel, nk=nk, final=final),
        grid=(T // tm, nk),
        in_specs=[pl.BlockSpec((tm, D), lambda i, k: (i, 0)), pl.BlockSpec((1, D), lambda i, k: (0, 0)),
                  pl.BlockSpec((1, 1, D), row), pl.BlockSpec((1, 1, D), row),
                  pl.BlockSpec((D, tk), lambda i, k: (0, k)), pl.BlockSpec((1, tk), lambda i, k: (0, k)),
                  pl.BlockSpec((tk, D), lambda i, k: (k, 0)), pl.BlockSpec((1, D), lambda i, k: (0, 0)),
                  pl.BlockSpec((1, 1, D), row), pl.BlockSpec((1, D), lambda i, k: (0, 0))],
        out_specs=pl.BlockSpec((tm, D), lambda i, k: (i, 0)),
        out_shape=jax.ShapeDtypeStruct((T, D), F32),
        scratch_shapes=[pltpu.VMEM((tm, D), BF16), pltpu.VMEM((tm, D), F32)],
        compiler_params=_cp(("parallel", "arbitrary")),
        name="mlp",
    )(x, g.reshape(1, D), sc, sh, w1, b1.reshape(1, F), w2, b2.reshape(1, D), gate, final_g.reshape(1, D))


def _pack_w_in(w_in):
    hy_e = 3 * HY_CH
    ga_e = hy_e + (GA_HEADS + 2 * GA_KV) * GA_HD
    mw = ML_HEADS * ML_HD
    ml_e = ga_e + 4 * mw + 16
    wa_e = ml_e + (WA_HEADS + 2 * WA_KV) * WA_HD
    hy, ga = w_in[:, :hy_e], w_in[:, hy_e:ga_e]
    ml = w_in[:, ga_e:ml_e]
    wa = w_in[:, ml_e:wa_e]
    gate = w_in[:, wa_e:]
    waq, wakv = wa[:, :WA_HEADS * WA_HD], wa[:, WA_HEADS * WA_HD:]
    pad = jnp.zeros((w_in.shape[0], Z_COLS - Z_WAKV - wakv.shape[1]), w_in.dtype)
    packed = jnp.concatenate([hy, waq, ga, ml[:, :2 * mw], ml[:, 2 * mw:3 * mw], ml[:, 3 * mw:4 * mw], gate, wakv, pad],
                             axis=1)
    wg = jnp.pad(ml[:, 4 * mw:], ((0, 0), (0, LANES - 16)))
    return packed.astype(BF16), wg


def _token_mixers(zl, zc, gl, gc, p, B, L, Lc, with_ctx_out):
    ya_l = _hyena_branch(zl, p, B, L)
    ya_c = _hyena_branch(zc, p, B, Lc) if with_ctx_out else None
    cos, sin = _rope_tables(L, GA_HD)
    ql, kl, vl = _ga_prep(zl, cos, sin, p['ga_q_g'], p['ga_k_g'], B, L, True)
    qc, kc, vc = _ga_prep(zc, cos[:Lc], sin[:Lc], p['ga_q_g'], p['ga_k_g'], B, Lc, False)
    r3 = lambda a, n: a.reshape(B, n, a.shape[-1])
    k_all = jnp.concatenate([r3(kl, L), r3(kc, Lc)], axis=1)
    vt_all = jnp.concatenate([vl, vc], axis=2)
    yb_l = _global_attention(ql, k_all, vt_all).reshape(B * L, -1)
    yb_c = _global_attention(qc, r3(kc, Lc), vc).reshape(B * Lc, -1) if with_ctx_out else None
    mq_l, mk_l, mv_l = _ml_prep(zl, p['ml_conv_w'], p['ml_conv_b'], B, L)
    mq_c, mk_c, mv_c = _ml_prep(zc, p['ml_conv_w'], p['ml_conv_b'], B, Lc)
    gb = jnp.pad(p['ml_gate_b'].astype(F32), (0, LANES - 16)).reshape(1, LANES)
    c0 = jnp.zeros((B, ML_HEADS, 2 * ML_HD, ML_HD), F32)
    m0 = jnp.zeros((B, ML_HEADS, 8, LANES), F32)
    zl3, zc3, gl3, gc3 = r3(zl, L), r3(zc, Lc), r3(gl, L), r3(gc, Lc)
    h_cf, cf, mf = _mlstm_scan(mq_c, r3(mk_c, Lc), mv_c, zc3, gc3, gb, c0, m0, False)
    yc_c, cb, mb = _mlstm_scan(mq_c, r3(mk_c, Lc), mv_c, zc3, gc3, gb, c0, m0, True, h_cf, p['ml_norm_g'])
    h_lf, _, _ = _mlstm_scan(mq_l, r3(mk_l, L), mv_l, zl3, gl3, gb, cf, mf, False)
    yc_l, _, _ = _mlstm_scan(mq_l, r3(mk_l, L), mv_l, zl3, gl3, gb, cb, mb, True, h_lf, p['ml_norm_g'])
    yc_l = yc_l.reshape(B * L, -1)
    yc_c = yc_c.reshape(B * Lc, -1)
    cosw, sinw = _rope_tables(L, WA_HD)
    wq_l, wk_l, wv_l = _wa_prep(zl, cosw, sinw, B, L, True)
    wq_c, wk_c, wv_c = _wa_prep(zc, cosw[:Lc], sinw[:Lc], B, Lc, False)
    yd_l = _window_attention(wq_l, r3(wk_c, Lc), wv_c, p['wa_sink'], r3(wk_l, L), wv_l).reshape(B * L, -1)
    yd_c = (_window_attention(wq_c, r3(wk_c, Lc), wv_c, p['wa_sink']).reshape(B * Lc, -1)
            if with_ctx_out else None)
    return (ya_l, yb_l, yc_l, yd_l), (ya_c, yb_c, yc_c, yd_c)


def kernel(x, c, ctx, c_ctx, w_mod, b_mod, ln1_g, ln2_g, w_in, hy_conv_w, hy_conv_b,
           hy_pe_w1, hy_pe_b1, hy_freq, hy_pe_w2, hy_pe_b2, hy_pe_w3, hy_decay, hy_skip,
           ga_q_g, ga_k_g, ml_conv_w, ml_conv_b, ml_gate_b, ml_norm_g, wa_sink, w_up, w_out,
           mlp_w1, mlp_b1, mlp_w2, mlp_b2, final_g):
    B, L, D = x.shape
    Lc = ctx.shape[1]
    R = -(-(B + 1) // 8) * 8
    cvec = jnp.zeros((R, D), F32).at[:B].set(c).at[B].set(c_ctx)
    mod = _modulation(cvec, w_mod, b_mod)
    xl = x.reshape(B * L, D)
    xc = ctx.reshape(B * Lc, D)
    for l in range(DEPTH):
        with_ctx_out = l < DEPTH - 1
        p = dict(hy_conv_w=hy_conv_w[l], hy_conv_b=hy_conv_b[l], hy_pe_w1=hy_pe_w1[l],
                 hy_pe_b1=hy_pe_b1[l], hy_freq=hy_freq[l], hy_pe_w2=hy_pe_w2[l], hy_pe_b2=hy_pe_b2[l],
                 hy_pe_w3=hy_pe_w3[l], hy_decay=hy_decay[l], hy_skip=hy_skip[l],
                 ga_q_g=ga_q_g[l], ga_k_g=ga_k_g[l], ml_conv_w=ml_conv_w[l], ml_conv_b=ml_conv_b[l],
                 ml_gate_b=ml_gate_b[l], ml_norm_g=ml_norm_g[l], wa_sink=wa_sink[l])
        ml_rows = mod[l, :B].reshape(B, 1, 6 * D)
        mc_rows = jnp.broadcast_to(mod[l, B].reshape(1, 1, 6 * D), (B, 1, 6 * D))
        part = lambda m, n: m[:, :, n * D:(n + 1) * D]
        w_pack, w_gate = _pack_w_in(w_in[l])
        zl = _normmod_matmul(xl, ln1_g[l], part(ml_rows, 1), part(ml_rows, 0), w_pack, L, Z_TN, BF16)
        zc = _normmod_matmul(xc, ln1_g[l], part(mc_rows, 1), part(mc_rows, 0), w_pack, Lc, Z_TN, BF16)
        gl = _normmod_matmul(xl, ln1_g[l], part(ml_rows, 1), part(ml_rows, 0), w_gate.astype(BF16), L, LANES, F32)
        gc = _normmod_matmul(xc, ln1_g[l], part(mc_rows, 1), part(mc_rows, 0), w_gate.astype(BF16), Lc, LANES, F32)
        ys_l, ys_c = _token_mixers(zl, zc, gl, gc, p, B, L, Lc, with_ctx_out)
        wup = w_up[l].astype(BF16)
        wout = w_out[l].astype(BF16)
        w1, w2 = mlp_w1[l].astype(BF16), mlp_w2[l].astype(BF16)
        xl = _merge(ys_l, zl, wup, wout, xl, part(ml_rows, 2), L)
        xl = _mlp(xl, ln2_g[l], part(ml_rows, 4), part(ml_rows, 3), w1, mlp_b1[l], w2, mlp_b2[l],
                  part(ml_rows, 5), final_g, L, final=(l == DEPTH - 1))
        if with_ctx_out:
            xc = _merge(ys_c, zc, wup, wout, xc, part(mc_rows, 2), Lc)
            xc = _mlp(xc, ln2_g[l], part(mc_rows, 4), part(mc_rows, 3), w1, mlp_b1[l], w2, mlp_b2[l],
                      part(mc_rows, 5), final_g, Lc, final=False)
    return xl.reshape(B, L, D)
```

```python
import functools
import math

import jax
import jax.numpy as jnp
import numpy as np
from jax import lax
from jax.experimental import pallas as pl
from jax.experimental.pallas import tpu as pltpu

F32 = jnp.float32
BF16 = jnp.bfloat16
HIGHEST = lax.Precision.HIGHEST

D_MODEL = 1024
DEPTH = 2
GRID_W = 64
HY_CH = 512
HY_ORDER = 2
HY_BANDS = 16
HY_EMB = 1 + 2 * HY_BANDS
HY_FFN = 64
GA_HEADS, GA_KV, GA_HD = 4, 2, 128
ML_HEADS, ML_HD = 4, 128
WA_HEADS, WA_KV, WA_HD = 8, 2, 64
WINDOW = 128
ROPE_BASE = 10000.0
D_FF = 4 * D_MODEL
EPS = 1e-6
NEG = -1e30
LOG2E = 1.4426950408889634

LANES = 128
V7X_VMEM_LIMIT = 48 * 1024 * 1024

Z_HY = 0
Z_WAQ = 1536
Z_GA = 2048
Z_MLQK = 3072
Z_MLV = 4096
Z_MLO = 4608
Z_GATE = 5120
Z_WAKV = 9216
Z_COLS = 9728
Z_TN = 2432

ML_CHUNK = 256
FFT_N2 = 128
GA_CB = 256
WA_PAIR = 2
WA_SUB = 4
GA_TK = 256
GA_UNROLL = 16


def _cp(sem, vmem=V7X_VMEM_LIMIT):
    return pltpu.CompilerParams(dimension_semantics=sem, vmem_limit_bytes=vmem)


def _rms(x, g):
    return x * lax.rsqrt(jnp.mean(x * x, axis=-1, keepdims=True) + EPS) * g


def _mod_kernel(c_ref, w_ref, b_ref, o_ref):
    c = c_ref[...]
    s = c * jax.nn.sigmoid(c)
    o_ref[0] = jnp.dot(s, w_ref[0], preferred_element_type=F32, precision=HIGHEST) + b_ref[0]


def _modulation(cvec, w_mod, b_mod):
    R = cvec.shape[0]
    tn = 1536
    return pl.pallas_call(
        _mod_kernel,
        grid=(DEPTH, 6 * D_MODEL // tn),
        in_specs=[pl.BlockSpec((R, D_MODEL), lambda l, j: (0, 0)),
                  pl.BlockSpec((1, D_MODEL, tn), lambda l, j: (l, 0, j)),
                  pl.BlockSpec((1, 1, tn), lambda l, j: (l, 0, j))],
        out_specs=pl.BlockSpec((1, R, tn), lambda l, j: (l, 0, j)),
        out_shape=jax.ShapeDtypeStruct((DEPTH, R, 6 * D_MODEL), F32),
        compiler_params=_cp(("parallel", "parallel")),
        name="modulation",
    )(cvec, w_mod, b_mod.reshape(DEPTH, 1, 6 * D_MODEL))


def _nmm_kernel(x_ref, g_ref, sc_ref, sh_ref, w_ref, wx_ref, o_ref, ox_ref, h_ref):
    @pl.when(pl.program_id(1) == 0)
    def _():
        y = _rms(x_ref[...], g_ref[...])
        h_ref[...] = (y * (1.0 + sc_ref[0]) + sh_ref[0]).astype(BF16)
        ox_ref[...] = jnp.dot(h_ref[...], wx_ref[...], preferred_element_type=F32)

    o_ref[...] = jnp.dot(h_ref[...], w_ref[...], preferred_element_type=F32).astype(o_ref.dtype)


def _normmod_matmul(x, g, sc, sh, w, wx, seq_len, tn):
    T, D = x.shape
    N = w.shape[1]
    NX = wx.shape[1]
    tm = min(1024, seq_len)
    per = seq_len // tm
    return pl.pallas_call(
        _nmm_kernel,
        grid=(T // tm, N // tn),
        in_specs=[pl.BlockSpec((tm, D), lambda i, j: (i, 0)),
                  pl.BlockSpec((1, D), lambda i, j: (0, 0)),
                  pl.BlockSpec((1, 1, D), lambda i, j: (i // per, 0, 0)),
                  pl.BlockSpec((1, 1, D), lambda i, j: (i // per, 0, 0)),
                  pl.BlockSpec((D, tn), lambda i, j: (0, j)),
                  pl.BlockSpec((D, NX), lambda i, j: (0, 0))],
        out_specs=[pl.BlockSpec((tm, tn), lambda i, j: (i, j)), pl.BlockSpec((tm, NX), lambda i, j: (i, 0))],
        out_shape=[jax.ShapeDtypeStruct((T, N), BF16), jax.ShapeDtypeStruct((T, NX), F32)],
        scratch_shapes=[pltpu.VMEM((tm, D), BF16)],
        compiler_params=_cp(("parallel", "arbitrary")),
        name="normmod_matmul",
    )(x, g.reshape(1, D), sc, sh, w, wx)


def _conv3(u, prev_row, next_row, w_ref, b_ref, c0, c1):
    tm = u.shape[0]
    row = lax.broadcasted_iota(jnp.int32, u.shape, 0)
    up = jnp.where(row == 0, prev_row, pltpu.roll(u, 1, 0))
    dn = jnp.where(row == tm - 1, next_row, pltpu.roll(u, tm - 1, 0))
    return (w_ref[0:1, c0:c1] * up + w_ref[1:2, c0:c1] * u + w_ref[2:3, c0:c1] * dn + b_ref[0:1, c0:c1])


def _halo_rows(zp_ref, zn_ref, per, c0, c1):
    i = pl.program_id(0)
    first = (i % per) == 0
    last = (i % per) == per - 1
    hp = zp_ref.shape[0]
    prev_row = jnp.where(first, 0.0, zp_ref[hp - 1:hp, c0:c1].astype(F32))
    next_row = jnp.where(last, 0.0, zn_ref[0:1, c0:c1].astype(F32))
    return prev_row, next_row


HALO = 16


def _halo_specs(tm, width, col_block, n_rows):
    nb = n_rows // HALO
    r = tm // HALO
    return [pl.BlockSpec((tm, width), lambda i: (i, col_block)),
            pl.BlockSpec((HALO, width), lambda i: (jnp.maximum(i * r - 1, 0), col_block)),
            pl.BlockSpec((HALO, width), lambda i: (jnp.minimum((i + 1) * r, nb - 1), col_block))]


def _hy_prep_kernel(z_ref, zp_ref, zn_ref, w_ref, b_ref, v_ref, x1_ref, x2_ref, *, per):
    outs = (v_ref, x1_ref, x2_ref)
    for c in range(3):
        c0, c1 = c * HY_CH, (c + 1) * HY_CH
        prev_row, next_row = _halo_rows(zp_ref, zn_ref, per, c0, c1)
        u = z_ref[:, c0:c1].astype(F32)
        outs[c][...] = _conv3(u, prev_row, next_row, w_ref, b_ref, c0, c1).astype(outs[c].dtype)


def _hy_prep(z, conv_w, conv_b, seq_len):
    T = z.shape[0]
    tm = min(512, seq_len)
    W = 3 * HY_CH
    out = jax.ShapeDtypeStruct((T, HY_CH), BF16)
    return pl.pallas_call(
        functools.partial(_hy_prep_kernel, per=seq_len // tm),
        grid=(T // tm,),
        in_specs=_halo_specs(tm, W, Z_HY // W, T) + [
            pl.BlockSpec((3, W), lambda i: (0, 0)), pl.BlockSpec((1, W), lambda i: (0, 0))],
        out_specs=[pl.BlockSpec((tm, HY_CH), lambda i: (i, 0))] * 3,
        out_shape=[out, out, out],
        compiler_params=_cp(("parallel",)),
        name="hyena_prep",
    )(z, z, z, conv_w, conv_b.reshape(1, W))


def _ml_prep_kernel(z_ref, zp_ref, zn_ref, zv_ref, w_ref, b_ref, q_ref, k_ref, v_ref, *, per):
    W = ML_HEADS * ML_HD
    for c in range(2):
        c0, c1 = c * W, (c + 1) * W
        prev_row, next_row = _halo_rows(zp_ref, zn_ref, per, c0, c1)
        u = z_ref[:, c0:c1].astype(F32)
        y = _conv3(u, prev_row, next_row, w_ref, b_ref, c0, c1)
        y = y * jax.nn.sigmoid(y)
        if c == 0:
            for h in range(ML_HEADS):
                q_ref[0, h * ML_HD:(h + 1) * ML_HD, :] = y[:, h * ML_HD:(h + 1) * ML_HD].T.astype(q_ref.dtype)
        else:
            k_ref[...] = (y * (ML_HD ** -0.5)).astype(k_ref.dtype)
    for h in range(ML_HEADS):
        v_ref[0, h * ML_HD:(h + 1) * ML_HD, :] = zv_ref[:, h * ML_HD:(h + 1) * ML_HD].astype(F32).T.astype(v_ref.dtype)


def _ml_prep(z, conv_w, conv_b, B, seq_len):
    T = z.shape[0]
    tm = min(512, seq_len)
    per = seq_len // tm
    W = 2 * ML_HEADS * ML_HD
    Wh = W // 2
    tspec = pl.BlockSpec((1, Wh, tm), lambda i: (i // per, 0, i % per))
    tshape = jax.ShapeDtypeStruct((B, Wh, seq_len), BF16)
    return pl.pallas_call(
        functools.partial(_ml_prep_kernel, per=per),
        grid=(T // tm,),
        in_specs=_halo_specs(tm, W, Z_MLQK // W, T) + [
            pl.BlockSpec((tm, Wh), lambda i: (i, Z_MLV // Wh)),
            pl.BlockSpec((3, W), lambda i: (0, 0)), pl.BlockSpec((1, W), lambda i: (0, 0))],
        out_specs=[tspec, pl.BlockSpec((tm, Wh), lambda i: (i, 0)), tspec],
        out_shape=[tshape, jax.ShapeDtypeStruct((T, Wh), BF16), tshape],
        compiler_params=_cp(("parallel",)),
        name="mlstm_prep",
    )(z, z, z, z, conv_w, conv_b.reshape(1, W))


def _rope_tables(L, hd):
    quarter = hd // 4
    inv = ROPE_BASE ** (-jnp.arange(quarter, dtype=F32) / quarter)
    t = jnp.arange(L)
    row = (t // GRID_W).astype(F32)
    col = (t % GRID_W).astype(F32)
    lane = jnp.arange(LANES)
    within = lane % hd
    is_col = (within // (hd // 2)) == 1
    second = ((within % (hd // 2)) // quarter) == 1
    j = within % quarter
    pos = jnp.where(is_col[None, :], col[:, None], row[:, None])
    ang = pos * inv[j][None, :]
    return jnp.cos(ang), jnp.where(second[None, :], jnp.sin(ang), -jnp.sin(ang))


def _rope(x, cos, sin, quarter):
    lane = lax.broadcasted_iota(jnp.int32, x.shape, 1)
    first = ((lane % (2 * quarter)) // quarter) == 0
    partner = jnp.where(first, pltpu.roll(x, LANES - quarter, 1), pltpu.roll(x, quarter, 1))
    return x * cos + partner * sin


def _ga_prep_kernel(z_ref, cos_ref, sin_ref, qg_ref, kg_ref, q_ref, k_ref, v_ref, *, rope):
    nq, nk = GA_HEADS, GA_KV
    for h in range(nq + nk):
        x = z_ref[:, h * GA_HD:(h + 1) * GA_HD].astype(F32)
        g = qg_ref[...] if h < nq else kg_ref[...]
        y = _rms(x, g)
        if rope:
            y = _rope(y, cos_ref[...], sin_ref[...], GA_HD // 4)
        if h < nq:
            q_ref[0, h * GA_HD:(h + 1) * GA_HD, :] = (y * (GA_HD ** -0.5 * LOG2E)).T.astype(q_ref.dtype)
        else:
            k_ref[:, (h - nq) * GA_HD:(h - nq + 1) * GA_HD] = y.astype(k_ref.dtype)
    for h in range(nk):
        v = z_ref[:, (nq + nk + h) * GA_HD:(nq + nk + h + 1) * GA_HD].astype(F32)
        v_ref[0, h * GA_HD:(h + 1) * GA_HD, :] = v.T.astype(v_ref.dtype)


def _ga_prep(z, cos, sin, qg, kg, B, seq_len, rope):
    T = z.shape[0]
    tm = min(512, seq_len)
    per = seq_len // tm
    W = (GA_HEADS + 2 * GA_KV) * GA_HD
    return pl.pallas_call(
        functools.partial(_ga_prep_kernel, rope=rope),
        grid=(T // tm,),
        in_specs=[pl.BlockSpec((tm, W), lambda i: (i, Z_GA // W)),
                  pl.BlockSpec((tm, LANES), lambda i: (i % per, 0)),
                  pl.BlockSpec((tm, LANES), lambda i: (i % per, 0)),
                  pl.BlockSpec((1, GA_HD), lambda i: (0, 0)),
                  pl.BlockSpec((1, GA_HD), lambda i: (0, 0))],
        out_specs=[pl.BlockSpec((1, GA_HEADS * GA_HD, tm), lambda i: (i // per, 0, i % per)),
                   pl.BlockSpec((tm, GA_KV * GA_HD), lambda i: (i, 0)),
                   pl.BlockSpec((1, GA_KV * GA_HD, tm), lambda i: (i // per, 0, i % per))],
        out_shape=[jax.ShapeDtypeStruct((B, GA_HEADS * GA_HD, seq_len), BF16),
                   jax.ShapeDtypeStruct((T, GA_KV * GA_HD), BF16),
                   jax.ShapeDtypeStruct((B, GA_KV * GA_HD, seq_len), BF16)],
        compiler_params=_cp(("parallel",)),
        name="global_attn_prep",
    )(z, cos, sin, qg.reshape(1, GA_HD), kg.reshape(1, GA_HD))


def _wa_prep_kernel(zq_ref, zkv_ref, cos_ref, sin_ref, q_ref, k_ref, v_ref, *, rope):
    quarter = WA_HD // 4
    for j in range(WA_HEADS * WA_HD // LANES):
        x = zq_ref[:, j * LANES:(j + 1) * LANES].astype(F32)
        if rope:
            x = _rope(x, cos_ref[...], sin_ref[...], quarter)
        q_ref[0, j * LANES:(j + 1) * LANES, :] = (x * (WA_HD ** -0.5 * LOG2E)).T.astype(q_ref.dtype)
    k = zkv_ref[:, 0:LANES].astype(F32)
    if rope:
        k = _rope(k, cos_ref[...], sin_ref[...], quarter)
    k_ref[...] = k.astype(k_ref.dtype)
    v_ref[0] = zkv_ref[:, LANES:].astype(F32).T.astype(v_ref.dtype)


def _wa_prep(z, cos, sin, B, seq_len, rope):
    T = z.shape[0]
    tm = min(512, seq_len)
    per = seq_len // tm
    WQ = WA_HEADS * WA_HD
    return pl.pallas_call(
        functools.partial(_wa_prep_kernel, rope=rope),
        grid=(T // tm,),
        in_specs=[pl.BlockSpec((tm, WQ), lambda i: (i, Z_WAQ // WQ)),
                  pl.BlockSpec((tm, 2 * LANES), lambda i: (i, Z_WAKV // (2 * LANES))),
                  pl.BlockSpec((tm, LANES), lambda i: (i % per, 0)),
                  pl.BlockSpec((tm, LANES), lambda i: (i % per, 0))],
        out_specs=[pl.BlockSpec((1, WQ, tm), lambda i: (i // per, 0, i % per)),
                   pl.BlockSpec((tm, LANES), lambda i: (i, 0)),
                   pl.BlockSpec((1, LANES, tm), lambda i: (i // per, 0, i % per))],
        out_shape=[jax.ShapeDtypeStruct((B, WQ, seq_len), BF16),
                   jax.ShapeDtypeStruct((T, LANES), BF16),
                   jax.ShapeDtypeStruct((B, LANES, seq_len), BF16)],
        compiler_params=_cp(("parallel",)),
        name="window_attn_prep",
    )(z, z, cos, sin)


def _ga_kernel(q_ref, k_ref, vt_ref, o_ref, acc_ref, m_ref, *, nchunks, tk, tq):
    acc_ref[...] = jnp.zeros_like(acc_ref)
    m_ref[...] = jnp.full_like(m_ref, NEG)
    qt = jnp.concatenate([q_ref[0, 0:GA_HD, :], q_ref[0, GA_HD:, :]], axis=1)

    W = GA_CB
    nblk = 2 * tq // W

    def scores(k):
        return tuple(jnp.dot(k, qt[:, i * W:(i + 1) * W], preferred_element_type=F32) for i in range(nblk))

    def softmax_pv(s_blocks, vt):
        cols = [slice(i * W, (i + 1) * W) for i in range(nblk)]
        m_old = [m_ref[:, cs] for cs in cols]
        m_new = [jnp.maximum(mo, jnp.max(s, axis=0, keepdims=True)) for mo, s in zip(m_old, s_blocks)]
        alpha = [jnp.exp2(mo - mn) for mo, mn in zip(m_old, m_new)]
        vaug = jnp.concatenate([vt, jnp.ones((16, vt.shape[1]), BF16)], axis=0)
        pv = [jnp.dot(vaug, jnp.exp2((s - mn).astype(BF16)), preferred_element_type=F32)
              for s, mn in zip(s_blocks, m_new)]
        for i, cs in enumerate(cols):
            acc_ref[:, cs] = alpha[i] * acc_ref[:, cs] + pv[i]
            m_ref[:, cs] = m_new[i]

    def body(j, s):
        s_next = scores(k_ref[0, pl.ds(pl.multiple_of((j + 1) * tk, tk), tk), :])
        softmax_pv(s, vt_ref[0, :, pl.ds(pl.multiple_of(j * tk, tk), tk)])
        return s_next
    s = scores(k_ref[0, 0:tk, :])
    if nchunks > 1:
        s = lax.fori_loop(0, nchunks - 1, body, s, unroll=GA_UNROLL if (nchunks - 1) % GA_UNROLL == 0 else 1)
    Lk = k_ref.shape[1]
    if Lk > nchunks * tk:
        s_tail = scores(k_ref[0, nchunks * tk:, :])
    softmax_pv(s, vt_ref[0, :, (nchunks - 1) * tk:nchunks * tk])
    if Lk > nchunks * tk:
        softmax_pv(s_tail, vt_ref[0, :, nchunks * tk:])
    o = acc_ref[0:GA_HD, :] / acc_ref[GA_HD:GA_HD + 1, :]
    for h in range(2):
        o_ref[0, :, h * GA_HD:(h + 1) * GA_HD] = o[:, h * tq:(h + 1) * tq].T.astype(o_ref.dtype)


def _global_attention(qt, k, vt):
    B, _, Lq = qt.shape
    Lk = k.shape[1]
    tq = min(512, Lq)
    tk = min(GA_TK, Lk)
    W = 2 * GA_HD
    return pl.pallas_call(
        functools.partial(_ga_kernel, nchunks=Lk // tk, tk=tk, tq=tq),
        grid=(B, GA_KV, Lq // tq),
        in_specs=[pl.BlockSpec((1, W, tq), lambda b, g, i: (b, g, i)),
                  pl.BlockSpec((1, Lk, GA_HD), lambda b, g, i: (b, 0, g)),
                  pl.BlockSpec((1, GA_HD, Lk), lambda b, g, i: (b, g, 0))],
        out_specs=pl.BlockSpec((1, tq, W), lambda b, g, i: (b, i, g)),
        out_shape=jax.ShapeDtypeStruct((B, Lq, GA_HEADS * GA_HD), BF16),
        scratch_shapes=[pltpu.VMEM((GA_HD + 16, 2 * tq), F32), pltpu.VMEM((1, 2 * tq), F32)],
        compiler_params=_cp(("parallel", "parallel", "parallel")),
        name="global_attention",
    )(qt, k, vt)


def _wa_kernel(*refs, band, nq, nsub):
    if band:
        q_ref = refs[0]
        kb = refs[1:nsub + 3]
        vb = refs[nsub + 3:2 * nsub + 5]
        kc_ref, vc_ref, sink_ref, o_ref = refs[2 * nsub + 5:]
    else:
        q_ref, kc_ref, vc_ref, sink_ref, o_ref = refs
    tq = WINDOW
    Lc = kc_ref.shape[1]
    step = pl.program_id(1)
    G = WA_HEADS // WA_KV
    cols = WA_PAIR * tq
    npr = WA_HEADS // WA_PAIR
    grp = [(pr * WA_PAIR) // G for pr in range(npr)]
    zeros = jnp.zeros((WA_HD, cols), BF16)
    if band:
        nb = 3 * tq
        c = lax.broadcasted_iota(jnp.int32, (nb + Lc, cols), 0)
        r = lax.broadcasted_iota(jnp.int32, (nb + Lc, cols), 1) % tq
        in_band = (c >= r) & (c <= r + 2 * WINDOW)
    ones_rows = jnp.ones((16, (3 * tq if band else 0) + Lc), BF16)
    chains = [(u, pr) for u in range(nsub) for pr in range(npr)]
    keys, vals, valid = [], [], []
    for u in range(nsub):
        if band:
            keys.append(jnp.concatenate([kb[u][0], kb[u + 1][0], kb[u + 2][0], kc_ref[0]], axis=0))
            vals.append(jnp.concatenate([vb[u][0], vb[u + 1][0], vb[u + 2][0], vc_ref[0]], axis=1))
            qi = step * nsub + u
            lo = jnp.where(qi == 0, tq, 0)
            hi = jnp.where(qi == nq - 1, 2 * tq, nb)
            valid.append((in_band & (c >= lo) & (c < hi)) | (c >= nb))
        else:
            keys.append(kc_ref[0])
            vals.append(vc_ref[0])
    s = {}
    for u, pr in chains:
        qg = jnp.concatenate([q_ref[0, (WA_PAIR * pr + h) * WA_HD:(WA_PAIR * pr + h + 1) * WA_HD, u * tq:(u + 1) * tq]
                              for h in range(WA_PAIR)], axis=1)
        qpad = jnp.concatenate([qg, zeros] if grp[pr] == 0 else [zeros, qg], axis=0)
        sp = jnp.dot(keys[u], qpad, preferred_element_type=F32)
        s[u, pr] = jnp.where(valid[u], sp, NEG) if band else sp
    sink = [sink_ref[pr] for pr in range(npr)]
    m = {ch: jnp.maximum(jnp.max(s[ch], axis=0, keepdims=True), sink[ch[1]]) for ch in chains}
    p = {ch: jnp.exp2(s[ch] - m[ch]).astype(BF16) for ch in chains}
    R = {}
    for u, pr in chains:
        vaug = jnp.concatenate([vals[u][grp[pr] * WA_HD:(grp[pr] + 1) * WA_HD, :], ones_rows], axis=0)
        R[u, pr] = jnp.dot(vaug, p[u, pr], preferred_element_type=F32)
    for u, pr in chains:
        o = R[u, pr][:WA_HD] / (R[u, pr][WA_HD:WA_HD + 1] + jnp.exp2(sink[pr] - m[u, pr]))
        ot = jnp.concatenate([o[:, h * tq:(h + 1) * tq] for h in range(WA_PAIR)], axis=0)
        wo = WA_PAIR * WA_HD
        o_ref[0, u * tq:(u + 1) * tq, pr * wo:(pr + 1) * wo] = ot.T.astype(o_ref.dtype)


def _window_attention(qt, kc, vtc, sink, kl=None, vtl=None):
    B, WQ, Lq = qt.shape
    Lc = kc.shape[1]
    tq = WINDOW
    nq = Lq // tq
    nsub = math.gcd(WA_SUB, nq)
    band = kl is not None
    npair = WA_HEADS // WA_PAIR
    sink_row = jnp.repeat(sink.astype(F32).reshape(npair, WA_PAIR) * LOG2E, tq, axis=1).reshape(npair, 1, WA_PAIR * tq)
    qspec = pl.BlockSpec((1, WQ, nsub * tq), lambda b, i: (b, 0, i))
    kcspec = pl.BlockSpec((1, Lc, LANES), lambda b, i: (b, 0, 0))
    vcspec = pl.BlockSpec((1, LANES, Lc), lambda b, i: (b, 0, 0))
    sspec = pl.BlockSpec((npair, 1, WA_PAIR * tq), lambda b, i: (0, 0, 0))
    if band:
        blk = lambda off: (lambda i: jnp.clip(i * nsub + off, 0, nq - 1))
        kspec = lambda f: pl.BlockSpec((1, tq, LANES), lambda b, i: (b, f(i), 0))
        vspec = lambda f: pl.BlockSpec((1, LANES, tq), lambda b, i: (b, 0, f(i)))
        offs = range(-1, nsub + 1)
        in_specs = ([qspec] + [kspec(blk(o)) for o in offs] + [vspec(blk(o)) for o in offs]
                    + [kcspec, vcspec, sspec])
        args = (qt,) + (kl,) * (nsub + 2) + (vtl,) * (nsub + 2) + (kc, vtc, sink_row)
    else:
        in_specs, args = [qspec, kcspec, vcspec, sspec], (qt, kc, vtc, sink_row)
    return pl.pallas_call(
        functools.partial(_wa_kernel, band=band, nq=nq, nsub=nsub),
        grid=(B, nq // nsub),
        in_specs=in_specs,
        out_specs=pl.BlockSpec((1, nsub * tq, WQ), lambda b, i: (b, i, 0)),
        out_shape=jax.ShapeDtypeStruct((B, Lq, WQ), BF16),
        compiler_params=_cp(("parallel", "parallel")),
        name="window_attention",
    )(*args)


def _mlstm_kernel(*refs, reverse, nc, bb):
    if reverse:
        (qt_ref, k_ref, vt_ref, g_ref, gb_ref, c0_ref, m0_ref, hft_ref, o_ref, gn_ref,
         y_ref, cf_ref, mf_ref, c_scr, m_scr) = refs
    else:
        (qt_ref, k_ref, vt_ref, g_ref, gb_ref, c0_ref, m0_ref,
         y_ref, cf_ref, mf_ref, c_scr, m_scr) = refs
    T = ML_CHUNK
    d = 1 if reverse else 0
    step = pl.program_id(1)

    @pl.when(step == 0)
    def _():
        c_scr[...] = c0_ref[...]
        m_scr[...] = m0_ref[...]

    si = lax.broadcasted_iota(jnp.int32, (T, T), 0)
    ti = lax.broadcasted_iota(jnp.int32, (T, T), 1)
    tri = ((ti >= si) if reverse else (ti <= si)).astype(F32)
    mask_t = (si >= ti) if reverse else (si <= ti)
    ones_rows = jnp.ones((ML_HD, T), BF16)
    e_last = 0 if reverse else T - 1
    chains = [(bi, h) for bi in range(bb) for h in range(ML_HEADS)]
    gates = []
    for bi in range(bb):
        G = g_ref[bi] + gb_ref[...]
        LF = jax.nn.log_sigmoid(G)
        Bc = jnp.dot(tri, LF, preferred_element_type=F32, precision=HIGHEST)
        gates.append((G.T, Bc.T, Bc - pltpu.roll(G, 4, 1)))
    st, ph = {}, {}
    for bi, h in chains:
        hs = slice(h * ML_HD, (h + 1) * ML_HD)
        st[bi, h] = jnp.dot(k_ref[bi, :, hs], qt_ref[bi, hs, :], preferred_element_type=F32)
    for bi, h in chains:
        GT, BT, Dc = gates[bi]
        fl, il = d * 8 + 4 + h, d * 8 + h
        b_row, i_row = BT[fl:fl + 1, :], GT[il:il + 1, :]
        log_d = jnp.where(mask_t, b_row - Dc[:, fl:fl + 1], NEG)
        m_prev = m_scr[bi, h, 0:1, 0:1]
        m_inter = b_row + m_prev
        m_t = jnp.maximum(m_inter, jnp.max(log_d, axis=0, keepdims=True))
        wqk = (st[bi, h] * jnp.exp(log_d - m_t)).astype(BF16)
        b_end = BT[fl:fl + 1, e_last:e_last + 1]
        log_w = b_end - b_row + i_row
        m_next = jnp.maximum(b_end + m_prev, jnp.max(log_w, axis=1, keepdims=True))
        ph[bi, h] = (wqk, jnp.exp(m_inter - m_t), jnp.exp(-m_t), jnp.exp(log_w - m_next),
                     jnp.exp(b_end + m_prev - m_next), m_next)
    for bi, h in chains:
        hs = slice(h * ML_HD, (h + 1) * ML_HD)
        wqk, cs, em, w_row, decay, m_next = ph[bi, h]
        kh, qt = k_ref[bi, :, hs], qt_ref[bi, hs, :]
        vaug = jnp.concatenate([vt_ref[bi, hs, :], ones_rows], axis=0)
        R = (jnp.dot(vaug, wqk, preferred_element_type=F32)
             + cs * jnp.dot(c_scr[bi, h].astype(BF16), qt, preferred_element_type=F32))
        hh = R[:ML_HD] / jnp.maximum(jnp.abs(R[ML_HD:]), em)
        if reverse:
            hsum = hft_ref[bi, hs, :] + hh
            hn = hsum * lax.rsqrt(jnp.mean(hsum * hsum, axis=0, keepdims=True) + EPS) * gn_ref[hs, :]
            y_ref[bi, :, hs] = (jax.nn.sigmoid(o_ref[bi, :, hs].astype(F32)) * hn.T).astype(y_ref.dtype)
        else:
            y_ref[bi, hs, :] = hh
        wv = (vaug.astype(F32) * w_row).astype(BF16)
        c_scr[bi, h] = decay * c_scr[bi, h] + jnp.dot(wv, kh, preferred_element_type=F32)
        m_scr[bi, h] = jnp.broadcast_to(m_next, (8, LANES))

    @pl.when(step == nc - 1)
    def _():
        cf_ref[...] = c_scr[...]
        mf_ref[...] = m_scr[...]


def _mlstm_scan(qt, k, vt, z, gates, gate_b, c0, m0, reverse, hft=None, norm_g=None):
    B, L, W = k.shape
    T = ML_CHUNK
    nc = L // T
    bb = 4 if B % 4 == 0 else 2
    cj =(lambda j: nc - 1 - j) if reverse else (lambda j: j)
    tok = pl.BlockSpec((bb, T, W), lambda b, j: (b, cj(j), 0))
    ttok = pl.BlockSpec((bb, W, T), lambda b, j: (b, 0, cj(j)))
    cspec = pl.BlockSpec((bb, ML_HEADS, 2 * ML_HD, ML_HD), lambda b, j: (b, 0, 0, 0))
    mspec = pl.BlockSpec((bb, ML_HEADS, 8, LANES), lambda b, j: (b, 0, 0, 0))
    in_specs = [ttok, tok, ttok, pl.BlockSpec((bb, T, LANES), lambda b, j: (b, cj(j), 0)),
                pl.BlockSpec((1, LANES), lambda b, j: (0, 0)), cspec, mspec]
    args = [qt, k, vt, gates, gate_b, c0, m0]
    if reverse:
        in_specs += [ttok, pl.BlockSpec((bb, T, W), lambda b, j: (b, cj(j), Z_MLO // W)),
                     pl.BlockSpec((W, T), lambda b, j: (0, 0))]
        args += [hft, z, jnp.broadcast_to(norm_g.astype(F32).reshape(W, 1), (W, T))]
    return pl.pallas_call(
        functools.partial(_mlstm_kernel, reverse=reverse, nc=nc, bb=bb),
        grid=(B // bb, nc),
        in_specs=in_specs,
        out_specs=[tok if reverse else ttok, cspec, mspec],
        out_shape=[jax.ShapeDtypeStruct((B, L, W), BF16) if reverse else jax.ShapeDtypeStruct((B, W, L), F32),
                   jax.ShapeDtypeStruct(c0.shape, F32), jax.ShapeDtypeStruct(m0.shape, F32)],
        scratch_shapes=[pltpu.VMEM((bb, ML_HEADS, 2 * ML_HD, ML_HD), F32), pltpu.VMEM((bb, ML_HEADS, 8, LANES), F32)],
        compiler_params=_cp(("parallel", "arbitrary")),
        name="mlstm_reverse" if reverse else "mlstm_forward",
    )(*args)


def _fft_dims(Lp):
    n1 = 2 * Lp // FFT_N2
    nt1 = Lp // FFT_N2
    nk1 = -(-(n1 // 2 + 1) // 8) * 8
    return n1, nt1, nk1


def _fft_tables(Lp):
    n1, nt1, nk1 = _fft_dims(Lp)
    N = 2 * Lp
    k1 = jnp.arange(nk1)
    t1 = jnp.arange(nt1)
    ang_a = (2.0 * math.pi / n1) * ((k1[:, None] * t1[None, :]) % n1).astype(F32)
    fa = jnp.stack([jnp.cos(ang_a), -jnp.sin(ang_a)], axis=1).reshape(2 * nk1, nt1)
    wgt = jnp.where((k1 == 0) | (k1 == n1 // 2), 1.0, 2.0) * (k1 <= n1 // 2) / N
    fai = jnp.stack([jnp.cos(ang_a) * wgt[:, None], -jnp.sin(ang_a) * wgt[:, None]], axis=1)
    fai = fai.reshape(2 * nk1, nt1).T
    k2 = jnp.arange(FFT_N2)
    t2 = jnp.arange(FFT_N2)
    idx = (t2[None, None, :] * k1[:, None, None] + n1 * t2[None, None, :] * k2[None, :, None]) % N
    phi = (2.0 * math.pi / N) * idx.astype(F32)
    gr, gi = jnp.cos(phi), -jnp.sin(phi)
    gfwd = jnp.concatenate([jnp.concatenate([gr, -gi], axis=2), jnp.concatenate([gi, gr], axis=2)], axis=1)
    grt, git = jnp.swapaxes(gr, 1, 2), jnp.swapaxes(gi, 1, 2)
    ginv = jnp.concatenate([jnp.concatenate([grt, git], axis=2), jnp.concatenate([-git, grt], axis=2)], axis=1)
    eye = jnp.eye(FFT_TK, dtype=F32)
    return (jnp.kron(fa, eye).astype(BF16), jnp.kron(fai, eye).astype(BF16), gfwd.astype(BF16), ginv.astype(BF16))


FFT_TB = 16
FFT_TK = 8
FFT_CB = 512


def _fa_kernel(fa_ref, x_ref, o_ref):
    nt1, tb, cb = x_ref.shape[1:]
    rows = o_ref.shape[1]
    x = x_ref[0].astype(F32).reshape(nt1, tb // FFT_TK, FFT_TK, cb)
    parts = []
    for j in range(tb // FFT_TK):
        xj = x[:, j].reshape(nt1 * FFT_TK, cb).astype(BF16)
        r = jnp.dot(fa_ref[...], xj, preferred_element_type=F32)
        parts.append(r.reshape(rows, 1, FFT_TK, cb))
    o_ref[0] = jnp.concatenate(parts, axis=1).reshape(rows, tb, cb).astype(o_ref.dtype)


def _fft_stage_a(fa, y, out_dtype):
    B, Lp, C = y.shape
    nt1 = Lp // FFT_N2
    rows = fa.shape[0] // FFT_TK
    return pl.pallas_call(
        _fa_kernel,
        grid=(B, FFT_N2 // FFT_TB, C // FFT_CB),
        in_specs=[pl.BlockSpec(fa.shape, lambda b, j, c: (0, 0)),
                  pl.BlockSpec((1, nt1, FFT_TB, FFT_CB), lambda b, j, c: (b, 0, j, c))],
        out_specs=pl.BlockSpec((1, rows, FFT_TB, FFT_CB), lambda b, j, c: (b, 0, j, c)),
        out_shape=jax.ShapeDtypeStruct((B, rows, FFT_N2, C), out_dtype),
        compiler_params=_cp(("parallel", "parallel", "parallel")),
        name="fft_stage_a",
    )(fa, y.reshape(B, nt1, FFT_N2, C))


def _fc_filter_kernel(g_ref, s_ref, ss_ref, h_ref, *, kb):
    C = HY_CH
    for i in range(kb):
        for n in range(HY_ORDER):
            cf, cb = (2 * n) * C, (2 * n + 1) * C
            scale = lax.rsqrt(ss_ref[0:1, cf:cf + C] + ss_ref[0:1, cb:cb + C] + EPS)
            sf = s_ref[0, 2 * i:2 * i + 2, :, cf:cf + C].reshape(2 * FFT_N2, C).astype(BF16)
            sb = s_ref[0, 2 * i:2 * i + 2, :, cb:cb + C].reshape(2 * FFT_N2, C).astype(BF16)
            xf = jnp.dot(g_ref[i], sf, preferred_element_type=F32)
            xb = jnp.dot(g_ref[i], sb, preferred_element_type=F32)
            h_ref[n, 2 * i] = (xf[:FFT_N2] + xb[:FFT_N2]) * scale
            h_ref[n, 2 * i + 1] = (xf[FFT_N2:] - xb[FFT_N2:]) * scale


def _fft_filter_spectrum(gfwd, s_filt, sumsq):
    rows = s_filt.shape[1]
    C4 = s_filt.shape[-1]
    kb = 2
    return pl.pallas_call(
        functools.partial(_fc_filter_kernel, kb=kb),
        grid=(rows // (2 * kb),),
        in_specs=[pl.BlockSpec((kb, 2 * FFT_N2, 2 * FFT_N2), lambda i: (i, 0, 0)),
                  pl.BlockSpec((1, 2 * kb, FFT_N2, C4), lambda i: (0, i, 0, 0)),
                  pl.BlockSpec((1, C4), lambda i: (0, 0))],
        out_specs=pl.BlockSpec((HY_ORDER, 2 * kb, FFT_N2, HY_CH), lambda i: (0, i, 0, 0)),
        out_shape=jax.ShapeDtypeStruct((HY_ORDER, rows, FFT_N2, HY_CH), F32),
        compiler_params=_cp(("parallel",)),
        name="fft_filter_spectrum",
    )(gfwd, s_filt, sumsq)


def _fc_kernel(g_ref, gi_ref, h_ref, s_ref, o_ref, *, kb):
    C = s_ref.shape[-1]
    x = [jnp.dot(g_ref[i], s_ref[0, 2 * i:2 * i + 2].reshape(2 * FFT_N2, C), preferred_element_type=F32)
         for i in range(kb)]
    z = []
    for i in range(kb):
        xr, xi = x[i][:FFT_N2], x[i][FFT_N2:]
        hr, hi = h_ref[0, 2 * i], h_ref[0, 2 * i + 1]
        z.append(jnp.concatenate([xr * hr - xi * hi, xr * hi + xi * hr], axis=0).astype(BF16))
    bm = [jnp.dot(gi_ref[i], z[i], preferred_element_type=F32) for i in range(kb)]
    for i in range(kb):
        o_ref[0, 2 * i:2 * i + 2] = bm[i].reshape(2, FFT_N2, C).astype(o_ref.dtype)


def _fft_stage_c(gfwd, ginv, hspec, order, s):
    B, rows, _, C = s.shape
    kb = 8 if rows % 16 == 0 else 4
    sspec =pl.BlockSpec((1, 2 * kb, FFT_N2, C), lambda i, b: (b, i, 0, 0))
    gspec = pl.BlockSpec((kb, 2 * FFT_N2, 2 * FFT_N2), lambda i, b: (i, 0, 0))
    return pl.pallas_call(
        functools.partial(_fc_kernel, kb=kb),
        grid=(rows // (2 * kb), B),
        in_specs=[gspec, gspec,
                  pl.BlockSpec((1, 2 * kb, FFT_N2, C), lambda i, b: (order, i, 0, 0)), sspec],
        out_specs=sspec,
        out_shape=jax.ShapeDtypeStruct(s.shape, BF16),
        compiler_params=_cp(("parallel", "arbitrary")),
        name="fft_stage_c",
    )(gfwd, ginv, hspec, s)


def _fai_kernel(fai_ref, b_ref, y_ref, gate_ref, skip_ref, o_ref):
    nt1, tb, cb = y_ref.shape[1:]
    rows = b_ref.shape[1]
    bm = b_ref[0].astype(F32).reshape(rows, tb // FFT_TK, FFT_TK, cb)
    parts = []
    for j in range(tb // FFT_TK):
        bj = bm[:, j].reshape(rows * FFT_TK, cb).astype(BF16)
        yf = jnp.dot(fai_ref[...], bj, preferred_element_type=F32)
        parts.append(yf.reshape(nt1, 1, FFT_TK, cb))
    yf = jnp.concatenate(parts, axis=1).reshape(nt1, tb, cb)
    out = gate_ref[0].astype(F32) * (yf + skip_ref[...].reshape(1, 1, cb) * y_ref[0].astype(F32))
    o_ref[0] = out.astype(o_ref.dtype)


def _fft_stage_a_inv(fai, bm, y, gate, skip):
    B, Lp, C = y.shape
    nt1 = Lp // FFT_N2
    rows = fai.shape[1] // FFT_TK
    tok = pl.BlockSpec((1, nt1, FFT_TB, FFT_CB), lambda b, j, c: (b, 0, j, c))
    out = pl.pallas_call(
        _fai_kernel,
        grid=(B, FFT_N2 // FFT_TB, C // FFT_CB),
        in_specs=[pl.BlockSpec(fai.shape, lambda b, j, c: (0, 0)),
                  pl.BlockSpec((1, rows, FFT_TB, FFT_CB), lambda b, j, c: (b, 0, j, c)),
                  tok, tok, pl.BlockSpec((1, FFT_CB), lambda b, j, c: (0, c))],
        out_specs=tok,
        out_shape=jax.ShapeDtypeStruct((B, nt1, FFT_N2, C), BF16),
        compiler_params=_cp(("parallel", "parallel", "parallel")),
        name="fft_stage_a_inv",
    )(fai, bm, y.reshape(B, nt1, FFT_N2, C), gate.reshape(B, nt1, FFT_N2, C), skip.astype(F32).reshape(1, C))
    return out.reshape(B, Lp, C)


def _hgen_kernel(z_ref, w1_ref, b1_ref, fr_ref, w2_ref, b2_ref, w3_ref, dec_ref, h_ref, ss_ref):
    i = pl.program_id(0)
    z = z_ref[...]
    tm = z.shape[0]
    h = jnp.sin(fr_ref[0:1, :] * (jnp.dot(z, w1_ref[...], preferred_element_type=F32, precision=HIGHEST)
                                 + b1_ref[...]))
    h = jnp.sin(fr_ref[1:2, :] * (jnp.dot(h, w2_ref[...], preferred_element_type=F32, precision=HIGHEST)
                                 + b2_ref[...]))
    h = jnp.dot(h, w3_ref[...], preferred_element_type=F32, precision=HIGHEST)
    h = h * jnp.exp(-z[:, 0:1] * jnp.abs(dec_ref[...]))
    row = lax.broadcasted_iota(jnp.int32, h.shape, 0) + i * tm
    col = lax.broadcasted_iota(jnp.int32, h.shape, 1)
    h = jnp.where((row == 0) & ((col // HY_CH) % 2 == 1), 0.0, h)
    h_ref[...] = h

    @pl.when(i == 0)
    def _():
        ss_ref[...] = jnp.zeros_like(ss_ref)

    ss_ref[...] += jnp.sum(h * h, axis=0, keepdims=True)


def _hyena_filters(L, p):
    t = jnp.arange(L, dtype=F32)
    tn = t / (L - 1)
    w = 2.0 * math.pi * t / L
    bands = jnp.linspace(1e-4, HY_BANDS - 1, HY_BANDS, dtype=F32)
    ang = w[:, None] * bands[None, :]
    z = jnp.concatenate([tn[:, None], jnp.cos(ang), -jnp.sin(ang)], axis=-1)
    z = jnp.pad(z, ((0, 0), (0, LANES - HY_EMB)))
    w1 = jnp.pad(p['hy_pe_w1'].astype(F32), ((0, LANES - HY_EMB), (0, 0)))
    nf = HY_ORDER * 2 * HY_CH
    tm = min(512, L)
    const = lambda shape: pl.BlockSpec(shape, lambda i: (0,) * len(shape))
    return pl.pallas_call(
        _hgen_kernel,
        grid=(L // tm,),
        in_specs=[pl.BlockSpec((tm, LANES), lambda i: (i, 0)), const((LANES, HY_FFN)), const((1, HY_FFN)),
                  const((2, HY_FFN)), const((HY_FFN, HY_FFN)), const((1, HY_FFN)), const((HY_FFN, nf)),
                  const((1, nf))],
        out_specs=[pl.BlockSpec((tm, nf), lambda i: (i, 0)), const((1, nf))],
        out_shape=[jax.ShapeDtypeStruct((L, nf), F32), jax.ShapeDtypeStruct((1, nf), F32)],
        compiler_params=_cp(("arbitrary",)),
        name="hyena_filter_gen",
    )(z, w1, p['hy_pe_b1'].reshape(1, HY_FFN), p['hy_freq'], p['hy_pe_w2'], p['hy_pe_b2'].reshape(1, HY_FFN),
      p['hy_pe_w3'], p['hy_decay'].reshape(1, nf))


def _hyena_branch(z, p, B, L):
    v, x1, x2 = _hy_prep(z, p['hy_conv_w'], p['hy_conv_b'], L)
    Lp = max(L, 2048)
    fa, fai, gfwd, ginv = _fft_tables(Lp)
    hfilt, sumsq = _hyena_filters(L, p)
    pad3 = lambda a: a.reshape(B, L, HY_CH) if Lp == L else jnp.pad(a.reshape(B, L, HY_CH), ((0, 0), (0, Lp - L), (0, 0)))
    hf = hfilt if Lp == L else jnp.pad(hfilt, ((0, Lp - L), (0, 0)))
    s_filt = _fft_stage_a(fa, hf[None], F32)
    hspec = _fft_filter_spectrum(gfwd, s_filt, sumsq)
    y = pad3(v)
    for n, gate in enumerate((pad3(x1), pad3(x2))):
        s = _fft_stage_a(fa, y, BF16)
        bm = _fft_stage_c(gfwd, ginv, hspec, n, s)
        y = _fft_stage_a_inv(fai, bm, y, gate, p['hy_skip'][n])
    return y[:, :L].reshape(B * L, HY_CH)


def _merge_kernel(ya_ref, yb_ref, yc_ref, yd_ref, g0_ref, g1_ref, g2_ref, g3_ref, wup_ref, wout_ref,
                  x_ref, gate_ref, o_ref):
    acc = None
    for n, (y_ref, g_ref) in enumerate(((ya_ref, g0_ref), (yb_ref, g1_ref), (yc_ref, g2_ref), (yd_ref, g3_ref))):
        t = jax.nn.sigmoid(g_ref[...].astype(F32)) * jnp.dot(y_ref[...], wup_ref[n], preferred_element_type=F32)
        acc = t if acc is None else acc + t
    yl = jnp.dot(acc.astype(BF16), wout_ref[...], preferred_element_type=F32)
    o_ref[...] = x_ref[...] + gate_ref[0] * yl


def _merge(ys, z, w_up, w_out, x, gate, seq_len):
    T, D = x.shape
    tm = min(512, seq_len)
    per = seq_len // tm
    Wy = ys[0].shape[1]
    yspec = pl.BlockSpec((tm, Wy), lambda i: (i, 0))
    gspecs = [pl.BlockSpec((tm, D), functools.partial(lambda i, n: (i, Z_GATE // D + n), n=n)) for n in range(4)]
    return pl.pallas_call(
        _merge_kernel,
        grid=(T // tm,),
        in_specs=[yspec] * 4 + gspecs + [
            pl.BlockSpec((4, Wy, D), lambda i: (0, 0, 0)), pl.BlockSpec((D, D), lambda i: (0, 0)),
            pl.BlockSpec((tm, D), lambda i: (i, 0)), pl.BlockSpec((1, 1, D), lambda i: (i // per, 0, 0))],
        out_specs=pl.BlockSpec((tm, D), lambda i: (i, 0)),
        out_shape=jax.ShapeDtypeStruct((T, D), F32),
        compiler_params=_cp(("parallel",)),
        name="merge_branches",
    )(*ys, z, z, z, z, w_up, w_out, x, gate)


def _mlp_kernel(x_ref, g_ref, sc_ref, sh_ref, w1_ref, b1_ref, w2_ref, b2_ref, gate_ref, fg_ref, o_ref,
                h_ref, acc_ref, *, nk, final):
    k = pl.program_id(1)

    @pl.when(k == 0)
    def _():
        y = _rms(x_ref[...], g_ref[...])
        h_ref[...] = (y * (1.0 + sc_ref[0]) + sh_ref[0]).astype(BF16)
        acc_ref[...] = jnp.zeros_like(acc_ref)

    a = jnp.maximum(jnp.dot(h_ref[...], w1_ref[...], preferred_element_type=F32) + b1_ref[...], 0.0)
    acc_ref[...] += jnp.dot((a * a).astype(BF16), w2_ref[...], preferred_element_type=F32)

    @pl.when(k == nk - 1)
    def _():
        out = x_ref[...] + gate_ref[0] * (acc_ref[...] + b2_ref[...])
        if final:
            out = _rms(out, fg_ref[...])
        o_ref[...] = out


def _mlp(x, g, sc, sh, w1, b1, w2, b2, gate, final_g, seq_len, final):
    T, D = x.shape
    F = w1.shape[1]
    tm = min(1024, seq_len)
    per = seq_len // tm
    tk = 1024
    nk = F // tk
    row = lambda i, k: (i // per, 0, 0)
    return pl.pallas_call(
        functools.partial(_mlp_kernel, nk=nk, final=final),
        grid=(T // tm, nk),
        in_specs=[pl.BlockSpec((tm, D), lambda i, k: (i, 0)), pl.BlockSpec((1, D), lambda i, k: (0, 0)),
                  pl.BlockSpec((1, 1, D), row), pl.BlockSpec((1, 1, D), row),
                  pl.BlockSpec((D, tk), lambda i, k: (0, k)), pl.BlockSpec((1, tk), lambda i, k: (0, k)),
                  pl.BlockSpec((tk, D), lambda i, k: (k, 0)), pl.BlockSpec((1, D), lambda i, k: (0, 0)),
                  pl.BlockSpec((1, 1, D), row), pl.BlockSpec((1, D), lambda i, k: (0, 0))],
        out_specs=pl.BlockSpec((tm, D), lambda i, k: (i, 0)),
        out_shape=jax.ShapeDtypeStruct((T, D), F32),
        scratch_shapes=[pltpu.VMEM((tm, D), BF16), pltpu.VMEM((tm, D), F32)],
        compiler_params=_cp(("parallel", "arbitrary")),
        name="mlp",
    )(x, g.reshape(1, D), sc, sh, w1, b1.reshape(1, F), w2, b2.reshape(1, D), gate, final_g.reshape(1, D))


def _pack_w_in(w_in):
    hy_e = 3 * HY_CH
    ga_e = hy_e + (GA_HEADS + 2 * GA_KV) * GA_HD
    mw = ML_HEADS * ML_HD
    ml_e = ga_e + 4 * mw + 16
    wa_e = ml_e + (WA_HEADS + 2 * WA_KV) * WA_HD
    hy, ga = w_in[:, :hy_e], w_in[:, hy_e:ga_e]
    ml = w_in[:, ga_e:ml_e]
    wa = w_in[:, ml_e:wa_e]
    gate = w_in[:, wa_e:]
    waq, wakv = wa[:, :WA_HEADS * WA_HD], wa[:, WA_HEADS * WA_HD:]
    pad = jnp.zeros((w_in.shape[0], Z_COLS - Z_WAKV - wakv.shape[1]), w_in.dtype)
    packed = jnp.concatenate([hy, waq, ga, ml[:, :2 * mw], ml[:, 2 * mw:3 * mw], ml[:, 3 * mw:4 * mw], gate, wakv, pad],
                             axis=1)
    wg = jnp.pad(ml[:, 4 * mw:], ((0, 0), (0, LANES - 16)))
    return packed.astype(BF16), wg


def _token_mixers(zl, zc, gl, gc, p, B, L, Lc, with_ctx_out):
    ya_l = _hyena_branch(zl, p, B, L)
    ya_c = _hyena_branch(zc, p, B, Lc) if with_ctx_out else None
    cos, sin = _rope_tables(L, GA_HD)
    ql, kl, vl = _ga_prep(zl, cos, sin, p['ga_q_g'], p['ga_k_g'], B, L, True)
    qc, kc, vc = _ga_prep(zc, cos[:Lc], sin[:Lc], p['ga_q_g'], p['ga_k_g'], B, Lc, False)
    r3 = lambda a, n: a.reshape(B, n, a.shape[-1])
    k_all = jnp.concatenate([r3(kl, L), r3(kc, Lc)], axis=1)
    vt_all = jnp.concatenate([vl, vc], axis=2)
    yb_l = _global_attention(ql, k_all, vt_all).reshape(B * L, -1)
    yb_c = _global_attention(qc, r3(kc, Lc), vc).reshape(B * Lc, -1) if with_ctx_out else None
    mq_l, mk_l, mv_l = _ml_prep(zl, p['ml_conv_w'], p['ml_conv_b'], B, L)
    mq_c, mk_c, mv_c = _ml_prep(zc, p['ml_conv_w'], p['ml_conv_b'], B, Lc)
    gb = jnp.pad(p['ml_gate_b'].astype(F32), (0, LANES - 16)).reshape(1, LANES)
    c0 = jnp.zeros((B, ML_HEADS, 2 * ML_HD, ML_HD), F32)
    m0 = jnp.zeros((B, ML_HEADS, 8, LANES), F32)
    zl3, zc3, gl3, gc3 = r3(zl, L), r3(zc, Lc), r3(gl, L), r3(gc, Lc)
    h_cf, cf, mf = _mlstm_scan(mq_c, r3(mk_c, Lc), mv_c, zc3, gc3, gb, c0, m0, False)
    yc_c, cb, mb = _mlstm_scan(mq_c, r3(mk_c, Lc), mv_c, zc3, gc3, gb, c0, m0, True, h_cf, p['ml_norm_g'])
    h_lf, _, _ = _mlstm_scan(mq_l, r3(mk_l, L), mv_l, zl3, gl3, gb, cf, mf, False)
    yc_l, _, _ = _mlstm_scan(mq_l, r3(mk_l, L), mv_l, zl3, gl3, gb, cb, mb, True, h_lf, p['ml_norm_g'])
    yc_l = yc_l.reshape(B * L, -1)
    yc_c = yc_c.reshape(B * Lc, -1)
    cosw, sinw = _rope_tables(L, WA_HD)
    wq_l, wk_l, wv_l = _wa_prep(zl, cosw, sinw, B, L, True)
    wq_c, wk_c, wv_c = _wa_prep(zc, cosw[:Lc], sinw[:Lc], B, Lc, False)
    yd_l = _window_attention(wq_l, r3(wk_c, Lc), wv_c, p['wa_sink'], r3(wk_l, L), wv_l).reshape(B * L, -1)
    yd_c = (_window_attention(wq_c, r3(wk_c, Lc), wv_c, p['wa_sink']).reshape(B * Lc, -1)
            if with_ctx_out else None)
    return (ya_l, yb_l, yc_l, yd_l), (ya_c, yb_c, yc_c, yd_c)


def kernel(x, c, ctx, c_ctx, w_mod, b_mod, ln1_g, ln2_g, w_in, hy_conv_w, hy_conv_b,
           hy_pe_w1, hy_pe_b1, hy_freq, hy_pe_w2, hy_pe_b2, hy_pe_w3, hy_decay, hy_skip,
           ga_q_g, ga_k_g, ml_conv_w, ml_conv_b, ml_gate_b, ml_norm_g, wa_sink, w_up, w_out,
           mlp_w1, mlp_b1, mlp_w2, mlp_b2, final_g):
    B, L, D = x.shape
    Lc = ctx.shape[1]
    R = -(-(B + 1) // 8) * 8
    cvec = jnp.zeros((R, D), F32).at[:B].set(c).at[B].set(c_ctx)
    mod = _modulation(cvec, w_mod, b_mod)
    xl = x.reshape(B * L, D)
    xc = ctx.reshape(B * Lc, D)
    for l in range(DEPTH):
        with_ctx_out = l < DEPTH - 1
        p = dict(hy_conv_w=hy_conv_w[l], hy_conv_b=hy_conv_b[l], hy_pe_w1=hy_pe_w1[l],
                 hy_pe_b1=hy_pe_b1[l], hy_freq=hy_freq[l], hy_pe_w2=hy_pe_w2[l], hy_pe_b2=hy_pe_b2[l],
                 hy_pe_w3=hy_pe_w3[l], hy_decay=hy_decay[l], hy_skip=hy_skip[l],
                 ga_q_g=ga_q_g[l], ga_k_g=ga_k_g[l], ml_conv_w=ml_conv_w[l], ml_conv_b=ml_conv_b[l],
                 ml_gate_b=ml_gate_b[l], ml_norm_g=ml_norm_g[l], wa_sink=wa_sink[l])
        ml_rows = mod[l, :B].reshape(B, 1, 6 * D)
        mc_rows = jnp.broadcast_to(mod[l, B].reshape(1, 1, 6 * D), (B, 1, 6 * D))
        part = lambda m, n: m[:, :, n * D:(n + 1) * D]
        w_pack, w_gate = _pack_w_in(w_in[l])
        wg = w_gate.astype(BF16)
        zl, gl = _normmod_matmul(xl, ln1_g[l], part(ml_rows, 1), part(ml_rows, 0), w_pack, wg, L, Z_TN)
        zc, gc = _normmod_matmul(xc, ln1_g[l], part(mc_rows, 1), part(mc_rows, 0), w_pack, wg, Lc, Z_TN)
        ys_l, ys_c = _token_mixers(zl, zc, gl, gc, p, B, L, Lc, with_ctx_out)
        wup = w_up[l].astype(BF16)
        wout = w_out[l].astype(BF16)
        w1, w2 = mlp_w1[l].astype(BF16), mlp_w2[l].astype(BF16)
        xl = _merge(ys_l, zl, wup, wout, xl, part(ml_rows, 2), L)
        xl = _mlp(xl, ln2_g[l], part(ml_rows, 4), part(ml_rows, 3), w1, mlp_b1[l], w2, mlp_b2[l],
                  part(ml_rows, 5), final_g, L, final=(l == DEPTH - 1))
        if with_ctx_out:
            xc = _merge(ys_c, zc, wup, wout, xc, part(mc_rows, 2), Lc)
            xc = _mlp(xc, ln2_g[l], part(mc_rows, 4), part(mc_rows, 3), w1, mlp_b1[l], w2, mlp_b2[l],
                      part(mc_rows, 5), final_g, Lc, final=False)
    return xl.reshape(B, L, D)
```

```python
import functools
import math

import jax
import jax.numpy as jnp
import numpy as np
from jax import lax
from jax.experimental import pallas as pl
from jax.experimental.pallas import tpu as pltpu

F32 = jnp.float32
BF16 = jnp.bfloat16
HIGHEST = lax.Precision.HIGHEST

D_MODEL = 1024
DEPTH = 2
GRID_W = 64
HY_CH = 512
HY_ORDER = 2
HY_BANDS = 16
HY_EMB = 1 + 2 * HY_BANDS
HY_FFN = 64
GA_HEADS, GA_KV, GA_HD = 4, 2, 128
ML_HEADS, ML_HD = 4, 128
WA_HEADS, WA_KV, WA_HD = 8, 2, 64
WINDOW = 128
ROPE_BASE = 10000.0
D_FF = 4 * D_MODEL
EPS = 1e-6
NEG = -1e30
LOG2E = 1.4426950408889634

LANES = 128
V7X_VMEM_LIMIT = 48 * 1024 * 1024

Z_HY = 0
Z_WAQ = 1536
Z_GA = 2048
Z_MLQK = 3072
Z_MLV = 4096
Z_MLO = 4608
Z_GATE = 5120
Z_WAKV = 9216
Z_COLS = 9728
Z_TN = 2432

ML_CHUNK = 256
FFT_N2 = 128
GA_CB = 256
WA_PAIR = 2
WA_SUB = 4
GA_TK = 256
GA_CARRY = 4
GA_UNROLL = 32


def _cp(sem, vmem=V7X_VMEM_LIMIT):
    return pltpu.CompilerParams(dimension_semantics=sem, vmem_limit_bytes=vmem)


def _rms(x, g):
    return x * lax.rsqrt(jnp.mean(x * x, axis=-1, keepdims=True) + EPS) * g


def _mod_kernel(c_ref, w_ref, b_ref, o_ref):
    c = c_ref[...]
    s = c * jax.nn.sigmoid(c)
    o_ref[0] = jnp.dot(s, w_ref[0], preferred_element_type=F32, precision=HIGHEST) + b_ref[0]


def _modulation(cvec, w_mod, b_mod):
    R = cvec.shape[0]
    tn = 1536
    return pl.pallas_call(
        _mod_kernel,
        grid=(DEPTH, 6 * D_MODEL // tn),
        in_specs=[pl.BlockSpec((R, D_MODEL), lambda l, j: (0, 0)),
                  pl.BlockSpec((1, D_MODEL, tn), lambda l, j: (l, 0, j)),
                  pl.BlockSpec((1, 1, tn), lambda l, j: (l, 0, j))],
        out_specs=pl.BlockSpec((1, R, tn), lambda l, j: (l, 0, j)),
        out_shape=jax.ShapeDtypeStruct((DEPTH, R, 6 * D_MODEL), F32),
        compiler_params=_cp(("parallel", "parallel")),
        name="modulation",
    )(cvec, w_mod, b_mod.reshape(DEPTH, 1, 6 * D_MODEL))


def _nmm_kernel(x_ref, g_ref, sc_ref, sh_ref, w_ref, wx_ref, o_ref, ox_ref, h_ref):
    @pl.when(pl.program_id(1) == 0)
    def _():
        y = _rms(x_ref[...], g_ref[...])
        h_ref[...] = (y * (1.0 + sc_ref[0]) + sh_ref[0]).astype(BF16)
        ox_ref[...] = jnp.dot(h_ref[...], wx_ref[...], preferred_element_type=F32)

    o_ref[...] = jnp.dot(h_ref[...], w_ref[...], preferred_element_type=F32).astype(o_ref.dtype)


def _normmod_matmul(x, g, sc, sh, w, wx, seq_len, tn):
    T, D = x.shape
    N = w.shape[1]
    NX = wx.shape[1]
    tm = min(1024, seq_len)
    per = seq_len // tm
    return pl.pallas_call(
        _nmm_kernel,
        grid=(T // tm, N // tn),
        in_specs=[pl.BlockSpec((tm, D), lambda i, j: (i, 0)),
                  pl.BlockSpec((1, D), lambda i, j: (0, 0)),
                  pl.BlockSpec((1, 1, D), lambda i, j: (i // per, 0, 0)),
                  pl.BlockSpec((1, 1, D), lambda i, j: (i // per, 0, 0)),
                  pl.BlockSpec((D, tn), lambda i, j: (0, j)),
                  pl.BlockSpec((D, NX), lambda i, j: (0, 0))],
        out_specs=[pl.BlockSpec((tm, tn), lambda i, j: (i, j)), pl.BlockSpec((tm, NX), lambda i, j: (i, 0))],
        out_shape=[jax.ShapeDtypeStruct((T, N), BF16), jax.ShapeDtypeStruct((T, NX), F32)],
        scratch_shapes=[pltpu.VMEM((tm, D), BF16)],
        compiler_params=_cp(("parallel", "arbitrary")),
        name="normmod_matmul",
    )(x, g.reshape(1, D), sc, sh, w, wx)


def _conv3(u, prev_row, next_row, w_ref, b_ref, c0, c1):
    tm = u.shape[0]
    row = lax.broadcasted_iota(jnp.int32, u.shape, 0)
    up = jnp.where(row == 0, prev_row, pltpu.roll(u, 1, 0))
    dn = jnp.where(row == tm - 1, next_row, pltpu.roll(u, tm - 1, 0))
    return (w_ref[0:1, c0:c1] * up + w_ref[1:2, c0:c1] * u + w_ref[2:3, c0:c1] * dn + b_ref[0:1, c0:c1])


def _halo_rows(zp_ref, zn_ref, per, c0, c1):
    i = pl.program_id(0)
    first = (i % per) == 0
    last = (i % per) == per - 1
    hp = zp_ref.shape[0]
    prev_row = jnp.where(first, 0.0, zp_ref[hp - 1:hp, c0:c1].astype(F32))
    next_row = jnp.where(last, 0.0, zn_ref[0:1, c0:c1].astype(F32))
    return prev_row, next_row


HALO = 16


def _halo_specs(tm, width, col_block, n_rows):
    nb = n_rows // HALO
    r = tm // HALO
    return [pl.BlockSpec((tm, width), lambda i: (i, col_block)),
            pl.BlockSpec((HALO, width), lambda i: (jnp.maximum(i * r - 1, 0), col_block)),
            pl.BlockSpec((HALO, width), lambda i: (jnp.minimum((i + 1) * r, nb - 1), col_block))]


def _hy_prep_kernel(z_ref, zp_ref, zn_ref, w_ref, b_ref, v_ref, x1_ref, x2_ref, *, per):
    outs = (v_ref, x1_ref, x2_ref)
    for c in range(3):
        c0, c1 = c * HY_CH, (c + 1) * HY_CH
        prev_row, next_row = _halo_rows(zp_ref, zn_ref, per, c0, c1)
        u = z_ref[:, c0:c1].astype(F32)
        outs[c][...] = _conv3(u, prev_row, next_row, w_ref, b_ref, c0, c1).astype(outs[c].dtype)


def _hy_prep(z, conv_w, conv_b, seq_len):
    T = z.shape[0]
    tm = min(512, seq_len)
    W = 3 * HY_CH
    out = jax.ShapeDtypeStruct((T, HY_CH), BF16)
    return pl.pallas_call(
        functools.partial(_hy_prep_kernel, per=seq_len // tm),
        grid=(T // tm,),
        in_specs=_halo_specs(tm, W, Z_HY // W, T) + [
            pl.BlockSpec((3, W), lambda i: (0, 0)), pl.BlockSpec((1, W), lambda i: (0, 0))],
        out_specs=[pl.BlockSpec((tm, HY_CH), lambda i: (i, 0))] * 3,
        out_shape=[out, out, out],
        compiler_params=_cp(("parallel",)),
        name="hyena_prep",
    )(z, z, z, conv_w, conv_b.reshape(1, W))


def _ml_prep_kernel(z_ref, zp_ref, zn_ref, zv_ref, w_ref, b_ref, q_ref, k_ref, v_ref, *, per):
    W = ML_HEADS * ML_HD
    for c in range(2):
        c0, c1 = c * W, (c + 1) * W
        prev_row, next_row = _halo_rows(zp_ref, zn_ref, per, c0, c1)
        u = z_ref[:, c0:c1].astype(F32)
        y = _conv3(u, prev_row, next_row, w_ref, b_ref, c0, c1)
        y = y * jax.nn.sigmoid(y)
        if c == 0:
            for h in range(ML_HEADS):
                q_ref[0, h * ML_HD:(h + 1) * ML_HD, :] = y[:, h * ML_HD:(h + 1) * ML_HD].T.astype(q_ref.dtype)
        else:
            k_ref[...] = (y * (ML_HD ** -0.5)).astype(k_ref.dtype)
    for h in range(ML_HEADS):
        v_ref[0, h * ML_HD:(h + 1) * ML_HD, :] = zv_ref[:, h * ML_HD:(h + 1) * ML_HD].astype(F32).T.astype(v_ref.dtype)


def _ml_prep(z, conv_w, conv_b, B, seq_len):
    T = z.shape[0]
    tm = min(512, seq_len)
    per = seq_len // tm
    W = 2 * ML_HEADS * ML_HD
    Wh = W // 2
    tspec = pl.BlockSpec((1, Wh, tm), lambda i: (i // per, 0, i % per))
    tshape = jax.ShapeDtypeStruct((B, Wh, seq_len), BF16)
    return pl.pallas_call(
        functools.partial(_ml_prep_kernel, per=per),
        grid=(T // tm,),
        in_specs=_halo_specs(tm, W, Z_MLQK // W, T) + [
            pl.BlockSpec((tm, Wh), lambda i: (i, Z_MLV // Wh)),
            pl.BlockSpec((3, W), lambda i: (0, 0)), pl.BlockSpec((1, W), lambda i: (0, 0))],
        out_specs=[tspec, pl.BlockSpec((tm, Wh), lambda i: (i, 0)), tspec],
        out_shape=[tshape, jax.ShapeDtypeStruct((T, Wh), BF16), tshape],
        compiler_params=_cp(("parallel",)),
        name="mlstm_prep",
    )(z, z, z, z, conv_w, conv_b.reshape(1, W))


def _rope_tables(L, hd):
    quarter = hd // 4
    inv = ROPE_BASE ** (-jnp.arange(quarter, dtype=F32) / quarter)
    t = jnp.arange(L)
    row = (t // GRID_W).astype(F32)
    col = (t % GRID_W).astype(F32)
    lane = jnp.arange(LANES)
    within = lane % hd
    is_col = (within // (hd // 2)) == 1
    second = ((within % (hd // 2)) // quarter) == 1
    j = within % quarter
    pos = jnp.where(is_col[None, :], col[:, None], row[:, None])
    ang = pos * inv[j][None, :]
    return jnp.cos(ang), jnp.where(second[None, :], jnp.sin(ang), -jnp.sin(ang))


def _rope(x, cos, sin, quarter):
    lane = lax.broadcasted_iota(jnp.int32, x.shape, 1)
    first = ((lane % (2 * quarter)) // quarter) == 0
    partner = jnp.where(first, pltpu.roll(x, LANES - quarter, 1), pltpu.roll(x, quarter, 1))
    return x * cos + partner * sin


def _ga_prep_kernel(z_ref, cos_ref, sin_ref, qg_ref, kg_ref, q_ref, k_ref, v_ref, *, rope):
    nq, nk = GA_HEADS, GA_KV
    for h in range(nq + nk):
        x = z_ref[:, h * GA_HD:(h + 1) * GA_HD].astype(F32)
        g = qg_ref[...] if h < nq else kg_ref[...]
        y = _rms(x, g)
        if rope:
            y = _rope(y, cos_ref[...], sin_ref[...], GA_HD // 4)
        if h < nq:
            q_ref[0, h * GA_HD:(h + 1) * GA_HD, :] = (y * (GA_HD ** -0.5 * LOG2E)).T.astype(q_ref.dtype)
        else:
            k_ref[:, (h - nq) * GA_HD:(h - nq + 1) * GA_HD] = y.astype(k_ref.dtype)
    for h in range(nk):
        v = z_ref[:, (nq + nk + h) * GA_HD:(nq + nk + h + 1) * GA_HD].astype(F32)
        v_ref[0, h * GA_HD:(h + 1) * GA_HD, :] = v.T.astype(v_ref.dtype)


def _ga_prep(z, cos, sin, qg, kg, B, seq_len, rope):
    T = z.shape[0]
    tm = min(512, seq_len)
    per = seq_len // tm
    W = (GA_HEADS + 2 * GA_KV) * GA_HD
    return pl.pallas_call(
        functools.partial(_ga_prep_kernel, rope=rope),
        grid=(T // tm,),
        in_specs=[pl.BlockSpec((tm, W), lambda i: (i, Z_GA // W)),
                  pl.BlockSpec((tm, LANES), lambda i: (i % per, 0)),
                  pl.BlockSpec((tm, LANES), lambda i: (i % per, 0)),
                  pl.BlockSpec((1, GA_HD), lambda i: (0, 0)),
                  pl.BlockSpec((1, GA_HD), lambda i: (0, 0))],
        out_specs=[pl.BlockSpec((1, GA_HEADS * GA_HD, tm), lambda i: (i // per, 0, i % per)),
                   pl.BlockSpec((tm, GA_KV * GA_HD), lambda i: (i, 0)),
                   pl.BlockSpec((1, GA_KV * GA_HD, tm), lambda i: (i // per, 0, i % per))],
        out_shape=[jax.ShapeDtypeStruct((B, GA_HEADS * GA_HD, seq_len), BF16),
                   jax.ShapeDtypeStruct((T, GA_KV * GA_HD), BF16),
                   jax.ShapeDtypeStruct((B, GA_KV * GA_HD, seq_len), BF16)],
        compiler_params=_cp(("parallel",)),
        name="global_attn_prep",
    )(z, cos, sin, qg.reshape(1, GA_HD), kg.reshape(1, GA_HD))


def _wa_prep_kernel(zq_ref, zkv_ref, cos_ref, sin_ref, q_ref, k_ref, v_ref, *, rope):
    quarter = WA_HD // 4
    for j in range(WA_HEADS * WA_HD // LANES):
        x = zq_ref[:, j * LANES:(j + 1) * LANES].astype(F32)
        if rope:
            x = _rope(x, cos_ref[...], sin_ref[...], quarter)
        q_ref[0, j * LANES:(j + 1) * LANES, :] = (x * (WA_HD ** -0.5 * LOG2E)).T.astype(q_ref.dtype)
    k = zkv_ref[:, 0:LANES].astype(F32)
    if rope:
        k = _rope(k, cos_ref[...], sin_ref[...], quarter)
    k_ref[...] = k.astype(k_ref.dtype)
    v_ref[0] = zkv_ref[:, LANES:].astype(F32).T.astype(v_ref.dtype)


def _wa_prep(z, cos, sin, B, seq_len, rope):
    T = z.shape[0]
    tm = min(512, seq_len)
    per = seq_len // tm
    WQ = WA_HEADS * WA_HD
    return pl.pallas_call(
        functools.partial(_wa_prep_kernel, rope=rope),
        grid=(T // tm,),
        in_specs=[pl.BlockSpec((tm, WQ), lambda i: (i, Z_WAQ // WQ)),
                  pl.BlockSpec((tm, 2 * LANES), lambda i: (i, Z_WAKV // (2 * LANES))),
                  pl.BlockSpec((tm, LANES), lambda i: (i % per, 0)),
                  pl.BlockSpec((tm, LANES), lambda i: (i % per, 0))],
        out_specs=[pl.BlockSpec((1, WQ, tm), lambda i: (i // per, 0, i % per)),
                   pl.BlockSpec((tm, LANES), lambda i: (i, 0)),
                   pl.BlockSpec((1, LANES, tm), lambda i: (i // per, 0, i % per))],
        out_shape=[jax.ShapeDtypeStruct((B, WQ, seq_len), BF16),
                   jax.ShapeDtypeStruct((T, LANES), BF16),
                   jax.ShapeDtypeStruct((B, LANES, seq_len), BF16)],
        compiler_params=_cp(("parallel",)),
        name="window_attn_prep",
    )(z, z, cos, sin)


def _ga_kernel(q_ref, k_ref, vt_ref, o_ref, acc_ref, m_ref, *, nchunks, tk, tq):
    acc_ref[...] = jnp.zeros_like(acc_ref)
    m_ref[...] = jnp.full_like(m_ref, NEG)
    qt = jnp.concatenate([q_ref[0, 0:GA_HD, :], q_ref[0, GA_HD:, :]], axis=1)

    W = GA_CB
    nblk = 2 * tq // W

    def scores(k, blocks=None):
        blocks = range(nblk) if blocks is None else blocks
        return tuple(jnp.dot(k, qt[:, i * W:(i + 1) * W], preferred_element_type=F32) for i in blocks)

    def softmax_pv(s_blocks, vt):
        cols = [slice(i * W, (i + 1) * W) for i in range(nblk)]
        m_old = [m_ref[:, cs] for cs in cols]
        m_new = [jnp.maximum(mo, jnp.max(s, axis=0, keepdims=True)) for mo, s in zip(m_old, s_blocks)]
        alpha = [jnp.exp2(mo - mn) for mo, mn in zip(m_old, m_new)]
        vaug = jnp.concatenate([vt, jnp.ones((16, vt.shape[1]), BF16)], axis=0)
        pv = [jnp.dot(vaug, jnp.exp2((s - mn).astype(BF16)), preferred_element_type=F32)
              for s, mn in zip(s_blocks, m_new)]
        for i, cs in enumerate(cols):
            acc_ref[:, cs] = alpha[i] * acc_ref[:, cs] + pv[i]
            m_ref[:, cs] = m_new[i]

    ahead, late = range(min(GA_CARRY, nblk)), range(min(GA_CARRY, nblk), nblk)

    def body(j, s):
        off = pl.multiple_of(j * tk, tk)
        s_late = scores(k_ref[0, pl.ds(off, tk), :], late)
        s_next = scores(k_ref[0, pl.ds(pl.multiple_of((j + 1) * tk, tk), tk), :], ahead)
        softmax_pv(s + s_late, vt_ref[0, :, pl.ds(off, tk)])
        return s_next
    s = scores(k_ref[0, 0:tk, :], ahead)
    if nchunks > 1:
        s = lax.fori_loop(0, nchunks - 1, body, s, unroll=GA_UNROLL if (nchunks - 1) % GA_UNROLL == 0 else 1)
    last = slice((nchunks - 1) * tk, nchunks * tk)
    softmax_pv(s + scores(k_ref[0, last, :], late), vt_ref[0, :, last])
    o = acc_ref[0:GA_HD, :] / acc_ref[GA_HD:GA_HD + 1, :]
    for h in range(2):
        o_ref[0, :, h * GA_HD:(h + 1) * GA_HD] = o[:, h * tq:(h + 1) * tq].T.astype(o_ref.dtype)


def _global_attention(qt, k, vt):
    B, _, Lq = qt.shape
    Lk = k.shape[1]
    tq = min(512, Lq)
    tk = min(GA_TK, Lk)
    W = 2 * GA_HD
    return pl.pallas_call(
        functools.partial(_ga_kernel, nchunks=Lk // tk, tk=tk, tq=tq),
        grid=(B, GA_KV, Lq // tq),
        in_specs=[pl.BlockSpec((1, W, tq), lambda b, g, i: (b, g, i)),
                  pl.BlockSpec((1, Lk, GA_HD), lambda b, g, i: (b, 0, g)),
                  pl.BlockSpec((1, GA_HD, Lk), lambda b, g, i: (b, g, 0))],
        out_specs=pl.BlockSpec((1, tq, W), lambda b, g, i: (b, i, g)),
        out_shape=jax.ShapeDtypeStruct((B, Lq, GA_HEADS * GA_HD), BF16),
        scratch_shapes=[pltpu.VMEM((GA_HD + 16, 2 * tq), F32), pltpu.VMEM((1, 2 * tq), F32)],
        compiler_params=_cp(("parallel", "parallel", "parallel")),
        name="global_attention",
    )(qt, k, vt)


def _wa_kernel(*refs, band, nq, nsub):
    if band:
        q_ref = refs[0]
        kb = refs[1:nsub + 3]
        vb = refs[nsub + 3:2 * nsub + 5]
        kc_ref, vc_ref, sink_ref, o_ref = refs[2 * nsub + 5:]
    else:
        q_ref, kc_ref, vc_ref, sink_ref, o_ref = refs
    tq = WINDOW
    Lc = kc_ref.shape[1]
    step = pl.program_id(1)
    G = WA_HEADS // WA_KV
    cols = WA_PAIR * tq
    npr = WA_HEADS // WA_PAIR
    grp = [(pr * WA_PAIR) // G for pr in range(npr)]
    zeros = jnp.zeros((WA_HD, cols), BF16)
    if band:
        nb = 3 * tq
        c = lax.broadcasted_iota(jnp.int32, (nb + Lc, cols), 0)
        r = lax.broadcasted_iota(jnp.int32, (nb + Lc, cols), 1) % tq
        in_band = (c >= r) & (c <= r + 2 * WINDOW)
    ones_rows = jnp.ones((16, (3 * tq if band else 0) + Lc), BF16)
    chains = [(u, pr) for u in range(nsub) for pr in range(npr)]
    keys, vals, valid = [], [], []
    for u in range(nsub):
        if band:
            keys.append(jnp.concatenate([kb[u][0], kb[u + 1][0], kb[u + 2][0], kc_ref[0]], axis=0))
            vals.append(jnp.concatenate([vb[u][0], vb[u + 1][0], vb[u + 2][0], vc_ref[0]], axis=1))
            qi = step * nsub + u
            lo = jnp.where(qi == 0, tq, 0)
            hi = jnp.where(qi == nq - 1, 2 * tq, nb)
            valid.append((in_band & (c >= lo) & (c < hi)) | (c >= nb))
        else:
            keys.append(kc_ref[0])
            vals.append(vc_ref[0])
    s = {}
    for u, pr in chains:
        qg = jnp.concatenate([q_ref[0, (WA_PAIR * pr + h) * WA_HD:(WA_PAIR * pr + h + 1) * WA_HD, u * tq:(u + 1) * tq]
                              for h in range(WA_PAIR)], axis=1)
        qpad = jnp.concatenate([qg, zeros] if grp[pr] == 0 else [zeros, qg], axis=0)
        sp = jnp.dot(keys[u], qpad, preferred_element_type=F32)
        s[u, pr] = jnp.where(valid[u], sp, NEG) if band else sp
    sink = [sink_ref[pr] for pr in range(npr)]
    m = {ch: jnp.maximum(jnp.max(s[ch], axis=0, keepdims=True), sink[ch[1]]) for ch in chains}
    p = {ch: jnp.exp2(s[ch] - m[ch]).astype(BF16) for ch in chains}
    R = {}
    for u, pr in chains:
        vaug = jnp.concatenate([vals[u][grp[pr] * WA_HD:(grp[pr] + 1) * WA_HD, :], ones_rows], axis=0)
        R[u, pr] = jnp.dot(vaug, p[u, pr], preferred_element_type=F32)
    for u, pr in chains:
        o = R[u, pr][:WA_HD] / (R[u, pr][WA_HD:WA_HD + 1] + jnp.exp2(sink[pr] - m[u, pr]))
        ot = jnp.concatenate([o[:, h * tq:(h + 1) * tq] for h in range(WA_PAIR)], axis=0)
        wo = WA_PAIR * WA_HD
        o_ref[0, u * tq:(u + 1) * tq, pr * wo:(pr + 1) * wo] = ot.T.astype(o_ref.dtype)


def _window_attention(qt, kc, vtc, sink, kl=None, vtl=None):
    B, WQ, Lq = qt.shape
    Lc = kc.shape[1]
    tq = WINDOW
    nq = Lq // tq
    nsub = math.gcd(WA_SUB, nq)
    band = kl is not None
    npair = WA_HEADS // WA_PAIR
    sink_row = jnp.repeat(sink.astype(F32).reshape(npair, WA_PAIR) * LOG2E, tq, axis=1).reshape(npair, 1, WA_PAIR * tq)
    qspec = pl.BlockSpec((1, WQ, nsub * tq), lambda b, i: (b, 0, i))
    kcspec = pl.BlockSpec((1, Lc, LANES), lambda b, i: (b, 0, 0))
    vcspec = pl.BlockSpec((1, LANES, Lc), lambda b, i: (b, 0, 0))
    sspec = pl.BlockSpec((npair, 1, WA_PAIR * tq), lambda b, i: (0, 0, 0))
    if band:
        blk = lambda off: (lambda i: jnp.clip(i * nsub + off, 0, nq - 1))
        kspec = lambda f: pl.BlockSpec((1, tq, LANES), lambda b, i: (b, f(i), 0))
        vspec = lambda f: pl.BlockSpec((1, LANES, tq), lambda b, i: (b, 0, f(i)))
        offs = range(-1, nsub + 1)
        in_specs = ([qspec] + [kspec(blk(o)) for o in offs] + [vspec(blk(o)) for o in offs]
                    + [kcspec, vcspec, sspec])
        args = (qt,) + (kl,) * (nsub + 2) + (vtl,) * (nsub + 2) + (kc, vtc, sink_row)
    else:
        in_specs, args = [qspec, kcspec, vcspec, sspec], (qt, kc, vtc, sink_row)
    return pl.pallas_call(
        functools.partial(_wa_kernel, band=band, nq=nq, nsub=nsub),
        grid=(B, nq // nsub),
        in_specs=in_specs,
        out_specs=pl.BlockSpec((1, nsub * tq, WQ), lambda b, i: (b, i, 0)),
        out_shape=jax.ShapeDtypeStruct((B, Lq, WQ), BF16),
        compiler_params=_cp(("parallel", "parallel")),
        name="window_attention",
    )(*args)


def _mlstm_kernel(*refs, reverse, nc, bb):
    if reverse:
        (qt_ref, k_ref, vt_ref, g_ref, gb_ref, c0_ref, m0_ref, hft_ref, o_ref, gn_ref,
         y_ref, cf_ref, mf_ref, c_scr, m_scr) = refs
    else:
        (qt_ref, k_ref, vt_ref, g_ref, gb_ref, c0_ref, m0_ref,
         y_ref, cf_ref, mf_ref, c_scr, m_scr) = refs
    T = ML_CHUNK
    d = 1 if reverse else 0
    step = pl.program_id(1)

    @pl.when(step == 0)
    def _():
        c_scr[...] = c0_ref[...]
        m_scr[...] = m0_ref[...]

    si = lax.broadcasted_iota(jnp.int32, (T, T), 0)
    ti = lax.broadcasted_iota(jnp.int32, (T, T), 1)
    tri = ((ti >= si) if reverse else (ti <= si)).astype(F32)
    mask_t = (si >= ti) if reverse else (si <= ti)
    ones_rows = jnp.ones((ML_HD, T), BF16)
    e_last = 0 if reverse else T - 1
    chains = [(bi, h) for bi in range(bb) for h in range(ML_HEADS)]
    gates = []
    for bi in range(bb):
        G = g_ref[bi] + gb_ref[...]
        LF = jax.nn.log_sigmoid(G)
        Bc = jnp.dot(tri, LF, preferred_element_type=F32, precision=HIGHEST)
        gates.append((G.T, Bc.T, Bc - pltpu.roll(G, 4, 1)))
    st, ph = {}, {}
    for bi, h in chains:
        hs = slice(h * ML_HD, (h + 1) * ML_HD)
        st[bi, h] = jnp.dot(k_ref[bi, :, hs], qt_ref[bi, hs, :], preferred_element_type=F32)
    for bi, h in chains:
        GT, BT, Dc = gates[bi]
        fl, il = d * 8 + 4 + h, d * 8 + h
        b_row, i_row = BT[fl:fl + 1, :], GT[il:il + 1, :]
        log_d = jnp.where(mask_t, b_row - Dc[:, fl:fl + 1], NEG)
        m_prev = m_scr[bi, h, 0:1, 0:1]
        m_inter = b_row + m_prev
        m_t = jnp.maximum(m_inter, jnp.max(log_d, axis=0, keepdims=True))
        wqk = (st[bi, h] * jnp.exp(log_d - m_t)).astype(BF16)
        b_end = BT[fl:fl + 1, e_last:e_last + 1]
        log_w = b_end - b_row + i_row
        m_next = jnp.maximum(b_end + m_prev, jnp.max(log_w, axis=1, keepdims=True))
        ph[bi, h] = (wqk, jnp.exp(m_inter - m_t), jnp.exp(-m_t), jnp.exp(log_w - m_next),
                     jnp.exp(b_end + m_prev - m_next), m_next)
    for bi, h in chains:
        hs = slice(h * ML_HD, (h + 1) * ML_HD)
        wqk, cs, em, w_row, decay, m_next = ph[bi, h]
        kh, qt = k_ref[bi, :, hs], qt_ref[bi, hs, :]
        vaug = jnp.concatenate([vt_ref[bi, hs, :], ones_rows], axis=0)
        R = (jnp.dot(vaug, wqk, preferred_element_type=F32)
             + cs * jnp.dot(c_scr[bi, h].astype(BF16), qt, preferred_element_type=F32))
        hh = R[:ML_HD] / jnp.maximum(jnp.abs(R[ML_HD:]), em)
        if reverse:
            hsum = hft_ref[bi, hs, :] + hh
            hn = hsum * lax.rsqrt(jnp.mean(hsum * hsum, axis=0, keepdims=True) + EPS) * gn_ref[hs, :]
            y_ref[bi, :, hs] = (jax.nn.sigmoid(o_ref[bi, :, hs].astype(F32)) * hn.T).astype(y_ref.dtype)
        else:
            y_ref[bi, hs, :] = hh
        wv = (vaug.astype(F32) * w_row).astype(BF16)
        c_scr[bi, h] = decay * c_scr[bi, h] + jnp.dot(wv, kh, preferred_element_type=F32)
        m_scr[bi, h] = jnp.broadcast_to(m_next, (8, LANES))

    @pl.when(step == nc - 1)
    def _():
        cf_ref[...] = c_scr[...]
        mf_ref[...] = m_scr[...]


def _mlstm_scan(qt, k, vt, z, gates, gate_b, c0, m0, reverse, hft=None, norm_g=None):
    B, L, W = k.shape
    T = ML_CHUNK
    nc = L // T
    bb = 4 if B % 4 == 0 else 2
    cj =(lambda j: nc - 1 - j) if reverse else (lambda j: j)
    tok = pl.BlockSpec((bb, T, W), lambda b, j: (b, cj(j), 0))
    ttok = pl.BlockSpec((bb, W, T), lambda b, j: (b, 0, cj(j)))
    cspec = pl.BlockSpec((bb, ML_HEADS, 2 * ML_HD, ML_HD), lambda b, j: (b, 0, 0, 0))
    mspec = pl.BlockSpec((bb, ML_HEADS, 8, LANES), lambda b, j: (b, 0, 0, 0))
    in_specs = [ttok, tok, ttok, pl.BlockSpec((bb, T, LANES), lambda b, j: (b, cj(j), 0)),
                pl.BlockSpec((1, LANES), lambda b, j: (0, 0)), cspec, mspec]
    args = [qt, k, vt, gates, gate_b, c0, m0]
    if reverse:
        in_specs += [ttok, pl.BlockSpec((bb, T, W), lambda b, j: (b, cj(j), Z_MLO // W)),
                     pl.BlockSpec((W, T), lambda b, j: (0, 0))]
        args += [hft, z, jnp.broadcast_to(norm_g.astype(F32).reshape(W, 1), (W, T))]
    return pl.pallas_call(
        functools.partial(_mlstm_kernel, reverse=reverse, nc=nc, bb=bb),
        grid=(B // bb, nc),
        in_specs=in_specs,
        out_specs=[tok if reverse else ttok, cspec, mspec],
        out_shape=[jax.ShapeDtypeStruct((B, L, W), BF16) if reverse else jax.ShapeDtypeStruct((B, W, L), F32),
                   jax.ShapeDtypeStruct(c0.shape, F32), jax.ShapeDtypeStruct(m0.shape, F32)],
        scratch_shapes=[pltpu.VMEM((bb, ML_HEADS, 2 * ML_HD, ML_HD), F32), pltpu.VMEM((bb, ML_HEADS, 8, LANES), F32)],
        compiler_params=_cp(("parallel", "arbitrary")),
        name="mlstm_reverse" if reverse else "mlstm_forward",
    )(*args)


def _fft_dims(Lp):
    n1 = 2 * Lp // FFT_N2
    nt1 = Lp // FFT_N2
    nk1 = -(-(n1 // 2 + 1) // 8) * 8
    return n1, nt1, nk1


def _fft_tables(Lp):
    n1, nt1, nk1 = _fft_dims(Lp)
    N = 2 * Lp
    k1 = jnp.arange(nk1)
    t1 = jnp.arange(nt1)
    ang_a = (2.0 * math.pi / n1) * ((k1[:, None] * t1[None, :]) % n1).astype(F32)
    fa = jnp.stack([jnp.cos(ang_a), -jnp.sin(ang_a)], axis=1).reshape(2 * nk1, nt1)
    wgt = jnp.where((k1 == 0) | (k1 == n1 // 2), 1.0, 2.0) * (k1 <= n1 // 2) / N
    fai = jnp.stack([jnp.cos(ang_a) * wgt[:, None], -jnp.sin(ang_a) * wgt[:, None]], axis=1)
    fai = fai.reshape(2 * nk1, nt1).T
    k2 = jnp.arange(FFT_N2)
    t2 = jnp.arange(FFT_N2)
    idx = (t2[None, None, :] * k1[:, None, None] + n1 * t2[None, None, :] * k2[None, :, None]) % N
    phi = (2.0 * math.pi / N) * idx.astype(F32)
    gr, gi = jnp.cos(phi), -jnp.sin(phi)
    gfwd = jnp.concatenate([jnp.concatenate([gr, -gi], axis=2), jnp.concatenate([gi, gr], axis=2)], axis=1)
    grt, git = jnp.swapaxes(gr, 1, 2), jnp.swapaxes(gi, 1, 2)
    ginv = jnp.concatenate([jnp.concatenate([grt, git], axis=2), jnp.concatenate([-git, grt], axis=2)], axis=1)
    eye = jnp.eye(FFT_TK, dtype=F32)
    return (jnp.kron(fa, eye).astype(BF16), jnp.kron(fai, eye).astype(BF16), gfwd.astype(BF16), ginv.astype(BF16))


FFT_TB = 32
FFT_TK = 8
FFT_CB = 512


def _fa_kernel(fa_ref, x_ref, o_ref):
    nt1, tb, cb = x_ref.shape[1:]
    rows = o_ref.shape[1]
    x = x_ref[0].astype(F32).reshape(nt1, tb // FFT_TK, FFT_TK, cb)
    parts = []
    for j in range(tb // FFT_TK):
        xj = x[:, j].reshape(nt1 * FFT_TK, cb).astype(BF16)
        r = jnp.dot(fa_ref[...], xj, preferred_element_type=F32)
        parts.append(r.reshape(rows, 1, FFT_TK, cb))
    o_ref[0] = jnp.concatenate(parts, axis=1).reshape(rows, tb, cb).astype(o_ref.dtype)


def _fft_stage_a(fa, y, out_dtype):
    B, Lp, C = y.shape
    nt1 = Lp // FFT_N2
    rows = fa.shape[0] // FFT_TK
    return pl.pallas_call(
        _fa_kernel,
        grid=(B, FFT_N2 // FFT_TB, C // FFT_CB),
        in_specs=[pl.BlockSpec(fa.shape, lambda b, j, c: (0, 0)),
                  pl.BlockSpec((1, nt1, FFT_TB, FFT_CB), lambda b, j, c: (b, 0, j, c))],
        out_specs=pl.BlockSpec((1, rows, FFT_TB, FFT_CB), lambda b, j, c: (b, 0, j, c)),
        out_shape=jax.ShapeDtypeStruct((B, rows, FFT_N2, C), out_dtype),
        compiler_params=_cp(("parallel", "parallel", "parallel")),
        name="fft_stage_a",
    )(fa, y.reshape(B, nt1, FFT_N2, C))


def _fc_filter_kernel(g_ref, s_ref, ss_ref, h_ref, *, kb):
    C = HY_CH
    for i in range(kb):
        for n in range(HY_ORDER):
            cf, cb = (2 * n) * C, (2 * n + 1) * C
            scale = lax.rsqrt(ss_ref[0:1, cf:cf + C] + ss_ref[0:1, cb:cb + C] + EPS)
            sf = s_ref[0, 2 * i:2 * i + 2, :, cf:cf + C].reshape(2 * FFT_N2, C).astype(BF16)
            sb = s_ref[0, 2 * i:2 * i + 2, :, cb:cb + C].reshape(2 * FFT_N2, C).astype(BF16)
            xf = jnp.dot(g_ref[i], sf, preferred_element_type=F32)
            xb = jnp.dot(g_ref[i], sb, preferred_element_type=F32)
            h_ref[n, 2 * i] = (xf[:FFT_N2] + xb[:FFT_N2]) * scale
            h_ref[n, 2 * i + 1] = (xf[FFT_N2:] - xb[FFT_N2:]) * scale


def _fft_filter_spectrum(gfwd, s_filt, sumsq):
    rows = s_filt.shape[1]
    C4 = s_filt.shape[-1]
    kb = 2
    return pl.pallas_call(
        functools.partial(_fc_filter_kernel, kb=kb),
        grid=(rows // (2 * kb),),
        in_specs=[pl.BlockSpec((kb, 2 * FFT_N2, 2 * FFT_N2), lambda i: (i, 0, 0)),
                  pl.BlockSpec((1, 2 * kb, FFT_N2, C4), lambda i: (0, i, 0, 0)),
                  pl.BlockSpec((1, C4), lambda i: (0, 0))],
        out_specs=pl.BlockSpec((HY_ORDER, 2 * kb, FFT_N2, HY_CH), lambda i: (0, i, 0, 0)),
        out_shape=jax.ShapeDtypeStruct((HY_ORDER, rows, FFT_N2, HY_CH), F32),
        compiler_params=_cp(("parallel",)),
        name="fft_filter_spectrum",
    )(gfwd, s_filt, sumsq)


def _fc_kernel(g_ref, gi_ref, h_ref, s_ref, o_ref, *, kb):
    C = s_ref.shape[-1]
    x = [jnp.dot(g_ref[i], s_ref[0, 2 * i:2 * i + 2].reshape(2 * FFT_N2, C), preferred_element_type=F32)
         for i in range(kb)]
    z = []
    for i in range(kb):
        xr, xi = x[i][:FFT_N2], x[i][FFT_N2:]
        hr, hi = h_ref[0, 2 * i], h_ref[0, 2 * i + 1]
        z.append(jnp.concatenate([xr * hr - xi * hi, xr * hi + xi * hr], axis=0).astype(BF16))
    bm = [jnp.dot(gi_ref[i], z[i], preferred_element_type=F32) for i in range(kb)]
    for i in range(kb):
        o_ref[0, 2 * i:2 * i + 2] = bm[i].reshape(2, FFT_N2, C).astype(o_ref.dtype)


def _fft_stage_c(gfwd, ginv, hspec, order, s):
    B, rows, _, C = s.shape
    kb = 8 if rows % 16 == 0 else 4
    sspec =pl.BlockSpec((1, 2 * kb, FFT_N2, C), lambda i, b: (b, i, 0, 0))
    gspec = pl.BlockSpec((kb, 2 * FFT_N2, 2 * FFT_N2), lambda i, b: (i, 0, 0))
    return pl.pallas_call(
        functools.partial(_fc_kernel, kb=kb),
        grid=(rows // (2 * kb), B),
        in_specs=[gspec, gspec,
                  pl.BlockSpec((1, 2 * kb, FFT_N2, C), lambda i, b: (order, i, 0, 0)), sspec],
        out_specs=sspec,
        out_shape=jax.ShapeDtypeStruct(s.shape, BF16),
        compiler_params=_cp(("parallel", "arbitrary")),
        name="fft_stage_c",
    )(gfwd, ginv, hspec, s)


def _fai_kernel(fai_ref, b_ref, y_ref, gate_ref, skip_ref, o_ref):
    nt1, tb, cb = y_ref.shape[1:]
    rows = b_ref.shape[1]
    bm = b_ref[0].astype(F32).reshape(rows, tb // FFT_TK, FFT_TK, cb)
    parts = []
    for j in range(tb // FFT_TK):
        bj = bm[:, j].reshape(rows * FFT_TK, cb).astype(BF16)
        yf = jnp.dot(fai_ref[...], bj, preferred_element_type=F32)
        parts.append(yf.reshape(nt1, 1, FFT_TK, cb))
    yf = jnp.concatenate(parts, axis=1).reshape(nt1, tb, cb)
    out = gate_ref[0].astype(F32) * (yf + skip_ref[...].reshape(1, 1, cb) * y_ref[0].astype(F32))
    o_ref[0] = out.astype(o_ref.dtype)


def _fft_stage_a_inv(fai, bm, y, gate, skip):
    B, Lp, C = y.shape
    nt1 = Lp // FFT_N2
    rows = fai.shape[1] // FFT_TK
    tok = pl.BlockSpec((1, nt1, FFT_TB, FFT_CB), lambda b, j, c: (b, 0, j, c))
    out = pl.pallas_call(
        _fai_kernel,
        grid=(B, FFT_N2 // FFT_TB, C // FFT_CB),
        in_specs=[pl.BlockSpec(fai.shape, lambda b, j, c: (0, 0)),
                  pl.BlockSpec((1, rows, FFT_TB, FFT_CB), lambda b, j, c: (b, 0, j, c)),
                  tok, tok, pl.BlockSpec((1, FFT_CB), lambda b, j, c: (0, c))],
        out_specs=tok,
        out_shape=jax.ShapeDtypeStruct((B, nt1, FFT_N2, C), BF16),
        compiler_params=_cp(("parallel", "parallel", "parallel")),
        name="fft_stage_a_inv",
    )(fai, bm, y.reshape(B, nt1, FFT_N2, C), gate.reshape(B, nt1, FFT_N2, C), skip.astype(F32).reshape(1, C))
    return out.reshape(B, Lp, C)


def _hgen_kernel(z_ref, w1_ref, b1_ref, fr_ref, w2_ref, b2_ref, w3_ref, dec_ref, h_ref, ss_ref):
    i = pl.program_id(0)
    z = z_ref[...]
    tm = z.shape[0]
    h = jnp.sin(fr_ref[0:1, :] * (jnp.dot(z, w1_ref[...], preferred_element_type=F32, precision=HIGHEST)
                                 + b1_ref[...]))
    h = jnp.sin(fr_ref[1:2, :] * (jnp.dot(h, w2_ref[...], preferred_element_type=F32, precision=HIGHEST)
                                 + b2_ref[...]))
    h = jnp.dot(h, w3_ref[...], preferred_element_type=F32, precision=HIGHEST)
    h = h * jnp.exp(-z[:, 0:1] * jnp.abs(dec_ref[...]))
    row = lax.broadcasted_iota(jnp.int32, h.shape, 0) + i * tm
    col = lax.broadcasted_iota(jnp.int32, h.shape, 1)
    h = jnp.where((row == 0) & ((col // HY_CH) % 2 == 1), 0.0, h)
    h_ref[...] = h

    @pl.when(i == 0)
    def _():
        ss_ref[...] = jnp.zeros_like(ss_ref)

    ss_ref[...] += jnp.sum(h * h, axis=0, keepdims=True)


def _hyena_filters(L, p):
    t = jnp.arange(L, dtype=F32)
    tn = t / (L - 1)
    w = 2.0 * math.pi * t / L
    bands = jnp.linspace(1e-4, HY_BANDS - 1, HY_BANDS, dtype=F32)
    ang = w[:, None] * bands[None, :]
    z = jnp.concatenate([tn[:, None], jnp.cos(ang), -jnp.sin(ang)], axis=-1)
    z = jnp.pad(z, ((0, 0), (0, LANES - HY_EMB)))
    w1 = jnp.pad(p['hy_pe_w1'].astype(F32), ((0, LANES - HY_EMB), (0, 0)))
    nf = HY_ORDER * 2 * HY_CH
    tm = min(512, L)
    const = lambda shape: pl.BlockSpec(shape, lambda i: (0,) * len(shape))
    return pl.pallas_call(
        _hgen_kernel,
        grid=(L // tm,),
        in_specs=[pl.BlockSpec((tm, LANES), lambda i: (i, 0)), const((LANES, HY_FFN)), const((1, HY_FFN)),
                  const((2, HY_FFN)), const((HY_FFN, HY_FFN)), const((1, HY_FFN)), const((HY_FFN, nf)),
                  const((1, nf))],
        out_specs=[pl.BlockSpec((tm, nf), lambda i: (i, 0)), const((1, nf))],
        out_shape=[jax.ShapeDtypeStruct((L, nf), F32), jax.ShapeDtypeStruct((1, nf), F32)],
        compiler_params=_cp(("arbitrary",)),
        name="hyena_filter_gen",
    )(z, w1, p['hy_pe_b1'].reshape(1, HY_FFN), p['hy_freq'], p['hy_pe_w2'], p['hy_pe_b2'].reshape(1, HY_FFN),
      p['hy_pe_w3'], p['hy_decay'].reshape(1, nf))


def _hyena_branch(z, p, B, L):
    v, x1, x2 = _hy_prep(z, p['hy_conv_w'], p['hy_conv_b'], L)
    Lp = max(L, 2048)
    fa, fai, gfwd, ginv = _fft_tables(Lp)
    hfilt, sumsq = _hyena_filters(L, p)
    pad3 = lambda a: a.reshape(B, L, HY_CH) if Lp == L else jnp.pad(a.reshape(B, L, HY_CH), ((0, 0), (0, Lp - L), (0, 0)))
    hf = hfilt if Lp == L else jnp.pad(hfilt, ((0, Lp - L), (0, 0)))
    s_filt = _fft_stage_a(fa, hf[None], F32)
    hspec = _fft_filter_spectrum(gfwd, s_filt, sumsq)
    y = pad3(v)
    for n, gate in enumerate((pad3(x1), pad3(x2))):
        s = _fft_stage_a(fa, y, BF16)
        bm = _fft_stage_c(gfwd, ginv, hspec, n, s)
        y = _fft_stage_a_inv(fai, bm, y, gate, p['hy_skip'][n])
    return y[:, :L].reshape(B * L, HY_CH)


def _merge_kernel(ya_ref, yb_ref, yc_ref, yd_ref, g0_ref, g1_ref, g2_ref, g3_ref, wup_ref, wout_ref,
                  x_ref, gate_ref, o_ref):
    acc = None
    for n, (y_ref, g_ref) in enumerate(((ya_ref, g0_ref), (yb_ref, g1_ref), (yc_ref, g2_ref), (yd_ref, g3_ref))):
        t = jax.nn.sigmoid(g_ref[...].astype(F32)) * jnp.dot(y_ref[...], wup_ref[n], preferred_element_type=F32)
        acc = t if acc is None else acc + t
    yl = jnp.dot(acc.astype(BF16), wout_ref[...], preferred_element_type=F32)
    o_ref[...] = x_ref[...] + gate_ref[0] * yl


def _merge(ys, z, w_up, w_out, x, gate, seq_len):
    T, D = x.shape
    tm = min(512, seq_len)
    per = seq_len // tm
    Wy = ys[0].shape[1]
    yspec = pl.BlockSpec((tm, Wy), lambda i: (i, 0))
    gspecs = [pl.BlockSpec((tm, D), functools.partial(lambda i, n: (i, Z_GATE // D + n), n=n)) for n in range(4)]
    return pl.pallas_call(
        _merge_kernel,
        grid=(T // tm,),
        in_specs=[yspec] * 4 + gspecs + [
            pl.BlockSpec((4, Wy, D), lambda i: (0, 0, 0)), pl.BlockSpec((D, D), lambda i: (0, 0)),
            pl.BlockSpec((tm, D), lambda i: (i, 0)), pl.BlockSpec((1, 1, D), lambda i: (i // per, 0, 0))],
        out_specs=pl.BlockSpec((tm, D), lambda i: (i, 0)),
        out_shape=jax.ShapeDtypeStruct((T, D), F32),
        compiler_params=_cp(("parallel",)),
        name="merge_branches",
    )(*ys, z, z, z, z, w_up, w_out, x, gate)


def _mlp_kernel(x_ref, g_ref, sc_ref, sh_ref, w1_ref, b1_ref, w2_ref, b2_ref, gate_ref, fg_ref, o_ref,
                h_ref, acc_ref, *, nk, final):
    k = pl.program_id(1)

    @pl.when(k == 0)
    def _():
        y = _rms(x_ref[...], g_ref[...])
        h_ref[...] = (y * (1.0 + sc_ref[0]) + sh_ref[0]).astype(BF16)
        acc_ref[...] = jnp.zeros_like(acc_ref)

    a = jnp.maximum(jnp.dot(h_ref[...], w1_ref[...], preferred_element_type=F32) + b1_ref[...], 0.0)
    acc_ref[...] += jnp.dot((a * a).astype(BF16), w2_ref[...], preferred_element_type=F32)

    @pl.when(k == nk - 1)
    def _():
        out = x_ref[...] + gate_ref[0] * (acc_ref[...] + b2_ref[...])
        if final:
            out = _rms(out, fg_ref[...])
        o_ref[...] = out


def _mlp(x, g, sc, sh, w1, b1, w2, b2, gate, final_g, seq_len, final):
    T, D = x.shape
    F = w1.shape[1]
    tm = min(1024, seq_len)
    per = seq_len // tm
    tk = 1024
    nk = F // tk
    row = lambda i, k: (i // per, 0, 0)
    return pl.pallas_call(
        functools.partial(_mlp_kernel, nk=nk, final=final),
        grid=(T // tm, nk),
        in_specs=[pl.BlockSpec((tm, D), lambda i, k: (i, 0)), pl.BlockSpec((1, D), lambda i, k: (0, 0)),
                  pl.BlockSpec((1, 1, D), row), pl.BlockSpec((1, 1, D), row),
                  pl.BlockSpec((D, tk), lambda i, k: (0, k)), pl.BlockSpec((1, tk), lambda i, k: (0, k)),
                  pl.BlockSpec((tk, D), lambda i, k: (k, 0)), pl.BlockSpec((1, D), lambda i, k: (0, 0)),
                  pl.BlockSpec((1, 1, D), row), pl.BlockSpec((1, D), lambda i, k: (0, 0))],
        out_specs=pl.BlockSpec((tm, D), lambda i, k: (i, 0)),
        out_shape=jax.ShapeDtypeStruct((T, D), F32),
        scratch_shapes=[pltpu.VMEM((tm, D), BF16), pltpu.VMEM((tm, D), F32)],
        compiler_params=_cp(("parallel", "arbitrary")),
        name="mlp",
    )(x, g.reshape(1, D), sc, sh, w1, b1.reshape(1, F), w2, b2.reshape(1, D), gate, final_g.reshape(1, D))


def _pack_w_in(w_in):
    hy_e = 3 * HY_CH
    ga_e = hy_e + (GA_HEADS + 2 * GA_KV) * GA_HD
    mw = ML_HEADS * ML_HD
    ml_e = ga_e + 4 * mw + 16
    wa_e = ml_e + (WA_HEADS + 2 * WA_KV) * WA_HD
    hy, ga = w_in[:, :hy_e], w_in[:, hy_e:ga_e]
    ml = w_in[:, ga_e:ml_e]
    wa = w_in[:, ml_e:wa_e]
    gate = w_in[:, wa_e:]
    waq, wakv = wa[:, :WA_HEADS * WA_HD], wa[:, WA_HEADS * WA_HD:]
    pad = jnp.zeros((w_in.shape[0], Z_COLS - Z_WAKV - wakv.shape[1]), w_in.dtype)
    packed = jnp.concatenate([hy, waq, ga, ml[:, :2 * mw], ml[:, 2 * mw:3 * mw], ml[:, 3 * mw:4 * mw], gate, wakv, pad],
                             axis=1)
    wg = jnp.pad(ml[:, 4 * mw:], ((0, 0), (0, LANES - 16)))
    return packed.astype(BF16), wg


def _token_mixers(zl, zc, gl, gc, p, B, L, Lc, with_ctx_out):
    ya_l = _hyena_branch(zl, p, B, L)
    ya_c = _hyena_branch(zc, p, B, Lc) if with_ctx_out else None
    cos, sin = _rope_tables(L, GA_HD)
    ql, kl, vl = _ga_prep(zl, cos, sin, p['ga_q_g'], p['ga_k_g'], B, L, True)
    qc, kc, vc = _ga_prep(zc, cos[:Lc], sin[:Lc], p['ga_q_g'], p['ga_k_g'], B, Lc, False)
    r3 = lambda a, n: a.reshape(B, n, a.shape[-1])
    k_all = jnp.concatenate([r3(kl, L), r3(kc, Lc)], axis=1)
    vt_all = jnp.concatenate([vl, vc], axis=2)
    yb_l = _global_attention(ql, k_all, vt_all).reshape(B * L, -1)
    yb_c = _global_attention(qc, r3(kc, Lc), vc).reshape(B * Lc, -1) if with_ctx_out else None
    mq_l, mk_l, mv_l = _ml_prep(zl, p['ml_conv_w'], p['ml_conv_b'], B, L)
    mq_c, mk_c, mv_c = _ml_prep(zc, p['ml_conv_w'], p['ml_conv_b'], B, Lc)
    gb = jnp.pad(p['ml_gate_b'].astype(F32), (0, LANES - 16)).reshape(1, LANES)
    c0 = jnp.zeros((B, ML_HEADS, 2 * ML_HD, ML_HD), F32)
    m0 = jnp.zeros((B, ML_HEADS, 8, LANES), F32)
    zl3, zc3, gl3, gc3 = r3(zl, L), r3(zc, Lc), r3(gl, L), r3(gc, Lc)
    h_cf, cf, mf = _mlstm_scan(mq_c, r3(mk_c, Lc), mv_c, zc3, gc3, gb, c0, m0, False)
    yc_c, cb, mb = _mlstm_scan(mq_c, r3(mk_c, Lc), mv_c, zc3, gc3, gb, c0, m0, True, h_cf, p['ml_norm_g'])
    h_lf, _, _ = _mlstm_scan(mq_l, r3(mk_l, L), mv_l, zl3, gl3, gb, cf, mf, False)
    yc_l, _, _ = _mlstm_scan(mq_l, r3(mk_l, L), mv_l, zl3, gl3, gb, cb, mb, True, h_lf, p['ml_norm_g'])
    yc_l = yc_l.reshape(B * L, -1)
    yc_c = yc_c.reshape(B * Lc, -1)
    cosw, sinw = _rope_tables(L, WA_HD)
    wq_l, wk_l, wv_l = _wa_prep(zl, cosw, sinw, B, L, True)
    wq_c, wk_c, wv_c = _wa_prep(zc, cosw[:Lc], sinw[:Lc], B, Lc, False)
    yd_l = _window_attention(wq_l, r3(wk_c, Lc), wv_c, p['wa_sink'], r3(wk_l, L), wv_l).reshape(B * L, -1)
    yd_c = (_window_attention(wq_c, r3(wk_c, Lc), wv_c, p['wa_sink']).reshape(B * Lc, -1)
            if with_ctx_out else None)
    return (ya_l, yb_l, yc_l, yd_l), (ya_c, yb_c, yc_c, yd_c)


def kernel(x, c, ctx, c_ctx, w_mod, b_mod, ln1_g, ln2_g, w_in, hy_conv_w, hy_conv_b,
           hy_pe_w1, hy_pe_b1, hy_freq, hy_pe_w2, hy_pe_b2, hy_pe_w3, hy_decay, hy_skip,
           ga_q_g, ga_k_g, ml_conv_w, ml_conv_b, ml_gate_b, ml_norm_g, wa_sink, w_up, w_out,
           mlp_w1, mlp_b1, mlp_w2, mlp_b2, final_g):
    B, L, D = x.shape
    Lc = ctx.shape[1]
    R = -(-(B + 1) // 8) * 8
    cvec = jnp.zeros((R, D), F32).at[:B].set(c).at[B].set(c_ctx)
    mod = _modulation(cvec, w_mod, b_mod)
    xl = x.reshape(B * L, D)
    xc = ctx.reshape(B * Lc, D)
    for l in range(DEPTH):
        with_ctx_out = l < DEPTH - 1
        p = dict(hy_conv_w=hy_conv_w[l], hy_conv_b=hy_conv_b[l], hy_pe_w1=hy_pe_w1[l],
                 hy_pe_b1=hy_pe_b1[l], hy_freq=hy_freq[l], hy_pe_w2=hy_pe_w2[l], hy_pe_b2=hy_pe_b2[l],
                 hy_pe_w3=hy_pe_w3[l], hy_decay=hy_decay[l], hy_skip=hy_skip[l],
                 ga_q_g=ga_q_g[l], ga_k_g=ga_k_g[l], ml_conv_w=ml_conv_w[l], ml_conv_b=ml_conv_b[l],
                 ml_gate_b=ml_gate_b[l], ml_norm_g=ml_norm_g[l], wa_sink=wa_sink[l])
        ml_rows = mod[l, :B].reshape(B, 1, 6 * D)
        mc_rows = jnp.broadcast_to(mod[l, B].reshape(1, 1, 6 * D), (B, 1, 6 * D))
        part = lambda m, n: m[:, :, n * D:(n + 1) * D]
        w_pack, w_gate = _pack_w_in(w_in[l])
        wg = w_gate.astype(BF16)
        zl, gl = _normmod_matmul(xl, ln1_g[l], part(ml_rows, 1), part(ml_rows, 0), w_pack, wg, L, Z_TN)
        zc, gc = _normmod_matmul(xc, ln1_g[l], part(mc_rows, 1), part(mc_rows, 0), w_pack, wg, Lc, Z_TN)
        ys_l, ys_c = _token_mixers(zl, zc, gl, gc, p, B, L, Lc, with_ctx_out)
        wup = w_up[l].astype(BF16)
        wout = w_out[l].astype(BF16)
        w1, w2 = mlp_w1[l].astype(BF16), mlp_w2[l].astype(BF16)
        xl = _merge(ys_l, zl, wup, wout, xl, part(ml_rows, 2), L)
        xl = _mlp(xl, ln2_g[l], part(ml_rows, 4), part(ml_rows, 3), w1, mlp_b1[l], w2, mlp_b2[l],
                  part(ml_rows, 5), final_g, L, final=(l == DEPTH - 1))
        if with_ctx_out:
            xc = _merge(ys_c, zc, wup, wout, xc, part(mc_rows, 2), Lc)
            xc = _mlp(xc, ln2_g[l], part(mc_rows, 4), part(mc_rows, 3), w1, mlp_b1[l], w2, mlp_b2[l],
                      part(mc_rows, 5), final_g, Lc, final=False)
    return xl.reshape(B, L, D)
```

```python
import functools
import math

import jax
import jax.numpy as jnp
import numpy as np
from jax import lax
from jax.experimental import pallas as pl
from jax.experimental.pallas import tpu as pltpu

F32 = jnp.float32
BF16 = jnp.bfloat16
HIGHEST = lax.Precision.HIGHEST

D_MODEL = 1024
DEPTH = 2
GRID_W = 64
HY_CH = 512
HY_ORDER = 2
HY_BANDS = 16
HY_EMB = 1 + 2 * HY_BANDS
HY_FFN = 64
GA_HEADS, GA_KV, GA_HD = 4, 2, 128
ML_HEADS, ML_HD = 4, 128
WA_HEADS, WA_KV, WA_HD = 8, 2, 64
WINDOW = 128
ROPE_BASE = 10000.0
D_FF = 4 * D_MODEL
EPS = 1e-6
NEG = -1e30
LOG2E = 1.4426950408889634

LANES = 128
V7X_VMEM_LIMIT = 48 * 1024 * 1024

Z_HY = 0
Z_WAQ = 1536
Z_GA = 2048
Z_MLQK = 3072
Z_MLV = 4096
Z_MLO = 4608
Z_GATE = 5120
Z_WAKV = 9216
Z_COLS = 9728
Z_TN = 2432

ML_CHUNK = 256
FFT_N2 = 128
GA_CB = 256
WA_PAIR = 2
WA_SUB = 4
GA_TK = 256
GA_CARRY = 4
GA_UNROLL = 32


def _cp(sem, vmem=V7X_VMEM_LIMIT):
    return pltpu.CompilerParams(dimension_semantics=sem, vmem_limit_bytes=vmem)


def _rms(x, g):
    return x * lax.rsqrt(jnp.mean(x * x, axis=-1, keepdims=True) + EPS) * g


def _mod_kernel(c_ref, w_ref, b_ref, o_ref):
    c = c_ref[...]
    s = c * jax.nn.sigmoid(c)
    o_ref[0] = jnp.dot(s, w_ref[0], preferred_element_type=F32, precision=HIGHEST) + b_ref[0]


def _modulation(cvec, w_mod, b_mod):
    R = cvec.shape[0]
    tn = 1536
    return pl.pallas_call(
        _mod_kernel,
        grid=(DEPTH, 6 * D_MODEL // tn),
        in_specs=[pl.BlockSpec((R, D_MODEL), lambda l, j: (0, 0)),
                  pl.BlockSpec((1, D_MODEL, tn), lambda l, j: (l, 0, j)),
                  pl.BlockSpec((1, 1, tn), lambda l, j: (l, 0, j))],
        out_specs=pl.BlockSpec((1, R, tn), lambda l, j: (l, 0, j)),
        out_shape=jax.ShapeDtypeStruct((DEPTH, R, 6 * D_MODEL), F32),
        compiler_params=_cp(("parallel", "parallel")),
        name="modulation",
    )(cvec, w_mod, b_mod.reshape(DEPTH, 1, 6 * D_MODEL))


def _nmm_kernel(x_ref, g_ref, sc_ref, sh_ref, w_ref, wx_ref, o_ref, ox_ref):
    h = (_rms(x_ref[...], g_ref[...]) * (1.0 + sc_ref[0]) + sh_ref[0]).astype(BF16)
    o_ref[...] = jnp.dot(h, w_ref[...], preferred_element_type=F32).astype(o_ref.dtype)

    @pl.when(pl.program_id(1) == 0)
    def _():
        ox_ref[...] = jnp.dot(h, wx_ref[...], preferred_element_type=F32)


def _normmod_matmul(x, g, sc, sh, w, wx, seq_len, tn):
    T, D = x.shape
    N = w.shape[1]
    NX = wx.shape[1]
    tm = min(1024, seq_len)
    per = seq_len // tm
    return pl.pallas_call(
        _nmm_kernel,
        grid=(T // tm, N // tn),
        in_specs=[pl.BlockSpec((tm, D), lambda i, j: (i, 0)),
                  pl.BlockSpec((1, D), lambda i, j: (0, 0)),
                  pl.BlockSpec((1, 1, D), lambda i, j: (i // per, 0, 0)),
                  pl.BlockSpec((1, 1, D), lambda i, j: (i // per, 0, 0)),
                  pl.BlockSpec((D, tn), lambda i, j: (0, j)),
                  pl.BlockSpec((D, NX), lambda i, j: (0, 0))],
        out_specs=[pl.BlockSpec((tm, tn), lambda i, j: (i, j)), pl.BlockSpec((tm, NX), lambda i, j: (i, 0))],
        out_shape=[jax.ShapeDtypeStruct((T, N), BF16), jax.ShapeDtypeStruct((T, NX), F32)],
        compiler_params=_cp(("parallel", "arbitrary")),
        name="normmod_matmul",
    )(x, g.reshape(1, D), sc, sh, w, wx)


def _conv3(u, prev_row, next_row, w_ref, b_ref, c0, c1):
    tm = u.shape[0]
    row = lax.broadcasted_iota(jnp.int32, u.shape, 0)
    up = jnp.where(row == 0, prev_row, pltpu.roll(u, 1, 0))
    dn = jnp.where(row == tm - 1, next_row, pltpu.roll(u, tm - 1, 0))
    return (w_ref[0:1, c0:c1] * up + w_ref[1:2, c0:c1] * u + w_ref[2:3, c0:c1] * dn + b_ref[0:1, c0:c1])


def _halo_rows(zp_ref, zn_ref, per, c0, c1):
    i = pl.program_id(0)
    first = (i % per) == 0
    last = (i % per) == per - 1
    hp = zp_ref.shape[0]
    prev_row = jnp.where(first, 0.0, zp_ref[hp - 1:hp, c0:c1].astype(F32))
    next_row = jnp.where(last, 0.0, zn_ref[0:1, c0:c1].astype(F32))
    return prev_row, next_row


HALO = 16


def _halo_specs(tm, width, col_block, n_rows):
    nb = n_rows // HALO
    r = tm // HALO
    return [pl.BlockSpec((tm, width), lambda i: (i, col_block)),
            pl.BlockSpec((HALO, width), lambda i: (jnp.maximum(i * r - 1, 0), col_block)),
            pl.BlockSpec((HALO, width), lambda i: (jnp.minimum((i + 1) * r, nb - 1), col_block))]


def _hy_prep_kernel(z_ref, zp_ref, zn_ref, w_ref, b_ref, v_ref, x1_ref, x2_ref, *, per):
    outs = (v_ref, x1_ref, x2_ref)
    for c in range(3):
        c0, c1 = c * HY_CH, (c + 1) * HY_CH
        prev_row, next_row = _halo_rows(zp_ref, zn_ref, per, c0, c1)
        u = z_ref[:, c0:c1].astype(F32)
        outs[c][...] = _conv3(u, prev_row, next_row, w_ref, b_ref, c0, c1).astype(outs[c].dtype)


def _hy_prep(z, conv_w, conv_b, seq_len):
    T = z.shape[0]
    tm = min(512, seq_len)
    W = 3 * HY_CH
    out = jax.ShapeDtypeStruct((T, HY_CH), BF16)
    return pl.pallas_call(
        functools.partial(_hy_prep_kernel, per=seq_len // tm),
        grid=(T // tm,),
        in_specs=_halo_specs(tm, W, Z_HY // W, T) + [
            pl.BlockSpec((3, W), lambda i: (0, 0)), pl.BlockSpec((1, W), lambda i: (0, 0))],
        out_specs=[pl.BlockSpec((tm, HY_CH), lambda i: (i, 0))] * 3,
        out_shape=[out, out, out],
        compiler_params=_cp(("parallel",)),
        name="hyena_prep",
    )(z, z, z, conv_w, conv_b.reshape(1, W))


def _ml_prep_kernel(z_ref, zp_ref, zn_ref, zv_ref, w_ref, b_ref, q_ref, k_ref, v_ref, *, per):
    W = ML_HEADS * ML_HD
    for c in range(2):
        c0, c1 = c * W, (c + 1) * W
        prev_row, next_row = _halo_rows(zp_ref, zn_ref, per, c0, c1)
        u = z_ref[:, c0:c1].astype(F32)
        y = _conv3(u, prev_row, next_row, w_ref, b_ref, c0, c1)
        y = y * jax.nn.sigmoid(y)
        if c == 0:
            for h in range(ML_HEADS):
                q_ref[0, h * ML_HD:(h + 1) * ML_HD, :] = y[:, h * ML_HD:(h + 1) * ML_HD].T.astype(q_ref.dtype)
        else:
            k_ref[...] = (y * (ML_HD ** -0.5)).astype(k_ref.dtype)
    for h in range(ML_HEADS):
        v_ref[0, h * ML_HD:(h + 1) * ML_HD, :] = zv_ref[:, h * ML_HD:(h + 1) * ML_HD].astype(F32).T.astype(v_ref.dtype)


def _ml_prep(z, conv_w, conv_b, B, seq_len):
    T = z.shape[0]
    tm = min(512, seq_len)
    per = seq_len // tm
    W = 2 * ML_HEADS * ML_HD
    Wh = W // 2
    tspec = pl.BlockSpec((1, Wh, tm), lambda i: (i // per, 0, i % per))
    tshape = jax.ShapeDtypeStruct((B, Wh, seq_len), BF16)
    return pl.pallas_call(
        functools.partial(_ml_prep_kernel, per=per),
        grid=(T // tm,),
        in_specs=_halo_specs(tm, W, Z_MLQK // W, T) + [
            pl.BlockSpec((tm, Wh), lambda i: (i, Z_MLV // Wh)),
            pl.BlockSpec((3, W), lambda i: (0, 0)), pl.BlockSpec((1, W), lambda i: (0, 0))],
        out_specs=[tspec, pl.BlockSpec((tm, Wh), lambda i: (i, 0)), tspec],
        out_shape=[tshape, jax.ShapeDtypeStruct((T, Wh), BF16), tshape],
        compiler_params=_cp(("parallel",)),
        name="mlstm_prep",
    )(z, z, z, z, conv_w, conv_b.reshape(1, W))


def _rope_tables(L, hd):
    quarter = hd // 4
    inv = ROPE_BASE ** (-jnp.arange(quarter, dtype=F32) / quarter)
    t = jnp.arange(L)
    row = (t // GRID_W).astype(F32)
    col = (t % GRID_W).astype(F32)
    lane = jnp.arange(LANES)
    within = lane % hd
    is_col = (within // (hd // 2)) == 1
    second = ((within % (hd // 2)) // quarter) == 1
    j = within % quarter
    pos = jnp.where(is_col[None, :], col[:, None], row[:, None])
    ang = pos * inv[j][None, :]
    return jnp.cos(ang), jnp.where(second[None, :], jnp.sin(ang), -jnp.sin(ang))


def _rope(x, cos, sin, quarter):
    lane = lax.broadcasted_iota(jnp.int32, x.shape, 1)
    first = ((lane % (2 * quarter)) // quarter) == 0
    partner = jnp.where(first, pltpu.roll(x, LANES - quarter, 1), pltpu.roll(x, quarter, 1))
    return x * cos + partner * sin


def _ga_prep_kernel(z_ref, cos_ref, sin_ref, qg_ref, kg_ref, q_ref, k_ref, v_ref, *, rope):
    nq, nk = GA_HEADS, GA_KV
    for h in range(nq + nk):
        x = z_ref[:, h * GA_HD:(h + 1) * GA_HD].astype(F32)
        g = qg_ref[...] if h < nq else kg_ref[...]
        y = _rms(x, g)
        if rope:
            y = _rope(y, cos_ref[...], sin_ref[...], GA_HD // 4)
        if h < nq:
            q_ref[0, h * GA_HD:(h + 1) * GA_HD, :] = (y * (GA_HD ** -0.5 * LOG2E)).T.astype(q_ref.dtype)
        else:
            k_ref[:, (h - nq) * GA_HD:(h - nq + 1) * GA_HD] = y.astype(k_ref.dtype)
    for h in range(nk):
        v = z_ref[:, (nq + nk + h) * GA_HD:(nq + nk + h + 1) * GA_HD].astype(F32)
        v_ref[0, h * GA_HD:(h + 1) * GA_HD, :] = v.T.astype(v_ref.dtype)


def _ga_prep(z, cos, sin, qg, kg, B, seq_len, rope):
    T = z.shape[0]
    tm = min(512, seq_len)
    per = seq_len // tm
    W = (GA_HEADS + 2 * GA_KV) * GA_HD
    return pl.pallas_call(
        functools.partial(_ga_prep_kernel, rope=rope),
        grid=(T // tm,),
        in_specs=[pl.BlockSpec((tm, W), lambda i: (i, Z_GA // W)),
                  pl.BlockSpec((tm, LANES), lambda i: (i % per, 0)),
                  pl.BlockSpec((tm, LANES), lambda i: (i % per, 0)),
                  pl.BlockSpec((1, GA_HD), lambda i: (0, 0)),
                  pl.BlockSpec((1, GA_HD), lambda i: (0, 0))],
        out_specs=[pl.BlockSpec((1, GA_HEADS * GA_HD, tm), lambda i: (i // per, 0, i % per)),
                   pl.BlockSpec((tm, GA_KV * GA_HD), lambda i: (i, 0)),
                   pl.BlockSpec((1, GA_KV * GA_HD, tm), lambda i: (i // per, 0, i % per))],
        out_shape=[jax.ShapeDtypeStruct((B, GA_HEADS * GA_HD, seq_len), BF16),
                   jax.ShapeDtypeStruct((T, GA_KV * GA_HD), BF16),
                   jax.ShapeDtypeStruct((B, GA_KV * GA_HD, seq_len), BF16)],
        compiler_params=_cp(("parallel",)),
        name="global_attn_prep",
    )(z, cos, sin, qg.reshape(1, GA_HD), kg.reshape(1, GA_HD))


def _wa_prep_kernel(zq_ref, zkv_ref, cos_ref, sin_ref, q_ref, k_ref, v_ref, *, rope):
    quarter = WA_HD // 4
    for j in range(WA_HEADS * WA_HD // LANES):
        x = zq_ref[:, j * LANES:(j + 1) * LANES].astype(F32)
        if rope:
            x = _rope(x, cos_ref[...], sin_ref[...], quarter)
        q_ref[0, j * LANES:(j + 1) * LANES, :] = (x * (WA_HD ** -0.5 * LOG2E)).T.astype(q_ref.dtype)
    k = zkv_ref[:, 0:LANES].astype(F32)
    if rope:
        k = _rope(k, cos_ref[...], sin_ref[...], quarter)
    k_ref[...] = k.astype(k_ref.dtype)
    v_ref[0] = zkv_ref[:, LANES:].astype(F32).T.astype(v_ref.dtype)


def _wa_prep(z, cos, sin, B, seq_len, rope):
    T = z.shape[0]
    tm = min(512, seq_len)
    per = seq_len // tm
    WQ = WA_HEADS * WA_HD
    return pl.pallas_call(
        functools.partial(_wa_prep_kernel, rope=rope),
        grid=(T // tm,),
        in_specs=[pl.BlockSpec((tm, WQ), lambda i: (i, Z_WAQ // WQ)),
                  pl.BlockSpec((tm, 2 * LANES), lambda i: (i, Z_WAKV // (2 * LANES))),
                  pl.BlockSpec((tm, LANES), lambda i: (i % per, 0)),
                  pl.BlockSpec((tm, LANES), lambda i: (i % per, 0))],
        out_specs=[pl.BlockSpec((1, WQ, tm), lambda i: (i // per, 0, i % per)),
                   pl.BlockSpec((tm, LANES), lambda i: (i, 0)),
                   pl.BlockSpec((1, LANES, tm), lambda i: (i // per, 0, i % per))],
        out_shape=[jax.ShapeDtypeStruct((B, WQ, seq_len), BF16),
                   jax.ShapeDtypeStruct((T, LANES), BF16),
                   jax.ShapeDtypeStruct((B, LANES, seq_len), BF16)],
        compiler_params=_cp(("parallel",)),
        name="window_attn_prep",
    )(z, z, cos, sin)


def _ga_kernel(q_ref, k_ref, vt_ref, o_ref, acc_ref, m_ref, *, nchunks, tk, tq):
    acc_ref[...] = jnp.zeros_like(acc_ref)
    m_ref[...] = jnp.full_like(m_ref, NEG)
    qt = jnp.concatenate([q_ref[0, 0:GA_HD, :], q_ref[0, GA_HD:, :]], axis=1)

    W = GA_CB
    nblk = 2 * tq // W

    def scores(k, blocks=None):
        blocks = range(nblk) if blocks is None else blocks
        return tuple(jnp.dot(k, qt[:, i * W:(i + 1) * W], preferred_element_type=F32) for i in blocks)

    def softmax_pv(s_blocks, vt):
        cols = [slice(i * W, (i + 1) * W) for i in range(nblk)]
        m_old = [m_ref[:, cs] for cs in cols]
        m_new = [jnp.maximum(mo, jnp.max(s, axis=0, keepdims=True)) for mo, s in zip(m_old, s_blocks)]
        alpha = [jnp.exp2(mo - mn) for mo, mn in zip(m_old, m_new)]
        vaug = jnp.concatenate([vt, jnp.ones((16, vt.shape[1]), BF16)], axis=0)
        pv = [jnp.dot(vaug, jnp.exp2((s - mn).astype(BF16)), preferred_element_type=F32)
              for s, mn in zip(s_blocks, m_new)]
        for i, cs in enumerate(cols):
            acc_ref[:, cs] = alpha[i] * acc_ref[:, cs] + pv[i]
            m_ref[:, cs] = m_new[i]

    ahead, late = range(min(GA_CARRY, nblk)), range(min(GA_CARRY, nblk), nblk)

    def body(j, s):
        off = pl.multiple_of(j * tk, tk)
        s_late = scores(k_ref[0, pl.ds(off, tk), :], late)
        s_next = scores(k_ref[0, pl.ds(pl.multiple_of((j + 1) * tk, tk), tk), :], ahead)
        softmax_pv(s + s_late, vt_ref[0, :, pl.ds(off, tk)])
        return s_next
    s = scores(k_ref[0, 0:tk, :], ahead)
    if nchunks > 1:
        s = lax.fori_loop(0, nchunks - 1, body, s, unroll=GA_UNROLL if (nchunks - 1) % GA_UNROLL == 0 else 1)
    last = slice((nchunks - 1) * tk, nchunks * tk)
    softmax_pv(s + scores(k_ref[0, last, :], late), vt_ref[0, :, last])
    o = acc_ref[0:GA_HD, :] / acc_ref[GA_HD:GA_HD + 1, :]
    for h in range(2):
        o_ref[0, :, h * GA_HD:(h + 1) * GA_HD] = o[:, h * tq:(h + 1) * tq].T.astype(o_ref.dtype)


def _global_attention(qt, k, vt):
    B, _, Lq = qt.shape
    Lk = k.shape[1]
    tq = min(512, Lq)
    tk = min(GA_TK, Lk)
    W = 2 * GA_HD
    return pl.pallas_call(
        functools.partial(_ga_kernel, nchunks=Lk // tk, tk=tk, tq=tq),
        grid=(B, GA_KV, Lq // tq),
        in_specs=[pl.BlockSpec((1, W, tq), lambda b, g, i: (b, g, i)),
                  pl.BlockSpec((1, Lk, GA_HD), lambda b, g, i: (b, 0, g)),
                  pl.BlockSpec((1, GA_HD, Lk), lambda b, g, i: (b, g, 0))],
        out_specs=pl.BlockSpec((1, tq, W), lambda b, g, i: (b, i, g)),
        out_shape=jax.ShapeDtypeStruct((B, Lq, GA_HEADS * GA_HD), BF16),
        scratch_shapes=[pltpu.VMEM((GA_HD + 16, 2 * tq), F32), pltpu.VMEM((1, 2 * tq), F32)],
        compiler_params=_cp(("parallel", "parallel", "parallel")),
        name="global_attention",
    )(qt, k, vt)


def _wa_kernel(*refs, band, nq, nsub):
    if band:
        q_ref = refs[0]
        kb = refs[1:nsub + 3]
        vb = refs[nsub + 3:2 * nsub + 5]
        kc_ref, vc_ref, sink_ref, o_ref = refs[2 * nsub + 5:]
    else:
        q_ref, kc_ref, vc_ref, sink_ref, o_ref = refs
    tq = WINDOW
    Lc = kc_ref.shape[1]
    step = pl.program_id(1)
    G = WA_HEADS // WA_KV
    cols = WA_PAIR * tq
    npr = WA_HEADS // WA_PAIR
    grp = [(pr * WA_PAIR) // G for pr in range(npr)]
    zeros = jnp.zeros((WA_HD, cols), BF16)
    if band:
        nb = 3 * tq
        c = lax.broadcasted_iota(jnp.int32, (nb + Lc, cols), 0)
        r = lax.broadcasted_iota(jnp.int32, (nb + Lc, cols), 1) % tq
        in_band = (c >= r) & (c <= r + 2 * WINDOW)
    ones_rows = jnp.ones((16, (3 * tq if band else 0) + Lc), BF16)
    chains = [(u, pr) for u in range(nsub) for pr in range(npr)]
    keys, vals, valid = [], [], []
    for u in range(nsub):
        if band:
            keys.append(jnp.concatenate([kb[u][0], kb[u + 1][0], kb[u + 2][0], kc_ref[0]], axis=0))
            vals.append(jnp.concatenate([vb[u][0], vb[u + 1][0], vb[u + 2][0], vc_ref[0]], axis=1))
            qi = step * nsub + u
            lo = jnp.where(qi == 0, tq, 0)
            hi = jnp.where(qi == nq - 1, 2 * tq, nb)
            valid.append((in_band & (c >= lo) & (c < hi)) | (c >= nb))
        else:
            keys.append(kc_ref[0])
            vals.append(vc_ref[0])
    s = {}
    for u, pr in chains:
        qg = jnp.concatenate([q_ref[0, (WA_PAIR * pr + h) * WA_HD:(WA_PAIR * pr + h + 1) * WA_HD, u * tq:(u + 1) * tq]
                              for h in range(WA_PAIR)], axis=1)
        qpad = jnp.concatenate([qg, zeros] if grp[pr] == 0 else [zeros, qg], axis=0)
        sp = jnp.dot(keys[u], qpad, preferred_element_type=F32)
        s[u, pr] = jnp.where(valid[u], sp, NEG) if band else sp
    sink = [sink_ref[pr] for pr in range(npr)]
    m = {ch: jnp.maximum(jnp.max(s[ch], axis=0, keepdims=True), sink[ch[1]]) for ch in chains}
    p = {ch: jnp.exp2(s[ch] - m[ch]).astype(BF16) for ch in chains}
    R = {}
    for u, pr in chains:
        vaug = jnp.concatenate([vals[u][grp[pr] * WA_HD:(grp[pr] + 1) * WA_HD, :], ones_rows], axis=0)
        R[u, pr] = jnp.dot(vaug, p[u, pr], preferred_element_type=F32)
    for u, pr in chains:
        o = R[u, pr][:WA_HD] / (R[u, pr][WA_HD:WA_HD + 1] + jnp.exp2(sink[pr] - m[u, pr]))
        ot = jnp.concatenate([o[:, h * tq:(h + 1) * tq] for h in range(WA_PAIR)], axis=0)
        wo = WA_PAIR * WA_HD
        o_ref[0, u * tq:(u + 1) * tq, pr * wo:(pr + 1) * wo] = ot.T.astype(o_ref.dtype)


def _window_attention(qt, kc, vtc, sink, kl=None, vtl=None):
    B, WQ, Lq = qt.shape
    Lc = kc.shape[1]
    tq = WINDOW
    nq = Lq // tq
    nsub = math.gcd(WA_SUB, nq)
    band = kl is not None
    npair = WA_HEADS // WA_PAIR
    sink_row = jnp.repeat(sink.astype(F32).reshape(npair, WA_PAIR) * LOG2E, tq, axis=1).reshape(npair, 1, WA_PAIR * tq)
    qspec = pl.BlockSpec((1, WQ, nsub * tq), lambda b, i: (b, 0, i))
    kcspec = pl.BlockSpec((1, Lc, LANES), lambda b, i: (b, 0, 0))
    vcspec = pl.BlockSpec((1, LANES, Lc), lambda b, i: (b, 0, 0))
    sspec = pl.BlockSpec((npair, 1, WA_PAIR * tq), lambda b, i: (0, 0, 0))
    if band:
        blk = lambda off: (lambda i: jnp.clip(i * nsub + off, 0, nq - 1))
        kspec = lambda f: pl.BlockSpec((1, tq, LANES), lambda b, i: (b, f(i), 0))
        vspec = lambda f: pl.BlockSpec((1, LANES, tq), lambda b, i: (b, 0, f(i)))
        offs = range(-1, nsub + 1)
        in_specs = ([qspec] + [kspec(blk(o)) for o in offs] + [vspec(blk(o)) for o in offs]
                    + [kcspec, vcspec, sspec])
        args = (qt,) + (kl,) * (nsub + 2) + (vtl,) * (nsub + 2) + (kc, vtc, sink_row)
    else:
        in_specs, args = [qspec, kcspec, vcspec, sspec], (qt, kc, vtc, sink_row)
    return pl.pallas_call(
        functools.partial(_wa_kernel, band=band, nq=nq, nsub=nsub),
        grid=(B, nq // nsub),
        in_specs=in_specs,
        out_specs=pl.BlockSpec((1, nsub * tq, WQ), lambda b, i: (b, i, 0)),
        out_shape=jax.ShapeDtypeStruct((B, Lq, WQ), BF16),
        compiler_params=_cp(("parallel", "parallel")),
        name="window_attention",
    )(*args)


def _mlstm_kernel(*refs, reverse, nc, bb):
    if reverse:
        (qt_ref, k_ref, vt_ref, g_ref, gb_ref, c0_ref, m0_ref, hft_ref, o_ref, gn_ref,
         y_ref, cf_ref, mf_ref, c_scr, m_scr) = refs
    else:
        (qt_ref, k_ref, vt_ref, g_ref, gb_ref, c0_ref, m0_ref,
         y_ref, cf_ref, mf_ref, c_scr, m_scr) = refs
    T = ML_CHUNK
    d = 1 if reverse else 0
    step = pl.program_id(1)

    @pl.when(step == 0)
    def _():
        c_scr[...] = c0_ref[...]
        m_scr[...] = m0_ref[...]

    si = lax.broadcasted_iota(jnp.int32, (T, T), 0)
    ti = lax.broadcasted_iota(jnp.int32, (T, T), 1)
    tri = ((ti >= si) if reverse else (ti <= si)).astype(F32)
    mask_t = (si >= ti) if reverse else (si <= ti)
    ones_rows = jnp.ones((ML_HD, T), BF16)
    e_last = 0 if reverse else T - 1
    chains = [(bi, h) for bi in range(bb) for h in range(ML_HEADS)]
    gates = []
    for bi in range(bb):
        G = g_ref[bi] + gb_ref[...]
        LF = jax.nn.log_sigmoid(G)
        Bc = jnp.dot(tri, LF, preferred_element_type=F32, precision=HIGHEST)
        gates.append((G.T, Bc.T, Bc - pltpu.roll(G, 4, 1)))
    st, ph = {}, {}
    for bi, h in chains:
        hs = slice(h * ML_HD, (h + 1) * ML_HD)
        st[bi, h] = jnp.dot(k_ref[bi, :, hs], qt_ref[bi, hs, :], preferred_element_type=F32)
    for bi, h in chains:
        GT, BT, Dc = gates[bi]
        fl, il = d * 8 + 4 + h, d * 8 + h
        b_row, i_row = BT[fl:fl + 1, :], GT[il:il + 1, :]
        log_d = jnp.where(mask_t, b_row - Dc[:, fl:fl + 1], NEG)
        m_prev = m_scr[bi, h, 0:1, 0:1]
        m_inter = b_row + m_prev
        m_t = jnp.maximum(m_inter, jnp.max(log_d, axis=0, keepdims=True))
        wqk = (st[bi, h] * jnp.exp(log_d - m_t)).astype(BF16)
        b_end = BT[fl:fl + 1, e_last:e_last + 1]
        log_w = b_end - b_row + i_row
        m_next = jnp.maximum(b_end + m_prev, jnp.max(log_w, axis=1, keepdims=True))
        ph[bi, h] = (wqk, jnp.exp(m_inter - m_t), jnp.exp(-m_t), jnp.exp(log_w - m_next),
                     jnp.exp(b_end + m_prev - m_next), m_next)
    for bi, h in chains:
        hs = slice(h * ML_HD, (h + 1) * ML_HD)
        wqk, cs, em, w_row, decay, m_next = ph[bi, h]
        kh, qt = k_ref[bi, :, hs], qt_ref[bi, hs, :]
        vaug = jnp.concatenate([vt_ref[bi, hs, :], ones_rows], axis=0)
        R = (jnp.dot(vaug, wqk, preferred_element_type=F32)
             + cs * jnp.dot(c_scr[bi, h].astype(BF16), qt, preferred_element_type=F32))
        hh = R[:ML_HD] / jnp.maximum(jnp.abs(R[ML_HD:]), em)
        if reverse:
            hsum = hft_ref[bi, hs, :] + hh
            hn = hsum * lax.rsqrt(jnp.mean(hsum * hsum, axis=0, keepdims=True) + EPS) * gn_ref[hs, :]
            y_ref[bi, :, hs] = (jax.nn.sigmoid(o_ref[bi, :, hs].astype(F32)) * hn.T).astype(y_ref.dtype)
        else:
            y_ref[bi, hs, :] = hh
        wv = (vaug.astype(F32) * w_row).astype(BF16)
        c_scr[bi, h] = decay * c_scr[bi, h] + jnp.dot(wv, kh, preferred_element_type=F32)
        m_scr[bi, h] = jnp.broadcast_to(m_next, (8, LANES))

    @pl.when(step == nc - 1)
    def _():
        cf_ref[...] = c_scr[...]
        mf_ref[...] = m_scr[...]


def _mlstm_scan(qt, k, vt, z, gates, gate_b, c0, m0, reverse, hft=None, norm_g=None):
    B, L, W = k.shape
    T = ML_CHUNK
    nc = L // T
    bb = 4 if B % 4 == 0 else 2
    cj =(lambda j: nc - 1 - j) if reverse else (lambda j: j)
    tok = pl.BlockSpec((bb, T, W), lambda b, j: (b, cj(j), 0))
    ttok = pl.BlockSpec((bb, W, T), lambda b, j: (b, 0, cj(j)))
    cspec = pl.BlockSpec((bb, ML_HEADS, 2 * ML_HD, ML_HD), lambda b, j: (b, 0, 0, 0))
    mspec = pl.BlockSpec((bb, ML_HEADS, 8, LANES), lambda b, j: (b, 0, 0, 0))
    in_specs = [ttok, tok, ttok, pl.BlockSpec((bb, T, LANES), lambda b, j: (b, cj(j), 0)),
                pl.BlockSpec((1, LANES), lambda b, j: (0, 0)), cspec, mspec]
    args = [qt, k, vt, gates, gate_b, c0, m0]
    if reverse:
        in_specs += [ttok, pl.BlockSpec((bb, T, W), lambda b, j: (b, cj(j), Z_MLO // W)),
                     pl.BlockSpec((W, T), lambda b, j: (0, 0))]
        args += [hft, z, jnp.broadcast_to(norm_g.astype(F32).reshape(W, 1), (W, T))]
    return pl.pallas_call(
        functools.partial(_mlstm_kernel, reverse=reverse, nc=nc, bb=bb),
        grid=(B // bb, nc),
        in_specs=in_specs,
        out_specs=[tok if reverse else ttok, cspec, mspec],
        out_shape=[jax.ShapeDtypeStruct((B, L, W), BF16) if reverse else jax.ShapeDtypeStruct((B, W, L), F32),
                   jax.ShapeDtypeStruct(c0.shape, F32), jax.ShapeDtypeStruct(m0.shape, F32)],
        scratch_shapes=[pltpu.VMEM((bb, ML_HEADS, 2 * ML_HD, ML_HD), F32), pltpu.VMEM((bb, ML_HEADS, 8, LANES), F32)],
        compiler_params=_cp(("parallel", "arbitrary")),
        name="mlstm_reverse" if reverse else "mlstm_forward",
    )(*args)


def _fft_dims(Lp):
    n1 = 2 * Lp // FFT_N2
    nt1 = Lp // FFT_N2
    nk1 = -(-(n1 // 2 + 1) // 8) * 8
    return n1, nt1, nk1


def _fft_tables(Lp):
    n1, nt1, nk1 = _fft_dims(Lp)
    N = 2 * Lp
    k1 = jnp.arange(nk1)
    t1 = jnp.arange(nt1)
    ang_a = (2.0 * math.pi / n1) * ((k1[:, None] * t1[None, :]) % n1).astype(F32)
    fa = jnp.stack([jnp.cos(ang_a), -jnp.sin(ang_a)], axis=1).reshape(2 * nk1, nt1)
    wgt = jnp.where((k1 == 0) | (k1 == n1 // 2), 1.0, 2.0) * (k1 <= n1 // 2) / N
    fai = jnp.stack([jnp.cos(ang_a) * wgt[:, None], -jnp.sin(ang_a) * wgt[:, None]], axis=1)
    fai = fai.reshape(2 * nk1, nt1).T
    k2 = jnp.arange(FFT_N2)
    t2 = jnp.arange(FFT_N2)
    idx = (t2[None, None, :] * k1[:, None, None] + n1 * t2[None, None, :] * k2[None, :, None]) % N
    phi = (2.0 * math.pi / N) * idx.astype(F32)
    gr, gi = jnp.cos(phi), -jnp.sin(phi)
    gfwd = jnp.concatenate([jnp.concatenate([gr, -gi], axis=2), jnp.concatenate([gi, gr], axis=2)], axis=1)
    grt, git = jnp.swapaxes(gr, 1, 2), jnp.swapaxes(gi, 1, 2)
    ginv = jnp.concatenate([jnp.concatenate([grt, git], axis=2), jnp.concatenate([-git, grt], axis=2)], axis=1)
    eye = jnp.eye(FFT_TK, dtype=F32)
    return (jnp.kron(fa, eye).astype(BF16), jnp.kron(fai, eye).astype(BF16), gfwd.astype(BF16), ginv.astype(BF16))


FFT_TB = 32
FFT_TK = 8
FFT_CB = 512


def _fa_kernel(fa_ref, x_ref, o_ref):
    nt1, tb, cb = x_ref.shape[1:]
    rows = o_ref.shape[1]
    x = x_ref[0].astype(F32).reshape(nt1, tb // FFT_TK, FFT_TK, cb)
    parts = []
    for j in range(tb // FFT_TK):
        xj = x[:, j].reshape(nt1 * FFT_TK, cb).astype(BF16)
        r = jnp.dot(fa_ref[...], xj, preferred_element_type=F32)
        parts.append(r.reshape(rows, 1, FFT_TK, cb))
    o_ref[0] = jnp.concatenate(parts, axis=1).reshape(rows, tb, cb).astype(o_ref.dtype)


def _fft_stage_a(fa, y, out_dtype):
    B, Lp, C = y.shape
    nt1 = Lp // FFT_N2
    rows = fa.shape[0] // FFT_TK
    return pl.pallas_call(
        _fa_kernel,
        grid=(B, FFT_N2 // FFT_TB, C // FFT_CB),
        in_specs=[pl.BlockSpec(fa.shape, lambda b, j, c: (0, 0)),
                  pl.BlockSpec((1, nt1, FFT_TB, FFT_CB), lambda b, j, c: (b, 0, j, c))],
        out_specs=pl.BlockSpec((1, rows, FFT_TB, FFT_CB), lambda b, j, c: (b, 0, j, c)),
        out_shape=jax.ShapeDtypeStruct((B, rows, FFT_N2, C), out_dtype),
        compiler_params=_cp(("parallel", "parallel", "parallel")),
        name="fft_stage_a",
    )(fa, y.reshape(B, nt1, FFT_N2, C))


def _fc_filter_kernel(g_ref, s_ref, ss_ref, h_ref, *, kb):
    C = HY_CH
    for i in range(kb):
        for n in range(HY_ORDER):
            cf, cb = (2 * n) * C, (2 * n + 1) * C
            scale = lax.rsqrt(ss_ref[0:1, cf:cf + C] + ss_ref[0:1, cb:cb + C] + EPS)
            sf = s_ref[0, 2 * i:2 * i + 2, :, cf:cf + C].reshape(2 * FFT_N2, C).astype(BF16)
            sb = s_ref[0, 2 * i:2 * i + 2, :, cb:cb + C].reshape(2 * FFT_N2, C).astype(BF16)
            xf = jnp.dot(g_ref[i], sf, preferred_element_type=F32)
            xb = jnp.dot(g_ref[i], sb, preferred_element_type=F32)
            h_ref[n, 2 * i] = (xf[:FFT_N2] + xb[:FFT_N2]) * scale
            h_ref[n, 2 * i + 1] = (xf[FFT_N2:] - xb[FFT_N2:]) * scale


def _fft_filter_spectrum(gfwd, s_filt, sumsq):
    rows = s_filt.shape[1]
    C4 = s_filt.shape[-1]
    kb = 2
    return pl.pallas_call(
        functools.partial(_fc_filter_kernel, kb=kb),
        grid=(rows // (2 * kb),),
        in_specs=[pl.BlockSpec((kb, 2 * FFT_N2, 2 * FFT_N2), lambda i: (i, 0, 0)),
                  pl.BlockSpec((1, 2 * kb, FFT_N2, C4), lambda i: (0, i, 0, 0)),
                  pl.BlockSpec((1, C4), lambda i: (0, 0))],
        out_specs=pl.BlockSpec((HY_ORDER, 2 * kb, FFT_N2, HY_CH), lambda i: (0, i, 0, 0)),
        out_shape=jax.ShapeDtypeStruct((HY_ORDER, rows, FFT_N2, HY_CH), F32),
        compiler_params=_cp(("parallel",)),
        name="fft_filter_spectrum",
    )(gfwd, s_filt, sumsq)


def _fc_kernel(g_ref, gi_ref, h_ref, s_ref, o_ref, *, kb):
    C = s_ref.shape[-1]
    x = [jnp.dot(g_ref[i], s_ref[0, 2 * i:2 * i + 2].reshape(2 * FFT_N2, C), preferred_element_type=F32)
         for i in range(kb)]
    z = []
    for i in range(kb):
        xr, xi = x[i][:FFT_N2], x[i][FFT_N2:]
        hr, hi = h_ref[0, 2 * i], h_ref[0, 2 * i + 1]
        z.append(jnp.concatenate([xr * hr - xi * hi, xr * hi + xi * hr], axis=0).astype(BF16))
    bm = [jnp.dot(gi_ref[i], z[i], preferred_element_type=F32) for i in range(kb)]
    for i in range(kb):
        o_ref[0, 2 * i:2 * i + 2] = bm[i].reshape(2, FFT_N2, C).astype(o_ref.dtype)


def _fft_stage_c(gfwd, ginv, hspec, order, s):
    B, rows, _, C = s.shape
    kb = 8 if rows % 16 == 0 else 4
    sspec =pl.BlockSpec((1, 2 * kb, FFT_N2, C), lambda i, b: (b, i, 0, 0))
    gspec = pl.BlockSpec((kb, 2 * FFT_N2, 2 * FFT_N2), lambda i, b: (i, 0, 0))
    return pl.pallas_call(
        functools.partial(_fc_kernel, kb=kb),
        grid=(rows // (2 * kb), B),
        in_specs=[gspec, gspec,
                  pl.BlockSpec((1, 2 * kb, FFT_N2, C), lambda i, b: (order, i, 0, 0)), sspec],
        out_specs=sspec,
        out_shape=jax.ShapeDtypeStruct(s.shape, BF16),
        compiler_params=_cp(("parallel", "arbitrary")),
        name="fft_stage_c",
    )(gfwd, ginv, hspec, s)


def _fai_kernel(fai_ref, b_ref, y_ref, gate_ref, skip_ref, o_ref):
    nt1, tb, cb = y_ref.shape[1:]
    rows = b_ref.shape[1]
    bm = b_ref[0].astype(F32).reshape(rows, tb // FFT_TK, FFT_TK, cb)
    parts = []
    for j in range(tb // FFT_TK):
        bj = bm[:, j].reshape(rows * FFT_TK, cb).astype(BF16)
        yf = jnp.dot(fai_ref[...], bj, preferred_element_type=F32)
        parts.append(yf.reshape(nt1, 1, FFT_TK, cb))
    yf = jnp.concatenate(parts, axis=1).reshape(nt1, tb, cb)
    out = gate_ref[0].astype(F32) * (yf + skip_ref[...].reshape(1, 1, cb) * y_ref[0].astype(F32))
    o_ref[0] = out.astype(o_ref.dtype)


def _fft_stage_a_inv(fai, bm, y, gate, skip):
    B, Lp, C = y.shape
    nt1 = Lp // FFT_N2
    rows = fai.shape[1] // FFT_TK
    tok = pl.BlockSpec((1, nt1, FFT_TB, FFT_CB), lambda b, j, c: (b, 0, j, c))
    out = pl.pallas_call(
        _fai_kernel,
        grid=(B, FFT_N2 // FFT_TB, C // FFT_CB),
        in_specs=[pl.BlockSpec(fai.shape, lambda b, j, c: (0, 0)),
                  pl.BlockSpec((1, rows, FFT_TB, FFT_CB), lambda b, j, c: (b, 0, j, c)),
                  tok, tok, pl.BlockSpec((1, FFT_CB), lambda b, j, c: (0, c))],
        out_specs=tok,
        out_shape=jax.ShapeDtypeStruct((B, nt1, FFT_N2, C), BF16),
        compiler_params=_cp(("parallel", "parallel", "parallel")),
        name="fft_stage_a_inv",
    )(fai, bm, y.reshape(B, nt1, FFT_N2, C), gate.reshape(B, nt1, FFT_N2, C), skip.astype(F32).reshape(1, C))
    return out.reshape(B, Lp, C)


def _hgen_kernel(z_ref, w1_ref, b1_ref, fr_ref, w2_ref, b2_ref, w3_ref, dec_ref, h_ref, ss_ref):
    i = pl.program_id(0)
    z = z_ref[...]
    tm = z.shape[0]
    h = jnp.sin(fr_ref[0:1, :] * (jnp.dot(z, w1_ref[...], preferred_element_type=F32, precision=HIGHEST)
                                 + b1_ref[...]))
    h = jnp.sin(fr_ref[1:2, :] * (jnp.dot(h, w2_ref[...], preferred_element_type=F32, precision=HIGHEST)
                                 + b2_ref[...]))
    h = jnp.dot(h, w3_ref[...], preferred_element_type=F32, precision=HIGHEST)
    h = h * jnp.exp(-z[:, 0:1] * jnp.abs(dec_ref[...]))
    row = lax.broadcasted_iota(jnp.int32, h.shape, 0) + i * tm
    col = lax.broadcasted_iota(jnp.int32, h.shape, 1)
    h = jnp.where((row == 0) & ((col // HY_CH) % 2 == 1), 0.0, h)
    h_ref[...] = h

    @pl.when(i == 0)
    def _():
        ss_ref[...] = jnp.zeros_like(ss_ref)

    ss_ref[...] += jnp.sum(h * h, axis=0, keepdims=True)


def _hyena_filters(L, p):
    t = jnp.arange(L, dtype=F32)
    tn = t / (L - 1)
    w = 2.0 * math.pi * t / L
    bands = jnp.linspace(1e-4, HY_BANDS - 1, HY_BANDS, dtype=F32)
    ang = w[:, None] * bands[None, :]
    z = jnp.concatenate([tn[:, None], jnp.cos(ang), -jnp.sin(ang)], axis=-1)
    z = jnp.pad(z, ((0, 0), (0, LANES - HY_EMB)))
    w1 = jnp.pad(p['hy_pe_w1'].astype(F32), ((0, LANES - HY_EMB), (0, 0)))
    nf = HY_ORDER * 2 * HY_CH
    tm = min(512, L)
    const = lambda shape: pl.BlockSpec(shape, lambda i: (0,) * len(shape))
    return pl.pallas_call(
        _hgen_kernel,
        grid=(L // tm,),
        in_specs=[pl.BlockSpec((tm, LANES), lambda i: (i, 0)), const((LANES, HY_FFN)), const((1, HY_FFN)),
                  const((2, HY_FFN)), const((HY_FFN, HY_FFN)), const((1, HY_FFN)), const((HY_FFN, nf)),
                  const((1, nf))],
        out_specs=[pl.BlockSpec((tm, nf), lambda i: (i, 0)), const((1, nf))],
        out_shape=[jax.ShapeDtypeStruct((L, nf), F32), jax.ShapeDtypeStruct((1, nf), F32)],
        compiler_params=_cp(("arbitrary",)),
        name="hyena_filter_gen",
    )(z, w1, p['hy_pe_b1'].reshape(1, HY_FFN), p['hy_freq'], p['hy_pe_w2'], p['hy_pe_b2'].reshape(1, HY_FFN),
      p['hy_pe_w3'], p['hy_decay'].reshape(1, nf))


def _hyena_branch(z, p, B, L):
    v, x1, x2 = _hy_prep(z, p['hy_conv_w'], p['hy_conv_b'], L)
    Lp = max(L, 2048)
    fa, fai, gfwd, ginv = _fft_tables(Lp)
    hfilt, sumsq = _hyena_filters(L, p)
    pad3 = lambda a: a.reshape(B, L, HY_CH) if Lp == L else jnp.pad(a.reshape(B, L, HY_CH), ((0, 0), (0, Lp - L), (0, 0)))
    hf = hfilt if Lp == L else jnp.pad(hfilt, ((0, Lp - L), (0, 0)))
    s_filt = _fft_stage_a(fa, hf[None], F32)
    hspec = _fft_filter_spectrum(gfwd, s_filt, sumsq)
    y = pad3(v)
    for n, gate in enumerate((pad3(x1), pad3(x2))):
        s = _fft_stage_a(fa, y, BF16)
        bm = _fft_stage_c(gfwd, ginv, hspec, n, s)
        y = _fft_stage_a_inv(fai, bm, y, gate, p['hy_skip'][n])
    return y[:, :L].reshape(B * L, HY_CH)


def _merge_kernel(ya_ref, yb_ref, yc_ref, yd_ref, g0_ref, g1_ref, g2_ref, g3_ref, wup_ref, wout_ref,
                  x_ref, gate_ref, o_ref):
    acc = None
    for n, (y_ref, g_ref) in enumerate(((ya_ref, g0_ref), (yb_ref, g1_ref), (yc_ref, g2_ref), (yd_ref, g3_ref))):
        t = jax.nn.sigmoid(g_ref[...].astype(F32)) * jnp.dot(y_ref[...], wup_ref[n], preferred_element_type=F32)
        acc = t if acc is None else acc + t
    yl = jnp.dot(acc.astype(BF16), wout_ref[...], preferred_element_type=F32)
    o_ref[...] = x_ref[...] + gate_ref[0] * yl


def _merge(ys, z, w_up, w_out, x, gate, seq_len):
    T, D = x.shape
    tm = min(512, seq_len)
    per = seq_len // tm
    Wy = ys[0].shape[1]
    yspec = pl.BlockSpec((tm, Wy), lambda i: (i, 0))
    gspecs = [pl.BlockSpec((tm, D), functools.partial(lambda i, n: (i, Z_GATE // D + n), n=n)) for n in range(4)]
    return pl.pallas_call(
        _merge_kernel,
        grid=(T // tm,),
        in_specs=[yspec] * 4 + gspecs + [
            pl.BlockSpec((4, Wy, D), lambda i: (0, 0, 0)), pl.BlockSpec((D, D), lambda i: (0, 0)),
            pl.BlockSpec((tm, D), lambda i: (i, 0)), pl.BlockSpec((1, 1, D), lambda i: (i // per, 0, 0))],
        out_specs=pl.BlockSpec((tm, D), lambda i: (i, 0)),
        out_shape=jax.ShapeDtypeStruct((T, D), F32),
        compiler_params=_cp(("parallel",)),
        name="merge_branches",
    )(*ys, z, z, z, z, w_up, w_out, x, gate)


def _mlp_kernel(x_ref, g_ref, sc_ref, sh_ref, w1_ref, b1_ref, w2_ref, b2_ref, gate_ref, fg_ref, o_ref,
                acc_ref, *, final):
    k = pl.program_id(1)

    @pl.when(k == 0)
    def _():
        acc_ref[...] = jnp.zeros_like(acc_ref)

    x = x_ref[...]
    h = (_rms(x, g_ref[...]) * (1.0 + sc_ref[0]) + sh_ref[0]).astype(BF16)
    a = jnp.maximum(jnp.dot(h, w1_ref[...], preferred_element_type=F32) + b1_ref[...], 0.0)
    acc = acc_ref[...] + jnp.dot((a * a).astype(BF16), w2_ref[...], preferred_element_type=F32)
    acc_ref[...] = acc
    out = x + gate_ref[0] * (acc + b2_ref[...])
    if final:
        out = _rms(out, fg_ref[...])
    o_ref[...] = out


def _mlp(x, g, sc, sh, w1, b1, w2, b2, gate, final_g, seq_len, final):
    T, D = x.shape
    F = w1.shape[1]
    tm = min(1024, seq_len)
    per = seq_len // tm
    tk = 1024
    nk = F // tk
    row = lambda i, k: (i // per, 0, 0)
    return pl.pallas_call(
        functools.partial(_mlp_kernel, final=final),
        grid=(T // tm, nk),
        in_specs=[pl.BlockSpec((tm, D), lambda i, k: (i, 0)), pl.BlockSpec((1, D), lambda i, k: (0, 0)),
                  pl.BlockSpec((1, 1, D), row), pl.BlockSpec((1, 1, D), row),
                  pl.BlockSpec((D, tk), lambda i, k: (0, k)), pl.BlockSpec((1, tk), lambda i, k: (0, k)),
                  pl.BlockSpec((tk, D), lambda i, k: (k, 0)), pl.BlockSpec((1, D), lambda i, k: (0, 0)),
                  pl.BlockSpec((1, 1, D), row), pl.BlockSpec((1, D), lambda i, k: (0, 0))],
        out_specs=pl.BlockSpec((tm, D), lambda i, k: (i, 0)),
        out_shape=jax.ShapeDtypeStruct((T, D), F32),
        scratch_shapes=[pltpu.VMEM((tm, D), F32)],
        compiler_params=_cp(("parallel", "arbitrary")),
        name="mlp",
    )(x, g.reshape(1, D), sc, sh, w1, b1.reshape(1, F), w2, b2.reshape(1, D), gate, final_g.reshape(1, D))


def _pack_w_in(w_in):
    hy_e = 3 * HY_CH
    ga_e = hy_e + (GA_HEADS + 2 * GA_KV) * GA_HD
    mw = ML_HEADS * ML_HD
    ml_e = ga_e + 4 * mw + 16
    wa_e = ml_e + (WA_HEADS + 2 * WA_KV) * WA_HD
    hy, ga = w_in[:, :hy_e], w_in[:, hy_e:ga_e]
    ml = w_in[:, ga_e:ml_e]
    wa = w_in[:, ml_e:wa_e]
    gate = w_in[:, wa_e:]
    waq, wakv = wa[:, :WA_HEADS * WA_HD], wa[:, WA_HEADS * WA_HD:]
    pad = jnp.zeros((w_in.shape[0], Z_COLS - Z_WAKV - wakv.shape[1]), w_in.dtype)
    packed = jnp.concatenate([hy, waq, ga, ml[:, :2 * mw], ml[:, 2 * mw:3 * mw], ml[:, 3 * mw:4 * mw], gate, wakv, pad],
                             axis=1)
    wg = jnp.pad(ml[:, 4 * mw:], ((0, 0), (0, LANES - 16)))
    return packed.astype(BF16), wg


def _token_mixers(zl, zc, gl, gc, p, B, L, Lc, with_ctx_out):
    ya_l = _hyena_branch(zl, p, B, L)
    ya_c = _hyena_branch(zc, p, B, Lc) if with_ctx_out else None
    cos, sin = _rope_tables(L, GA_HD)
    ql, kl, vl = _ga_prep(zl, cos, sin, p['ga_q_g'], p['ga_k_g'], B, L, True)
    qc, kc, vc = _ga_prep(zc, cos[:Lc], sin[:Lc], p['ga_q_g'], p['ga_k_g'], B, Lc, False)
    r3 = lambda a, n: a.reshape(B, n, a.shape[-1])
    k_all = jnp.concatenate([r3(kl, L), r3(kc, Lc)], axis=1)
    vt_all = jnp.concatenate([vl, vc], axis=2)
    yb_l = _global_attention(ql, k_all, vt_all).reshape(B * L, -1)
    yb_c = _global_attention(qc, r3(kc, Lc), vc).reshape(B * Lc, -1) if with_ctx_out else None
    mq_l, mk_l, mv_l = _ml_prep(zl, p['ml_conv_w'], p['ml_conv_b'], B, L)
    mq_c, mk_c, mv_c = _ml_prep(zc, p['ml_conv_w'], p['ml_conv_b'], B, Lc)
    gb = jnp.pad(p['ml_gate_b'].astype(F32), (0, LANES - 16)).reshape(1, LANES)
    c0 = jnp.zeros((B, ML_HEADS, 2 * ML_HD, ML_HD), F32)
    m0 = jnp.zeros((B, ML_HEADS, 8, LANES), F32)
    zl3, zc3, gl3, gc3 = r3(zl, L), r3(zc, Lc), r3(gl, L), r3(gc, Lc)
    h_cf, cf, mf = _mlstm_scan(mq_c, r3(mk_c, Lc), mv_c, zc3, gc3, gb, c0, m0, False)
    yc_c, cb, mb = _mlstm_scan(mq_c, r3(mk_c, Lc), mv_c, zc3, gc3, gb, c0, m0, True, h_cf, p['ml_norm_g'])
    h_lf, _, _ = _mlstm_scan(mq_l, r3(mk_l, L), mv_l, zl3, gl3, gb, cf, mf, False)
    yc_l, _, _ = _mlstm_scan(mq_l, r3(mk_l, L), mv_l, zl3, gl3, gb, cb, mb, True, h_lf, p['ml_norm_g'])
    yc_l = yc_l.reshape(B * L, -1)
    yc_c = yc_c.reshape(B * Lc, -1)
    cosw, sinw = _rope_tables(L, WA_HD)
    wq_l, wk_l, wv_l = _wa_prep(zl, cosw, sinw, B, L, True)
    wq_c, wk_c, wv_c = _wa_prep(zc, cosw[:Lc], sinw[:Lc], B, Lc, False)
    yd_l = _window_attention(wq_l, r3(wk_c, Lc), wv_c, p['wa_sink'], r3(wk_l, L), wv_l).reshape(B * L, -1)
    yd_c = (_window_attention(wq_c, r3(wk_c, Lc), wv_c, p['wa_sink']).reshape(B * Lc, -1)
            if with_ctx_out else None)
    return (ya_l, yb_l, yc_l, yd_l), (ya_c, yb_c, yc_c, yd_c)


def kernel(x, c, ctx, c_ctx, w_mod, b_mod, ln1_g, ln2_g, w_in, hy_conv_w, hy_conv_b,
           hy_pe_w1, hy_pe_b1, hy_freq, hy_pe_w2, hy_pe_b2, hy_pe_w3, hy_decay, hy_skip,
           ga_q_g, ga_k_g, ml_conv_w, ml_conv_b, ml_gate_b, ml_norm_g, wa_sink, w_up, w_out,
           mlp_w1, mlp_b1, mlp_w2, mlp_b2, final_g):
    B, L, D = x.shape
    Lc = ctx.shape[1]
    R = -(-(B + 1) // 8) * 8
    cvec = jnp.zeros((R, D), F32).at[:B].set(c).at[B].set(c_ctx)
    mod = _modulation(cvec, w_mod, b_mod)
    xl = x.reshape(B * L, D)
    xc = ctx.reshape(B * Lc, D)
    for l in range(DEPTH):
        with_ctx_out = l < DEPTH - 1
        p = dict(hy_conv_w=hy_conv_w[l], hy_conv_b=hy_conv_b[l], hy_pe_w1=hy_pe_w1[l],
                 hy_pe_b1=hy_pe_b1[l], hy_freq=hy_freq[l], hy_pe_w2=hy_pe_w2[l], hy_pe_b2=hy_pe_b2[l],
                 hy_pe_w3=hy_pe_w3[l], hy_decay=hy_decay[l], hy_skip=hy_skip[l],
                 ga_q_g=ga_q_g[l], ga_k_g=ga_k_g[l], ml_conv_w=ml_conv_w[l], ml_conv_b=ml_conv_b[l],
                 ml_gate_b=ml_gate_b[l], ml_norm_g=ml_norm_g[l], wa_sink=wa_sink[l])
        ml_rows = mod[l, :B].reshape(B, 1, 6 * D)
        mc_rows = jnp.broadcast_to(mod[l, B].reshape(1, 1, 6 * D), (B, 1, 6 * D))
        part = lambda m, n: m[:, :, n * D:(n + 1) * D]
        w_pack, w_gate = _pack_w_in(w_in[l])
        wg = w_gate.astype(BF16)
        zl, gl = _normmod_matmul(xl, ln1_g[l], part(ml_rows, 1), part(ml_rows, 0), w_pack, wg, L, Z_TN)
        zc, gc = _normmod_matmul(xc, ln1_g[l], part(mc_rows, 1), part(mc_rows, 0), w_pack, wg, Lc, Z_TN)
        ys_l, ys_c = _token_mixers(zl, zc, gl, gc, p, B, L, Lc, with_ctx_out)
        wup = w_up[l].astype(BF16)
        wout = w_out[l].astype(BF16)
        w1, w2 = mlp_w1[l].astype(BF16), mlp_w2[l].astype(BF16)
        xl = _merge(ys_l, zl, wup, wout, xl, part(ml_rows, 2), L)
        xl = _mlp(xl, ln2_g[l], part(ml_rows, 4), part(ml_rows, 3), w1, mlp_b1[l], w2, mlp_b2[l],
                  part(ml_rows, 5), final_g, L, final=(l == DEPTH - 1))
        if with_ctx_out:
            xc = _merge(ys_c, zc, wup, wout, xc, part(mc_rows, 2), Lc)
            xc = _mlp(xc, ln2_g[l], part(mc_rows, 4), part(mc_rows, 3), w1, mlp_b1[l], w2, mlp_b2[l],
                      part(mc_rows, 5), final_g, Lc, final=False)
    return xl.reshape(B, L, D)
```

```python
import functools
import math

import jax
import jax.numpy as jnp
from jax import lax
from jax.experimental import pallas as pl
from jax.experimental.pallas import tpu as pltpu

F32 = jnp.float32
BF16 = jnp.bfloat16
HIGHEST = lax.Precision.HIGHEST

D_MODEL = 1024
DEPTH = 2
GRID_W = 64
HY_CH = 512
HY_ORDER = 2
HY_BANDS = 16
HY_EMB = 1 + 2 * HY_BANDS
HY_FFN = 64
GA_HEADS, GA_KV, GA_HD = 4, 2, 128
ML_HEADS, ML_HD = 4, 128
WA_HEADS, WA_KV, WA_HD = 8, 2, 64
WINDOW = 128
ROPE_BASE = 10000.0
D_FF = 4 * D_MODEL
EPS = 1e-6
NEG = -1e30
LOG2E = 1.4426950408889634

LANES = 128
V7X_VMEM_LIMIT = 48 * 1024 * 1024

Z_HY = 0
Z_WAQ = 1536
Z_GA = 2048
Z_MLQK = 3072
Z_MLV = 4096
Z_MLO = 4608
Z_GATE = 5120
Z_WAKV = 9216
Z_COLS = 9728
Z_TN = 2432

ML_CHUNK = 256
FFT_N2 = 128
GA_CB = 256
WA_PAIR = 2
WA_SUB = 8
GA_TK = 256
GA_CARRY = 4
GA_UNROLL = 32


def _cp(sem, vmem=V7X_VMEM_LIMIT):
    return pltpu.CompilerParams(dimension_semantics=sem, vmem_limit_bytes=vmem)


def _rms(x, g):
    return x * lax.rsqrt(jnp.mean(x * x, axis=-1, keepdims=True) + EPS) * g


def _mod_kernel(c_ref, w_ref, b_ref, o_ref):
    c = c_ref[...]
    s = c * jax.nn.sigmoid(c)
    o_ref[0] = jnp.dot(s, w_ref[0], preferred_element_type=F32, precision=HIGHEST) + b_ref[0]


def _modulation(cvec, w_mod, b_mod):
    R = cvec.shape[0]
    tn = 1536
    return pl.pallas_call(
        _mod_kernel,
        grid=(DEPTH, 6 * D_MODEL // tn),
        in_specs=[pl.BlockSpec((R, D_MODEL), lambda l, j: (0, 0)),
                  pl.BlockSpec((1, D_MODEL, tn), lambda l, j: (l, 0, j)),
                  pl.BlockSpec((1, 1, tn), lambda l, j: (l, 0, j))],
        out_specs=pl.BlockSpec((1, R, tn), lambda l, j: (l, 0, j)),
        out_shape=jax.ShapeDtypeStruct((DEPTH, R, 6 * D_MODEL), F32),
        compiler_params=_cp(("parallel", "parallel")),
        name="modulation",
    )(cvec, w_mod, b_mod.reshape(DEPTH, 1, 6 * D_MODEL))


def _nmm_kernel(x_ref, g_ref, sc_ref, sh_ref, w_ref, wx_ref, o_ref, ox_ref, h_ref):
    @pl.when(pl.program_id(1) == 0)
    def _():
        y = _rms(x_ref[...], g_ref[...])
        h_ref[...] = (y * (1.0 + sc_ref[0]) + sh_ref[0]).astype(BF16)
        ox_ref[...] = jnp.dot(h_ref[...], wx_ref[...], preferred_element_type=F32)

    o_ref[...] = jnp.dot(h_ref[...], w_ref[...], preferred_element_type=F32).astype(o_ref.dtype)


def _normmod_matmul(x, g, sc, sh, w, wx, seq_len, tn):
    T, D = x.shape
    N = w.shape[1]
    NX = wx.shape[1]
    tm = min(1024, seq_len)
    per = seq_len // tm
    return pl.pallas_call(
        _nmm_kernel,
        grid=(T // tm, N // tn),
        in_specs=[pl.BlockSpec((tm, D), lambda i, j: (i, 0)),
                  pl.BlockSpec((1, D), lambda i, j: (0, 0)),
                  pl.BlockSpec((1, 1, D), lambda i, j: (i // per, 0, 0)),
                  pl.BlockSpec((1, 1, D), lambda i, j: (i // per, 0, 0)),
                  pl.BlockSpec((D, tn), lambda i, j: (0, j)),
                  pl.BlockSpec((D, NX), lambda i, j: (0, 0))],
        out_specs=[pl.BlockSpec((tm, tn), lambda i, j: (i, j)), pl.BlockSpec((tm, NX), lambda i, j: (i, 0))],
        out_shape=[jax.ShapeDtypeStruct((T, N), BF16), jax.ShapeDtypeStruct((T, NX), F32)],
        scratch_shapes=[pltpu.VMEM((tm, D), BF16)],
        compiler_params=_cp(("parallel", "arbitrary")),
        name="normmod_matmul",
    )(x, g.reshape(1, D), sc, sh, w, wx)


def _conv3(u, prev_row, next_row, w_ref, b_ref, c0, c1):
    tm = u.shape[0]
    row = lax.broadcasted_iota(jnp.int32, u.shape, 0)
    up = jnp.where(row == 0, prev_row, pltpu.roll(u, 1, 0))
    dn = jnp.where(row == tm - 1, next_row, pltpu.roll(u, tm - 1, 0))
    return (w_ref[0:1, c0:c1] * up + w_ref[1:2, c0:c1] * u + w_ref[2:3, c0:c1] * dn + b_ref[0:1, c0:c1])


def _halo_rows(zp_ref, zn_ref, per, c0, c1):
    i = pl.program_id(0)
    first = (i % per) == 0
    last = (i % per) == per - 1
    hp = zp_ref.shape[0]
    prev_row = jnp.where(first, 0.0, zp_ref[hp - 1:hp, c0:c1].astype(F32))
    next_row = jnp.where(last, 0.0, zn_ref[0:1, c0:c1].astype(F32))
    return prev_row, next_row


HALO = 16


def _halo_specs(tm, width, col_block, n_rows):
    nb = n_rows // HALO
    r = tm // HALO
    return [pl.BlockSpec((tm, width), lambda i: (i, col_block)),
            pl.BlockSpec((HALO, width), lambda i: (jnp.maximum(i * r - 1, 0), col_block)),
            pl.BlockSpec((HALO, width), lambda i: (jnp.minimum((i + 1) * r, nb - 1), col_block))]


def _hy_prep_kernel(z_ref, zp_ref, zn_ref, w_ref, b_ref, v_ref, x1_ref, x2_ref, *, per):
    outs = (v_ref, x1_ref, x2_ref)
    for c in range(3):
        c0, c1 = c * HY_CH, (c + 1) * HY_CH
        prev_row, next_row = _halo_rows(zp_ref, zn_ref, per, c0, c1)
        u = z_ref[:, c0:c1].astype(F32)
        outs[c][...] = _conv3(u, prev_row, next_row, w_ref, b_ref, c0, c1).astype(outs[c].dtype)


def _hy_prep(z, conv_w, conv_b, seq_len):
    T = z.shape[0]
    tm = min(512, seq_len)
    W = 3 * HY_CH
    out = jax.ShapeDtypeStruct((T, HY_CH), BF16)
    return pl.pallas_call(
        functools.partial(_hy_prep_kernel, per=seq_len // tm),
        grid=(T // tm,),
        in_specs=_halo_specs(tm, W, Z_HY // W, T) + [
            pl.BlockSpec((3, W), lambda i: (0, 0)), pl.BlockSpec((1, W), lambda i: (0, 0))],
        out_specs=[pl.BlockSpec((tm, HY_CH), lambda i: (i, 0))] * 3,
        out_shape=[out, out, out],
        compiler_params=_cp(("parallel",)),
        name="hyena_prep",
    )(z, z, z, conv_w, conv_b.reshape(1, W))


def _ml_prep_kernel(z_ref, zp_ref, zn_ref, zv_ref, w_ref, b_ref, q_ref, k_ref, v_ref, *, per):
    W = ML_HEADS * ML_HD
    for c in range(2):
        c0, c1 = c * W, (c + 1) * W
        prev_row, next_row = _halo_rows(zp_ref, zn_ref, per, c0, c1)
        u = z_ref[:, c0:c1].astype(F32)
        y = _conv3(u, prev_row, next_row, w_ref, b_ref, c0, c1)
        y = y * jax.nn.sigmoid(y)
        if c == 0:
            for h in range(ML_HEADS):
                q_ref[0, h * ML_HD:(h + 1) * ML_HD, :] = y[:, h * ML_HD:(h + 1) * ML_HD].T.astype(q_ref.dtype)
        else:
            k_ref[...] = (y * (ML_HD ** -0.5)).astype(k_ref.dtype)
    for h in range(ML_HEADS):
        v_ref[0, h * ML_HD:(h + 1) * ML_HD, :] = zv_ref[:, h * ML_HD:(h + 1) * ML_HD].astype(F32).T.astype(v_ref.dtype)


def _ml_prep(z, conv_w, conv_b, B, seq_len):
    T = z.shape[0]
    tm = min(512, seq_len)
    per = seq_len // tm
    W = 2 * ML_HEADS * ML_HD
    Wh = W // 2
    tspec = pl.BlockSpec((1, Wh, tm), lambda i: (i // per, 0, i % per))
    tshape = jax.ShapeDtypeStruct((B, Wh, seq_len), BF16)
    return pl.pallas_call(
        functools.partial(_ml_prep_kernel, per=per),
        grid=(T // tm,),
        in_specs=_halo_specs(tm, W, Z_MLQK // W, T) + [
            pl.BlockSpec((tm, Wh), lambda i: (i, Z_MLV // Wh)),
            pl.BlockSpec((3, W), lambda i: (0, 0)), pl.BlockSpec((1, W), lambda i: (0, 0))],
        out_specs=[tspec, pl.BlockSpec((tm, Wh), lambda i: (i, 0)), tspec],
        out_shape=[tshape, jax.ShapeDtypeStruct((T, Wh), BF16), tshape],
        compiler_params=_cp(("parallel",)),
        name="mlstm_prep",
    )(z, z, z, z, conv_w, conv_b.reshape(1, W))


def _rope_tables(L, hd):
    quarter = hd // 4
    inv = ROPE_BASE ** (-jnp.arange(quarter, dtype=F32) / quarter)
    t = jnp.arange(L)
    row = (t // GRID_W).astype(F32)
    col = (t % GRID_W).astype(F32)
    lane = jnp.arange(LANES)
    within = lane % hd
    is_col = (within // (hd // 2)) == 1
    second = ((within % (hd // 2)) // quarter) == 1
    j = within % quarter
    pos = jnp.where(is_col[None, :], col[:, None], row[:, None])
    ang = pos * inv[j][None, :]
    return jnp.cos(ang), jnp.where(second[None, :], jnp.sin(ang), -jnp.sin(ang))


def _rope(x, cos, sin, quarter):
    src = lax.broadcasted_iota(jnp.int32, (LANES, LANES), 0)
    dst = lax.broadcasted_iota(jnp.int32, (LANES, LANES), 1)
    first = ((dst % (2 * quarter)) // quarter) == 0
    perm = jnp.where(src == jnp.where(first, dst + quarter, dst - quarter), 1.0, 0.0).astype(BF16)
    partner = jnp.dot(x.astype(BF16), perm, preferred_element_type=F32)
    return x * cos + partner * sin


def _rms_lanes(x, g):
    ss = jnp.dot((x * x).astype(BF16), jnp.ones((LANES, LANES), BF16), preferred_element_type=F32)
    return x * lax.rsqrt(ss * (1.0 / LANES) + EPS) * g


def _transpose_bf16(x):
    r = lax.broadcasted_iota(jnp.int32, (LANES, LANES), 0)
    c = lax.broadcasted_iota(jnp.int32, (LANES, LANES), 1)
    eye = jnp.where(r == c, 1.0, 0.0).astype(BF16)
    return lax.dot_general(eye, x.astype(BF16), (((1,), (1,)), ((), ())), preferred_element_type=F32).astype(BF16)


def _ga_prep_kernel(z_ref, cos_ref, sin_ref, qg_ref, kg_ref, q_ref, k_ref, v_ref, *, rope):
    nq, nk = GA_HEADS, GA_KV
    heads = range(nq + nk)
    for h in range(nk):
        v_ref[0, h * GA_HD:(h + 1) * GA_HD, :] = _transpose_bf16(
            z_ref[:, (nq + nk + h) * GA_HD:(nq + nk + h + 1) * GA_HD])
    y = [_rms_lanes(z_ref[:, h * GA_HD:(h + 1) * GA_HD].astype(F32), qg_ref[...] if h < nq else kg_ref[...])
         for h in heads]
    if rope:
        y = [_rope(y[h], cos_ref[...], sin_ref[...], GA_HD // 4) for h in heads]
    for h in heads:
        if h < nq:
            q_ref[0, h * GA_HD:(h + 1) * GA_HD, :] = _transpose_bf16(y[h] * (GA_HD ** -0.5 * LOG2E))
        else:
            k_ref[:, (h - nq) * GA_HD:(h - nq + 1) * GA_HD] = y[h].astype(k_ref.dtype)


def _ga_prep(z, cos, sin, qg, kg, B, seq_len, rope):
    T = z.shape[0]
    tm = min(512, seq_len)
    per = seq_len // tm
    W = (GA_HEADS + 2 * GA_KV) * GA_HD
    return pl.pallas_call(
        functools.partial(_ga_prep_kernel, rope=rope),
        grid=(T // tm,),
        in_specs=[pl.BlockSpec((tm, W), lambda i: (i, Z_GA // W)),
                  pl.BlockSpec((tm, LANES), lambda i: (i % per, 0)),
                  pl.BlockSpec((tm, LANES), lambda i: (i % per, 0)),
                  pl.BlockSpec((1, GA_HD), lambda i: (0, 0)),
                  pl.BlockSpec((1, GA_HD), lambda i: (0, 0))],
        out_specs=[pl.BlockSpec((1, GA_HEADS * GA_HD, tm), lambda i: (i // per, 0, i % per)),
                   pl.BlockSpec((tm, GA_KV * GA_HD), lambda i: (i, 0)),
                   pl.BlockSpec((1, GA_KV * GA_HD, tm), lambda i: (i // per, 0, i % per))],
        out_shape=[jax.ShapeDtypeStruct((B, GA_HEADS * GA_HD, seq_len), BF16),
                   jax.ShapeDtypeStruct((T, GA_KV * GA_HD), BF16),
                   jax.ShapeDtypeStruct((B, GA_KV * GA_HD, seq_len), BF16)],
        compiler_params=_cp(("parallel",)),
        name="global_attn_prep",
    )(z, cos, sin, qg.reshape(1, GA_HD), kg.reshape(1, GA_HD))


def _wa_prep_kernel(zq_ref, zkv_ref, cos_ref, sin_ref, q_ref, k_ref, v_ref, *, rope):
    quarter = WA_HD // 4
    nqb = WA_HEADS * WA_HD // LANES
    v_ref[0] = _transpose_bf16(zkv_ref[:, LANES:])
    x = [zq_ref[:, j * LANES:(j + 1) * LANES].astype(F32) for j in range(nqb)] + [zkv_ref[:, 0:LANES].astype(F32)]
    if rope:
        x = [_rope(xj, cos_ref[...], sin_ref[...], quarter) for xj in x]
    for j in range(nqb):
        q_ref[0, j * LANES:(j + 1) * LANES, :] = _transpose_bf16(x[j] * (WA_HD ** -0.5 * LOG2E))
    k_ref[...] = x[nqb].astype(k_ref.dtype)


def _wa_prep(z, cos, sin, B, seq_len, rope):
    T = z.shape[0]
    tm = min(512, seq_len)
    per = seq_len // tm
    WQ = WA_HEADS * WA_HD
    return pl.pallas_call(
        functools.partial(_wa_prep_kernel, rope=rope),
        grid=(T // tm,),
        in_specs=[pl.BlockSpec((tm, WQ), lambda i: (i, Z_WAQ // WQ)),
                  pl.BlockSpec((tm, 2 * LANES), lambda i: (i, Z_WAKV // (2 * LANES))),
                  pl.BlockSpec((tm, LANES), lambda i: (i % per, 0)),
                  pl.BlockSpec((tm, LANES), lambda i: (i % per, 0))],
        out_specs=[pl.BlockSpec((1, WQ, tm), lambda i: (i // per, 0, i % per)),
                   pl.BlockSpec((tm, LANES), lambda i: (i, 0)),
                   pl.BlockSpec((1, LANES, tm), lambda i: (i // per, 0, i % per))],
        out_shape=[jax.ShapeDtypeStruct((B, WQ, seq_len), BF16),
                   jax.ShapeDtypeStruct((T, LANES), BF16),
                   jax.ShapeDtypeStruct((B, LANES, seq_len), BF16)],
        compiler_params=_cp(("parallel",)),
        name="window_attn_prep",
    )(z, z, cos, sin)


def _ga_kernel(q_ref, k_ref, vt_ref, o_ref, acc_ref, m_ref, *, nchunks, tk, tq):
    acc_ref[...] = jnp.zeros_like(acc_ref)
    m_ref[...] = jnp.full_like(m_ref, NEG)
    qt = jnp.concatenate([q_ref[0, 0:GA_HD, :], q_ref[0, GA_HD:, :]], axis=1)

    W = GA_CB
    nblk = 2 * tq // W

    def scores(k, blocks=None):
        blocks = range(nblk) if blocks is None else blocks
        return tuple(jnp.dot(k, qt[:, i * W:(i + 1) * W], preferred_element_type=F32) for i in blocks)

    def softmax_pv(s_blocks, vt):
        cols = [slice(i * W, (i + 1) * W) for i in range(nblk)]
        m_old = [m_ref[:, cs] for cs in cols]
        m_new = [jnp.maximum(mo, jnp.max(s, axis=0, keepdims=True)) for mo, s in zip(m_old, s_blocks)]
        alpha = [jnp.exp2(mo - mn) for mo, mn in zip(m_old, m_new)]
        vaug = jnp.concatenate([vt, jnp.ones((16, vt.shape[1]), BF16)], axis=0)
        pv = [jnp.dot(vaug, jnp.exp2((s - mn).astype(BF16)), preferred_element_type=F32)
              for s, mn in zip(s_blocks, m_new)]
        for i, cs in enumerate(cols):
            acc_ref[:, cs] = alpha[i] * acc_ref[:, cs] + pv[i]
            m_ref[:, cs] = m_new[i]

    ahead, late = range(min(GA_CARRY, nblk)), range(min(GA_CARRY, nblk), nblk)

    def body(j, s):
        off = pl.multiple_of(j * tk, tk)
        s_late = scores(k_ref[0, pl.ds(off, tk), :], late)
        s_next = scores(k_ref[0, pl.ds(pl.multiple_of((j + 1) * tk, tk), tk), :], ahead)
        softmax_pv(s + s_late, vt_ref[0, :, pl.ds(off, tk)])
        return s_next
    s = scores(k_ref[0, 0:tk, :], ahead)
    if nchunks > 1:
        s = lax.fori_loop(0, nchunks - 1, body, s, unroll=GA_UNROLL if (nchunks - 1) % GA_UNROLL == 0 else 1)
    last = slice((nchunks - 1) * tk, nchunks * tk)
    softmax_pv(s + scores(k_ref[0, last, :], late), vt_ref[0, :, last])
    o = acc_ref[0:GA_HD, :] / acc_ref[GA_HD:GA_HD + 1, :]
    for h in range(2):
        o_ref[0, :, h * GA_HD:(h + 1) * GA_HD] = o[:, h * tq:(h + 1) * tq].T.astype(o_ref.dtype)


def _global_attention(qt, k, vt):
    B, _, Lq = qt.shape
    Lk = k.shape[1]
    tq = min(512, Lq)
    tk = min(GA_TK, Lk)
    W = 2 * GA_HD
    return pl.pallas_call(
        functools.partial(_ga_kernel, nchunks=Lk // tk, tk=tk, tq=tq),
        grid=(B, GA_KV, Lq // tq),
        in_specs=[pl.BlockSpec((1, W, tq), lambda b, g, i: (b, g, i)),
                  pl.BlockSpec((1, Lk, GA_HD), lambda b, g, i: (b, 0, g)),
                  pl.BlockSpec((1, GA_HD, Lk), lambda b, g, i: (b, g, 0))],
        out_specs=pl.BlockSpec((1, tq, W), lambda b, g, i: (b, i, g)),
        out_shape=jax.ShapeDtypeStruct((B, Lq, GA_HEADS * GA_HD), BF16),
        scratch_shapes=[pltpu.VMEM((GA_HD + 16, 2 * tq), F32), pltpu.VMEM((1, 2 * tq), F32)],
        compiler_params=_cp(("parallel", "parallel", "parallel")),
        name="global_attention",
    )(qt, k, vt)


def _wa_kernel(*refs, band, nq, nsub):
    if band:
        q_ref = refs[0]
        kb = refs[1:nsub + 3]
        vb = refs[nsub + 3:2 * nsub + 5]
        kc_ref, vc_ref, sink_ref, o_ref = refs[2 * nsub + 5:]
    else:
        q_ref, kc_ref, vc_ref, sink_ref, o_ref = refs
    tq = WINDOW
    Lc = kc_ref.shape[1]
    step = pl.program_id(1)
    G = WA_HEADS // WA_KV
    cols = WA_PAIR * tq
    npr = WA_HEADS // WA_PAIR
    grp = [(pr * WA_PAIR) // G for pr in range(npr)]
    zeros = jnp.zeros((WA_HD, cols), BF16)
    if band:
        nb = 3 * tq
        c = lax.broadcasted_iota(jnp.int32, (nb + Lc, cols), 0)
        r = lax.broadcasted_iota(jnp.int32, (nb + Lc, cols), 1) % tq
        in_band = (c >= r) & (c <= r + 2 * WINDOW)
    ones_rows = jnp.ones((16, (3 * tq if band else 0) + Lc), BF16)
    chains = [(u, pr) for u in range(nsub) for pr in range(npr)]
    keys, vals, valid = [], [], []
    for u in range(nsub):
        if band:
            keys.append(jnp.concatenate([kb[u][0], kb[u + 1][0], kb[u + 2][0], kc_ref[0]], axis=0))
            vals.append(jnp.concatenate([vb[u][0], vb[u + 1][0], vb[u + 2][0], vc_ref[0]], axis=1))
            qi = step * nsub + u
            lo = jnp.where(qi == 0, tq, 0)
            hi = jnp.where(qi == nq - 1, 2 * tq, nb)
            valid.append((in_band & (c >= lo) & (c < hi)) | (c >= nb))
        else:
            keys.append(kc_ref[0])
            vals.append(vc_ref[0])
    s = {}
    for u, pr in chains:
        qg = jnp.concatenate([q_ref[0, (WA_PAIR * pr + h) * WA_HD:(WA_PAIR * pr + h + 1) * WA_HD, u * tq:(u + 1) * tq]
                              for h in range(WA_PAIR)], axis=1)
        qpad = jnp.concatenate([qg, zeros] if grp[pr] == 0 else [zeros, qg], axis=0)
        sp = jnp.dot(keys[u], qpad, preferred_element_type=F32)
        s[u, pr] = jnp.where(valid[u], sp, NEG) if band else sp
    sink = [sink_ref[pr] for pr in range(npr)]
    m = {ch: jnp.maximum(jnp.max(s[ch], axis=0, keepdims=True), sink[ch[1]]) for ch in chains}
    p = {ch: jnp.exp2(s[ch] - m[ch]).astype(BF16) for ch in chains}
    R = {}
    for u, pr in chains:
        vaug = jnp.concatenate([vals[u][grp[pr] * WA_HD:(grp[pr] + 1) * WA_HD, :], ones_rows], axis=0)
        R[u, pr] = jnp.dot(vaug, p[u, pr], preferred_element_type=F32)
    for u, pr in chains:
        o = R[u, pr][:WA_HD] / (R[u, pr][WA_HD:WA_HD + 1] + jnp.exp2(sink[pr] - m[u, pr]))
        ot = jnp.concatenate([o[:, h * tq:(h + 1) * tq] for h in range(WA_PAIR)], axis=0)
        wo = WA_PAIR * WA_HD
        o_ref[0, u * tq:(u + 1) * tq, pr * wo:(pr + 1) * wo] = ot.T.astype(o_ref.dtype)


def _window_attention(qt, kc, vtc, sink, kl=None, vtl=None):
    B, WQ, Lq = qt.shape
    Lc = kc.shape[1]
    tq = WINDOW
    nq = Lq // tq
    nsub = math.gcd(WA_SUB, nq)
    band = kl is not None
    npair = WA_HEADS // WA_PAIR
    sink_row = jnp.repeat(sink.astype(F32).reshape(npair, WA_PAIR) * LOG2E, tq, axis=1).reshape(npair, 1, WA_PAIR * tq)
    qspec = pl.BlockSpec((1, WQ, nsub * tq), lambda b, i: (b, 0, i))
    kcspec = pl.BlockSpec((1, Lc, LANES), lambda b, i: (b, 0, 0))
    vcspec = pl.BlockSpec((1, LANES, Lc), lambda b, i: (b, 0, 0))
    sspec = pl.BlockSpec((npair, 1, WA_PAIR * tq), lambda b, i: (0, 0, 0))
    if band:
        blk = lambda off: (lambda i: jnp.clip(i * nsub + off, 0, nq - 1))
        kspec = lambda f: pl.BlockSpec((1, tq, LANES), lambda b, i: (b, f(i), 0))
        vspec = lambda f: pl.BlockSpec((1, LANES, tq), lambda b, i: (b, 0, f(i)))
        offs = range(-1, nsub + 1)
        in_specs = ([qspec] + [kspec(blk(o)) for o in offs] + [vspec(blk(o)) for o in offs]
                    + [kcspec, vcspec, sspec])
        args = (qt,) + (kl,) * (nsub + 2) + (vtl,) * (nsub + 2) + (kc, vtc, sink_row)
    else:
        in_specs, args = [qspec, kcspec, vcspec, sspec], (qt, kc, vtc, sink_row)
    return pl.pallas_call(
        functools.partial(_wa_kernel, band=band, nq=nq, nsub=nsub),
        grid=(B, nq // nsub),
        in_specs=in_specs,
        out_specs=pl.BlockSpec((1, nsub * tq, WQ), lambda b, i: (b, i, 0)),
        out_shape=jax.ShapeDtypeStruct((B, Lq, WQ), BF16),
        compiler_params=_cp(("parallel", "parallel")),
        name="window_attention",
    )(*args)


def _mlstm_kernel(*refs, reverse, nc, bb):
    if reverse:
        (qt_ref, k_ref, vt_ref, g_ref, gb_ref, c0_ref, m0_ref, hft_ref, o_ref, gn_ref,
         y_ref, cf_ref, mf_ref, c_scr, m_scr) = refs
    else:
        (qt_ref, k_ref, vt_ref, g_ref, gb_ref, c0_ref, m0_ref,
         y_ref, cf_ref, mf_ref, c_scr, m_scr) = refs
    T = ML_CHUNK
    d = 1 if reverse else 0
    step = pl.program_id(1)

    @pl.when(step == 0)
    def _():
        c_scr[...] = c0_ref[...]
        m_scr[...] = m0_ref[...]

    si = lax.broadcasted_iota(jnp.int32, (T, T), 0)
    ti = lax.broadcasted_iota(jnp.int32, (T, T), 1)
    tri = ((ti >= si) if reverse else (ti <= si)).astype(F32)
    mask_t = (si >= ti) if reverse else (si <= ti)
    ones_rows = jnp.ones((ML_HD, T), BF16)
    e_last = 0 if reverse else T - 1
    chains = [(bi, h) for bi in range(bb) for h in range(ML_HEADS)]
    gates = []
    for bi in range(bb):
        G = g_ref[bi] + gb_ref[...]
        LF = jax.nn.log_sigmoid(G)
        Bc = jnp.dot(tri, LF, preferred_element_type=F32, precision=HIGHEST)
        gates.append((G.T, Bc.T, Bc - pltpu.roll(G, 4, 1)))
    st, ph = {}, {}
    for bi, h in chains:
        hs = slice(h * ML_HD, (h + 1) * ML_HD)
        st[bi, h] = jnp.dot(k_ref[bi, :, hs], qt_ref[bi, hs, :], preferred_element_type=F32)
    for bi, h in chains:
        GT, BT, Dc = gates[bi]
        fl, il = d * 8 + 4 + h, d * 8 + h
        b_row, i_row = BT[fl:fl + 1, :], GT[il:il + 1, :]
        log_d = jnp.where(mask_t, b_row - Dc[:, fl:fl + 1], NEG)
        m_prev = m_scr[bi, h, 0:1, 0:1]
        m_inter = b_row + m_prev
        m_t = jnp.maximum(m_inter, jnp.max(log_d, axis=0, keepdims=True))
        wqk = (st[bi, h] * jnp.exp(log_d - m_t)).astype(BF16)
        b_end = BT[fl:fl + 1, e_last:e_last + 1]
        log_w = b_end - b_row + i_row
        m_next = jnp.maximum(b_end + m_prev, jnp.max(log_w, axis=1, keepdims=True))
        ph[bi, h] = (wqk, jnp.exp(m_inter - m_t), jnp.exp(-m_t), jnp.exp(log_w - m_next),
                     jnp.exp(b_end + m_prev - m_next), m_next)
    for bi, h in chains:
        hs = slice(h * ML_HD, (h + 1) * ML_HD)
        wqk, cs, em, w_row, decay, m_next = ph[bi, h]
        kh, qt = k_ref[bi, :, hs], qt_ref[bi, hs, :]
        vaug = jnp.concatenate([vt_ref[bi, hs, :], ones_rows], axis=0)
        R = (jnp.dot(vaug, wqk, preferred_element_type=F32)
             + cs * jnp.dot(c_scr[bi, h].astype(BF16), qt, preferred_element_type=F32))
        hh = R[:ML_HD] / jnp.maximum(jnp.abs(R[ML_HD:]), em)
        if reverse:
            hsum = hft_ref[bi, hs, :] + hh
            hn = hsum * lax.rsqrt(jnp.mean(hsum * hsum, axis=0, keepdims=True) + EPS) * gn_ref[hs, :]
            y_ref[bi, :, hs] = (jax.nn.sigmoid(o_ref[bi, :, hs].astype(F32)) * hn.T).astype(y_ref.dtype)
        else:
            y_ref[bi, hs, :] = hh
        wv = (vaug.astype(F32) * w_row).astype(BF16)
        c_scr[bi, h] = decay * c_scr[bi, h] + jnp.dot(wv, kh, preferred_element_type=F32)
        m_scr[bi, h] = jnp.broadcast_to(m_next, (8, LANES))

    @pl.when(step == nc - 1)
    def _():
        cf_ref[...] = c_scr[...]
        mf_ref[...] = m_scr[...]


def _mlstm_scan(qt, k, vt, z, gates, gate_b, c0, m0, reverse, hft=None, norm_g=None):
    B, L, W = k.shape
    T = ML_CHUNK
    nc = L // T
    bb = 4 if B % 4 == 0 else 2
    cj =(lambda j: nc - 1 - j) if reverse else (lambda j: j)
    tok = pl.BlockSpec((bb, T, W), lambda b, j: (b, cj(j), 0))
    ttok = pl.BlockSpec((bb, W, T), lambda b, j: (b, 0, cj(j)))
    cspec = pl.BlockSpec((bb, ML_HEADS, 2 * ML_HD, ML_HD), lambda b, j: (b, 0, 0, 0))
    mspec = pl.BlockSpec((bb, ML_HEADS, 8, LANES), lambda b, j: (b, 0, 0, 0))
    in_specs = [ttok, tok, ttok, pl.BlockSpec((bb, T, LANES), lambda b, j: (b, cj(j), 0)),
                pl.BlockSpec((1, LANES), lambda b, j: (0, 0)), cspec, mspec]
    args = [qt, k, vt, gates, gate_b, c0, m0]
    if reverse:
        in_specs += [ttok, pl.BlockSpec((bb, T, W), lambda b, j: (b, cj(j), Z_MLO // W)),
                     pl.BlockSpec((W, T), lambda b, j: (0, 0))]
        args += [hft, z, jnp.broadcast_to(norm_g.astype(F32).reshape(W, 1), (W, T))]
    return pl.pallas_call(
        functools.partial(_mlstm_kernel, reverse=reverse, nc=nc, bb=bb),
        grid=(B // bb, nc),
        in_specs=in_specs,
        out_specs=[tok if reverse else ttok, cspec, mspec],
        out_shape=[jax.ShapeDtypeStruct((B, L, W), BF16) if reverse else jax.ShapeDtypeStruct((B, W, L), F32),
                   jax.ShapeDtypeStruct(c0.shape, F32), jax.ShapeDtypeStruct(m0.shape, F32)],
        scratch_shapes=[pltpu.VMEM((bb, ML_HEADS, 2 * ML_HD, ML_HD), F32), pltpu.VMEM((bb, ML_HEADS, 8, LANES), F32)],
        compiler_params=_cp(("parallel", "arbitrary")),
        name="mlstm_reverse" if reverse else "mlstm_forward",
    )(*args)


def _fft_dims(Lp):
    n1 = 2 * Lp // FFT_N2
    nt1 = Lp // FFT_N2
    nk1 = -(-(n1 // 2 + 1) // 8) * 8
    return n1, nt1, nk1


def _fft_tables(Lp):
    n1, nt1, nk1 = _fft_dims(Lp)
    N = 2 * Lp
    k1 = jnp.arange(nk1)
    t1 = jnp.arange(nt1)
    ang_a = (2.0 * math.pi / n1) * ((k1[:, None] * t1[None, :]) % n1).astype(F32)
    fa = jnp.stack([jnp.cos(ang_a), -jnp.sin(ang_a)], axis=1).reshape(2 * nk1, nt1)
    wgt = jnp.where((k1 == 0) | (k1 == n1 // 2), 1.0, 2.0) * (k1 <= n1 // 2) / N
    fai = jnp.stack([jnp.cos(ang_a) * wgt[:, None], -jnp.sin(ang_a) * wgt[:, None]], axis=1)
    fai = fai.reshape(2 * nk1, nt1).T
    k2 = jnp.arange(FFT_N2)
    t2 = jnp.arange(FFT_N2)
    idx = (t2[None, None, :] * k1[:, None, None] + n1 * t2[None, None, :] * k2[None, :, None]) % N
    phi = (2.0 * math.pi / N) * idx.astype(F32)
    gr, gi = jnp.cos(phi), -jnp.sin(phi)
    gfwd = jnp.concatenate([jnp.concatenate([gr, -gi], axis=2), jnp.concatenate([gi, gr], axis=2)], axis=1)
    grt, git = jnp.swapaxes(gr, 1, 2), jnp.swapaxes(gi, 1, 2)
    ginv = jnp.concatenate([jnp.concatenate([grt, git], axis=2), jnp.concatenate([-git, grt], axis=2)], axis=1)
    eye = jnp.eye(FFT_TK, dtype=F32)
    return (jnp.kron(fa, eye).astype(BF16), jnp.kron(fai, eye).astype(BF16), gfwd.astype(BF16), ginv.astype(BF16))


FFT_TB = 32
FFT_TK = 8
FFT_CB = 512


def _fa_kernel(fa_ref, x_ref, o_ref):
    nt1, tb, cb = x_ref.shape[1:]
    rows = o_ref.shape[1]
    x = x_ref[0].astype(F32).reshape(nt1, tb // FFT_TK, FFT_TK, cb)
    parts = []
    for j in range(tb // FFT_TK):
        xj = x[:, j].reshape(nt1 * FFT_TK, cb).astype(BF16)
        r = jnp.dot(fa_ref[...], xj, preferred_element_type=F32)
        parts.append(r.reshape(rows, 1, FFT_TK, cb))
    o_ref[0] = jnp.concatenate(parts, axis=1).reshape(rows, tb, cb).astype(o_ref.dtype)


def _fft_stage_a(fa, y, out_dtype):
    B, Lp, C = y.shape
    nt1 = Lp // FFT_N2
    rows = fa.shape[0] // FFT_TK
    return pl.pallas_call(
        _fa_kernel,
        grid=(B, FFT_N2 // FFT_TB, C // FFT_CB),
        in_specs=[pl.BlockSpec(fa.shape, lambda b, j, c: (0, 0)),
                  pl.BlockSpec((1, nt1, FFT_TB, FFT_CB), lambda b, j, c: (b, 0, j, c))],
        out_specs=pl.BlockSpec((1, rows, FFT_TB, FFT_CB), lambda b, j, c: (b, 0, j, c)),
        out_shape=jax.ShapeDtypeStruct((B, rows, FFT_N2, C), out_dtype),
        compiler_params=_cp(("parallel", "parallel", "parallel")),
        name="fft_stage_a",
    )(fa, y.reshape(B, nt1, FFT_N2, C))


def _fc_filter_kernel(g_ref, s_ref, ss_ref, h_ref, *, kb):
    C = HY_CH
    for i in range(kb):
        for n in range(HY_ORDER):
            cf, cb = (2 * n) * C, (2 * n + 1) * C
            scale = lax.rsqrt(ss_ref[0:1, cf:cf + C] + ss_ref[0:1, cb:cb + C] + EPS)
            sf = s_ref[0, 2 * i:2 * i + 2, :, cf:cf + C].reshape(2 * FFT_N2, C).astype(BF16)
            sb = s_ref[0, 2 * i:2 * i + 2, :, cb:cb + C].reshape(2 * FFT_N2, C).astype(BF16)
            xf = jnp.dot(g_ref[i], sf, preferred_element_type=F32)
            xb = jnp.dot(g_ref[i], sb, preferred_element_type=F32)
            h_ref[n, 2 * i] = (xf[:FFT_N2] + xb[:FFT_N2]) * scale
            h_ref[n, 2 * i + 1] = (xf[FFT_N2:] - xb[FFT_N2:]) * scale


def _fft_filter_spectrum(gfwd, s_filt, sumsq):
    rows = s_filt.shape[1]
    C4 = s_filt.shape[-1]
    kb = 2
    return pl.pallas_call(
        functools.partial(_fc_filter_kernel, kb=kb),
        grid=(rows // (2 * kb),),
        in_specs=[pl.BlockSpec((kb, 2 * FFT_N2, 2 * FFT_N2), lambda i: (i, 0, 0)),
                  pl.BlockSpec((1, 2 * kb, FFT_N2, C4), lambda i: (0, i, 0, 0)),
                  pl.BlockSpec((1, C4), lambda i: (0, 0))],
        out_specs=pl.BlockSpec((HY_ORDER, 2 * kb, FFT_N2, HY_CH), lambda i: (0, i, 0, 0)),
        out_shape=jax.ShapeDtypeStruct((HY_ORDER, rows, FFT_N2, HY_CH), F32),
        compiler_params=_cp(("parallel",)),
        name="fft_filter_spectrum",
    )(gfwd, s_filt, sumsq)


def _fc_kernel(g_ref, gi_ref, h_ref, s_ref, o_ref, *, kb):
    C = s_ref.shape[-1]
    x = [jnp.dot(g_ref[i], s_ref[0, 2 * i:2 * i + 2].reshape(2 * FFT_N2, C), preferred_element_type=F32)
         for i in range(kb)]
    z = []
    for i in range(kb):
        xr, xi = x[i][:FFT_N2], x[i][FFT_N2:]
        hr, hi = h_ref[0, 2 * i], h_ref[0, 2 * i + 1]
        z.append(jnp.concatenate([xr * hr - xi * hi, xr * hi + xi * hr], axis=0).astype(BF16))
    bm = [jnp.dot(gi_ref[i], z[i], preferred_element_type=F32) for i in range(kb)]
    for i in range(kb):
        o_ref[0, 2 * i:2 * i + 2] = bm[i].reshape(2, FFT_N2, C).astype(o_ref.dtype)


def _fft_stage_c(gfwd, ginv, hspec, order, s):
    B, rows, _, C = s.shape
    kb = 8 if rows % 16 == 0 else 4
    sspec =pl.BlockSpec((1, 2 * kb, FFT_N2, C), lambda i, b: (b, i, 0, 0))
    gspec = pl.BlockSpec((kb, 2 * FFT_N2, 2 * FFT_N2), lambda i, b: (i, 0, 0))
    return pl.pallas_call(
        functools.partial(_fc_kernel, kb=kb),
        grid=(rows // (2 * kb), B),
        in_specs=[gspec, gspec,
                  pl.BlockSpec((1, 2 * kb, FFT_N2, C), lambda i, b: (order, i, 0, 0)), sspec],
        out_specs=sspec,
        out_shape=jax.ShapeDtypeStruct(s.shape, BF16),
        compiler_params=_cp(("parallel", "arbitrary")),
        name="fft_stage_c",
    )(gfwd, ginv, hspec, s)


def _fai_kernel(fai_ref, b_ref, y_ref, gate_ref, skip_ref, o_ref):
    nt1, tb, cb = y_ref.shape[1:]
    rows = b_ref.shape[1]
    bm = b_ref[0].astype(F32).reshape(rows, tb // FFT_TK, FFT_TK, cb)
    parts = []
    for j in range(tb // FFT_TK):
        bj = bm[:, j].reshape(rows * FFT_TK, cb).astype(BF16)
        yf = jnp.dot(fai_ref[...], bj, preferred_element_type=F32)
        parts.append(yf.reshape(nt1, 1, FFT_TK, cb))
    yf = jnp.concatenate(parts, axis=1).reshape(nt1, tb, cb)
    out = gate_ref[0].astype(F32) * (yf + skip_ref[...].reshape(1, 1, cb) * y_ref[0].astype(F32))
    o_ref[0] = out.astype(o_ref.dtype)


def _fft_stage_a_inv(fai, bm, y, gate, skip):
    B, Lp, C = y.shape
    nt1 = Lp // FFT_N2
    rows = fai.shape[1] // FFT_TK
    tok = pl.BlockSpec((1, nt1, FFT_TB, FFT_CB), lambda b, j, c: (b, 0, j, c))
    out = pl.pallas_call(
        _fai_kernel,
        grid=(B, FFT_N2 // FFT_TB, C // FFT_CB),
        in_specs=[pl.BlockSpec(fai.shape, lambda b, j, c: (0, 0)),
                  pl.BlockSpec((1, rows, FFT_TB, FFT_CB), lambda b, j, c: (b, 0, j, c)),
                  tok, tok, pl.BlockSpec((1, FFT_CB), lambda b, j, c: (0, c))],
        out_specs=tok,
        out_shape=jax.ShapeDtypeStruct((B, nt1, FFT_N2, C), BF16),
        compiler_params=_cp(("parallel", "parallel", "parallel")),
        name="fft_stage_a_inv",
    )(fai, bm, y.reshape(B, nt1, FFT_N2, C), gate.reshape(B, nt1, FFT_N2, C), skip.astype(F32).reshape(1, C))
    return out.reshape(B, Lp, C)


def _hgen_kernel(z_ref, w1_ref, b1_ref, fr_ref, w2_ref, b2_ref, w3_ref, dec_ref, h_ref, ss_ref):
    i = pl.program_id(0)
    z = z_ref[...]
    tm = z.shape[0]
    h = jnp.sin(fr_ref[0:1, :] * (jnp.dot(z, w1_ref[...], preferred_element_type=F32, precision=HIGHEST)
                                 + b1_ref[...]))
    h = jnp.sin(fr_ref[1:2, :] * (jnp.dot(h, w2_ref[...], preferred_element_type=F32, precision=HIGHEST)
                                 + b2_ref[...]))
    h = jnp.dot(h, w3_ref[...], preferred_element_type=F32, precision=HIGHEST)
    h = h * jnp.exp(-z[:, 0:1] * jnp.abs(dec_ref[...]))
    row = lax.broadcasted_iota(jnp.int32, h.shape, 0) + i * tm
    col = lax.broadcasted_iota(jnp.int32, h.shape, 1)
    h = jnp.where((row == 0) & ((col // HY_CH) % 2 == 1), 0.0, h)
    h_ref[...] = h

    @pl.when(i == 0)
    def _():
        ss_ref[...] = jnp.zeros_like(ss_ref)

    ss_ref[...] += jnp.sum(h * h, axis=0, keepdims=True)


def _hyena_filters(L, p):
    t = jnp.arange(L, dtype=F32)
    tn = t / (L - 1)
    w = 2.0 * math.pi * t / L
    bands = jnp.linspace(1e-4, HY_BANDS - 1, HY_BANDS, dtype=F32)
    ang = w[:, None] * bands[None, :]
    z = jnp.concatenate([tn[:, None], jnp.cos(ang), -jnp.sin(ang)], axis=-1)
    z = jnp.pad(z, ((0, 0), (0, LANES - HY_EMB)))
    w1 = jnp.pad(p['hy_pe_w1'].astype(F32), ((0, LANES - HY_EMB), (0, 0)))
    nf = HY_ORDER * 2 * HY_CH
    tm = min(512, L)
    const = lambda shape: pl.BlockSpec(shape, lambda i: (0,) * len(shape))
    return pl.pallas_call(
        _hgen_kernel,
        grid=(L // tm,),
        in_specs=[pl.BlockSpec((tm, LANES), lambda i: (i, 0)), const((LANES, HY_FFN)), const((1, HY_FFN)),
                  const((2, HY_FFN)), const((HY_FFN, HY_FFN)), const((1, HY_FFN)), const((HY_FFN, nf)),
                  const((1, nf))],
        out_specs=[pl.BlockSpec((tm, nf), lambda i: (i, 0)), const((1, nf))],
        out_shape=[jax.ShapeDtypeStruct((L, nf), F32), jax.ShapeDtypeStruct((1, nf), F32)],
        compiler_params=_cp(("arbitrary",)),
        name="hyena_filter_gen",
    )(z, w1, p['hy_pe_b1'].reshape(1, HY_FFN), p['hy_freq'], p['hy_pe_w2'], p['hy_pe_b2'].reshape(1, HY_FFN),
      p['hy_pe_w3'], p['hy_decay'].reshape(1, nf))


def _hyena_branch(z, p, B, L):
    v, x1, x2 = _hy_prep(z, p['hy_conv_w'], p['hy_conv_b'], L)
    Lp = max(L, 2048)
    fa, fai, gfwd, ginv = _fft_tables(Lp)
    hfilt, sumsq = _hyena_filters(L, p)
    pad3 = lambda a: a.reshape(B, L, HY_CH) if Lp == L else jnp.pad(a.reshape(B, L, HY_CH), ((0, 0), (0, Lp - L), (0, 0)))
    hf = hfilt if Lp == L else jnp.pad(hfilt, ((0, Lp - L), (0, 0)))
    s_filt = _fft_stage_a(fa, hf[None], F32)
    hspec = _fft_filter_spectrum(gfwd, s_filt, sumsq)
    y = pad3(v)
    for n, gate in enumerate((pad3(x1), pad3(x2))):
        s = _fft_stage_a(fa, y, BF16)
        bm = _fft_stage_c(gfwd, ginv, hspec, n, s)
        y = _fft_stage_a_inv(fai, bm, y, gate, p['hy_skip'][n])
    return y[:, :L].reshape(B * L, HY_CH)


def _merge_kernel(ya_ref, yb_ref, yc_ref, yd_ref, g0_ref, g1_ref, g2_ref, g3_ref, wup_ref, wout_ref,
                  x_ref, gate_ref, o_ref):
    acc = None
    for n, (y_ref, g_ref) in enumerate(((ya_ref, g0_ref), (yb_ref, g1_ref), (yc_ref, g2_ref), (yd_ref, g3_ref))):
        t = jax.nn.sigmoid(g_ref[...].astype(F32)) * jnp.dot(y_ref[...], wup_ref[n], preferred_element_type=F32)
        acc = t if acc is None else acc + t
    yl = jnp.dot(acc.astype(BF16), wout_ref[...], preferred_element_type=F32)
    o_ref[...] = x_ref[...] + gate_ref[0] * yl


def _merge(ys, z, w_up, w_out, x, gate, seq_len):
    T, D = x.shape
    tm = min(512, seq_len)
    per = seq_len // tm
    Wy = ys[0].shape[1]
    yspec = pl.BlockSpec((tm, Wy), lambda i: (i, 0))
    gspecs = [pl.BlockSpec((tm, D), functools.partial(lambda i, n: (i, Z_GATE // D + n), n=n)) for n in range(4)]
    return pl.pallas_call(
        _merge_kernel,
        grid=(T // tm,),
        in_specs=[yspec] * 4 + gspecs + [
            pl.BlockSpec((4, Wy, D), lambda i: (0, 0, 0)), pl.BlockSpec((D, D), lambda i: (0, 0)),
            pl.BlockSpec((tm, D), lambda i: (i, 0)), pl.BlockSpec((1, 1, D), lambda i: (i // per, 0, 0))],
        out_specs=pl.BlockSpec((tm, D), lambda i: (i, 0)),
        out_shape=jax.ShapeDtypeStruct((T, D), F32),
        compiler_params=_cp(("parallel",)),
        name="merge_branches",
    )(*ys, z, z, z, z, w_up, w_out, x, gate)


def _mlp_kernel(x_ref, g_ref, sc_ref, sh_ref, w1_ref, b1_ref, w2_ref, b2_ref, gate_ref, fg_ref, o_ref,
                h_ref, acc_ref, *, nk, final):
    k = pl.program_id(1)

    @pl.when(k == 0)
    def _():
        y = _rms(x_ref[...], g_ref[...])
        h_ref[...] = (y * (1.0 + sc_ref[0]) + sh_ref[0]).astype(BF16)
        acc_ref[...] = jnp.zeros_like(acc_ref)

    a = jnp.maximum(jnp.dot(h_ref[...], w1_ref[...], preferred_element_type=F32) + b1_ref[...], 0.0)
    acc_ref[...] += jnp.dot((a * a).astype(BF16), w2_ref[...], preferred_element_type=F32)

    @pl.when(k == nk - 1)
    def _():
        out = x_ref[...] + gate_ref[0] * (acc_ref[...] + b2_ref[...])
        if final:
            out = _rms(out, fg_ref[...])
        o_ref[...] = out


def _mlp(x, g, sc, sh, w1, b1, w2, b2, gate, final_g, seq_len, final):
    T, D = x.shape
    F = w1.shape[1]
    tm = min(1024, seq_len)
    per = seq_len // tm
    tk = 1024
    nk = F // tk
    row = lambda i, k: (i // per, 0, 0)
    return pl.pallas_call(
        functools.partial(_mlp_kernel, nk=nk, final=final),
        grid=(T // tm, nk),
        in_specs=[pl.BlockSpec((tm, D), lambda i, k: (i, 0)), pl.BlockSpec((1, D), lambda i, k: (0, 0)),
                  pl.BlockSpec((1, 1, D), row), pl.BlockSpec((1, 1, D), row),
                  pl.BlockSpec((D, tk), lambda i, k: (0, k)), pl.BlockSpec((1, tk), lambda i, k: (0, k)),
                  pl.BlockSpec((tk, D), lambda i, k: (k, 0)), pl.BlockSpec((1, D), lambda i, k: (0, 0)),
                  pl.BlockSpec((1, 1, D), row), pl.BlockSpec((1, D), lambda i, k: (0, 0))],
        out_specs=pl.BlockSpec((tm, D), lambda i, k: (i, 0)),
        out_shape=jax.ShapeDtypeStruct((T, D), F32),
        scratch_shapes=[pltpu.VMEM((tm, D), BF16), pltpu.VMEM((tm, D), F32)],
        compiler_params=_cp(("parallel", "arbitrary")),
        name="mlp",
    )(x, g.reshape(1, D), sc, sh, w1, b1.reshape(1, F), w2, b2.reshape(1, D), gate, final_g.reshape(1, D))


def _pack_w_in(w_in):
    hy_e = 3 * HY_CH
    ga_e = hy_e + (GA_HEADS + 2 * GA_KV) * GA_HD
    mw = ML_HEADS * ML_HD
    ml_e = ga_e + 4 * mw + 16
    wa_e = ml_e + (WA_HEADS + 2 * WA_KV) * WA_HD
    hy, ga = w_in[:, :hy_e], w_in[:, hy_e:ga_e]
    ml = w_in[:, ga_e:ml_e]
    wa = w_in[:, ml_e:wa_e]
    gate = w_in[:, wa_e:]
    waq, wakv = wa[:, :WA_HEADS * WA_HD], wa[:, WA_HEADS * WA_HD:]
    pad = jnp.zeros((w_in.shape[0], Z_COLS - Z_WAKV - wakv.shape[1]), w_in.dtype)
    packed = jnp.concatenate([hy, waq, ga, ml[:, :2 * mw], ml[:, 2 * mw:3 * mw], ml[:, 3 * mw:4 * mw], gate, wakv, pad],
                             axis=1)
    wg = jnp.pad(ml[:, 4 * mw:], ((0, 0), (0, LANES - 16)))
    return packed.astype(BF16), wg


def _token_mixers(zl, zc, gl, gc, p, B, L, Lc, with_ctx_out):
    ya_l = _hyena_branch(zl, p, B, L)
    ya_c = _hyena_branch(zc, p, B, Lc) if with_ctx_out else None
    cos, sin = _rope_tables(L, GA_HD)
    ql, kl, vl = _ga_prep(zl, cos, sin, p['ga_q_g'], p['ga_k_g'], B, L, True)
    qc, kc, vc = _ga_prep(zc, cos[:Lc], sin[:Lc], p['ga_q_g'], p['ga_k_g'], B, Lc, False)
    r3 = lambda a, n: a.reshape(B, n, a.shape[-1])
    k_all = jnp.concatenate([r3(kl, L), r3(kc, Lc)], axis=1)
    vt_all = jnp.concatenate([vl, vc], axis=2)
    yb_l = _global_attention(ql, k_all, vt_all).reshape(B * L, -1)
    yb_c = _global_attention(qc, r3(kc, Lc), vc).reshape(B * Lc, -1) if with_ctx_out else None
    mq_l, mk_l, mv_l = _ml_prep(zl, p['ml_conv_w'], p['ml_conv_b'], B, L)
    mq_c, mk_c, mv_c = _ml_prep(zc, p['ml_conv_w'], p['ml_conv_b'], B, Lc)
    gb = jnp.pad(p['ml_gate_b'].astype(F32), (0, LANES - 16)).reshape(1, LANES)
    c0 = jnp.zeros((B, ML_HEADS, 2 * ML_HD, ML_HD), F32)
    m0 = jnp.zeros((B, ML_HEADS, 8, LANES), F32)
    zl3, zc3, gl3, gc3 = r3(zl, L), r3(zc, Lc), r3(gl, L), r3(gc, Lc)
    h_cf, cf, mf = _mlstm_scan(mq_c, r3(mk_c, Lc), mv_c, zc3, gc3, gb, c0, m0, False)
    yc_c, cb, mb = _mlstm_scan(mq_c, r3(mk_c, Lc), mv_c, zc3, gc3, gb, c0, m0, True, h_cf, p['ml_norm_g'])
    h_lf, _, _ = _mlstm_scan(mq_l, r3(mk_l, L), mv_l, zl3, gl3, gb, cf, mf, False)
    yc_l, _, _ = _mlstm_scan(mq_l, r3(mk_l, L), mv_l, zl3, gl3, gb, cb, mb, True, h_lf, p['ml_norm_g'])
    yc_l = yc_l.reshape(B * L, -1)
    yc_c = yc_c.reshape(B * Lc, -1)
    cosw, sinw = _rope_tables(L, WA_HD)
    wq_l, wk_l, wv_l = _wa_prep(zl, cosw, sinw, B, L, True)
    wq_c, wk_c, wv_c = _wa_prep(zc, cosw[:Lc], sinw[:Lc], B, Lc, False)
    yd_l = _window_attention(wq_l, r3(wk_c, Lc), wv_c, p['wa_sink'], r3(wk_l, L), wv_l).reshape(B * L, -1)
    yd_c = (_window_attention(wq_c, r3(wk_c, Lc), wv_c, p['wa_sink']).reshape(B * Lc, -1)
            if with_ctx_out else None)
    return (ya_l, yb_l, yc_l, yd_l), (ya_c, yb_c, yc_c, yd_c)


def kernel(x, c, ctx, c_ctx, w_mod, b_mod, ln1_g, ln2_g, w_in, hy_conv_w, hy_conv_b,
           hy_pe_w1, hy_pe_b1, hy_freq, hy_pe_w2, hy_pe_b2, hy_pe_w3, hy_decay, hy_skip,
           ga_q_g, ga_k_g, ml_conv_w, ml_conv_b, ml_gate_b, ml_norm_g, wa_sink, w_up, w_out,
           mlp_w1, mlp_b1, mlp_w2, mlp_b2, final_g):
    B, L, D = x.shape
    Lc = ctx.shape[1]
    R = -(-(B + 1) // 8) * 8
    cvec = jnp.zeros((R, D), F32).at[:B].set(c).at[B].set(c_ctx)
    mod = _modulation(cvec, w_mod, b_mod)
    xl = x.reshape(B * L, D)
    xc = ctx.reshape(B * Lc, D)
    for l in range(DEPTH):
        with_ctx_out = l < DEPTH - 1
        p = dict(hy_conv_w=hy_conv_w[l], hy_conv_b=hy_conv_b[l], hy_pe_w1=hy_pe_w1[l],
                 hy_pe_b1=hy_pe_b1[l], hy_freq=hy_freq[l], hy_pe_w2=hy_pe_w2[l], hy_pe_b2=hy_pe_b2[l],
                 hy_pe_w3=hy_pe_w3[l], hy_decay=hy_decay[l], hy_skip=hy_skip[l],
                 ga_q_g=ga_q_g[l], ga_k_g=ga_k_g[l], ml_conv_w=ml_conv_w[l], ml_conv_b=ml_conv_b[l],
                 ml_gate_b=ml_gate_b[l], ml_norm_g=ml_norm_g[l], wa_sink=wa_sink[l])
        ml_rows = mod[l, :B].reshape(B, 1, 6 * D)
        mc_rows = jnp.broadcast_to(mod[l, B].reshape(1, 1, 6 * D), (B, 1, 6 * D))
        part = lambda m, n: m[:, :, n * D:(n + 1) * D]
        w_pack, w_gate = _pack_w_in(w_in[l])
        wg = w_gate.astype(BF16)
        zl, gl = _normmod_matmul(xl, ln1_g[l], part(ml_rows, 1), part(ml_rows, 0), w_pack, wg, L, Z_TN)
        zc, gc = _normmod_matmul(xc, ln1_g[l], part(mc_rows, 1), part(mc_rows, 0), w_pack, wg, Lc, Z_TN)
        ys_l, ys_c = _token_mixers(zl, zc, gl, gc, p, B, L, Lc, with_ctx_out)
        wup = w_up[l].astype(BF16)
        wout = w_out[l].astype(BF16)
        w1, w2 = mlp_w1[l].astype(BF16), mlp_w2[l].astype(BF16)
        xl = _merge(ys_l, zl, wup, wout, xl, part(ml_rows, 2), L)
        xl = _mlp(xl, ln2_g[l], part(ml_rows, 4), part(ml_rows, 3), w1, mlp_b1[l], w2, mlp_b2[l],
                  part(ml_rows, 5), final_g, L, final=(l == DEPTH - 1))
        if with_ctx_out:
            xc = _merge(ys_c, zc, wup, wout, xc, part(mc_rows, 2), Lc)
            xc = _mlp(xc, ln2_g[l], part(mc_rows, 4), part(mc_rows, 3), w1, mlp_b1[l], w2, mlp_b2[l],
                      part(mc_rows, 5), final_g, Lc, final=False)
    return xl.reshape(B, L, D)
```

```python
import functools
import math

import jax
import jax.numpy as jnp
from jax import lax
from jax.experimental import pallas as pl
from jax.experimental.pallas import tpu as pltpu

F32 = jnp.float32
BF16 = jnp.bfloat16
HIGHEST = lax.Precision.HIGHEST

D_MODEL = 1024
DEPTH = 2
GRID_W = 64
HY_CH = 512
HY_ORDER = 2
HY_BANDS = 16
HY_EMB = 1 + 2 * HY_BANDS
HY_FFN = 64
GA_HEADS, GA_KV, GA_HD = 4, 2, 128
ML_HEADS, ML_HD = 4, 128
WA_HEADS, WA_KV, WA_HD = 8, 2, 64
WINDOW = 128
ROPE_BASE = 10000.0
D_FF = 4 * D_MODEL
EPS = 1e-6
NEG = -1e30
LOG2E = 1.4426950408889634

LANES = 128
V7X_VMEM_LIMIT = 48 * 1024 * 1024

Z_HY = 0
Z_WAQ = 1536
Z_GA = 2048
Z_MLQK = 3072
Z_MLV = 4096
Z_MLO = 4608
Z_GATE = 5120
Z_WAKV = 9216
Z_COLS = 9728
Z_TN = 2432

ML_CHUNK = 256
FFT_N2 = 128
GA_CB = 256
WA_PAIR = 2
WA_SUB = 8
GA_TK = 256
GA_CARRY = 4
GA_UNROLL = 32


def _cp(sem, vmem=V7X_VMEM_LIMIT):
    return pltpu.CompilerParams(dimension_semantics=sem, vmem_limit_bytes=vmem)


def _rms(x, g):
    return x * lax.rsqrt(jnp.mean(x * x, axis=-1, keepdims=True) + EPS) * g


def _mod_kernel(c_ref, w_ref, b_ref, o_ref):
    c = c_ref[...]
    s = c * jax.nn.sigmoid(c)
    o_ref[0] = jnp.dot(s, w_ref[0], preferred_element_type=F32, precision=HIGHEST) + b_ref[0]


def _modulation(cvec, w_mod, b_mod):
    R = cvec.shape[0]
    tn = 1536
    return pl.pallas_call(
        _mod_kernel,
        grid=(DEPTH, 6 * D_MODEL // tn),
        in_specs=[pl.BlockSpec((R, D_MODEL), lambda l, j: (0, 0)),
                  pl.BlockSpec((1, D_MODEL, tn), lambda l, j: (l, 0, j)),
                  pl.BlockSpec((1, 1, tn), lambda l, j: (l, 0, j))],
        out_specs=pl.BlockSpec((1, R, tn), lambda l, j: (l, 0, j)),
        out_shape=jax.ShapeDtypeStruct((DEPTH, R, 6 * D_MODEL), F32),
        compiler_params=_cp(("parallel", "parallel")),
        name="modulation",
    )(cvec, w_mod, b_mod.reshape(DEPTH, 1, 6 * D_MODEL))


def _nmm_kernel(x_ref, g_ref, sc_ref, sh_ref, w_ref, wx_ref, o_ref, ox_ref, h_ref):
    @pl.when(pl.program_id(1) == 0)
    def _():
        y = _rms(x_ref[...], g_ref[...])
        h_ref[...] = (y * (1.0 + sc_ref[0]) + sh_ref[0]).astype(BF16)
        ox_ref[...] = jnp.dot(h_ref[...], wx_ref[...], preferred_element_type=F32)

    o_ref[...] = jnp.dot(h_ref[...], w_ref[...], preferred_element_type=F32).astype(o_ref.dtype)


def _normmod_matmul(x, g, sc, sh, w, wx, seq_len, tn):
    T, D = x.shape
    N = w.shape[1]
    NX = wx.shape[1]
    tm = min(1024, seq_len)
    per = seq_len // tm
    return pl.pallas_call(
        _nmm_kernel,
        grid=(T // tm, N // tn),
        in_specs=[pl.BlockSpec((tm, D), lambda i, j: (i, 0)),
                  pl.BlockSpec((1, D), lambda i, j: (0, 0)),
                  pl.BlockSpec((1, 1, D), lambda i, j: (i // per, 0, 0)),
                  pl.BlockSpec((1, 1, D), lambda i, j: (i // per, 0, 0)),
                  pl.BlockSpec((D, tn), lambda i, j: (0, j)),
                  pl.BlockSpec((D, NX), lambda i, j: (0, 0))],
        out_specs=[pl.BlockSpec((tm, tn), lambda i, j: (i, j)), pl.BlockSpec((tm, NX), lambda i, j: (i, 0))],
        out_shape=[jax.ShapeDtypeStruct((T, N), BF16), jax.ShapeDtypeStruct((T, NX), F32)],
        scratch_shapes=[pltpu.VMEM((tm, D), BF16)],
        compiler_params=_cp(("parallel", "arbitrary")),
        name="normmod_matmul",
    )(x, g.reshape(1, D), sc, sh, w, wx)


def _conv3(u, prev_row, next_row, w_ref, b_ref, c0, c1):
    tm = u.shape[0]
    row = lax.broadcasted_iota(jnp.int32, u.shape, 0)
    up = jnp.where(row == 0, prev_row, pltpu.roll(u, 1, 0))
    dn = jnp.where(row == tm - 1, next_row, pltpu.roll(u, tm - 1, 0))
    return (w_ref[0:1, c0:c1] * up + w_ref[1:2, c0:c1] * u + w_ref[2:3, c0:c1] * dn + b_ref[0:1, c0:c1])


def _halo_rows(zp_ref, zn_ref, per, c0, c1):
    i = pl.program_id(0)
    first = (i % per) == 0
    last = (i % per) == per - 1
    hp = zp_ref.shape[0]
    prev_row = jnp.where(first, 0.0, zp_ref[hp - 1:hp, c0:c1].astype(F32))
    next_row = jnp.where(last, 0.0, zn_ref[0:1, c0:c1].astype(F32))
    return prev_row, next_row


HALO = 16


def _halo_specs(tm, width, col_block, n_rows):
    nb = n_rows // HALO
    r = tm // HALO
    return [pl.BlockSpec((tm, width), lambda i: (i, col_block)),
            pl.BlockSpec((HALO, width), lambda i: (jnp.maximum(i * r - 1, 0), col_block)),
            pl.BlockSpec((HALO, width), lambda i: (jnp.minimum((i + 1) * r, nb - 1), col_block))]


def _hy_prep_kernel(z_ref, zp_ref, zn_ref, w_ref, b_ref, v_ref, x1_ref, x2_ref, *, per):
    outs = (v_ref, x1_ref, x2_ref)
    for c in range(3):
        c0, c1 = c * HY_CH, (c + 1) * HY_CH
        prev_row, next_row = _halo_rows(zp_ref, zn_ref, per, c0, c1)
        u = z_ref[:, c0:c1].astype(F32)
        outs[c][...] = _conv3(u, prev_row, next_row, w_ref, b_ref, c0, c1).astype(outs[c].dtype)


def _hy_prep(z, conv_w, conv_b, seq_len):
    T = z.shape[0]
    tm = min(512, seq_len)
    W = 3 * HY_CH
    out = jax.ShapeDtypeStruct((T, HY_CH), BF16)
    return pl.pallas_call(
        functools.partial(_hy_prep_kernel, per=seq_len // tm),
        grid=(T // tm,),
        in_specs=_halo_specs(tm, W, Z_HY // W, T) + [
            pl.BlockSpec((3, W), lambda i: (0, 0)), pl.BlockSpec((1, W), lambda i: (0, 0))],
        out_specs=[pl.BlockSpec((tm, HY_CH), lambda i: (i, 0))] * 3,
        out_shape=[out, out, out],
        compiler_params=_cp(("parallel",)),
        name="hyena_prep",
    )(z, z, z, conv_w, conv_b.reshape(1, W))


def _ml_prep_kernel(z_ref, zp_ref, zn_ref, zv_ref, w_ref, b_ref, q_ref, k_ref, v_ref, *, per):
    W = ML_HEADS * ML_HD
    for c in range(2):
        c0, c1 = c * W, (c + 1) * W
        prev_row, next_row = _halo_rows(zp_ref, zn_ref, per, c0, c1)
        u = z_ref[:, c0:c1].astype(F32)
        y = _conv3(u, prev_row, next_row, w_ref, b_ref, c0, c1)
        y = y * jax.nn.sigmoid(y)
        if c == 0:
            for h in range(ML_HEADS):
                q_ref[0, h * ML_HD:(h + 1) * ML_HD, :] = y[:, h * ML_HD:(h + 1) * ML_HD].T.astype(q_ref.dtype)
        else:
            k_ref[...] = (y * (ML_HD ** -0.5)).astype(k_ref.dtype)
    for h in range(ML_HEADS):
        v_ref[0, h * ML_HD:(h + 1) * ML_HD, :] = zv_ref[:, h * ML_HD:(h + 1) * ML_HD].astype(F32).T.astype(v_ref.dtype)


def _ml_prep(z, conv_w, conv_b, B, seq_len):
    T = z.shape[0]
    tm = min(512, seq_len)
    per = seq_len // tm
    W = 2 * ML_HEADS * ML_HD
    Wh = W // 2
    tspec = pl.BlockSpec((1, Wh, tm), lambda i: (i // per, 0, i % per))
    tshape = jax.ShapeDtypeStruct((B, Wh, seq_len), BF16)
    return pl.pallas_call(
        functools.partial(_ml_prep_kernel, per=per),
        grid=(T // tm,),
        in_specs=_halo_specs(tm, W, Z_MLQK // W, T) + [
            pl.BlockSpec((tm, Wh), lambda i: (i, Z_MLV // Wh)),
            pl.BlockSpec((3, W), lambda i: (0, 0)), pl.BlockSpec((1, W), lambda i: (0, 0))],
        out_specs=[tspec, pl.BlockSpec((tm, Wh), lambda i: (i, 0)), tspec],
        out_shape=[tshape, jax.ShapeDtypeStruct((T, Wh), BF16), tshape],
        compiler_params=_cp(("parallel",)),
        name="mlstm_prep",
    )(z, z, z, z, conv_w, conv_b.reshape(1, W))


def _rope_tables(L, hd):
    quarter = hd // 4
    inv = ROPE_BASE ** (-jnp.arange(quarter, dtype=F32) / quarter)
    t = jnp.arange(L)
    row = (t // GRID_W).astype(F32)
    col = (t % GRID_W).astype(F32)
    lane = jnp.arange(LANES)
    within = lane % hd
    is_col = (within // (hd // 2)) == 1
    second = ((within % (hd // 2)) // quarter) == 1
    j = within % quarter
    pos = jnp.where(is_col[None, :], col[:, None], row[:, None])
    ang = pos * inv[j][None, :]
    return jnp.cos(ang), jnp.where(second[None, :], jnp.sin(ang), -jnp.sin(ang))


def _rope(x, cos, sin, quarter):
    src = lax.broadcasted_iota(jnp.int32, (LANES, LANES), 0)
    dst = lax.broadcasted_iota(jnp.int32, (LANES, LANES), 1)
    first = ((dst % (2 * quarter)) // quarter) == 0
    perm = jnp.where(src == jnp.where(first, dst + quarter, dst - quarter), 1.0, 0.0).astype(BF16)
    partner = jnp.dot(x.astype(BF16), perm, preferred_element_type=F32)
    return x * cos + partner * sin


def _rms_lanes(x, g):
    ss = jnp.dot((x * x).astype(BF16), jnp.ones((LANES, LANES), BF16), preferred_element_type=F32)
    return x * lax.rsqrt(ss * (1.0 / LANES) + EPS) * g


def _transpose_bf16(x):
    r = lax.broadcasted_iota(jnp.int32, (LANES, LANES), 0)
    c = lax.broadcasted_iota(jnp.int32, (LANES, LANES), 1)
    eye = jnp.where(r == c, 1.0, 0.0).astype(BF16)
    return lax.dot_general(eye, x.astype(BF16), (((1,), (1,)), ((), ())), preferred_element_type=F32).astype(BF16)


def _ga_prep_kernel(z_ref, cos_ref, sin_ref, qg_ref, kg_ref, q_ref, k_ref, v_ref, *, rope):
    nq, nk = GA_HEADS, GA_KV
    heads = range(nq + nk)
    for h in range(nk):
        v_ref[0, h * GA_HD:(h + 1) * GA_HD, :] = _transpose_bf16(
            z_ref[:, (nq + nk + h) * GA_HD:(nq + nk + h + 1) * GA_HD])
    y = [_rms_lanes(z_ref[:, h * GA_HD:(h + 1) * GA_HD].astype(F32), qg_ref[...] if h < nq else kg_ref[...])
         for h in heads]
    if rope:
        y = [_rope(y[h], cos_ref[...], sin_ref[...], GA_HD // 4) for h in heads]
    for h in heads:
        if h < nq:
            q_ref[0, h * GA_HD:(h + 1) * GA_HD, :] = _transpose_bf16(y[h] * (GA_HD ** -0.5 * LOG2E))
        else:
            k_ref[:, (h - nq) * GA_HD:(h - nq + 1) * GA_HD] = y[h].astype(k_ref.dtype)


def _ga_prep(z, cos, sin, qg, kg, B, seq_len, rope):
    T = z.shape[0]
    tm = min(512, seq_len)
    per = seq_len // tm
    W = (GA_HEADS + 2 * GA_KV) * GA_HD
    return pl.pallas_call(
        functools.partial(_ga_prep_kernel, rope=rope),
        grid=(T // tm,),
        in_specs=[pl.BlockSpec((tm, W), lambda i: (i, Z_GA // W)),
                  pl.BlockSpec((tm, LANES), lambda i: (i % per, 0)),
                  pl.BlockSpec((tm, LANES), lambda i: (i % per, 0)),
                  pl.BlockSpec((1, GA_HD), lambda i: (0, 0)),
                  pl.BlockSpec((1, GA_HD), lambda i: (0, 0))],
        out_specs=[pl.BlockSpec((1, GA_HEADS * GA_HD, tm), lambda i: (i // per, 0, i % per)),
                   pl.BlockSpec((tm, GA_KV * GA_HD), lambda i: (i, 0)),
                   pl.BlockSpec((1, GA_KV * GA_HD, tm), lambda i: (i // per, 0, i % per))],
        out_shape=[jax.ShapeDtypeStruct((B, GA_HEADS * GA_HD, seq_len), BF16),
                   jax.ShapeDtypeStruct((T, GA_KV * GA_HD), BF16),
                   jax.ShapeDtypeStruct((B, GA_KV * GA_HD, seq_len), BF16)],
        compiler_params=_cp(("parallel",)),
        name="global_attn_prep",
    )(z, cos, sin, qg.reshape(1, GA_HD), kg.reshape(1, GA_HD))


def _wa_prep_kernel(zq_ref, zkv_ref, cos_ref, sin_ref, q_ref, k_ref, v_ref, *, rope):
    quarter = WA_HD // 4
    nqb = WA_HEADS * WA_HD // LANES
    v_ref[0] = _transpose_bf16(zkv_ref[:, LANES:])
    x = [zq_ref[:, j * LANES:(j + 1) * LANES].astype(F32) for j in range(nqb)] + [zkv_ref[:, 0:LANES].astype(F32)]
    if rope:
        x = [_rope(xj, cos_ref[...], sin_ref[...], quarter) for xj in x]
    for j in range(nqb):
        q_ref[0, j * LANES:(j + 1) * LANES, :] = _transpose_bf16(x[j] * (WA_HD ** -0.5 * LOG2E))
    k_ref[...] = x[nqb].astype(k_ref.dtype)


def _wa_prep(z, cos, sin, B, seq_len, rope):
    T = z.shape[0]
    tm = min(512, seq_len)
    per = seq_len // tm
    WQ = WA_HEADS * WA_HD
    return pl.pallas_call(
        functools.partial(_wa_prep_kernel, rope=rope),
        grid=(T // tm,),
        in_specs=[pl.BlockSpec((tm, WQ), lambda i: (i, Z_WAQ // WQ)),
                  pl.BlockSpec((tm, 2 * LANES), lambda i: (i, Z_WAKV // (2 * LANES))),
                  pl.BlockSpec((tm, LANES), lambda i: (i % per, 0)),
                  pl.BlockSpec((tm, LANES), lambda i: (i % per, 0))],
        out_specs=[pl.BlockSpec((1, WQ, tm), lambda i: (i // per, 0, i % per)),
                   pl.BlockSpec((tm, LANES), lambda i: (i, 0)),
                   pl.BlockSpec((1, LANES, tm), lambda i: (i // per, 0, i % per))],
        out_shape=[jax.ShapeDtypeStruct((B, WQ, seq_len), BF16),
                   jax.ShapeDtypeStruct((T, LANES), BF16),
                   jax.ShapeDtypeStruct((B, LANES, seq_len), BF16)],
        compiler_params=_cp(("parallel",)),
        name="window_attn_prep",
    )(z, z, cos, sin)


def _ga_kernel(q_ref, k_ref, vt_ref, o_ref, acc_ref, m_ref, *, nchunks, tk, tq):
    acc_ref[...] = jnp.zeros_like(acc_ref)
    m_ref[...] = jnp.full_like(m_ref, NEG)
    qt = jnp.concatenate([q_ref[0, 0:GA_HD, :], q_ref[0, GA_HD:, :]], axis=1)

    W = GA_CB
    nblk = 2 * tq // W

    def scores(k, blocks=None):
        blocks = range(nblk) if blocks is None else blocks
        return tuple(jnp.dot(k, qt[:, i * W:(i + 1) * W], preferred_element_type=F32) for i in blocks)

    def softmax_pv(s_blocks, vt):
        cols = [slice(i * W, (i + 1) * W) for i in range(nblk)]
        m_old = [m_ref[:, cs] for cs in cols]
        m_new = [jnp.maximum(mo, jnp.max(s, axis=0, keepdims=True)) for mo, s in zip(m_old, s_blocks)]
        alpha = [jnp.exp2(mo - mn) for mo, mn in zip(m_old, m_new)]
        vaug = jnp.concatenate([vt, jnp.ones((16, vt.shape[1]), BF16)], axis=0)
        pv = [jnp.dot(vaug, jnp.exp2((s - mn).astype(BF16)), preferred_element_type=F32)
              for s, mn in zip(s_blocks, m_new)]
        for i, cs in enumerate(cols):
            acc_ref[:, cs] = alpha[i] * acc_ref[:, cs] + pv[i]
            m_ref[:, cs] = m_new[i]

    ahead, late = range(min(GA_CARRY, nblk)), range(min(GA_CARRY, nblk), nblk)

    def body(j, s):
        off = pl.multiple_of(j * tk, tk)
        s_late = scores(k_ref[0, pl.ds(off, tk), :], late)
        s_next = scores(k_ref[0, pl.ds(pl.multiple_of((j + 1) * tk, tk), tk), :], ahead)
        softmax_pv(s + s_late, vt_ref[0, :, pl.ds(off, tk)])
        return s_next
    s = scores(k_ref[0, 0:tk, :], ahead)
    if nchunks > 1:
        s = lax.fori_loop(0, nchunks - 1, body, s, unroll=GA_UNROLL if (nchunks - 1) % GA_UNROLL == 0 else 1)
    last = slice((nchunks - 1) * tk, nchunks * tk)
    softmax_pv(s + scores(k_ref[0, last, :], late), vt_ref[0, :, last])
    o = acc_ref[0:GA_HD, :] / acc_ref[GA_HD:GA_HD + 1, :]
    for h in range(2):
        o_ref[0, :, h * GA_HD:(h + 1) * GA_HD] = o[:, h * tq:(h + 1) * tq].T.astype(o_ref.dtype)


def _global_attention(qt, k, vt):
    B, _, Lq = qt.shape
    Lk = k.shape[1]
    tq = min(512, Lq)
    tk = min(GA_TK, Lk)
    W = 2 * GA_HD
    return pl.pallas_call(
        functools.partial(_ga_kernel, nchunks=Lk // tk, tk=tk, tq=tq),
        grid=(B, GA_KV, Lq // tq),
        in_specs=[pl.BlockSpec((1, W, tq), lambda b, g, i: (b, g, i)),
                  pl.BlockSpec((1, Lk, GA_HD), lambda b, g, i: (b, 0, g)),
                  pl.BlockSpec((1, GA_HD, Lk), lambda b, g, i: (b, g, 0))],
        out_specs=pl.BlockSpec((1, tq, W), lambda b, g, i: (b, i, g)),
        out_shape=jax.ShapeDtypeStruct((B, Lq, GA_HEADS * GA_HD), BF16),
        scratch_shapes=[pltpu.VMEM((GA_HD + 16, 2 * tq), F32), pltpu.VMEM((1, 2 * tq), F32)],
        compiler_params=_cp(("parallel", "parallel", "parallel")),
        name="global_attention",
    )(qt, k, vt)


def _wa_kernel(*refs, band, nq, nsub):
    if band:
        q_ref = refs[0]
        kb = refs[1:nsub + 3]
        vb = refs[nsub + 3:2 * nsub + 5]
        kc_ref, vc_ref, sink_ref, o_ref = refs[2 * nsub + 5:]
    else:
        q_ref, kc_ref, vc_ref, sink_ref, o_ref = refs
    tq = WINDOW
    Lc = kc_ref.shape[1]
    step = pl.program_id(1)
    G = WA_HEADS // WA_KV
    cols = WA_PAIR * tq
    npr = WA_HEADS // WA_PAIR
    grp = [(pr * WA_PAIR) // G for pr in range(npr)]
    zeros = jnp.zeros((WA_HD, cols), BF16)
    if band:
        nb = 3 * tq
        c = lax.broadcasted_iota(jnp.int32, (nb + Lc, cols), 0)
        r = lax.broadcasted_iota(jnp.int32, (nb + Lc, cols), 1) % tq
        in_band = (c >= r) & (c <= r + 2 * WINDOW)
    ones_rows = jnp.ones((16, (3 * tq if band else 0) + Lc), BF16)
    chains = [(u, pr) for u in range(nsub) for pr in range(npr)]
    keys, vals, valid = [], [], []
    for u in range(nsub):
        if band:
            keys.append(jnp.concatenate([kb[u][0], kb[u + 1][0], kb[u + 2][0], kc_ref[0]], axis=0))
            vals.append(jnp.concatenate([vb[u][0], vb[u + 1][0], vb[u + 2][0], vc_ref[0]], axis=1))
            qi = step * nsub + u
            lo = jnp.where(qi == 0, tq, 0)
            hi = jnp.where(qi == nq - 1, 2 * tq, nb)
            valid.append((in_band & (c >= lo) & (c < hi)) | (c >= nb))
        else:
            keys.append(kc_ref[0])
            vals.append(vc_ref[0])
    s = {}
    for u, pr in chains:
        qg = jnp.concatenate([q_ref[0, (WA_PAIR * pr + h) * WA_HD:(WA_PAIR * pr + h + 1) * WA_HD, u * tq:(u + 1) * tq]
                              for h in range(WA_PAIR)], axis=1)
        qpad = jnp.concatenate([qg, zeros] if grp[pr] == 0 else [zeros, qg], axis=0)
        sp = jnp.dot(keys[u], qpad, preferred_element_type=F32)
        s[u, pr] = jnp.where(valid[u], sp, NEG) if band else sp
    sink = [sink_ref[pr] for pr in range(npr)]
    m = {ch: jnp.maximum(jnp.max(s[ch], axis=0, keepdims=True), sink[ch[1]]) for ch in chains}
    p = {ch: jnp.exp2(s[ch] - m[ch]).astype(BF16) for ch in chains}
    R = {}
    for u, pr in chains:
        vaug = jnp.concatenate([vals[u][grp[pr] * WA_HD:(grp[pr] + 1) * WA_HD, :], ones_rows], axis=0)
        R[u, pr] = jnp.dot(vaug, p[u, pr], preferred_element_type=F32)
    for u, pr in chains:
        o = R[u, pr][:WA_HD] / (R[u, pr][WA_HD:WA_HD + 1] + jnp.exp2(sink[pr] - m[u, pr]))
        ot = jnp.concatenate([o[:, h * tq:(h + 1) * tq] for h in range(WA_PAIR)], axis=0)
        wo = WA_PAIR * WA_HD
        o_ref[0, u * tq:(u + 1) * tq, pr * wo:(pr + 1) * wo] = ot.T.astype(o_ref.dtype)


def _window_attention(qt, kc, vtc, sink, kl=None, vtl=None):
    B, WQ, Lq = qt.shape
    Lc = kc.shape[1]
    tq = WINDOW
    nq = Lq // tq
    nsub = math.gcd(WA_SUB, nq)
    band = kl is not None
    npair = WA_HEADS // WA_PAIR
    sink_row = jnp.repeat(sink.astype(F32).reshape(npair, WA_PAIR) * LOG2E, tq, axis=1).reshape(npair, 1, WA_PAIR * tq)
    qspec = pl.BlockSpec((1, WQ, nsub * tq), lambda b, i: (b, 0, i))
    kcspec = pl.BlockSpec((1, Lc, LANES), lambda b, i: (b, 0, 0))
    vcspec = pl.BlockSpec((1, LANES, Lc), lambda b, i: (b, 0, 0))
    sspec = pl.BlockSpec((npair, 1, WA_PAIR * tq), lambda b, i: (0, 0, 0))
    if band:
        blk = lambda off: (lambda i: jnp.clip(i * nsub + off, 0, nq - 1))
        kspec = lambda f: pl.BlockSpec((1, tq, LANES), lambda b, i: (b, f(i), 0))
        vspec = lambda f: pl.BlockSpec((1, LANES, tq), lambda b, i: (b, 0, f(i)))
        offs = range(-1, nsub + 1)
        in_specs = ([qspec] + [kspec(blk(o)) for o in offs] + [vspec(blk(o)) for o in offs]
                    + [kcspec, vcspec, sspec])
        args = (qt,) + (kl,) * (nsub + 2) + (vtl,) * (nsub + 2) + (kc, vtc, sink_row)
    else:
        in_specs, args = [qspec, kcspec, vcspec, sspec], (qt, kc, vtc, sink_row)
    return pl.pallas_call(
        functools.partial(_wa_kernel, band=band, nq=nq, nsub=nsub),
        grid=(B, nq // nsub),
        in_specs=in_specs,
        out_specs=pl.BlockSpec((1, nsub * tq, WQ), lambda b, i: (b, i, 0)),
        out_shape=jax.ShapeDtypeStruct((B, Lq, WQ), BF16),
        compiler_params=_cp(("parallel", "parallel")),
        name="window_attention",
    )(*args)


def _mlstm_kernel(*refs, reverse, nc, bb):
    if reverse:
        (qt_ref, k_ref, vt_ref, g_ref, gb_ref, c0_ref, m0_ref, hft_ref, o_ref, gn_ref,
         y_ref, cf_ref, mf_ref, c_scr, m_scr) = refs
    else:
        (qt_ref, k_ref, vt_ref, g_ref, gb_ref, c0_ref, m0_ref,
         y_ref, cf_ref, mf_ref, c_scr, m_scr) = refs
    T = ML_CHUNK
    d = 1 if reverse else 0
    step = pl.program_id(1)

    @pl.when(step == 0)
    def _():
        c_scr[...] = c0_ref[...]
        m_scr[...] = m0_ref[...]

    si = lax.broadcasted_iota(jnp.int32, (T, T), 0)
    ti = lax.broadcasted_iota(jnp.int32, (T, T), 1)
    tri = ((ti >= si) if reverse else (ti <= si)).astype(F32)
    mask_t = (si >= ti) if reverse else (si <= ti)
    ones_rows = jnp.ones((ML_HD, T), BF16)
    e_last = 0 if reverse else T - 1
    chains = [(bi, h) for bi in range(bb) for h in range(ML_HEADS)]
    gates = []
    for bi in range(bb):
        G = g_ref[bi] + gb_ref[...]
        LF = jax.nn.log_sigmoid(G)
        Bc = jnp.dot(tri, LF, preferred_element_type=F32, precision=HIGHEST)
        gates.append((G.T, Bc.T, Bc - pltpu.roll(G, 4, 1)))
    st, ph = {}, {}
    for bi, h in chains:
        hs = slice(h * ML_HD, (h + 1) * ML_HD)
        st[bi, h] = jnp.dot(k_ref[bi, :, hs], qt_ref[bi, hs, :], preferred_element_type=F32)
    for bi, h in chains:
        GT, BT, Dc = gates[bi]
        fl, il = d * 8 + 4 + h, d * 8 + h
        b_row, i_row = BT[fl:fl + 1, :], GT[il:il + 1, :]
        log_d = jnp.where(mask_t, b_row - Dc[:, fl:fl + 1], NEG)
        m_prev = m_scr[bi, h, 0:1, 0:1]
        m_inter = b_row + m_prev
        m_t = jnp.maximum(m_inter, jnp.max(log_d, axis=0, keepdims=True))
        wqk = (st[bi, h] * jnp.exp(log_d - m_t)).astype(BF16)
        b_end = BT[fl:fl + 1, e_last:e_last + 1]
        log_w = b_end - b_row + i_row
        m_next = jnp.maximum(b_end + m_prev, jnp.max(log_w, axis=1, keepdims=True))
        ph[bi, h] = (wqk, jnp.exp(m_inter - m_t), jnp.exp(-m_t), jnp.exp(log_w - m_next),
                     jnp.exp(b_end + m_prev - m_next), m_next)
    for bi, h in chains:
        hs = slice(h * ML_HD, (h + 1) * ML_HD)
        wqk, cs, em, w_row, decay, m_next = ph[bi, h]
        kh, qt = k_ref[bi, :, hs], qt_ref[bi, hs, :]
        vaug = jnp.concatenate([vt_ref[bi, hs, :], ones_rows], axis=0)
        R = (jnp.dot(vaug, wqk, preferred_element_type=F32)
             + cs * jnp.dot(c_scr[bi, h].astype(BF16), qt, preferred_element_type=F32))
        hh = R[:ML_HD] / jnp.maximum(jnp.abs(R[ML_HD:]), em)
        if reverse:
            hsum = hft_ref[bi, hs, :] + hh
            hn = hsum * lax.rsqrt(jnp.mean(hsum * hsum, axis=0, keepdims=True) + EPS) * gn_ref[hs, :]
            y_ref[bi, :, hs] = (jax.nn.sigmoid(o_ref[bi, :, hs].astype(F32)) * hn.T).astype(y_ref.dtype)
        else:
            y_ref[bi, hs, :] = hh
        wv = (vaug.astype(F32) * w_row).astype(BF16)
        c_scr[bi, h] = decay * c_scr[bi, h] + jnp.dot(wv, kh, preferred_element_type=F32)
        m_scr[bi, h] = jnp.broadcast_to(m_next, (8, LANES))

    @pl.when(step == nc - 1)
    def _():
        cf_ref[...] = c_scr[...]
        mf_ref[...] = m_scr[...]


def _mlstm_scan(qt, k, vt, z, gates, gate_b, c0, m0, reverse, hft=None, norm_g=None):
    B, L, W = k.shape
    T = ML_CHUNK
    nc = L // T
    bb = 4 if B % 4 == 0 else 2
    cj =(lambda j: nc - 1 - j) if reverse else (lambda j: j)
    tok = pl.BlockSpec((bb, T, W), lambda b, j: (b, cj(j), 0))
    ttok = pl.BlockSpec((bb, W, T), lambda b, j: (b, 0, cj(j)))
    cspec = pl.BlockSpec((bb, ML_HEADS, 2 * ML_HD, ML_HD), lambda b, j: (b, 0, 0, 0))
    mspec = pl.BlockSpec((bb, ML_HEADS, 8, LANES), lambda b, j: (b, 0, 0, 0))
    in_specs = [ttok, tok, ttok, pl.BlockSpec((bb, T, LANES), lambda b, j: (b, cj(j), 0)),
                pl.BlockSpec((1, LANES), lambda b, j: (0, 0)), cspec, mspec]
    args = [qt, k, vt, gates, gate_b, c0, m0]
    if reverse:
        in_specs += [ttok, pl.BlockSpec((bb, T, W), lambda b, j: (b, cj(j), Z_MLO // W)),
                     pl.BlockSpec((W, T), lambda b, j: (0, 0))]
        args += [hft, z, jnp.broadcast_to(norm_g.astype(F32).reshape(W, 1), (W, T))]
    return pl.pallas_call(
        functools.partial(_mlstm_kernel, reverse=reverse, nc=nc, bb=bb),
        grid=(B // bb, nc),
        in_specs=in_specs,
        out_specs=[tok if reverse else ttok, cspec, mspec],
        out_shape=[jax.ShapeDtypeStruct((B, L, W), BF16) if reverse else jax.ShapeDtypeStruct((B, W, L), F32),
                   jax.ShapeDtypeStruct(c0.shape, F32), jax.ShapeDtypeStruct(m0.shape, F32)],
        scratch_shapes=[pltpu.VMEM((bb, ML_HEADS, 2 * ML_HD, ML_HD), F32), pltpu.VMEM((bb, ML_HEADS, 8, LANES), F32)],
        compiler_params=_cp(("parallel", "arbitrary")),
        name="mlstm_reverse" if reverse else "mlstm_forward",
    )(*args)


def _fft_dims(Lp):
    n1 = 2 * Lp // FFT_N2
    nt1 = Lp // FFT_N2
    nk1 = -(-(n1 // 2 + 1) // 8) * 8
    return n1, nt1, nk1


def _fft_tables(Lp):
    n1, nt1, nk1 = _fft_dims(Lp)
    N = 2 * Lp
    k1 = jnp.arange(nk1)
    t1 = jnp.arange(nt1)
    ang_a = (2.0 * math.pi / n1) * ((k1[:, None] * t1[None, :]) % n1).astype(F32)
    fa = jnp.stack([jnp.cos(ang_a), -jnp.sin(ang_a)], axis=1).reshape(2 * nk1, nt1)
    wgt = jnp.where((k1 == 0) | (k1 == n1 // 2), 1.0, 2.0) * (k1 <= n1 // 2) / N
    fai = jnp.stack([jnp.cos(ang_a) * wgt[:, None], -jnp.sin(ang_a) * wgt[:, None]], axis=1)
    fai = fai.reshape(2 * nk1, nt1).T
    k2 = jnp.arange(FFT_N2)
    t2 = jnp.arange(FFT_N2)
    idx = (t2[None, None, :] * k1[:, None, None] + n1 * t2[None, None, :] * k2[None, :, None]) % N
    phi = (2.0 * math.pi / N) * idx.astype(F32)
    gr, gi = jnp.cos(phi), -jnp.sin(phi)
    gfwd = jnp.concatenate([jnp.concatenate([gr, -gi], axis=2), jnp.concatenate([gi, gr], axis=2)], axis=1)
    grt, git = jnp.swapaxes(gr, 1, 2), jnp.swapaxes(gi, 1, 2)
    ginv = jnp.concatenate([jnp.concatenate([grt, git], axis=2), jnp.concatenate([-git, grt], axis=2)], axis=1)
    eye = jnp.eye(FFT_TK, dtype=F32)
    return (jnp.kron(fa, eye).astype(BF16), jnp.kron(fai, eye).astype(BF16), gfwd.astype(BF16), ginv.astype(BF16))


FFT_TB = 32
FFT_TK = 8
FFT_CB = 512


def _stage_a_tile(fa_ref, x, rows):
    nt1, tb, cb = x.shape
    x = x.reshape(nt1, tb // FFT_TK, FFT_TK, cb)
    parts = []
    for j in range(tb // FFT_TK):
        xj = x[:, j].reshape(nt1 * FFT_TK, cb).astype(BF16)
        r = jnp.dot(fa_ref[...], xj, preferred_element_type=F32)
        parts.append(r.reshape(rows, 1, FFT_TK, cb))
    return jnp.concatenate(parts, axis=1).reshape(rows, tb, cb)


def _fa_kernel(fa_ref, x_ref, o_ref):
    o_ref[0] = _stage_a_tile(fa_ref, x_ref[0].astype(F32), o_ref.shape[1]).astype(o_ref.dtype)


def _fft_stage_a(fa, y, out_dtype):
    B, Lp, C = y.shape
    nt1 = Lp // FFT_N2
    rows = fa.shape[0] // FFT_TK
    return pl.pallas_call(
        _fa_kernel,
        grid=(B, FFT_N2 // FFT_TB, C // FFT_CB),
        in_specs=[pl.BlockSpec(fa.shape, lambda b, j, c: (0, 0)),
                  pl.BlockSpec((1, nt1, FFT_TB, FFT_CB), lambda b, j, c: (b, 0, j, c))],
        out_specs=pl.BlockSpec((1, rows, FFT_TB, FFT_CB), lambda b, j, c: (b, 0, j, c)),
        out_shape=jax.ShapeDtypeStruct((B, rows, FFT_N2, C), out_dtype),
        compiler_params=_cp(("parallel", "parallel", "parallel")),
        name="fft_stage_a",
    )(fa, y.reshape(B, nt1, FFT_N2, C))


def _fc_filter_kernel(g_ref, s_ref, ss_ref, h_ref, *, kb):
    C = HY_CH
    for i in range(kb):
        for n in range(HY_ORDER):
            cf, cb = (2 * n) * C, (2 * n + 1) * C
            scale = lax.rsqrt(ss_ref[0:1, cf:cf + C] + ss_ref[0:1, cb:cb + C] + EPS)
            sf = s_ref[0, 2 * i:2 * i + 2, :, cf:cf + C].reshape(2 * FFT_N2, C).astype(BF16)
            sb = s_ref[0, 2 * i:2 * i + 2, :, cb:cb + C].reshape(2 * FFT_N2, C).astype(BF16)
            xf = jnp.dot(g_ref[i], sf, preferred_element_type=F32)
            xb = jnp.dot(g_ref[i], sb, preferred_element_type=F32)
            h_ref[n, 2 * i] = (xf[:FFT_N2] + xb[:FFT_N2]) * scale
            h_ref[n, 2 * i + 1] = (xf[FFT_N2:] - xb[FFT_N2:]) * scale


def _fft_filter_spectrum(gfwd, s_filt, sumsq):
    rows = s_filt.shape[1]
    C4 = s_filt.shape[-1]
    kb = 2
    return pl.pallas_call(
        functools.partial(_fc_filter_kernel, kb=kb),
        grid=(rows // (2 * kb),),
        in_specs=[pl.BlockSpec((kb, 2 * FFT_N2, 2 * FFT_N2), lambda i: (i, 0, 0)),
                  pl.BlockSpec((1, 2 * kb, FFT_N2, C4), lambda i: (0, i, 0, 0)),
                  pl.BlockSpec((1, C4), lambda i: (0, 0))],
        out_specs=pl.BlockSpec((HY_ORDER, 2 * kb, FFT_N2, HY_CH), lambda i: (0, i, 0, 0)),
        out_shape=jax.ShapeDtypeStruct((HY_ORDER, rows, FFT_N2, HY_CH), F32),
        compiler_params=_cp(("parallel",)),
        name="fft_filter_spectrum",
    )(gfwd, s_filt, sumsq)


def _fc_kernel(g_ref, gi_ref, h_ref, s_ref, o_ref, *, kb):
    C = s_ref.shape[-1]
    x = [jnp.dot(g_ref[i], s_ref[0, 2 * i:2 * i + 2].reshape(2 * FFT_N2, C), preferred_element_type=F32)
         for i in range(kb)]
    z = []
    for i in range(kb):
        xr, xi = x[i][:FFT_N2], x[i][FFT_N2:]
        hr, hi = h_ref[0, 2 * i], h_ref[0, 2 * i + 1]
        z.append(jnp.concatenate([xr * hr - xi * hi, xr * hi + xi * hr], axis=0).astype(BF16))
    bm = [jnp.dot(gi_ref[i], z[i], preferred_element_type=F32) for i in range(kb)]
    for i in range(kb):
        o_ref[0, 2 * i:2 * i + 2] = bm[i].reshape(2, FFT_N2, C).astype(o_ref.dtype)


def _fft_stage_c(gfwd, ginv, hspec, order, s):
    B, rows, _, C = s.shape
    kb = 8 if rows % 16 == 0 else 4
    sspec =pl.BlockSpec((1, 2 * kb, FFT_N2, C), lambda i, b: (b, i, 0, 0))
    gspec = pl.BlockSpec((kb, 2 * FFT_N2, 2 * FFT_N2), lambda i, b: (i, 0, 0))
    return pl.pallas_call(
        functools.partial(_fc_kernel, kb=kb),
        grid=(rows // (2 * kb), B),
        in_specs=[gspec, gspec,
                  pl.BlockSpec((1, 2 * kb, FFT_N2, C), lambda i, b: (order, i, 0, 0)), sspec],
        out_specs=sspec,
        out_shape=jax.ShapeDtypeStruct(s.shape, BF16),
        compiler_params=_cp(("parallel", "arbitrary")),
        name="fft_stage_c",
    )(gfwd, ginv, hspec, s)


def _fai_kernel(fai_ref, b_ref, y_ref, gate_ref, skip_ref, *rest, fused):
    if fused:
        fa_ref, o_ref, s_ref = rest
    else:
        (o_ref,) = rest
    nt1, tb, cb = y_ref.shape[1:]
    rows = b_ref.shape[1]
    bm = b_ref[0].astype(F32).reshape(rows, tb // FFT_TK, FFT_TK, cb)
    parts = []
    for j in range(tb // FFT_TK):
        bj = bm[:, j].reshape(rows * FFT_TK, cb).astype(BF16)
        yf = jnp.dot(fai_ref[...], bj, preferred_element_type=F32)
        parts.append(yf.reshape(nt1, 1, FFT_TK, cb))
    yf = jnp.concatenate(parts, axis=1).reshape(nt1, tb, cb)
    out = (gate_ref[0].astype(F32) * (yf + skip_ref[...].reshape(1, 1, cb) * y_ref[0].astype(F32))).astype(o_ref.dtype)
    o_ref[0] = out
    if fused:
        s_ref[0] = _stage_a_tile(fa_ref, out.astype(F32), s_ref.shape[1]).astype(s_ref.dtype)


def _fft_stage_a_inv(fai, bm, y, gate, skip, fa_next=None):
    B, Lp, C = y.shape
    nt1 = Lp // FFT_N2
    rows = fai.shape[1] // FFT_TK
    tok = pl.BlockSpec((1, nt1, FFT_TB, FFT_CB), lambda b, j, c: (b, 0, j, c))
    spec_blk = pl.BlockSpec((1, rows, FFT_TB, FFT_CB), lambda b, j, c: (b, 0, j, c))
    in_specs = [pl.BlockSpec(fai.shape, lambda b, j, c: (0, 0)), spec_blk,
                tok, tok, pl.BlockSpec((1, FFT_CB), lambda b, j, c: (0, c))]
    args = [fai, bm, y.reshape(B, nt1, FFT_N2, C), gate.reshape(B, nt1, FFT_N2, C), skip.astype(F32).reshape(1, C)]
    out_specs, out_shape = [tok], [jax.ShapeDtypeStruct((B, nt1, FFT_N2, C), BF16)]
    if fa_next is not None:
        in_specs.append(pl.BlockSpec(fa_next.shape, lambda b, j, c: (0, 0)))
        args.append(fa_next)
        out_specs.append(spec_blk)
        out_shape.append(jax.ShapeDtypeStruct((B, rows, FFT_N2, C), BF16))
    res = pl.pallas_call(
        functools.partial(_fai_kernel, fused=fa_next is not None),
        grid=(B, FFT_N2 // FFT_TB, C // FFT_CB),
        in_specs=in_specs,
        out_specs=out_specs,
        out_shape=out_shape,
        compiler_params=_cp(("parallel", "parallel", "parallel")),
        name="fft_stage_a_inv",
    )(*args)
    out = res[0].reshape(B, Lp, C)
    return (out, res[1]) if fa_next is not None else out


def _hgen_kernel(z_ref, w1_ref, b1_ref, fr_ref, w2_ref, b2_ref, w3_ref, dec_ref, h_ref, ss_ref):
    i = pl.program_id(0)
    z = z_ref[...]
    tm = z.shape[0]
    h = jnp.sin(fr_ref[0:1, :] * (jnp.dot(z, w1_ref[...], preferred_element_type=F32, precision=HIGHEST)
                                 + b1_ref[...]))
    h = jnp.sin(fr_ref[1:2, :] * (jnp.dot(h, w2_ref[...], preferred_element_type=F32, precision=HIGHEST)
                                 + b2_ref[...]))
    h = jnp.dot(h, w3_ref[...], preferred_element_type=F32, precision=HIGHEST)
    h = h * jnp.exp(-z[:, 0:1] * jnp.abs(dec_ref[...]))
    row = lax.broadcasted_iota(jnp.int32, h.shape, 0) + i * tm
    col = lax.broadcasted_iota(jnp.int32, h.shape, 1)
    h = jnp.where((row == 0) & ((col // HY_CH) % 2 == 1), 0.0, h)
    h_ref[...] = h

    @pl.when(i == 0)
    def _():
        ss_ref[...] = jnp.zeros_like(ss_ref)

    ss_ref[...] += jnp.sum(h * h, axis=0, keepdims=True)


def _hyena_filters(L, p):
    t = jnp.arange(L, dtype=F32)
    tn = t / (L - 1)
    w = 2.0 * math.pi * t / L
    bands = jnp.linspace(1e-4, HY_BANDS - 1, HY_BANDS, dtype=F32)
    ang = w[:, None] * bands[None, :]
    z = jnp.concatenate([tn[:, None], jnp.cos(ang), -jnp.sin(ang)], axis=-1)
    z = jnp.pad(z, ((0, 0), (0, LANES - HY_EMB)))
    w1 = jnp.pad(p['hy_pe_w1'].astype(F32), ((0, LANES - HY_EMB), (0, 0)))
    nf = HY_ORDER * 2 * HY_CH
    tm = min(512, L)
    const = lambda shape: pl.BlockSpec(shape, lambda i: (0,) * len(shape))
    return pl.pallas_call(
        _hgen_kernel,
        grid=(L // tm,),
        in_specs=[pl.BlockSpec((tm, LANES), lambda i: (i, 0)), const((LANES, HY_FFN)), const((1, HY_FFN)),
                  const((2, HY_FFN)), const((HY_FFN, HY_FFN)), const((1, HY_FFN)), const((HY_FFN, nf)),
                  const((1, nf))],
        out_specs=[pl.BlockSpec((tm, nf), lambda i: (i, 0)), const((1, nf))],
        out_shape=[jax.ShapeDtypeStruct((L, nf), F32), jax.ShapeDtypeStruct((1, nf), F32)],
        compiler_params=_cp(("arbitrary",)),
        name="hyena_filter_gen",
    )(z, w1, p['hy_pe_b1'].reshape(1, HY_FFN), p['hy_freq'], p['hy_pe_w2'], p['hy_pe_b2'].reshape(1, HY_FFN),
      p['hy_pe_w3'], p['hy_decay'].reshape(1, nf))


def _hyena_branch(z, p, B, L):
    v, x1, x2 = _hy_prep(z, p['hy_conv_w'], p['hy_conv_b'], L)
    Lp = max(L, 2048)
    fa, fai, gfwd, ginv = _fft_tables(Lp)
    hfilt, sumsq = _hyena_filters(L, p)
    pad3 = lambda a: a.reshape(B, L, HY_CH) if Lp == L else jnp.pad(a.reshape(B, L, HY_CH), ((0, 0), (0, Lp - L), (0, 0)))
    hf = hfilt if Lp == L else jnp.pad(hfilt, ((0, Lp - L), (0, 0)))
    s_filt = _fft_stage_a(fa, hf[None], F32)
    hspec = _fft_filter_spectrum(gfwd, s_filt, sumsq)
    y = pad3(v)
    s = _fft_stage_a(fa, y, BF16)
    bm = _fft_stage_c(gfwd, ginv, hspec, 0, s)
    y, s = _fft_stage_a_inv(fai, bm, y, pad3(x1), p['hy_skip'][0], fa_next=fa)
    bm = _fft_stage_c(gfwd, ginv, hspec, 1, s)
    y = _fft_stage_a_inv(fai, bm, y, pad3(x2), p['hy_skip'][1])
    return y[:, :L].reshape(B * L, HY_CH)


def _merge_kernel(ya_ref, yb_ref, yc_ref, yd_ref, g0_ref, g1_ref, g2_ref, g3_ref, wup_ref, wout_ref,
                  x_ref, gate_ref, o_ref):
    acc = None
    for n, (y_ref, g_ref) in enumerate(((ya_ref, g0_ref), (yb_ref, g1_ref), (yc_ref, g2_ref), (yd_ref, g3_ref))):
        t = jax.nn.sigmoid(g_ref[...].astype(F32)) * jnp.dot(y_ref[...], wup_ref[n], preferred_element_type=F32)
        acc = t if acc is None else acc + t
    yl = jnp.dot(acc.astype(BF16), wout_ref[...], preferred_element_type=F32)
    o_ref[...] = x_ref[...] + gate_ref[0] * yl


def _merge(ys, z, w_up, w_out, x, gate, seq_len):
    T, D = x.shape
    tm = min(512, seq_len)
    per = seq_len // tm
    Wy = ys[0].shape[1]
    yspec = pl.BlockSpec((tm, Wy), lambda i: (i, 0))
    gspecs = [pl.BlockSpec((tm, D), functools.partial(lambda i, n: (i, Z_GATE // D + n), n=n)) for n in range(4)]
    return pl.pallas_call(
        _merge_kernel,
        grid=(T // tm,),
        in_specs=[yspec] * 4 + gspecs + [
            pl.BlockSpec((4, Wy, D), lambda i: (0, 0, 0)), pl.BlockSpec((D, D), lambda i: (0, 0)),
            pl.BlockSpec((tm, D), lambda i: (i, 0)), pl.BlockSpec((1, 1, D), lambda i: (i // per, 0, 0))],
        out_specs=pl.BlockSpec((tm, D), lambda i: (i, 0)),
        out_shape=jax.ShapeDtypeStruct((T, D), F32),
        compiler_params=_cp(("parallel",)),
        name="merge_branches",
    )(*ys, z, z, z, z, w_up, w_out, x, gate)


def _mlp_kernel(x_ref, g_ref, sc_ref, sh_ref, w1_ref, b1_ref, w2_ref, b2_ref, gate_ref, fg_ref, o_ref,
                h_ref, acc_ref, *, nk, final):
    k = pl.program_id(1)

    @pl.when(k == 0)
    def _():
        y = _rms(x_ref[...], g_ref[...])
        h_ref[...] = (y * (1.0 + sc_ref[0]) + sh_ref[0]).astype(BF16)
        acc_ref[...] = jnp.zeros_like(acc_ref)

    a = jnp.maximum(jnp.dot(h_ref[...], w1_ref[...], preferred_element_type=F32) + b1_ref[...], 0.0)
    acc_ref[...] += jnp.dot((a * a).astype(BF16), w2_ref[...], preferred_element_type=F32)

    @pl.when(k == nk - 1)
    def _():
        out = x_ref[...] + gate_ref[0] * (acc_ref[...] + b2_ref[...])
        if final:
            out = _rms(out, fg_ref[...])
        o_ref[...] = out


def _mlp(x, g, sc, sh, w1, b1, w2, b2, gate, final_g, seq_len, final):
    T, D = x.shape
    F = w1.shape[1]
    tm = min(1024, seq_len)
    per = seq_len // tm
    tk = 1024
    nk = F // tk
    row = lambda i, k: (i // per, 0, 0)
    return pl.pallas_call(
        functools.partial(_mlp_kernel, nk=nk, final=final),
        grid=(T // tm, nk),
        in_specs=[pl.BlockSpec((tm, D), lambda i, k: (i, 0)), pl.BlockSpec((1, D), lambda i, k: (0, 0)),
                  pl.BlockSpec((1, 1, D), row), pl.BlockSpec((1, 1, D), row),
                  pl.BlockSpec((D, tk), lambda i, k: (0, k)), pl.BlockSpec((1, tk), lambda i, k: (0, k)),
                  pl.BlockSpec((tk, D), lambda i, k: (k, 0)), pl.BlockSpec((1, D), lambda i, k: (0, 0)),
                  pl.BlockSpec((1, 1, D), row), pl.BlockSpec((1, D), lambda i, k: (0, 0))],
        out_specs=pl.BlockSpec((tm, D), lambda i, k: (i, 0)),
        out_shape=jax.ShapeDtypeStruct((T, D), F32),
        scratch_shapes=[pltpu.VMEM((tm, D), BF16), pltpu.VMEM((tm, D), F32)],
        compiler_params=_cp(("parallel", "arbitrary")),
        name="mlp",
    )(x, g.reshape(1, D), sc, sh, w1, b1.reshape(1, F), w2, b2.reshape(1, D), gate, final_g.reshape(1, D))


def _pack_w_in(w_in):
    hy_e = 3 * HY_CH
    ga_e = hy_e + (GA_HEADS + 2 * GA_KV) * GA_HD
    mw = ML_HEADS * ML_HD
    ml_e = ga_e + 4 * mw + 16
    wa_e = ml_e + (WA_HEADS + 2 * WA_KV) * WA_HD
    hy, ga = w_in[:, :hy_e], w_in[:, hy_e:ga_e]
    ml = w_in[:, ga_e:ml_e]
    wa = w_in[:, ml_e:wa_e]
    gate = w_in[:, wa_e:]
    waq, wakv = wa[:, :WA_HEADS * WA_HD], wa[:, WA_HEADS * WA_HD:]
    pad = jnp.zeros((w_in.shape[0], Z_COLS - Z_WAKV - wakv.shape[1]), w_in.dtype)
    packed = jnp.concatenate([hy, waq, ga, ml[:, :2 * mw], ml[:, 2 * mw:3 * mw], ml[:, 3 * mw:4 * mw], gate, wakv, pad],
                             axis=1)
    wg = jnp.pad(ml[:, 4 * mw:], ((0, 0), (0, LANES - 16)))
    return packed.astype(BF16), wg


def _token_mixers(zl, zc, gl, gc, p, B, L, Lc, with_ctx_out):
    ya_l = _hyena_branch(zl, p, B, L)
    ya_c = _hyena_branch(zc, p, B, Lc) if with_ctx_out else None
    cos, sin = _rope_tables(L, GA_HD)
    ql, kl, vl = _ga_prep(zl, cos, sin, p['ga_q_g'], p['ga_k_g'], B, L, True)
    qc, kc, vc = _ga_prep(zc, cos[:Lc], sin[:Lc], p['ga_q_g'], p['ga_k_g'], B, Lc, False)
    r3 = lambda a, n: a.reshape(B, n, a.shape[-1])
    k_all = jnp.concatenate([r3(kl, L), r3(kc, Lc)], axis=1)
    vt_all = jnp.concatenate([vl, vc], axis=2)
    yb_l = _global_attention(ql, k_all, vt_all).reshape(B * L, -1)
    yb_c = _global_attention(qc, r3(kc, Lc), vc).reshape(B * Lc, -1) if with_ctx_out else None
    mq_l, mk_l, mv_l = _ml_prep(zl, p['ml_conv_w'], p['ml_conv_b'], B, L)
    mq_c, mk_c, mv_c = _ml_prep(zc, p['ml_conv_w'], p['ml_conv_b'], B, Lc)
    gb = jnp.pad(p['ml_gate_b'].astype(F32), (0, LANES - 16)).reshape(1, LANES)
    c0 = jnp.zeros((B, ML_HEADS, 2 * ML_HD, ML_HD), F32)
    m0 = jnp.zeros((B, ML_HEADS, 8, LANES), F32)
    zl3, zc3, gl3, gc3 = r3(zl, L), r3(zc, Lc), r3(gl, L), r3(gc, Lc)
    h_cf, cf, mf = _mlstm_scan(mq_c, r3(mk_c, Lc), mv_c, zc3, gc3, gb, c0, m0, False)
    yc_c, cb, mb = _mlstm_scan(mq_c, r3(mk_c, Lc), mv_c, zc3, gc3, gb, c0, m0, True, h_cf, p['ml_norm_g'])
    h_lf, _, _ = _mlstm_scan(mq_l, r3(mk_l, L), mv_l, zl3, gl3, gb, cf, mf, False)
    yc_l, _, _ = _mlstm_scan(mq_l, r3(mk_l, L), mv_l, zl3, gl3, gb, cb, mb, True, h_lf, p['ml_norm_g'])
    yc_l = yc_l.reshape(B * L, -1)
    yc_c = yc_c.reshape(B * Lc, -1)
    cosw, sinw = _rope_tables(L, WA_HD)
    wq_l, wk_l, wv_l = _wa_prep(zl, cosw, sinw, B, L, True)
    wq_c, wk_c, wv_c = _wa_prep(zc, cosw[:Lc], sinw[:Lc], B, Lc, False)
    yd_l = _window_attention(wq_l, r3(wk_c, Lc), wv_c, p['wa_sink'], r3(wk_l, L), wv_l).reshape(B * L, -1)
    yd_c = (_window_attention(wq_c, r3(wk_c, Lc), wv_c, p['wa_sink']).reshape(B * Lc, -1)
            if with_ctx_out else None)
    return (ya_l, yb_l, yc_l, yd_l), (ya_c, yb_c, yc_c, yd_c)


def kernel(x, c, ctx, c_ctx, w_mod, b_mod, ln1_g, ln2_g, w_in, hy_conv_w, hy_conv_b,
           hy_pe_w1, hy_pe_b1, hy_freq, hy_pe_w2, hy_pe_b2, hy_pe_w3, hy_decay, hy_skip,
           ga_q_g, ga_k_g, ml_conv_w, ml_conv_b, ml_gate_b, ml_norm_g, wa_sink, w_up, w_out,
           mlp_w1, mlp_b1, mlp_w2, mlp_b2, final_g):
    B, L, D = x.shape
    Lc = ctx.shape[1]
    R = -(-(B + 1) // 8) * 8
    cvec = jnp.zeros((R, D), F32).at[:B].set(c).at[B].set(c_ctx)
    mod = _modulation(cvec, w_mod, b_mod)
    xl = x.reshape(B * L, D)
    xc = ctx.reshape(B * Lc, D)
    for l in range(DEPTH):
        with_ctx_out = l < DEPTH - 1
        p = dict(hy_conv_w=hy_conv_w[l], hy_conv_b=hy_conv_b[l], hy_pe_w1=hy_pe_w1[l],
                 hy_pe_b1=hy_pe_b1[l], hy_freq=hy_freq[l], hy_pe_w2=hy_pe_w2[l], hy_pe_b2=hy_pe_b2[l],
                 hy_pe_w3=hy_pe_w3[l], hy_decay=hy_decay[l], hy_skip=hy_skip[l],
                 ga_q_g=ga_q_g[l], ga_k_g=ga_k_g[l], ml_conv_w=ml_conv_w[l], ml_conv_b=ml_conv_b[l],
                 ml_gate_b=ml_gate_b[l], ml_norm_g=ml_norm_g[l], wa_sink=wa_sink[l])
        ml_rows = mod[l, :B].reshape(B, 1, 6 * D)
        mc_rows = jnp.broadcast_to(mod[l, B].reshape(1, 1, 6 * D), (B, 1, 6 * D))
        part = lambda m, n: m[:, :, n * D:(n + 1) * D]
        w_pack, w_gate = _pack_w_in(w_in[l])
        wg = w_gate.astype(BF16)
        zl, gl = _normmod_matmul(xl, ln1_g[l], part(ml_rows, 1), part(ml_rows, 0), w_pack, wg, L, Z_TN)
        zc, gc = _normmod_matmul(xc, ln1_g[l], part(mc_rows, 1), part(mc_rows, 0), w_pack, wg, Lc, Z_TN)
        ys_l, ys_c = _token_mixers(zl, zc, gl, gc, p, B, L, Lc, with_ctx_out)
        wup = w_up[l].astype(BF16)
        wout = w_out[l].astype(BF16)
        w1, w2 = mlp_w1[l].astype(BF16), mlp_w2[l].astype(BF16)
        xl = _merge(ys_l, zl, wup, wout, xl, part(ml_rows, 2), L)
        xl = _mlp(xl, ln2_g[l], part(ml_rows, 4), part(ml_rows, 3), w1, mlp_b1[l], w2, mlp_b2[l],
                  part(ml_rows, 5), final_g, L, final=(l == DEPTH - 1))
        if with_ctx_out:
            xc = _merge(ys_c, zc, wup, wout, xc, part(mc_rows, 2), Lc)
            xc = _mlp(xc, ln2_g[l], part(mc_rows, 4), part(mc_rows, 3), w1, mlp_b1[l], w2, mlp_b2[l],
                      part(mc_rows, 5), final_g, Lc, final=False)
    return xl.reshape(B, L, D)
```

```python
import functools
import math

import jax
import jax.numpy as jnp
from jax import lax
from jax.experimental import pallas as pl
from jax.experimental.pallas import tpu as pltpu

F32 = jnp.float32
BF16 = jnp.bfloat16
HIGHEST = lax.Precision.HIGHEST

D_MODEL = 1024
DEPTH = 2
GRID_W = 64
HY_CH = 512
HY_ORDER = 2
HY_BANDS = 16
HY_EMB = 1 + 2 * HY_BANDS
HY_FFN = 64
GA_HEADS, GA_KV, GA_HD = 4, 2, 128
ML_HEADS, ML_HD = 4, 128
WA_HEADS, WA_KV, WA_HD = 8, 2, 64
WINDOW = 128
ROPE_BASE = 10000.0
D_FF = 4 * D_MODEL
EPS = 1e-6
NEG = -1e30
LOG2E = 1.4426950408889634

LANES = 128
V7X_VMEM_LIMIT = 48 * 1024 * 1024

Z_HY = 0
Z_WAQ = 1536
Z_GA = 2048
Z_MLQK = 3072
Z_MLV = 4096
Z_MLO = 4608
Z_GATE = 5120
Z_WAKV = 9216
Z_COLS = 9728
Z_TN = 2432

ML_CHUNK = 256
FFT_N2 = 128
GA_CB = 256
WA_PAIR = 2
WA_SUB = 8
GA_TK = 256
GA_CARRY = 4
GA_UNROLL = 32


def _cp(sem, vmem=V7X_VMEM_LIMIT):
    return pltpu.CompilerParams(dimension_semantics=sem, vmem_limit_bytes=vmem)


def _rms(x, g):
    return x * lax.rsqrt(jnp.mean(x * x, axis=-1, keepdims=True) + EPS) * g


def _mod_kernel(c_ref, w_ref, b_ref, o_ref):
    c = c_ref[...]
    s = c * jax.nn.sigmoid(c)
    o_ref[0] = jnp.dot(s, w_ref[0], preferred_element_type=F32, precision=HIGHEST) + b_ref[0]


def _modulation(cvec, w_mod, b_mod):
    R = cvec.shape[0]
    tn = 1536
    return pl.pallas_call(
        _mod_kernel,
        grid=(DEPTH, 6 * D_MODEL // tn),
        in_specs=[pl.BlockSpec((R, D_MODEL), lambda l, j: (0, 0)),
                  pl.BlockSpec((1, D_MODEL, tn), lambda l, j: (l, 0, j)),
                  pl.BlockSpec((1, 1, tn), lambda l, j: (l, 0, j))],
        out_specs=pl.BlockSpec((1, R, tn), lambda l, j: (l, 0, j)),
        out_shape=jax.ShapeDtypeStruct((DEPTH, R, 6 * D_MODEL), F32),
        compiler_params=_cp(("parallel", "parallel")),
        name="modulation",
    )(cvec, w_mod, b_mod.reshape(DEPTH, 1, 6 * D_MODEL))


def _nmm_kernel(x_ref, g_ref, sc_ref, sh_ref, w_ref, wx_ref, o_ref, ox_ref, h_ref):
    @pl.when(pl.program_id(1) == 0)
    def _():
        y = _rms(x_ref[...], g_ref[...])
        h_ref[...] = (y * (1.0 + sc_ref[0]) + sh_ref[0]).astype(BF16)
        ox_ref[...] = jnp.dot(h_ref[...], wx_ref[...], preferred_element_type=F32)

    o_ref[...] = jnp.dot(h_ref[...], w_ref[...], preferred_element_type=F32).astype(o_ref.dtype)


def _normmod_matmul(x, g, sc, sh, w, wx, seq_len, tn):
    T, D = x.shape
    N = w.shape[1]
    NX = wx.shape[1]
    tm = min(1024, seq_len)
    per = seq_len // tm
    return pl.pallas_call(
        _nmm_kernel,
        grid=(T // tm, N // tn),
        in_specs=[pl.BlockSpec((tm, D), lambda i, j: (i, 0)),
                  pl.BlockSpec((1, D), lambda i, j: (0, 0)),
                  pl.BlockSpec((1, 1, D), lambda i, j: (i // per, 0, 0)),
                  pl.BlockSpec((1, 1, D), lambda i, j: (i // per, 0, 0)),
                  pl.BlockSpec((D, tn), lambda i, j: (0, j)),
                  pl.BlockSpec((D, NX), lambda i, j: (0, 0))],
        out_specs=[pl.BlockSpec((tm, tn), lambda i, j: (i, j)), pl.BlockSpec((tm, NX), lambda i, j: (i, 0))],
        out_shape=[jax.ShapeDtypeStruct((T, N), BF16), jax.ShapeDtypeStruct((T, NX), F32)],
        scratch_shapes=[pltpu.VMEM((tm, D), BF16)],
        compiler_params=_cp(("parallel", "arbitrary")),
        name="normmod_matmul",
    )(x, g.reshape(1, D), sc, sh, w, wx)


def _conv3(u, prev_row, next_row, w_ref, b_ref, c0, c1):
    tm = u.shape[0]
    row = lax.broadcasted_iota(jnp.int32, u.shape, 0)
    up = jnp.where(row == 0, prev_row, pltpu.roll(u, 1, 0))
    dn = jnp.where(row == tm - 1, next_row, pltpu.roll(u, tm - 1, 0))
    return (w_ref[0:1, c0:c1] * up + w_ref[1:2, c0:c1] * u + w_ref[2:3, c0:c1] * dn + b_ref[0:1, c0:c1])


def _halo_rows(zp_ref, zn_ref, per, c0, c1):
    i = pl.program_id(0)
    first = (i % per) == 0
    last = (i % per) == per - 1
    hp = zp_ref.shape[0]
    prev_row = jnp.where(first, 0.0, zp_ref[hp - 1:hp, c0:c1].astype(F32))
    next_row = jnp.where(last, 0.0, zn_ref[0:1, c0:c1].astype(F32))
    return prev_row, next_row


HALO = 16


def _halo_specs(tm, width, col_block, n_rows):
    nb = n_rows // HALO
    r = tm // HALO
    return [pl.BlockSpec((tm, width), lambda i: (i, col_block)),
            pl.BlockSpec((HALO, width), lambda i: (jnp.maximum(i * r - 1, 0), col_block)),
            pl.BlockSpec((HALO, width), lambda i: (jnp.minimum((i + 1) * r, nb - 1), col_block))]


def _hy_prep_kernel(z_ref, zp_ref, zn_ref, w_ref, b_ref, v_ref, x1_ref, x2_ref, *, per):
    outs = (v_ref, x1_ref, x2_ref)
    for c in range(3):
        c0, c1 = c * HY_CH, (c + 1) * HY_CH
        prev_row, next_row = _halo_rows(zp_ref, zn_ref, per, c0, c1)
        u = z_ref[:, c0:c1].astype(F32)
        outs[c][...] = _conv3(u, prev_row, next_row, w_ref, b_ref, c0, c1).astype(outs[c].dtype)


def _hy_prep(z, conv_w, conv_b, seq_len):
    T = z.shape[0]
    tm = min(512, seq_len)
    W = 3 * HY_CH
    out = jax.ShapeDtypeStruct((T, HY_CH), BF16)
    return pl.pallas_call(
        functools.partial(_hy_prep_kernel, per=seq_len // tm),
        grid=(T // tm,),
        in_specs=_halo_specs(tm, W, Z_HY // W, T) + [
            pl.BlockSpec((3, W), lambda i: (0, 0)), pl.BlockSpec((1, W), lambda i: (0, 0))],
        out_specs=[pl.BlockSpec((tm, HY_CH), lambda i: (i, 0))] * 3,
        out_shape=[out, out, out],
        compiler_params=_cp(("parallel",)),
        name="hyena_prep",
    )(z, z, z, conv_w, conv_b.reshape(1, W))


def _ml_prep_kernel(z_ref, zp_ref, zn_ref, zv_ref, w_ref, b_ref, q_ref, k_ref, v_ref, *, per):
    W = ML_HEADS * ML_HD
    for c in range(2):
        c0, c1 = c * W, (c + 1) * W
        prev_row, next_row = _halo_rows(zp_ref, zn_ref, per, c0, c1)
        u = z_ref[:, c0:c1].astype(F32)
        y = _conv3(u, prev_row, next_row, w_ref, b_ref, c0, c1)
        y = y * jax.nn.sigmoid(y)
        if c == 0:
            for h in range(ML_HEADS):
                q_ref[0, h * ML_HD:(h + 1) * ML_HD, :] = y[:, h * ML_HD:(h + 1) * ML_HD].T.astype(q_ref.dtype)
        else:
            k_ref[...] = (y * (ML_HD ** -0.5)).astype(k_ref.dtype)
    for h in range(ML_HEADS):
        v_ref[0, h * ML_HD:(h + 1) * ML_HD, :] = zv_ref[:, h * ML_HD:(h + 1) * ML_HD].astype(F32).T.astype(v_ref.dtype)


def _ml_prep(z, conv_w, conv_b, B, seq_len):
    T = z.shape[0]
    tm = min(512, seq_len)
    per = seq_len // tm
    W = 2 * ML_HEADS * ML_HD
    Wh = W // 2
    tspec = pl.BlockSpec((1, Wh, tm), lambda i: (i // per, 0, i % per))
    tshape = jax.ShapeDtypeStruct((B, Wh, seq_len), BF16)
    return pl.pallas_call(
        functools.partial(_ml_prep_kernel, per=per),
        grid=(T // tm,),
        in_specs=_halo_specs(tm, W, Z_MLQK // W, T) + [
            pl.BlockSpec((tm, Wh), lambda i: (i, Z_MLV // Wh)),
            pl.BlockSpec((3, W), lambda i: (0, 0)), pl.BlockSpec((1, W), lambda i: (0, 0))],
        out_specs=[tspec, pl.BlockSpec((tm, Wh), lambda i: (i, 0)), tspec],
        out_shape=[tshape, jax.ShapeDtypeStruct((T, Wh), BF16), tshape],
        compiler_params=_cp(("parallel",)),
        name="mlstm_prep",
    )(z, z, z, z, conv_w, conv_b.reshape(1, W))


def _rope_tables(L, hd):
    quarter = hd // 4
    inv = ROPE_BASE ** (-jnp.arange(quarter, dtype=F32) / quarter)
    t = jnp.arange(L)
    row = (t // GRID_W).astype(F32)
    col = (t % GRID_W).astype(F32)
    lane = jnp.arange(LANES)
    within = lane % hd
    is_col = (within // (hd // 2)) == 1
    second = ((within % (hd // 2)) // quarter) == 1
    j = within % quarter
    pos = jnp.where(is_col[None, :], col[:, None], row[:, None])
    ang = pos * inv[j][None, :]
    return jnp.cos(ang), jnp.where(second[None, :], jnp.sin(ang), -jnp.sin(ang))


def _rope(x, cos, sin, quarter):
    src = lax.broadcasted_iota(jnp.int32, (LANES, LANES), 0)
    dst = lax.broadcasted_iota(jnp.int32, (LANES, LANES), 1)
    first = ((dst % (2 * quarter)) // quarter) == 0
    perm = jnp.where(src == jnp.where(first, dst + quarter, dst - quarter), 1.0, 0.0).astype(BF16)
    partner = jnp.dot(x.astype(BF16), perm, preferred_element_type=F32)
    return x * cos + partner * sin


def _rms_lanes(x, g):
    ss = jnp.dot((x * x).astype(BF16), jnp.ones((LANES, LANES), BF16), preferred_element_type=F32)
    return x * lax.rsqrt(ss * (1.0 / LANES) + EPS) * g


def _transpose_bf16(x):
    r = lax.broadcasted_iota(jnp.int32, (LANES, LANES), 0)
    c = lax.broadcasted_iota(jnp.int32, (LANES, LANES), 1)
    eye = jnp.where(r == c, 1.0, 0.0).astype(BF16)
    return lax.dot_general(eye, x.astype(BF16), (((1,), (1,)), ((), ())), preferred_element_type=F32).astype(BF16)


def _ga_prep_kernel(z_ref, cos_ref, sin_ref, qg_ref, kg_ref, q_ref, k_ref, v_ref, *, rope):
    nq, nk = GA_HEADS, GA_KV
    heads = range(nq + nk)
    for h in range(nk):
        v_ref[0, h * GA_HD:(h + 1) * GA_HD, :] = _transpose_bf16(
            z_ref[:, (nq + nk + h) * GA_HD:(nq + nk + h + 1) * GA_HD])
    y = [_rms_lanes(z_ref[:, h * GA_HD:(h + 1) * GA_HD].astype(F32), qg_ref[...] if h < nq else kg_ref[...])
         for h in heads]
    if rope:
        y = [_rope(y[h], cos_ref[...], sin_ref[...], GA_HD // 4) for h in heads]
    for h in heads:
        if h < nq:
            q_ref[0, h * GA_HD:(h + 1) * GA_HD, :] = _transpose_bf16(y[h] * (GA_HD ** -0.5 * LOG2E))
        else:
            k_ref[:, (h - nq) * GA_HD:(h - nq + 1) * GA_HD] = y[h].astype(k_ref.dtype)


def _ga_prep(z, cos, sin, qg, kg, B, seq_len, rope):
    T = z.shape[0]
    tm = min(512, seq_len)
    per = seq_len // tm
    W = (GA_HEADS + 2 * GA_KV) * GA_HD
    return pl.pallas_call(
        functools.partial(_ga_prep_kernel, rope=rope),
        grid=(T // tm,),
        in_specs=[pl.BlockSpec((tm, W), lambda i: (i, Z_GA // W)),
                  pl.BlockSpec((tm, LANES), lambda i: (i % per, 0)),
                  pl.BlockSpec((tm, LANES), lambda i: (i % per, 0)),
                  pl.BlockSpec((1, GA_HD), lambda i: (0, 0)),
                  pl.BlockSpec((1, GA_HD), lambda i: (0, 0))],
        out_specs=[pl.BlockSpec((1, GA_HEADS * GA_HD, tm), lambda i: (i // per, 0, i % per)),
                   pl.BlockSpec((tm, GA_KV * GA_HD), lambda i: (i, 0)),
                   pl.BlockSpec((1, GA_KV * GA_HD, tm), lambda i: (i // per, 0, i % per))],
        out_shape=[jax.ShapeDtypeStruct((B, GA_HEADS * GA_HD, seq_len), BF16),
                   jax.ShapeDtypeStruct((T, GA_KV * GA_HD), BF16),
                   jax.ShapeDtypeStruct((B, GA_KV * GA_HD, seq_len), BF16)],
        compiler_params=_cp(("parallel",)),
        name="global_attn_prep",
    )(z, cos, sin, qg.reshape(1, GA_HD), kg.reshape(1, GA_HD))


def _wa_prep_kernel(zq_ref, zkv_ref, cos_ref, sin_ref, q_ref, k_ref, v_ref, *, rope):
    quarter = WA_HD // 4
    nqb = WA_HEADS * WA_HD // LANES
    v_ref[0] = _transpose_bf16(zkv_ref[:, LANES:])
    x = [zq_ref[:, j * LANES:(j + 1) * LANES].astype(F32) for j in range(nqb)] + [zkv_ref[:, 0:LANES].astype(F32)]
    if rope:
        x = [_rope(xj, cos_ref[...], sin_ref[...], quarter) for xj in x]
    for j in range(nqb):
        q_ref[0, j * LANES:(j + 1) * LANES, :] = _transpose_bf16(x[j] * (WA_HD ** -0.5 * LOG2E))
    k_ref[...] = x[nqb].astype(k_ref.dtype)


def _wa_prep(z, cos, sin, B, seq_len, rope):
    T = z.shape[0]
    tm = min(512, seq_len)
    per = seq_len // tm
    WQ = WA_HEADS * WA_HD
    return pl.pallas_call(
        functools.partial(_wa_prep_kernel, rope=rope),
        grid=(T // tm,),
        in_specs=[pl.BlockSpec((tm, WQ), lambda i: (i, Z_WAQ // WQ)),
                  pl.BlockSpec((tm, 2 * LANES), lambda i: (i, Z_WAKV // (2 * LANES))),
                  pl.BlockSpec((tm, LANES), lambda i: (i % per, 0)),
                  pl.BlockSpec((tm, LANES), lambda i: (i % per, 0))],
        out_specs=[pl.BlockSpec((1, WQ, tm), lambda i: (i // per, 0, i % per)),
                   pl.BlockSpec((tm, LANES), lambda i: (i, 0)),
                   pl.BlockSpec((1, LANES, tm), lambda i: (i // per, 0, i % per))],
        out_shape=[jax.ShapeDtypeStruct((B, WQ, seq_len), BF16),
                   jax.ShapeDtypeStruct((T, LANES), BF16),
                   jax.ShapeDtypeStruct((B, LANES, seq_len), BF16)],
        compiler_params=_cp(("parallel",)),
        name="window_attn_prep",
    )(z, z, cos, sin)


def _ga_kernel(q_ref, k_ref, vt_ref, o_ref, acc_ref, m_ref, *, nchunks, tk, tq):
    acc_ref[...] = jnp.zeros_like(acc_ref)
    m_ref[...] = jnp.full_like(m_ref, NEG)
    qt = jnp.concatenate([q_ref[0, 0:GA_HD, :], q_ref[0, GA_HD:, :]], axis=1)

    W = GA_CB
    nblk = 2 * tq // W

    def scores(k, blocks=None):
        blocks = range(nblk) if blocks is None else blocks
        return tuple(jnp.dot(k, qt[:, i * W:(i + 1) * W], preferred_element_type=F32) for i in blocks)

    def softmax_pv(s_blocks, vt):
        cols = [slice(i * W, (i + 1) * W) for i in range(nblk)]
        m_old = [m_ref[:, cs] for cs in cols]
        m_new = [jnp.maximum(mo, jnp.max(s, axis=0, keepdims=True)) for mo, s in zip(m_old, s_blocks)]
        alpha = [jnp.exp2(mo - mn) for mo, mn in zip(m_old, m_new)]
        vaug = jnp.concatenate([vt, jnp.ones((16, vt.shape[1]), BF16)], axis=0)
        pv = [jnp.dot(vaug, jnp.exp2((s - mn).astype(BF16)), preferred_element_type=F32)
              for s, mn in zip(s_blocks, m_new)]
        for i, cs in enumerate(cols):
            acc_ref[:, cs] = alpha[i] * acc_ref[:, cs] + pv[i]
            m_ref[:, cs] = m_new[i]

    ahead, late = range(min(GA_CARRY, nblk)), range(min(GA_CARRY, nblk), nblk)

    def body(j, s):
        off = pl.multiple_of(j * tk, tk)
        s_late = scores(k_ref[0, pl.ds(off, tk), :], late)
        s_next = scores(k_ref[0, pl.ds(pl.multiple_of((j + 1) * tk, tk), tk), :], ahead)
        softmax_pv(s + s_late, vt_ref[0, :, pl.ds(off, tk)])
        return s_next
    s = scores(k_ref[0, 0:tk, :], ahead)
    if nchunks > 1:
        s = lax.fori_loop(0, nchunks - 1, body, s, unroll=GA_UNROLL if (nchunks - 1) % GA_UNROLL == 0 else 1)
    last = slice((nchunks - 1) * tk, nchunks * tk)
    softmax_pv(s + scores(k_ref[0, last, :], late), vt_ref[0, :, last])
    o = acc_ref[0:GA_HD, :] / acc_ref[GA_HD:GA_HD + 1, :]
    for h in range(2):
        o_ref[0, :, h * GA_HD:(h + 1) * GA_HD] = o[:, h * tq:(h + 1) * tq].T.astype(o_ref.dtype)


def _global_attention(qt, k, vt):
    B, _, Lq = qt.shape
    Lk = k.shape[1]
    tq = min(512, Lq)
    tk = min(GA_TK, Lk)
    W = 2 * GA_HD
    return pl.pallas_call(
        functools.partial(_ga_kernel, nchunks=Lk // tk, tk=tk, tq=tq),
        grid=(B, GA_KV, Lq // tq),
        in_specs=[pl.BlockSpec((1, W, tq), lambda b, g, i: (b, g, i)),
                  pl.BlockSpec((1, Lk, GA_HD), lambda b, g, i: (b, 0, g)),
                  pl.BlockSpec((1, GA_HD, Lk), lambda b, g, i: (b, g, 0))],
        out_specs=pl.BlockSpec((1, tq, W), lambda b, g, i: (b, i, g)),
        out_shape=jax.ShapeDtypeStruct((B, Lq, GA_HEADS * GA_HD), BF16),
        scratch_shapes=[pltpu.VMEM((GA_HD + 16, 2 * tq), F32), pltpu.VMEM((1, 2 * tq), F32)],
        compiler_params=_cp(("parallel", "parallel", "parallel")),
        name="global_attention",
    )(qt, k, vt)


def _wa_kernel(*refs, band, nq, nsub):
    if band:
        q_ref = refs[0]
        kb = refs[1:nsub + 3]
        vb = refs[nsub + 3:2 * nsub + 5]
        kc_ref, vc_ref, sink_ref, o_ref = refs[2 * nsub + 5:]
    else:
        q_ref, kc_ref, vc_ref, sink_ref, o_ref = refs
    tq = WINDOW
    Lc = kc_ref.shape[1]
    step = pl.program_id(1)
    G = WA_HEADS // WA_KV
    cols = WA_PAIR * tq
    npr = WA_HEADS // WA_PAIR
    grp = [(pr * WA_PAIR) // G for pr in range(npr)]
    zeros = jnp.zeros((WA_HD, cols), BF16)
    if band:
        nb = 3 * tq
        c = lax.broadcasted_iota(jnp.int32, (nb + Lc, cols), 0)
        r = lax.broadcasted_iota(jnp.int32, (nb + Lc, cols), 1) % tq
        in_band = (c >= r) & (c <= r + 2 * WINDOW)
    ones_rows = jnp.ones((16, (3 * tq if band else 0) + Lc), BF16)
    chains = [(u, pr) for u in range(nsub) for pr in range(npr)]
    keys, vals, valid = [], [], []
    for u in range(nsub):
        if band:
            keys.append(jnp.concatenate([kb[u][0], kb[u + 1][0], kb[u + 2][0], kc_ref[0]], axis=0))
            vals.append(jnp.concatenate([vb[u][0], vb[u + 1][0], vb[u + 2][0], vc_ref[0]], axis=1))
            qi = step * nsub + u
            lo = jnp.where(qi == 0, tq, 0)
            hi = jnp.where(qi == nq - 1, 2 * tq, nb)
            valid.append((in_band & (c >= lo) & (c < hi)) | (c >= nb))
        else:
            keys.append(kc_ref[0])
            vals.append(vc_ref[0])
    s = {}
    for u, pr in chains:
        qg = jnp.concatenate([q_ref[0, (WA_PAIR * pr + h) * WA_HD:(WA_PAIR * pr + h + 1) * WA_HD, u * tq:(u + 1) * tq]
                              for h in range(WA_PAIR)], axis=1)
        qpad = jnp.concatenate([qg, zeros] if grp[pr] == 0 else [zeros, qg], axis=0)
        sp = jnp.dot(keys[u], qpad, preferred_element_type=F32)
        s[u, pr] = jnp.where(valid[u], sp, NEG) if band else sp
    sink = [sink_ref[pr] for pr in range(npr)]
    m = {ch: jnp.maximum(jnp.max(s[ch], axis=0, keepdims=True), sink[ch[1]]) for ch in chains}
    p = {ch: jnp.exp2(s[ch] - m[ch]).astype(BF16) for ch in chains}
    R = {}
    for u, pr in chains:
        vaug = jnp.concatenate([vals[u][grp[pr] * WA_HD:(grp[pr] + 1) * WA_HD, :], ones_rows], axis=0)
        R[u, pr] = jnp.dot(vaug, p[u, pr], preferred_element_type=F32)
    for u, pr in chains:
        o = R[u, pr][:WA_HD] / (R[u, pr][WA_HD:WA_HD + 1] + jnp.exp2(sink[pr] - m[u, pr]))
        ot = jnp.concatenate([o[:, h * tq:(h + 1) * tq] for h in range(WA_PAIR)], axis=0)
        wo = WA_PAIR * WA_HD
        o_ref[0, u * tq:(u + 1) * tq, pr * wo:(pr + 1) * wo] = ot.T.astype(o_ref.dtype)


def _window_attention(qt, kc, vtc, sink, kl=None, vtl=None):
    B, WQ, Lq = qt.shape
    Lc = kc.shape[1]
    tq = WINDOW
    nq = Lq // tq
    nsub = math.gcd(WA_SUB, nq)
    band = kl is not None
    npair = WA_HEADS // WA_PAIR
    sink_row = jnp.repeat(sink.astype(F32).reshape(npair, WA_PAIR) * LOG2E, tq, axis=1).reshape(npair, 1, WA_PAIR * tq)
    qspec = pl.BlockSpec((1, WQ, nsub * tq), lambda b, i: (b, 0, i))
    kcspec = pl.BlockSpec((1, Lc, LANES), lambda b, i: (b, 0, 0))
    vcspec = pl.BlockSpec((1, LANES, Lc), lambda b, i: (b, 0, 0))
    sspec = pl.BlockSpec((npair, 1, WA_PAIR * tq), lambda b, i: (0, 0, 0))
    if band:
        blk = lambda off: (lambda i: jnp.clip(i * nsub + off, 0, nq - 1))
        kspec = lambda f: pl.BlockSpec((1, tq, LANES), lambda b, i: (b, f(i), 0))
        vspec = lambda f: pl.BlockSpec((1, LANES, tq), lambda b, i: (b, 0, f(i)))
        offs = range(-1, nsub + 1)
        in_specs = ([qspec] + [kspec(blk(o)) for o in offs] + [vspec(blk(o)) for o in offs]
                    + [kcspec, vcspec, sspec])
        args = (qt,) + (kl,) * (nsub + 2) + (vtl,) * (nsub + 2) + (kc, vtc, sink_row)
    else:
        in_specs, args = [qspec, kcspec, vcspec, sspec], (qt, kc, vtc, sink_row)
    return pl.pallas_call(
        functools.partial(_wa_kernel, band=band, nq=nq, nsub=nsub),
        grid=(B, nq // nsub),
        in_specs=in_specs,
        out_specs=pl.BlockSpec((1, nsub * tq, WQ), lambda b, i: (b, i, 0)),
        out_shape=jax.ShapeDtypeStruct((B, Lq, WQ), BF16),
        compiler_params=_cp(("parallel", "parallel")),
        name="window_attention",
    )(*args)


def _mlstm_kernel(*refs, reverse, nc, bb):
    if reverse:
        (qt_ref, k_ref, vt_ref, g_ref, gb_ref, c0_ref, m0_ref, hft_ref, o_ref, gn_ref,
         y_ref, cf_ref, mf_ref, c_scr, m_scr) = refs
    else:
        (qt_ref, k_ref, vt_ref, g_ref, gb_ref, c0_ref, m0_ref,
         y_ref, cf_ref, mf_ref, c_scr, m_scr) = refs
    T = ML_CHUNK
    d = 1 if reverse else 0
    step = pl.program_id(1)

    @pl.when(step == 0)
    def _():
        c_scr[...] = c0_ref[...]
        m_scr[...] = m0_ref[...]

    si = lax.broadcasted_iota(jnp.int32, (T, T), 0)
    ti = lax.broadcasted_iota(jnp.int32, (T, T), 1)
    tri = ((ti >= si) if reverse else (ti <= si)).astype(F32)
    mask_t = (si >= ti) if reverse else (si <= ti)
    ones_rows = jnp.ones((ML_HD, T), BF16)
    e_last = 0 if reverse else T - 1
    chains = [(bi, h) for bi in range(bb) for h in range(ML_HEADS)]
    gates = []
    for bi in range(bb):
        G = g_ref[bi] + gb_ref[...]
        LF = jax.nn.log_sigmoid(G)
        Bc = jnp.dot(tri, LF, preferred_element_type=F32, precision=HIGHEST)
        gates.append((G.T, Bc.T, Bc - pltpu.roll(G, 4, 1)))
    st, ph = {}, {}
    for bi, h in chains:
        hs = slice(h * ML_HD, (h + 1) * ML_HD)
        st[bi, h] = jnp.dot(k_ref[bi, :, hs], qt_ref[bi, hs, :], preferred_element_type=F32)
    for bi, h in chains:
        GT, BT, Dc = gates[bi]
        fl, il = d * 8 + 4 + h, d * 8 + h
        b_row, i_row = BT[fl:fl + 1, :], GT[il:il + 1, :]
        log_d = jnp.where(mask_t, b_row - Dc[:, fl:fl + 1], NEG)
        m_prev = m_scr[bi, h, 0:1, 0:1]
        m_inter = b_row + m_prev
        m_t = jnp.maximum(m_inter, jnp.max(log_d, axis=0, keepdims=True))
        wqk = (st[bi, h] * jnp.exp(log_d - m_t)).astype(BF16)
        b_end = BT[fl:fl + 1, e_last:e_last + 1]
        log_w = b_end - b_row + i_row
        m_next = jnp.maximum(b_end + m_prev, jnp.max(log_w, axis=1, keepdims=True))
        ph[bi, h] = (wqk, jnp.exp(m_inter - m_t), jnp.exp(-m_t), jnp.exp(log_w - m_next),
                     jnp.exp(b_end + m_prev - m_next), m_next)
    for bi, h in chains:
        hs = slice(h * ML_HD, (h + 1) * ML_HD)
        wqk, cs, em, w_row, decay, m_next = ph[bi, h]
        kh, qt = k_ref[bi, :, hs], qt_ref[bi, hs, :]
        vaug = jnp.concatenate([vt_ref[bi, hs, :], ones_rows], axis=0)
        R = (jnp.dot(vaug, wqk, preferred_element_type=F32)
             + cs * jnp.dot(c_scr[bi, h].astype(BF16), qt, preferred_element_type=F32))
        hh = R[:ML_HD] / jnp.maximum(jnp.abs(R[ML_HD:]), em)
        if reverse:
            hsum = hft_ref[bi, hs, :] + hh
            hn = hsum * lax.rsqrt(jnp.mean(hsum * hsum, axis=0, keepdims=True) + EPS) * gn_ref[hs, :]
            y_ref[bi, :, hs] = (jax.nn.sigmoid(o_ref[bi, :, hs].astype(F32)) * hn.T).astype(y_ref.dtype)
        else:
            y_ref[bi, hs, :] = hh
        wv = (vaug.astype(F32) * w_row).astype(BF16)
        c_scr[bi, h] = decay * c_scr[bi, h] + jnp.dot(wv, kh, preferred_element_type=F32)
        m_scr[bi, h] = jnp.broadcast_to(m_next, (8, LANES))

    @pl.when(step == nc - 1)
    def _():
        cf_ref[...] = c_scr[...]
        mf_ref[...] = m_scr[...]


def _mlstm_scan(qt, k, vt, z, gates, gate_b, c0, m0, reverse, hft=None, norm_g=None):
    B, L, W = k.shape
    T = ML_CHUNK
    nc = L // T
    bb = 4 if B % 4 == 0 else 2
    cj =(lambda j: nc - 1 - j) if reverse else (lambda j: j)
    tok = pl.BlockSpec((bb, T, W), lambda b, j: (b, cj(j), 0))
    ttok = pl.BlockSpec((bb, W, T), lambda b, j: (b, 0, cj(j)))
    cspec = pl.BlockSpec((bb, ML_HEADS, 2 * ML_HD, ML_HD), lambda b, j: (b, 0, 0, 0))
    mspec = pl.BlockSpec((bb, ML_HEADS, 8, LANES), lambda b, j: (b, 0, 0, 0))
    in_specs = [ttok, tok, ttok, pl.BlockSpec((bb, T, LANES), lambda b, j: (b, cj(j), 0)),
                pl.BlockSpec((1, LANES), lambda b, j: (0, 0)), cspec, mspec]
    args = [qt, k, vt, gates, gate_b, c0, m0]
    if reverse:
        in_specs += [ttok, pl.BlockSpec((bb, T, W), lambda b, j: (b, cj(j), Z_MLO // W)),
                     pl.BlockSpec((W, T), lambda b, j: (0, 0))]
        args += [hft, z, jnp.broadcast_to(norm_g.astype(F32).reshape(W, 1), (W, T))]
    return pl.pallas_call(
        functools.partial(_mlstm_kernel, reverse=reverse, nc=nc, bb=bb),
        grid=(B // bb, nc),
        in_specs=in_specs,
        out_specs=[tok if reverse else ttok, cspec, mspec],
        out_shape=[jax.ShapeDtypeStruct((B, L, W), BF16) if reverse else jax.ShapeDtypeStruct((B, W, L), F32),
                   jax.ShapeDtypeStruct(c0.shape, F32), jax.ShapeDtypeStruct(m0.shape, F32)],
        scratch_shapes=[pltpu.VMEM((bb, ML_HEADS, 2 * ML_HD, ML_HD), F32), pltpu.VMEM((bb, ML_HEADS, 8, LANES), F32)],
        compiler_params=_cp(("parallel", "arbitrary")),
        name="mlstm_reverse" if reverse else "mlstm_forward",
    )(*args)


def _fft_dims(Lp):
    n1 = 2 * Lp // FFT_N2
    nt1 = Lp // FFT_N2
    nk1 = -(-(n1 // 2 + 1) // 8) * 8
    return n1, nt1, nk1


def _fft_tables(Lp):
    n1, nt1, nk1 = _fft_dims(Lp)
    N = 2 * Lp
    k1 = jnp.arange(nk1)
    t1 = jnp.arange(nt1)
    ang_a = (2.0 * math.pi / n1) * ((k1[:, None] * t1[None, :]) % n1).astype(F32)
    fa = jnp.stack([jnp.cos(ang_a), -jnp.sin(ang_a)], axis=1).reshape(2 * nk1, nt1)
    wgt = jnp.where((k1 == 0) | (k1 == n1 // 2), 1.0, 2.0) * (k1 <= n1 // 2) / N
    fai = jnp.stack([jnp.cos(ang_a) * wgt[:, None], -jnp.sin(ang_a) * wgt[:, None]], axis=1)
    fai = fai.reshape(2 * nk1, nt1).T
    k2 = jnp.arange(FFT_N2)
    t2 = jnp.arange(FFT_N2)
    idx = (t2[None, None, :] * k1[:, None, None] + n1 * t2[None, None, :] * k2[None, :, None]) % N
    phi = (2.0 * math.pi / N) * idx.astype(F32)
    gr, gi = jnp.cos(phi), -jnp.sin(phi)
    gfwd = jnp.concatenate([jnp.concatenate([gr, -gi], axis=2), jnp.concatenate([gi, gr], axis=2)], axis=1)
    grt, git = jnp.swapaxes(gr, 1, 2), jnp.swapaxes(gi, 1, 2)
    ginv = jnp.concatenate([jnp.concatenate([grt, git], axis=2), jnp.concatenate([-git, grt], axis=2)], axis=1)
    eye = jnp.eye(FFT_TK, dtype=F32)
    return (jnp.kron(fa, eye).astype(BF16), jnp.kron(fai, eye).astype(BF16), gfwd.astype(BF16), ginv.astype(BF16))


FFT_TB = 32
FFT_TK = 8
FFT_CB = 512


def _stage_a_tile(fa_ref, x, rows):
    nt1, tb, cb = x.shape
    x = x.reshape(nt1, tb // FFT_TK, FFT_TK, cb)
    parts = []
    for j in range(tb // FFT_TK):
        xj = x[:, j].reshape(nt1 * FFT_TK, cb).astype(BF16)
        r = jnp.dot(fa_ref[...], xj, preferred_element_type=F32)
        parts.append(r.reshape(rows, 1, FFT_TK, cb))
    return jnp.concatenate(parts, axis=1).reshape(rows, tb, cb)


def _fa_kernel(fa_ref, x_ref, o_ref):
    o_ref[0] = _stage_a_tile(fa_ref, x_ref[0].astype(F32), o_ref.shape[1]).astype(o_ref.dtype)


def _fft_stage_a(fa, y, out_dtype):
    B, Lp, C = y.shape
    nt1 = Lp // FFT_N2
    rows = fa.shape[0] // FFT_TK
    return pl.pallas_call(
        _fa_kernel,
        grid=(B, FFT_N2 // FFT_TB, C // FFT_CB),
        in_specs=[pl.BlockSpec(fa.shape, lambda b, j, c: (0, 0)),
                  pl.BlockSpec((1, nt1, FFT_TB, FFT_CB), lambda b, j, c: (b, 0, j, c))],
        out_specs=pl.BlockSpec((1, rows, FFT_TB, FFT_CB), lambda b, j, c: (b, 0, j, c)),
        out_shape=jax.ShapeDtypeStruct((B, rows, FFT_N2, C), out_dtype),
        compiler_params=_cp(("parallel", "parallel", "parallel")),
        name="fft_stage_a",
    )(fa, y.reshape(B, nt1, FFT_N2, C))


def _fc_filter_kernel(g_ref, s_ref, ss_ref, h_ref, *, kb):
    C = HY_CH
    for i in range(kb):
        for n in range(HY_ORDER):
            cf, cb = (2 * n) * C, (2 * n + 1) * C
            scale = lax.rsqrt(ss_ref[0:1, cf:cf + C] + ss_ref[0:1, cb:cb + C] + EPS)
            sf = s_ref[0, 2 * i:2 * i + 2, :, cf:cf + C].reshape(2 * FFT_N2, C).astype(BF16)
            sb = s_ref[0, 2 * i:2 * i + 2, :, cb:cb + C].reshape(2 * FFT_N2, C).astype(BF16)
            xf = jnp.dot(g_ref[i], sf, preferred_element_type=F32)
            xb = jnp.dot(g_ref[i], sb, preferred_element_type=F32)
            h_ref[n, 2 * i] = (xf[:FFT_N2] + xb[:FFT_N2]) * scale
            h_ref[n, 2 * i + 1] = (xf[FFT_N2:] - xb[FFT_N2:]) * scale


def _fft_filter_spectrum(gfwd, s_filt, sumsq):
    rows = s_filt.shape[1]
    C4 = s_filt.shape[-1]
    kb = 2
    return pl.pallas_call(
        functools.partial(_fc_filter_kernel, kb=kb),
        grid=(rows // (2 * kb),),
        in_specs=[pl.BlockSpec((kb, 2 * FFT_N2, 2 * FFT_N2), lambda i: (i, 0, 0)),
                  pl.BlockSpec((1, 2 * kb, FFT_N2, C4), lambda i: (0, i, 0, 0)),
                  pl.BlockSpec((1, C4), lambda i: (0, 0))],
        out_specs=pl.BlockSpec((HY_ORDER, 2 * kb, FFT_N2, HY_CH), lambda i: (0, i, 0, 0)),
        out_shape=jax.ShapeDtypeStruct((HY_ORDER, rows, FFT_N2, HY_CH), F32),
        compiler_params=_cp(("parallel",)),
        name="fft_filter_spectrum",
    )(gfwd, s_filt, sumsq)


def _fc_kernel(g_ref, gi_ref, h_ref, s_ref, o_ref, *, kb):
    C = s_ref.shape[-1]
    x = [jnp.dot(g_ref[i], s_ref[0, 2 * i:2 * i + 2].reshape(2 * FFT_N2, C), preferred_element_type=F32)
         for i in range(kb)]
    z = []
    for i in range(kb):
        xr, xi = x[i][:FFT_N2], x[i][FFT_N2:]
        hr, hi = h_ref[0, 2 * i], h_ref[0, 2 * i + 1]
        z.append(jnp.concatenate([xr * hr - xi * hi, xr * hi + xi * hr], axis=0).astype(BF16))
    bm = [jnp.dot(gi_ref[i], z[i], preferred_element_type=F32) for i in range(kb)]
    for i in range(kb):
        o_ref[0, 2 * i:2 * i + 2] = bm[i].reshape(2, FFT_N2, C).astype(o_ref.dtype)


def _fft_stage_c(gfwd, ginv, hspec, order, s):
    B, rows, _, C = s.shape
    kb = 12 if rows % 24 == 0 else 4
    sspec =pl.BlockSpec((1, 2 * kb, FFT_N2, C), lambda i, b: (b, i, 0, 0))
    gspec = pl.BlockSpec((kb, 2 * FFT_N2, 2 * FFT_N2), lambda i, b: (i, 0, 0))
    return pl.pallas_call(
        functools.partial(_fc_kernel, kb=kb),
        grid=(rows // (2 * kb), B),
        in_specs=[gspec, gspec,
                  pl.BlockSpec((1, 2 * kb, FFT_N2, C), lambda i, b: (order, i, 0, 0)), sspec],
        out_specs=sspec,
        out_shape=jax.ShapeDtypeStruct(s.shape, BF16),
        compiler_params=_cp(("parallel", "arbitrary")),
        name="fft_stage_c",
    )(gfwd, ginv, hspec, s)


def _fai_kernel(fai_ref, b_ref, y_ref, gate_ref, skip_ref, *rest, fused):
    if fused:
        fa_ref, o_ref, s_ref = rest
    else:
        (o_ref,) = rest
    nt1, tb, cb = y_ref.shape[1:]
    rows = b_ref.shape[1]
    bm = b_ref[0].astype(F32).reshape(rows, tb // FFT_TK, FFT_TK, cb)
    parts = []
    for j in range(tb // FFT_TK):
        bj = bm[:, j].reshape(rows * FFT_TK, cb).astype(BF16)
        yf = jnp.dot(fai_ref[...], bj, preferred_element_type=F32)
        parts.append(yf.reshape(nt1, 1, FFT_TK, cb))
    yf = jnp.concatenate(parts, axis=1).reshape(nt1, tb, cb)
    out = (gate_ref[0].astype(F32) * (yf + skip_ref[...].reshape(1, 1, cb) * y_ref[0].astype(F32))).astype(o_ref.dtype)
    o_ref[0] = out
    if fused:
        s_ref[0] = _stage_a_tile(fa_ref, out.astype(F32), s_ref.shape[1]).astype(s_ref.dtype)


def _fft_stage_a_inv(fai, bm, y, gate, skip, fa_next=None):
    B, Lp, C = y.shape
    nt1 = Lp // FFT_N2
    rows = fai.shape[1] // FFT_TK
    tok = pl.BlockSpec((1, nt1, FFT_TB, FFT_CB), lambda b, j, c: (b, 0, j, c))
    spec_blk = pl.BlockSpec((1, rows, FFT_TB, FFT_CB), lambda b, j, c: (b, 0, j, c))
    in_specs = [pl.BlockSpec(fai.shape, lambda b, j, c: (0, 0)), spec_blk,
                tok, tok, pl.BlockSpec((1, FFT_CB), lambda b, j, c: (0, c))]
    args = [fai, bm, y.reshape(B, nt1, FFT_N2, C), gate.reshape(B, nt1, FFT_N2, C), skip.astype(F32).reshape(1, C)]
    out_specs, out_shape = [tok], [jax.ShapeDtypeStruct((B, nt1, FFT_N2, C), BF16)]
    if fa_next is not None:
        in_specs.append(pl.BlockSpec(fa_next.shape, lambda b, j, c: (0, 0)))
        args.append(fa_next)
        out_specs.append(spec_blk)
        out_shape.append(jax.ShapeDtypeStruct((B, rows, FFT_N2, C), BF16))
    res = pl.pallas_call(
        functools.partial(_fai_kernel, fused=fa_next is not None),
        grid=(B, FFT_N2 // FFT_TB, C // FFT_CB),
        in_specs=in_specs,
        out_specs=out_specs,
        out_shape=out_shape,
        compiler_params=_cp(("parallel", "parallel", "parallel")),
        name="fft_stage_a_inv",
    )(*args)
    out = res[0].reshape(B, Lp, C)
    return (out, res[1]) if fa_next is not None else out


def _hgen_kernel(z_ref, w1_ref, b1_ref, fr_ref, w2_ref, b2_ref, w3_ref, dec_ref, h_ref, ss_ref):
    i = pl.program_id(0)
    z = z_ref[...]
    tm = z.shape[0]
    h = jnp.sin(fr_ref[0:1, :] * (jnp.dot(z, w1_ref[...], preferred_element_type=F32, precision=HIGHEST)
                                 + b1_ref[...]))
    h = jnp.sin(fr_ref[1:2, :] * (jnp.dot(h, w2_ref[...], preferred_element_type=F32, precision=HIGHEST)
                                 + b2_ref[...]))
    h = jnp.dot(h, w3_ref[...], preferred_element_type=F32, precision=HIGHEST)
    h = h * jnp.exp(-z[:, 0:1] * jnp.abs(dec_ref[...]))
    row = lax.broadcasted_iota(jnp.int32, h.shape, 0) + i * tm
    col = lax.broadcasted_iota(jnp.int32, h.shape, 1)
    h = jnp.where((row == 0) & ((col // HY_CH) % 2 == 1), 0.0, h)
    h_ref[...] = h

    @pl.when(i == 0)
    def _():
        ss_ref[...] = jnp.zeros_like(ss_ref)

    ss_ref[...] += jnp.sum(h * h, axis=0, keepdims=True)


def _hyena_filters(L, p):
    t = jnp.arange(L, dtype=F32)
    tn = t / (L - 1)
    w = 2.0 * math.pi * t / L
    bands = jnp.linspace(1e-4, HY_BANDS - 1, HY_BANDS, dtype=F32)
    ang = w[:, None] * bands[None, :]
    z = jnp.concatenate([tn[:, None], jnp.cos(ang), -jnp.sin(ang)], axis=-1)
    z = jnp.pad(z, ((0, 0), (0, LANES - HY_EMB)))
    w1 = jnp.pad(p['hy_pe_w1'].astype(F32), ((0, LANES - HY_EMB), (0, 0)))
    nf = HY_ORDER * 2 * HY_CH
    tm = min(512, L)
    const = lambda shape: pl.BlockSpec(shape, lambda i: (0,) * len(shape))
    return pl.pallas_call(
        _hgen_kernel,
        grid=(L // tm,),
        in_specs=[pl.BlockSpec((tm, LANES), lambda i: (i, 0)), const((LANES, HY_FFN)), const((1, HY_FFN)),
                  const((2, HY_FFN)), const((HY_FFN, HY_FFN)), const((1, HY_FFN)), const((HY_FFN, nf)),
                  const((1, nf))],
        out_specs=[pl.BlockSpec((tm, nf), lambda i: (i, 0)), const((1, nf))],
        out_shape=[jax.ShapeDtypeStruct((L, nf), F32), jax.ShapeDtypeStruct((1, nf), F32)],
        compiler_params=_cp(("arbitrary",)),
        name="hyena_filter_gen",
    )(z, w1, p['hy_pe_b1'].reshape(1, HY_FFN), p['hy_freq'], p['hy_pe_w2'], p['hy_pe_b2'].reshape(1, HY_FFN),
      p['hy_pe_w3'], p['hy_decay'].reshape(1, nf))


def _hyena_branch(z, p, B, L):
    v, x1, x2 = _hy_prep(z, p['hy_conv_w'], p['hy_conv_b'], L)
    Lp = max(L, 2048)
    fa, fai, gfwd, ginv = _fft_tables(Lp)
    hfilt, sumsq = _hyena_filters(L, p)
    pad3 = lambda a: a.reshape(B, L, HY_CH) if Lp == L else jnp.pad(a.reshape(B, L, HY_CH), ((0, 0), (0, Lp - L), (0, 0)))
    hf = hfilt if Lp == L else jnp.pad(hfilt, ((0, Lp - L), (0, 0)))
    s_filt = _fft_stage_a(fa, hf[None], F32)
    hspec = _fft_filter_spectrum(gfwd, s_filt, sumsq)
    y = pad3(v)
    s = _fft_stage_a(fa, y, BF16)
    bm = _fft_stage_c(gfwd, ginv, hspec, 0, s)
    y, s = _fft_stage_a_inv(fai, bm, y, pad3(x1), p['hy_skip'][0], fa_next=fa)
    bm = _fft_stage_c(gfwd, ginv, hspec, 1, s)
    y = _fft_stage_a_inv(fai, bm, y, pad3(x2), p['hy_skip'][1])
    return y[:, :L].reshape(B * L, HY_CH)


def _merge_kernel(ya_ref, yb_ref, yc_ref, yd_ref, g0_ref, g1_ref, g2_ref, g3_ref, wup_ref, wout_ref,
                  x_ref, gate_ref, o_ref):
    acc = None
    for n, (y_ref, g_ref) in enumerate(((ya_ref, g0_ref), (yb_ref, g1_ref), (yc_ref, g2_ref), (yd_ref, g3_ref))):
        t = jax.nn.sigmoid(g_ref[...].astype(F32)) * jnp.dot(y_ref[...], wup_ref[n], preferred_element_type=F32)
        acc = t if acc is None else acc + t
    yl = jnp.dot(acc.astype(BF16), wout_ref[...], preferred_element_type=F32)
    o_ref[...] = x_ref[...] + gate_ref[0] * yl


def _merge(ys, z, w_up, w_out, x, gate, seq_len):
    T, D = x.shape
    tm = min(512, seq_len)
    per = seq_len // tm
    Wy = ys[0].shape[1]
    yspec = pl.BlockSpec((tm, Wy), lambda i: (i, 0))
    gspecs = [pl.BlockSpec((tm, D), functools.partial(lambda i, n: (i, Z_GATE // D + n), n=n)) for n in range(4)]
    return pl.pallas_call(
        _merge_kernel,
        grid=(T // tm,),
        in_specs=[yspec] * 4 + gspecs + [
            pl.BlockSpec((4, Wy, D), lambda i: (0, 0, 0)), pl.BlockSpec((D, D), lambda i: (0, 0)),
            pl.BlockSpec((tm, D), lambda i: (i, 0)), pl.BlockSpec((1, 1, D), lambda i: (i // per, 0, 0))],
        out_specs=pl.BlockSpec((tm, D), lambda i: (i, 0)),
        out_shape=jax.ShapeDtypeStruct((T, D), F32),
        compiler_params=_cp(("parallel",)),
        name="merge_branches",
    )(*ys, z, z, z, z, w_up, w_out, x, gate)


def _mlp_kernel(x_ref, g_ref, sc_ref, sh_ref, w1_ref, b1_ref, w2_ref, b2_ref, gate_ref, fg_ref, o_ref,
                h_ref, acc_ref, *, nk, final):
    k = pl.program_id(1)

    @pl.when(k == 0)
    def _():
        y = _rms(x_ref[...], g_ref[...])
        h_ref[...] = (y * (1.0 + sc_ref[0]) + sh_ref[0]).astype(BF16)
        acc_ref[...] = jnp.zeros_like(acc_ref)

    a = jnp.maximum(jnp.dot(h_ref[...], w1_ref[...], preferred_element_type=F32) + b1_ref[...], 0.0)
    acc_ref[...] += jnp.dot((a * a).astype(BF16), w2_ref[...], preferred_element_type=F32)

    @pl.when(k == nk - 1)
    def _():
        out = x_ref[...] + gate_ref[0] * (acc_ref[...] + b2_ref[...])
        if final:
            out = _rms(out, fg_ref[...])
        o_ref[...] = out


def _mlp(x, g, sc, sh, w1, b1, w2, b2, gate, final_g, seq_len, final):
    T, D = x.shape
    F = w1.shape[1]
    tm = min(1024, seq_len)
    per = seq_len // tm
    tk = 1024
    nk = F // tk
    row = lambda i, k: (i // per, 0, 0)
    return pl.pallas_call(
        functools.partial(_mlp_kernel, nk=nk, final=final),
        grid=(T // tm, nk),
        in_specs=[pl.BlockSpec((tm, D), lambda i, k: (i, 0)), pl.BlockSpec((1, D), lambda i, k: (0, 0)),
                  pl.BlockSpec((1, 1, D), row), pl.BlockSpec((1, 1, D), row),
                  pl.BlockSpec((D, tk), lambda i, k: (0, k)), pl.BlockSpec((1, tk), lambda i, k: (0, k)),
                  pl.BlockSpec((tk, D), lambda i, k: (k, 0)), pl.BlockSpec((1, D), lambda i, k: (0, 0)),
                  pl.BlockSpec((1, 1, D), row), pl.BlockSpec((1, D), lambda i, k: (0, 0))],
        out_specs=pl.BlockSpec((tm, D), lambda i, k: (i, 0)),
        out_shape=jax.ShapeDtypeStruct((T, D), F32),
        scratch_shapes=[pltpu.VMEM((tm, D), BF16), pltpu.VMEM((tm, D), F32)],
        compiler_params=_cp(("parallel", "arbitrary")),
        name="mlp",
    )(x, g.reshape(1, D), sc, sh, w1, b1.reshape(1, F), w2, b2.reshape(1, D), gate, final_g.reshape(1, D))


def _pack_w_in(w_in):
    hy_e = 3 * HY_CH
    ga_e = hy_e + (GA_HEADS + 2 * GA_KV) * GA_HD
    mw = ML_HEADS * ML_HD
    ml_e = ga_e + 4 * mw + 16
    wa_e = ml_e + (WA_HEADS + 2 * WA_KV) * WA_HD
    hy, ga = w_in[:, :hy_e], w_in[:, hy_e:ga_e]
    ml = w_in[:, ga_e:ml_e]
    wa = w_in[:, ml_e:wa_e]
    gate = w_in[:, wa_e:]
    waq, wakv = wa[:, :WA_HEADS * WA_HD], wa[:, WA_HEADS * WA_HD:]
    pad = jnp.zeros((w_in.shape[0], Z_COLS - Z_WAKV - wakv.shape[1]), w_in.dtype)
    packed = jnp.concatenate([hy, waq, ga, ml[:, :2 * mw], ml[:, 2 * mw:3 * mw], ml[:, 3 * mw:4 * mw], gate, wakv, pad],
                             axis=1)
    wg = jnp.pad(ml[:, 4 * mw:], ((0, 0), (0, LANES - 16)))
    return packed.astype(BF16), wg


def _token_mixers(zl, zc, gl, gc, p, B, L, Lc, with_ctx_out):
    ya_l = _hyena_branch(zl, p, B, L)
    ya_c = _hyena_branch(zc, p, B, Lc) if with_ctx_out else None
    cos, sin = _rope_tables(L, GA_HD)
    ql, kl, vl = _ga_prep(zl, cos, sin, p['ga_q_g'], p['ga_k_g'], B, L, True)
    qc, kc, vc = _ga_prep(zc, cos[:Lc], sin[:Lc], p['ga_q_g'], p['ga_k_g'], B, Lc, False)
    r3 = lambda a, n: a.reshape(B, n, a.shape[-1])
    k_all = jnp.concatenate([r3(kl, L), r3(kc, Lc)], axis=1)
    vt_all = jnp.concatenate([vl, vc], axis=2)
    yb_l = _global_attention(ql, k_all, vt_all).reshape(B * L, -1)
    yb_c = _global_attention(qc, r3(kc, Lc), vc).reshape(B * Lc, -1) if with_ctx_out else None
    mq_l, mk_l, mv_l = _ml_prep(zl, p['ml_conv_w'], p['ml_conv_b'], B, L)
    mq_c, mk_c, mv_c = _ml_prep(zc, p['ml_conv_w'], p['ml_conv_b'], B, Lc)
    gb = jnp.pad(p['ml_gate_b'].astype(F32), (0, LANES - 16)).reshape(1, LANES)
    c0 = jnp.zeros((B, ML_HEADS, 2 * ML_HD, ML_HD), F32)
    m0 = jnp.zeros((B, ML_HEADS, 8, LANES), F32)
    zl3, zc3, gl3, gc3 = r3(zl, L), r3(zc, Lc), r3(gl, L), r3(gc, Lc)
    h_cf, cf, mf = _mlstm_scan(mq_c, r3(mk_c, Lc), mv_c, zc3, gc3, gb, c0, m0, False)
    yc_c, cb, mb = _mlstm_scan(mq_c, r3(mk_c, Lc), mv_c, zc3, gc3, gb, c0, m0, True, h_cf, p['ml_norm_g'])
    h_lf, _, _ = _mlstm_scan(mq_l, r3(mk_l, L), mv_l, zl3, gl3, gb, cf, mf, False)
    yc_l, _, _ = _mlstm_scan(mq_l, r3(mk_l, L), mv_l, zl3, gl3, gb, cb, mb, True, h_lf, p['ml_norm_g'])
    yc_l = yc_l.reshape(B * L, -1)
    yc_c = yc_c.reshape(B * Lc, -1)
    cosw, sinw = _rope_tables(L, WA_HD)
    wq_l, wk_l, wv_l = _wa_prep(zl, cosw, sinw, B, L, True)
    wq_c, wk_c, wv_c = _wa_prep(zc, cosw[:Lc], sinw[:Lc], B, Lc, False)
    yd_l = _window_attention(wq_l, r3(wk_c, Lc), wv_c, p['wa_sink'], r3(wk_l, L), wv_l).reshape(B * L, -1)
    yd_c = (_window_attention(wq_c, r3(wk_c, Lc), wv_c, p['wa_sink']).reshape(B * Lc, -1)
            if with_ctx_out else None)
    return (ya_l, yb_l, yc_l, yd_l), (ya_c, yb_c, yc_c, yd_c)


def kernel(x, c, ctx, c_ctx, w_mod, b_mod, ln1_g, ln2_g, w_in, hy_conv_w, hy_conv_b,
           hy_pe_w1, hy_pe_b1, hy_freq, hy_pe_w2, hy_pe_b2, hy_pe_w3, hy_decay, hy_skip,
           ga_q_g, ga_k_g, ml_conv_w, ml_conv_b, ml_gate_b, ml_norm_g, wa_sink, w_up, w_out,
           mlp_w1, mlp_b1, mlp_w2, mlp_b2, final_g):
    B, L, D = x.shape
    Lc = ctx.shape[1]
    R = -(-(B + 1) // 8) * 8
    cvec = jnp.zeros((R, D), F32).at[:B].set(c).at[B].set(c_ctx)
    mod = _modulation(cvec, w_mod, b_mod)
    xl = x.reshape(B * L, D)
    xc = ctx.reshape(B * Lc, D)
    for l in range(DEPTH):
        with_ctx_out = l < DEPTH - 1
        p = dict(hy_conv_w=hy_conv_w[l], hy_conv_b=hy_conv_b[l], hy_pe_w1=hy_pe_w1[l],
                 hy_pe_b1=hy_pe_b1[l], hy_freq=hy_freq[l], hy_pe_w2=hy_pe_w2[l], hy_pe_b2=hy_pe_b2[l],
                 hy_pe_w3=hy_pe_w3[l], hy_decay=hy_decay[l], hy_skip=hy_skip[l],
                 ga_q_g=ga_q_g[l], ga_k_g=ga_k_g[l], ml_conv_w=ml_conv_w[l], ml_conv_b=ml_conv_b[l],
                 ml_gate_b=ml_gate_b[l], ml_norm_g=ml_norm_g[l], wa_sink=wa_sink[l])
        ml_rows = mod[l, :B].reshape(B, 1, 6 * D)
        mc_rows = jnp.broadcast_to(mod[l, B].reshape(1, 1, 6 * D), (B, 1, 6 * D))
        part = lambda m, n: m[:, :, n * D:(n + 1) * D]
        w_pack, w_gate = _pack_w_in(w_in[l])
        wg = w_gate.astype(BF16)
        zl, gl = _normmod_matmul(xl, ln1_g[l], part(ml_rows, 1), part(ml_rows, 0), w_pack, wg, L, Z_TN)
        zc, gc = _normmod_matmul(xc, ln1_g[l], part(mc_rows, 1), part(mc_rows, 0), w_pack, wg, Lc, Z_TN)
        ys_l, ys_c = _token_mixers(zl, zc, gl, gc, p, B, L, Lc, with_ctx_out)
        wup = w_up[l].astype(BF16)
        wout = w_out[l].astype(BF16)
        w1, w2 = mlp_w1[l].astype(BF16), mlp_w2[l].astype(BF16)
        xl = _merge(ys_l, zl, wup, wout, xl, part(ml_rows, 2), L)
        xl = _mlp(xl, ln2_g[l], part(ml_rows, 4), part(ml_rows, 3), w1, mlp_b1[l], w2, mlp_b2[l],
                  part(ml_rows, 5), final_g, L, final=(l == DEPTH - 1))
        if with_ctx_out:
            xc = _merge(ys_c, zc, wup, wout, xc, part(mc_rows, 2), Lc)
            xc = _mlp(xc, ln2_g[l], part(mc_rows, 4), part(mc_rows, 3), w1, mlp_b1[l], w2, mlp_b2[l],
                      part(mc_rows, 5), final_g, Lc, final=False)
    return xl.reshape(B, L, D)
```
